```python
import math
import jax, jax.numpy as jnp
from jax import lax
import numpy as np

D_MODEL = 1024
BATCH = 4
SEQ = 8192
DEPTH = 1

ATT_GROUPS = ((128, 1), (512, 4), (2048, 16))
N_GROUPS = len(ATT_GROUPS)
HEADS_PER_GROUP = 8
HEAD_DIM = 64
N_ATT_HEADS = N_GROUPS * HEADS_PER_GROUP
QKV_WIDTH = N_ATT_HEADS * HEAD_DIM
ATT_OUT_WIDTH = HEADS_PER_GROUP * HEAD_DIM
ATT_BLOCK = 128
N_BUCKETS = 32
MAX_DISTANCE = 2048
CONV_CH = 768
CONV_WIDTH = 31
N_EXPERTS = 32
TOP_K = 4
D_FF = 1024
SWIGLU_LIMIT = 7.0
SWIGLU_ALPHA = 1.702
MOE_BLOCK = 512
IN_WIDTH = 3 * QKV_WIDTH + 2 * CONV_CH + 2 * D_MODEL
LN_EPS = 1e-5
NEG_INF = -1e30
DEEPNORM_ALPHA = (2 * DEPTH) ** 0.25
DEEPNORM_BETA = (8 * DEPTH) ** -0.25

kernel_name = 'hybrid_dilated_attn_conformer_conv_moe_deepnorm'


def layer_norm(x, g, b):
    xf = x.astype(jnp.float32)
    mu = jnp.mean(xf, axis=-1, keepdims=True)
    var = jnp.mean(jnp.square(xf - mu), axis=-1, keepdims=True)
    y = (xf - mu) * lax.rsqrt(var + LN_EPS) * g.astype(jnp.float32) + b.astype(jnp.float32)
    return y.astype(x.dtype)


def t5_bucket(dist):
    max_exact = N_BUCKETS // 2
    log_ratio = jnp.log(jnp.maximum(dist, max_exact).astype(jnp.float32) / max_exact) / math.log(MAX_DISTANCE / max_exact)
    large = jnp.minimum(max_exact + (log_ratio * (N_BUCKETS - max_exact)).astype(jnp.int32), N_BUCKETS - 1)
    return jnp.where(dist < max_exact, dist, large)


def dilated_window_attention(q, k, v, bias_table, window, dil):
    B, S, H, E = q.shape
    L = S // dil
    nb = -(-L // ATT_BLOCK)
    Lp = nb * ATT_BLOCK
    sub_win = window // dil

    def to_sub(t):
        return jnp.swapaxes(t.reshape(B, L, dil, H, E), 1, 2)

    def key_blocks(t):
        tp = jnp.pad(t, ((0, 0), (0, 0), (ATT_BLOCK, Lp - L), (0, 0), (0, 0)))
        prev = tp[:, :, :Lp].reshape(B, dil, nb, ATT_BLOCK, H, E)
        cur = tp[:, :, ATT_BLOCK:].reshape(B, dil, nb, ATT_BLOCK, H, E)
        return jnp.concatenate([prev, cur], axis=3)

    qs = jnp.pad(to_sub(q), ((0, 0), (0, 0), (0, Lp - L), (0, 0), (0, 0))).reshape(B, dil, nb, ATT_BLOCK, H, E)
    kb = key_blocks(to_sub(k))
    vb = key_blocks(to_sub(v))

    qi = jnp.arange(ATT_BLOCK)[:, None]
    kj = jnp.arange(2 * ATT_BLOCK)[None, :]
    dist = qi - kj + ATT_BLOCK
    in_band = (dist >= 0) & (dist <= sub_win)
    key_pos = jnp.arange(nb)[:, None, None] * ATT_BLOCK + kj[None] - ATT_BLOCK
    mask = in_band[None] & (key_pos >= 0)
    bias = bias_table[t5_bucket(jnp.maximum(dist, 0) * dil)]
    bias = jnp.transpose(bias, (2, 0, 1)).astype(jnp.float32)

    s = jnp.einsum('brnqhe,brnkhe->brnhqk', qs, kb, preferred_element_type=jnp.float32) * (HEAD_DIM ** -0.5) + bias
    s = jnp.where(mask[:, None], s, NEG_INF)
    m = jnp.max(s, axis=-1, keepdims=True)
    p = jnp.exp(s - m)
    den = jnp.sum(p, axis=-1)
    o = jnp.einsum('brnhqk,brnkhe->brnqhe', p, vb.astype(jnp.float32))
    o = o / jnp.swapaxes(den, 3, 4)[..., None]

    def from_sub(t):
        tail = t.shape[4:]
        t = t.reshape((B, dil, Lp) + tail)[:, :, :L]
        return jnp.swapaxes(t, 1, 2).reshape((B, S) + tail)

    return (from_sub(o), from_sub(jnp.swapaxes(m[..., 0], 3, 4)), from_sub(jnp.swapaxes(den, 3, 4)))


def token_mixer(h, w_in, rel_bias, w_dw, b_dw, conv_ln_g, conv_ln_b, w_o_attn, w_o_conv, w_out):
    B, S, _ = h.shape
    proj = jnp.matmul(h, w_in)
    q, k, v, u, g_attn, g_conv = jnp.split(
        proj, [QKV_WIDTH, 2 * QKV_WIDTH, 3 * QKV_WIDTH, 3 * QKV_WIDTH + 2 * CONV_CH,
               3 * QKV_WIDTH + 2 * CONV_CH + D_MODEL], axis=-1)
    q = q.reshape(B, S, N_GROUPS, HEADS_PER_GROUP, HEAD_DIM)
    k = k.reshape(B, S, N_GROUPS, HEADS_PER_GROUP, HEAD_DIM)
    v = v.reshape(B, S, N_GROUPS, HEADS_PER_GROUP, HEAD_DIM)

    outs, maxes, dens = [], [], []
    for gi, (window, dil) in enumerate(ATT_GROUPS):
        table = rel_bias[:, gi * HEADS_PER_GROUP:(gi + 1) * HEADS_PER_GROUP]
        o_g, m_g, d_g = dilated_window_attention(q[:, :, gi], k[:, :, gi], v[:, :, gi], table, window, dil)
        outs.append(o_g)
        maxes.append(m_g)
        dens.append(d_g)
    outs = jnp.stack(outs)
    maxes = jnp.stack(maxes)
    dens = jnp.stack(dens)
    wts = dens * jnp.exp(maxes - jnp.max(maxes, axis=0, keepdims=True))
    attn = jnp.sum(wts[..., None] * outs, axis=0) / jnp.sum(wts, axis=0)[..., None]
    attn = attn.reshape(B, S, ATT_OUT_WIDTH).astype(h.dtype)

    u_val, u_gate = jnp.split(u, 2, axis=-1)
    glu = u_val * jax.nn.sigmoid(u_gate)
    dw = lax.conv_general_dilated(glu, w_dw, window_strides=(1,), padding=[(CONV_WIDTH - 1, 0)],
                                  dimension_numbers=('NWC', 'WIO', 'NWC'), feature_group_count=CONV_CH) + b_dw
    conv = jax.nn.silu(layer_norm(dw, conv_ln_g, conv_ln_b))

    merged = (jax.nn.sigmoid(g_attn) * jnp.matmul(attn, w_o_attn)
              + jax.nn.sigmoid(g_conv) * jnp.matmul(conv, w_o_conv))
    return jnp.matmul(merged, w_out)


def routed_experts(h, w_router, b_router, w_gate_up, b_gate_up, w_down, b_down):
    B, S, D = h.shape
    T = B * S
    xf = h.reshape(T, D)
    logits = jnp.matmul(xf, w_router, preferred_element_type=jnp.float32) + b_router.astype(jnp.float32)
    top_v, top_e = lax.top_k(logits, TOP_K)
    gates = jax.nn.softmax(top_v, axis=-1)

    flat_e = top_e.reshape(-1)
    flat_g = gates.reshape(-1)
    flat_tok = jnp.arange(T * TOP_K, dtype=jnp.int32) // TOP_K
    order = jnp.argsort(flat_e)
    sorted_e = flat_e[order]
    counts = jnp.bincount(flat_e, length=N_EXPERTS)
    starts = jnp.cumsum(counts) - counts
    padded = (counts + MOE_BLOCK - 1) // MOE_BLOCK * MOE_BLOCK
    pad_ends = jnp.cumsum(padded)
    pad_starts = pad_ends - padded
    dest = pad_starts[sorted_e] + jnp.arange(T * TOP_K, dtype=jnp.int32) - starts[sorted_e]
    n_rows = T * TOP_K + N_EXPERTS * MOE_BLOCK
    n_blocks = n_rows // MOE_BLOCK
    row_tok = jnp.zeros((n_rows,), jnp.int32).at[dest].set(flat_tok[order])
    row_w = jnp.zeros((n_rows,), jnp.float32).at[dest].set(flat_g[order])
    blk_e = jnp.minimum(jnp.searchsorted(pad_ends, jnp.arange(n_blocks) * MOE_BLOCK, side='right'), N_EXPERTS - 1)
    xs = xf[row_tok].reshape(n_blocks, MOE_BLOCK, D)
    ws = row_w.reshape(n_blocks, MOE_BLOCK)

    def expert_block(args):
        xb, wb, e = args
        hgu = jnp.matmul(xb, w_gate_up[e]) + b_gate_up[e]
        gate = jnp.minimum(hgu[:, :D_FF], SWIGLU_LIMIT)
        up = jnp.clip(hgu[:, D_FF:], -SWIGLU_LIMIT, SWIGLU_LIMIT)
        act = (up + 1.0) * gate * jax.nn.sigmoid(SWIGLU_ALPHA * gate)
        y = jnp.matmul(act, w_down[e]) + b_down[e]
        return y.astype(jnp.float32) * wb[:, None]

    ys = lax.map(expert_block, (xs, ws, blk_e)).reshape(n_rows, D)
    out = jax.ops.segment_sum(ys, row_tok, num_segments=T)
    return out.reshape(B, S, D).astype(h.dtype)


def setup_inputs(seed: int = 0) -> dict:
    key = jax.random.key(seed)
    ks = jax.random.split(key, 20)
    D = D_MODEL
    f32 = jnp.float32

    def nrm(k, shape, scale):
        return jax.random.normal(k, shape, f32) * scale

    col_scale = jnp.concatenate([jnp.ones((2 * QKV_WIDTH,), f32), jnp.full((QKV_WIDTH,), DEEPNORM_BETA, f32),
                                 jnp.ones((2 * CONV_CH + 2 * D,), f32)])
    return {
        'x': nrm(ks[0], (BATCH, SEQ, D), 1.0),
        'w_in': nrm(ks[1], (DEPTH, D, IN_WIDTH), D ** -0.5) * col_scale,
        'rel_bias': nrm(ks[2], (N_BUCKETS, N_ATT_HEADS), 0.5),
        'w_dw': nrm(ks[3], (DEPTH, CONV_WIDTH, 1, CONV_CH), CONV_WIDTH ** -0.5),
        'b_dw': nrm(ks[4], (DEPTH, CONV_CH), 0.02),
        'conv_ln_g': 1.0 + nrm(ks[5], (DEPTH, CONV_CH), 0.02),
        'conv_ln_b': nrm(ks[6], (DEPTH, CONV_CH), 0.02),
        'w_o_attn': nrm(ks[7], (DEPTH, ATT_OUT_WIDTH, D), ATT_OUT_WIDTH ** -0.5 * DEEPNORM_BETA),
        'w_o_conv': nrm(ks[8], (DEPTH, CONV_CH, D), CONV_CH ** -0.5 * DEEPNORM_BETA),
        'w_out': nrm(ks[9], (DEPTH, D, D), D ** -0.5 * DEEPNORM_BETA),
        'ln1_g': 1.0 + nrm(ks[10], (DEPTH, D), 0.02),
        'ln1_b': nrm(ks[11], (DEPTH, D), 0.02),
        'w_router': nrm(ks[12], (DEPTH, D, N_EXPERTS), D ** -0.5),
        'b_router': nrm(ks[13], (DEPTH, N_EXPERTS), 0.01),
        'w_gate_up': nrm(ks[14], (DEPTH, N_EXPERTS, D, 2 * D_FF), D ** -0.5 * DEEPNORM_BETA),
        'b_gate_up': nrm(ks[15], (DEPTH, N_EXPERTS, 2 * D_FF), 0.01),
        'w_down': nrm(ks[16], (DEPTH, N_EXPERTS, D_FF, D), D_FF ** -0.5 * DEEPNORM_BETA),
        'b_down': nrm(ks[17], (DEPTH, N_EXPERTS, D), 0.01),
        'ln2_g': 1.0 + nrm(ks[18], (DEPTH, D), 0.02),
        'ln2_b': nrm(ks[19], (DEPTH, D), 0.02),
    }


def reference(x, w_in, rel_bias, w_dw, b_dw, conv_ln_g, conv_ln_b, w_o_attn, w_o_conv, w_out,
              ln1_g, ln1_b, w_router, b_router, w_gate_up, b_gate_up, w_down, b_down, ln2_g, ln2_b):
    h = x
    for l in range(DEPTH):
        mix = token_mixer(h, w_in[l], rel_bias, w_dw[l], b_dw[l], conv_ln_g[l], conv_ln_b[l],
                          w_o_attn[l], w_o_conv[l], w_out[l])
        h = layer_norm(DEEPNORM_ALPHA * h + mix, ln1_g[l], ln1_b[l])
        ffn = routed_experts(h, w_router[l], b_router[l], w_gate_up[l], b_gate_up[l], w_down[l], b_down[l])
        h = layer_norm(DEEPNORM_ALPHA * h + ffn, ln2_g[l], ln2_b[l])
    return h
```

```python
import functools
import math

import jax
import jax.numpy as jnp
from jax import lax
from jax.experimental import pallas as pl
from jax.experimental.pallas import tpu as pltpu

D_MODEL = 1024
ATT_GROUPS = ((128, 1), (512, 4), (2048, 16))
N_GROUPS = len(ATT_GROUPS)
HEADS_PER_GROUP = 8
HEAD_DIM = 64
GROUP_WIDTH = HEADS_PER_GROUP * HEAD_DIM
QKV_WIDTH = N_GROUPS * GROUP_WIDTH
ATT_BLOCK = 128
N_BUCKETS = 32
MAX_DISTANCE = 2048
CONV_CH = 768
CONV_WIDTH = 31
CONV_HALO = 32
N_EXPERTS = 32
TOP_K = 4
D_FF = 1024
SWIGLU_LIMIT = 7.0
SWIGLU_ALPHA = 1.702
IN_WIDTH = 3 * QKV_WIDTH + 2 * CONV_CH + 2 * D_MODEL
LN_EPS = 1e-5
NEG_INF = -1e30
DEPTH = 1
DEEPNORM_ALPHA = (2 * DEPTH) ** 0.25

LANES = 128
EXPERT_BLOCK = 512
VMEM_LIMIT = 56 * 1024 * 1024

F32 = jnp.float32
BF16 = jnp.bfloat16


def _params(sem, vmem=VMEM_LIMIT):
    return pltpu.CompilerParams(dimension_semantics=sem, vmem_limit_bytes=vmem)


def _sigmoid(x):
    return 1.0 / (1.0 + jnp.exp(-x))


def _layer_norm(x, g, b):
    mu = jnp.mean(x, axis=-1, keepdims=True)
    xc = x - mu
    var = jnp.mean(xc * xc, axis=-1, keepdims=True)
    return xc * lax.rsqrt(var + LN_EPS) * g + b


def _in_proj_body(x_ref, w_ref, o_ref, xb_ref):
    @pl.when(pl.program_id(1) == 0)
    def _():
        xb_ref[...] = x_ref[...].astype(BF16)

    o_ref[...] = jnp.dot(xb_ref[...], w_ref[...], preferred_element_type=F32).astype(o_ref.dtype)


def in_proj(x2, w_in_b, tm=1024, tn=1024):
    T = x2.shape[0]
    tm = min(tm, T)
    return pl.pallas_call(
        _in_proj_body,
        grid=(T // tm, IN_WIDTH // tn),
        in_specs=[pl.BlockSpec((tm, D_MODEL), lambda m, n: (m, 0)),
                  pl.BlockSpec((D_MODEL, tn), lambda m, n: (0, n))],
        out_specs=pl.BlockSpec((tm, tn), lambda m, n: (m, n)),
        out_shape=jax.ShapeDtypeStruct((T, IN_WIDTH), BF16),
        scratch_shapes=[pltpu.VMEM((tm, D_MODEL), BF16)],
        compiler_params=_params(("arbitrary", "arbitrary")),
        name="in_proj",
    )(x2, w_in_b)


def _t5_bucket(dist):
    max_exact = N_BUCKETS // 2
    log_ratio = jnp.log(jnp.maximum(dist, max_exact).astype(F32) / max_exact) / math.log(MAX_DISTANCE / max_exact)
    large = jnp.minimum(max_exact + (log_ratio * (N_BUCKETS - max_exact)).astype(jnp.int32), N_BUCKETS - 1)
    return jnp.where(dist < max_exact, dist, large)


def _bias_body(tbl_ref, bucket_ref, band_ref, o_ref):
    g = pl.program_id(0)
    h = pl.program_id(1)
    col = g * HEADS_PER_GROUP + h
    bucket = bucket_ref[...]
    acc = jnp.zeros(bucket.shape, F32)
    for k in range(N_BUCKETS):
        acc = jnp.where(bucket == k, tbl_ref[k, col], acc)
    band = band_ref[...] != 0
    kj = lax.broadcasted_iota(jnp.int32, bucket.shape, 1)
    o_ref[0] = jnp.where(band, acc, NEG_INF)
    o_ref[1] = jnp.where(band & (kj >= ATT_BLOCK), acc, NEG_INF)


def bias_tables(rel_bias):
    qi = jnp.arange(ATT_BLOCK)[:, None]
    kj = jnp.arange(2 * ATT_BLOCK)[None, :]
    dist = qi - kj + ATT_BLOCK
    buckets, bands = [], []
    for window, dil in ATT_GROUPS:
        bands.append(((dist >= 0) & (dist <= window // dil)).astype(jnp.int32))
        buckets.append(_t5_bucket(jnp.maximum(dist, 0) * dil).astype(jnp.int32))
    buckets = jnp.stack(buckets)
    bands = jnp.stack(bands)
    blk = (None, ATT_BLOCK, 2 * ATT_BLOCK)
    return pl.pallas_call(
        _bias_body,
        grid=(N_GROUPS, HEADS_PER_GROUP),
        in_specs=[pl.BlockSpec(memory_space=pltpu.SMEM),
                  pl.BlockSpec(blk, lambda g, h: (g, 0, 0)),
                  pl.BlockSpec(blk, lambda g, h: (g, 0, 0))],
        out_specs=pl.BlockSpec((None, None, 2, ATT_BLOCK, 2 * ATT_BLOCK), lambda g, h: (g, h, 0, 0, 0)),
        out_shape=jax.ShapeDtypeStruct((N_GROUPS, HEADS_PER_GROUP, 2, ATT_BLOCK, 2 * ATT_BLOCK), F32),
        compiler_params=_params(("arbitrary", "arbitrary")),
        name="bias_tables",
    )(rel_bias.astype(F32), buckets, bands)


def _attn_body(q_ref, kc_ref, kp_ref, vc_ref, vp_ref, bm_ref, o_ref, st_ref, k_all, v_all, *, tq):
    i = pl.program_id(2)
    nsub = tq // ATT_BLOCK
    k_all[0:ATT_BLOCK] = kp_ref[...]
    k_all[ATT_BLOCK:] = kc_ref[...]
    v_all[0:ATT_BLOCK] = vp_ref[...]
    v_all[ATT_BLOCK:] = vc_ref[...]
    lane = lax.broadcasted_iota(jnp.int32, (ATT_BLOCK, LANES), 1)
    lo = lane < HEAD_DIM
    nt_dims = (((1,), (1,)), ((), ()))

    def sub_block(s, carry):
        r0 = pl.multiple_of(s * ATT_BLOCK, ATT_BLOCK)
        variant = jnp.where((i == 0) & (s == 0), 1, 0)
        st = jnp.zeros((ATT_BLOCK, LANES), F32)
        for j in range(HEADS_PER_GROUP // 2):
            cs = slice(j * LANES, (j + 1) * LANES)
            qp = q_ref[pl.ds(r0, ATT_BLOCK), cs] * jnp.asarray(HEAD_DIM ** -0.5, BF16)
            kp = k_all[pl.ds(r0, 2 * ATT_BLOCK), cs]
            vp = v_all[pl.ds(r0, 2 * ATT_BLOCK), cs]
            halves = []
            for hh in range(2):
                h = 2 * j + hh
                qh = jnp.where(lo if hh == 0 else ~lo, qp, jnp.zeros_like(qp))
                sc = lax.dot_general(qh, kp, nt_dims, preferred_element_type=F32) + bm_ref[h, variant]
                m = jnp.max(sc, axis=-1, keepdims=True)
                p = jnp.exp(sc - m)
                den = jnp.sum(p, axis=-1, keepdims=True)
                pv = jnp.dot(p.astype(BF16), vp, preferred_element_type=F32)
                halves.append(pv * (1.0 / den))
                st = jnp.where(lane == h, m, st)
                st = jnp.where(lane == HEADS_PER_GROUP + h, den, st)
            o_ref[pl.ds(r0, ATT_BLOCK), cs] = jnp.where(lo, halves[0], halves[1])
        st_ref[pl.ds(r0, ATT_BLOCK), :] = st
        return carry

    lax.fori_loop(0, nsub, sub_block, 0)


def attention_group(proj3, bm_g, gi, tq=512):
    B, S, _ = proj3.shape
    _, dil = ATT_GROUPS[gi]
    L = S // dil
    tq = min(tq, L)
    nblk = IN_WIDTH // GROUP_WIDTH
    pv = proj3.reshape(B, L, dil * IN_WIDTH)
    sub = tq // ATT_BLOCK
    qcol, kcol, vcol = gi, N_GROUPS + gi, 2 * N_GROUPS + gi

    def cur(col):
        return pl.BlockSpec((None, tq, GROUP_WIDTH), lambda b, r, i: (b, i, r * nblk + col))

    def prev(col):
        return pl.BlockSpec((None, ATT_BLOCK, GROUP_WIDTH),
                            lambda b, r, i: (b, jnp.maximum(i * sub - 1, 0), r * nblk + col))

    o, st = pl.pallas_call(
        functools.partial(_attn_body, tq=tq),
        grid=(B, dil, L // tq),
        in_specs=[cur(qcol), cur(kcol), prev(kcol), cur(vcol), prev(vcol),
                  pl.BlockSpec((HEADS_PER_GROUP, 2, ATT_BLOCK, 2 * ATT_BLOCK), lambda b, r, i: (0, 0, 0, 0))],
        out_specs=[pl.BlockSpec((None, tq, GROUP_WIDTH), lambda b, r, i: (b, i, r)),
                   pl.BlockSpec((None, tq, LANES), lambda b, r, i: (b, i, r))],
        out_shape=[jax.ShapeDtypeStruct((B, L, dil * GROUP_WIDTH), F32),
                   jax.ShapeDtypeStruct((B, L, dil * LANES), F32)],
        scratch_shapes=[pltpu.VMEM((ATT_BLOCK + tq, GROUP_WIDTH), BF16),
                        pltpu.VMEM((ATT_BLOCK + tq, GROUP_WIDTH), BF16)],
        compiler_params=_params(("arbitrary", "arbitrary", "arbitrary")),
        name=f"attention_g{gi}",
    )(pv, pv, pv, pv, pv, bm_g)
    return o.reshape(B, S, GROUP_WIDTH), st.reshape(B, S, LANES)


def _split_bf16(x):
    hi = x.astype(BF16)
    lo = (x - hi.astype(F32)).astype(BF16)
    return hi, lo


def _mixer_out_body(o0, o1, o2, s0, s1, s2, uv_ref, ug_ref, uvh_ref, ugh_ref, ga_ref, gc_ref, x_ref,
                    wdw_ref, bdw_ref, cg_ref, cb_ref, woa_ref, woc_ref, wout_ref, g1_ref, b1_ref,
                    h_ref, glu_ref, dw_ref, *, tm, chunk):
    i = pl.program_id(1)

    sts = [s0[...], s1[...], s2[...]]
    mx = jnp.maximum(jnp.maximum(sts[0], sts[1]), sts[2])
    wts = [pltpu.roll(st, LANES - HEADS_PER_GROUP, axis=1) * jnp.exp(st - mx) for st in sts]
    wsum = wts[0] + wts[1] + wts[2]
    row = lax.broadcasted_iota(jnp.int32, (LANES, GROUP_WIDTH), 0)
    colh = lax.broadcasted_iota(jnp.int32, (LANES, GROUP_WIDTH), 1) // HEAD_DIM
    expand = (row == colh).astype(BF16)
    attn = jnp.zeros((tm, GROUP_WIDTH), F32)
    head_lane = lax.broadcasted_iota(jnp.int32, (tm, LANES), 1) < HEADS_PER_GROUP
    for wt, o_ref in zip(wts, (o0, o1, o2)):
        c_hi, c_lo = _split_bf16(jnp.where(head_lane, wt / wsum, 0.0))
        c = (jnp.dot(c_hi, expand, preferred_element_type=F32)
             + jnp.dot(c_lo, expand, preferred_element_type=F32))
        attn = attn + c * o_ref[...]
    a_out = jnp.dot(attn.astype(BF16), woa_ref[...], preferred_element_type=F32)

    gh = uvh_ref[...].astype(F32) * _sigmoid(ugh_ref[...].astype(F32))
    glu_ref[0:CONV_HALO] = jnp.where(i == 0, 0.0, gh)
    glu_ref[CONV_HALO:] = uv_ref[...].astype(F32) * _sigmoid(ug_ref[...].astype(F32))
    first_tap = CONV_HALO - (CONV_WIDTH - 1)

    for r0 in range(0, tm, chunk):
        acc = jnp.broadcast_to(bdw_ref[...], (chunk, CONV_CH))
        for j in range(CONV_WIDTH):
            acc = acc + wdw_ref[j:j + 1, :] * glu_ref[r0 + first_tap + j:r0 + first_tap + j + chunk, :]
        dw_ref[r0:r0 + chunk, :] = acc
    cn = _layer_norm(dw_ref[...], cg_ref[...], cb_ref[...])
    conv = cn * _sigmoid(cn)
    c_out = jnp.dot(conv.astype(BF16), woc_ref[...], preferred_element_type=F32)

    merged = (_sigmoid(ga_ref[...].astype(F32)) * a_out + _sigmoid(gc_ref[...].astype(F32)) * c_out)
    mix = jnp.dot(merged.astype(BF16), wout_ref[...], preferred_element_type=F32)
    h_ref[...] = _layer_norm(DEEPNORM_ALPHA * x_ref[...] + mix, g1_ref[...], b1_ref[...])


def mixer_out(o_list, st_list, proj3, x3, w_dw, b_dw, cg, cb, woa_b, woc_b, wout_b, g1, b1, tm=256, chunk=32):
    B, S, _ = x3.shape
    tm = min(tm, S)
    halo_blocks = tm // CONV_HALO
    uv_col = (3 * QKV_WIDTH) // CONV_CH
    ga_col = (3 * QKV_WIDTH + 2 * CONV_CH) // D_MODEL

    def tile(width, col=0):
        return pl.BlockSpec((None, tm, width), lambda b, i: (b, i, col))

    def halo(col):
        return pl.BlockSpec((None, CONV_HALO, CONV_CH), lambda b, i: (b, jnp.maximum(i * halo_blocks - 1, 0), col))

    def whole(shape):
        return pl.BlockSpec(shape, lambda b, i: (0,) * len(shape))

    return pl.pallas_call(
        functools.partial(_mixer_out_body, tm=tm, chunk=chunk),
        grid=(B, S // tm),
        in_specs=[tile(GROUP_WIDTH)] * 3 + [tile(LANES)] * 3
        + [tile(CONV_CH, uv_col), tile(CONV_CH, uv_col + 1), halo(uv_col), halo(uv_col + 1),
           tile(D_MODEL, ga_col), tile(D_MODEL, ga_col + 1), tile(D_MODEL),
           whole((CONV_WIDTH, CONV_CH)), whole((1, CONV_CH)), whole((1, CONV_CH)), whole((1, CONV_CH)),
           whole((GROUP_WIDTH, D_MODEL)), whole((CONV_CH, D_MODEL)), whole((D_MODEL, D_MODEL)),
           whole((1, D_MODEL)), whole((1, D_MODEL))],
        out_specs=tile(D_MODEL),
        out_shape=jax.ShapeDtypeStruct((B, S, D_MODEL), F32),
        scratch_shapes=[pltpu.VMEM((CONV_HALO + tm, CONV_CH), F32), pltpu.VMEM((tm, CONV_CH), F32)],
        compiler_params=_params(("arbitrary", "arbitrary")),
        name="mixer_out",
    )(*o_list, *st_list, proj3, proj3, proj3, proj3, proj3, proj3, x3,
      w_dw, b_dw, cg, cb, woa_b, woc_b, wout_b, g1, b1)


def _router_body(h_ref, w_ref, b_ref, gates_ref, ids_ref, rank_ref, cnt_ref, carry_ref, *, tm):
    step = pl.program_id(0)

    @pl.when(step == 0)
    def _():
        carry_ref[...] = jnp.zeros_like(carry_ref)

    h_hi, h_lo = _split_bf16(h_ref[...])
    w_hi, w_lo = _split_bf16(w_ref[...])
    logits = (jnp.dot(h_hi, w_hi, preferred_element_type=F32)
              + (jnp.dot(h_lo, w_hi, preferred_element_type=F32) + jnp.dot(h_hi, w_lo, preferred_element_type=F32))
              + b_ref[...])
    lane = lax.broadcasted_iota(jnp.int32, (tm, LANES), 1)
    work = jnp.where(lane < N_EXPERTS, logits, -jnp.inf)

    vals, hots = [], []
    ids = jnp.zeros((tm, LANES), jnp.int32)
    for k in range(TOP_K):
        v = jnp.max(work, axis=-1, keepdims=True)
        idx = jnp.min(jnp.where(work == v, lane, LANES), axis=-1, keepdims=True)
        hot = lane == idx
        work = jnp.where(hot, -jnp.inf, work)
        ids = jnp.where(lane == k, idx, ids)
        vals.append(v)
        hots.append(hot)

    es = [jnp.exp(v - vals[0]) for v in vals]
    esum = es[0] + es[1] + es[2] + es[3]
    gates = jnp.zeros((tm, LANES), F32)
    for k in range(TOP_K):
        gates = jnp.where(lane == k, es[k] / esum, gates)

    sel = (hots[0] | hots[1] | hots[2] | hots[3])
    r_i = lax.broadcasted_iota(jnp.int32, (tm, tm), 0)
    c_i = lax.broadcasted_iota(jnp.int32, (tm, tm), 1)
    tri = (c_i < r_i).astype(BF16)
    before = jnp.dot(tri, sel.astype(BF16), preferred_element_type=F32) + carry_ref[...]
    rank = jnp.zeros((tm, LANES), jnp.int32)
    for k in range(TOP_K):
        rk = jnp.sum(jnp.where(hots[k], before, 0.0), axis=-1, keepdims=True)
        rank = jnp.where(lane == k, rk.astype(jnp.int32), rank)
    carry_ref[...] = carry_ref[...] + jnp.sum(sel.astype(F32), axis=0, keepdims=True)

    gates_ref[...] = gates
    ids_ref[...] = ids
    rank_ref[...] = rank
    cnt_ref[...] = carry_ref[...].astype(jnp.int32)


def router(h2, w_router, b_router, tm=512):
    T = h2.shape[0]
    tm = min(tm, T)
    w_pad = jnp.pad(w_router.astype(F32), ((0, 0), (0, LANES - N_EXPERTS)))
    b_pad = jnp.pad(b_router.astype(F32), (0, LANES - N_EXPERTS)).reshape(1, LANES)
    tile = pl.BlockSpec((tm, LANES), lambda i: (i, 0))
    return pl.pallas_call(
        functools.partial(_router_body, tm=tm),
        grid=(T // tm,),
        in_specs=[pl.BlockSpec((tm, D_MODEL), lambda i: (i, 0)),
                  pl.BlockSpec((D_MODEL, LANES), lambda i: (0, 0)),
                  pl.BlockSpec((1, LANES), lambda i: (0, 0))],
        out_specs=[tile, tile, tile, pl.BlockSpec((1, LANES), lambda i: (0, 0))],
        out_shape=[jax.ShapeDtypeStruct((T, LANES), F32), jax.ShapeDtypeStruct((T, LANES), jnp.int32),
                   jax.ShapeDtypeStruct((T, LANES), jnp.int32), jax.ShapeDtypeStruct((1, LANES), jnp.int32)],
        scratch_shapes=[pltpu.VMEM((1, LANES), F32)],
        compiler_params=_params(("arbitrary",)),
        name="router",
    )(h2, w_pad, b_pad)


def _row_copy(src, dst, s_row, d_row, sem):
    return pltpu.make_async_copy(src.at[pl.ds(s_row, 1), :], dst.at[pl.ds(d_row, 1), :], sem)


def _dispatch_body(pos_ref, h_ref, xs_in, xs_ref, sem, *, tm):
    del xs_in

    def issue(t, c):
        for k in range(TOP_K):
            _row_copy(h_ref, xs_ref, t, pos_ref[k, t], sem).start()
        return c

    lax.fori_loop(0, tm, issue, 0)

    def drain(t, c):
        for k in range(TOP_K):
            _row_copy(h_ref, xs_ref, 0, 0, sem).wait()
        return c

    lax.fori_loop(0, tm, drain, 0)


def dispatch(h2, pos_t, n_rows, tm=512):
    T = h2.shape[0]
    tm = min(tm, T)
    xs0 = jnp.zeros((n_rows, D_MODEL), F32)
    return pl.pallas_call(
        functools.partial(_dispatch_body, tm=tm),
        grid=(T // tm,),
        in_specs=[pl.BlockSpec((TOP_K, tm), lambda i: (0, i), memory_space=pltpu.SMEM),
                  pl.BlockSpec((tm, D_MODEL), lambda i: (i, 0)),
                  pl.BlockSpec(memory_space=pl.ANY)],
        out_specs=pl.BlockSpec(memory_space=pl.ANY),
        out_shape=jax.ShapeDtypeStruct((n_rows, D_MODEL), F32),
        scratch_shapes=[pltpu.SemaphoreType.DMA],
        input_output_aliases={2: 0},
        compiler_params=_params(("arbitrary",)),
        name="dispatch",
    )(pos_t, h2, xs0)


def _combine_body(pos_ref, gates_ref, h_ref, g2_ref, b2_ref, ys_ref, o_ref, buf, sem, *, tm):
    def issue(t, c):
        for k in range(TOP_K):
            _row_copy(ys_ref, buf.at[k], pos_ref[k, t], t, sem).start()
        return c

    lax.fori_loop(0, tm, issue, 0)

    def drain(t, c):
        for k in range(TOP_K):
            _row_copy(ys_ref, buf.at[k], 0, 0, sem).wait()
        return c

    lax.fori_loop(0, tm, drain, 0)

    gates = gates_ref[...]
    ffn = gates[:, 0:1] * buf[0]
    for k in range(1, TOP_K):
        ffn = ffn + gates[:, k:k + 1] * buf[k]
    o_ref[...] = _layer_norm(DEEPNORM_ALPHA * h_ref[...] + ffn, g2_ref[...], b2_ref[...])


def combine(ys, pos_t, gates_pad, h2, g2, b2, tm=256):
    T = h2.shape[0]
    tm = min(tm, T)
    return pl.pallas_call(
        functools.partial(_combine_body, tm=tm),
        grid=(T // tm,),
        in_specs=[pl.BlockSpec((TOP_K, tm), lambda i: (0, i), memory_space=pltpu.SMEM),
                  pl.BlockSpec((tm, LANES), lambda i: (i, 0)),
                  pl.BlockSpec((tm, D_MODEL), lambda i: (i, 0)),
                  pl.BlockSpec((1, D_MODEL), lambda i: (0, 0)),
                  pl.BlockSpec((1, D_MODEL), lambda i: (0, 0)),
                  pl.BlockSpec(memory_space=pl.ANY)],
        out_specs=pl.BlockSpec((tm, D_MODEL), lambda i: (i, 0)),
        out_shape=jax.ShapeDtypeStruct((T, D_MODEL), F32),
        scratch_shapes=[pltpu.VMEM((TOP_K, tm, D_MODEL), F32), pltpu.SemaphoreType.DMA],
        compiler_params=_params(("arbitrary",)),
        name="combine",
    )(pos_t, gates_pad, h2, g2, b2, ys)


def _experts_body(blk_e, blk_src, n_used, xs_ref, wgu_ref, bgu_ref, wdn_ref, bdn_ref, ys_ref):
    del blk_e, blk_src

    @pl.when(pl.program_id(0) < n_used[0])
    def _():
        hgu = jnp.dot(xs_ref[...].astype(BF16), wgu_ref[...], preferred_element_type=F32) + bgu_ref[...]
        gate = jnp.minimum(hgu[:, :D_FF], SWIGLU_LIMIT)
        up = jnp.clip(hgu[:, D_FF:], -SWIGLU_LIMIT, SWIGLU_LIMIT)
        act = (up + 1.0) * gate * _sigmoid(SWIGLU_ALPHA * gate)
        ys_ref[...] = jnp.dot(act.astype(BF16), wdn_ref[...], preferred_element_type=F32) + bdn_ref[...]


def experts(xs, blk_e, blk_src, n_used, wgu_b, bgu, wdn_b, bdn):
    n_rows = xs.shape[0]
    nb = n_rows // EXPERT_BLOCK
    grid_spec = pltpu.PrefetchScalarGridSpec(
        num_scalar_prefetch=3,
        grid=(nb,),
        in_specs=[pl.BlockSpec((EXPERT_BLOCK, D_MODEL), lambda i, e, s, n: (s[i], 0)),
                  pl.BlockSpec((None, D_MODEL, 2 * D_FF), lambda i, e, s, n: (e[i], 0, 0)),
                  pl.BlockSpec((None, 1, 2 * D_FF), lambda i, e, s, n: (e[i], 0, 0)),
                  pl.BlockSpec((None, D_FF, D_MODEL), lambda i, e, s, n: (e[i], 0, 0)),
                  pl.BlockSpec((None, 1, D_MODEL), lambda i, e, s, n: (e[i], 0, 0))],
        out_specs=pl.BlockSpec((EXPERT_BLOCK, D_MODEL), lambda i, e, s, n: (s[i], 0)),
    )
    return pl.pallas_call(
        _experts_body,
        grid_spec=grid_spec,
        out_shape=jax.ShapeDtypeStruct((n_rows, D_MODEL), F32),
        compiler_params=_params(("arbitrary",)),
        name="experts",
    )(blk_e, blk_src, n_used, xs, wgu_b, bgu, wdn_b, bdn)


def routing_layout(ids, rank, counts, n_blocks):
    padded = (counts + EXPERT_BLOCK - 1) // EXPERT_BLOCK * EXPERT_BLOCK
    pad_ends = jnp.cumsum(padded)
    pad_starts = pad_ends - padded
    pos_t = (pad_starts[ids] + rank).T.astype(jnp.int32)
    n_used = jnp.maximum(pad_ends[-1] // EXPERT_BLOCK, 1).astype(jnp.int32)
    blk = jnp.minimum(jnp.arange(n_blocks, dtype=jnp.int32), n_used - 1)
    blk_e = jnp.minimum(jnp.searchsorted(pad_ends, blk * EXPERT_BLOCK, side='right'), N_EXPERTS - 1).astype(jnp.int32)
    return pos_t, blk_e, blk, n_used.reshape(1)


def kernel(x, w_in, rel_bias, w_dw, b_dw, conv_ln_g, conv_ln_b, w_o_attn, w_o_conv, w_out, ln1_g, ln1_b,
           w_router, b_router, w_gate_up, b_gate_up, w_down, b_down, ln2_g, ln2_b):
    B, S, D = x.shape
    T = B * S
    h = x
    bm = bias_tables(rel_bias)
    for l in range(DEPTH):
        proj = in_proj(h.reshape(T, D), w_in[l].astype(BF16)).reshape(B, S, IN_WIDTH)
        o_list, st_list = [], []
        for gi in range(N_GROUPS):
            o_g, st_g = attention_group(proj, bm[gi], gi)
            o_list.append(o_g)
            st_list.append(st_g)
        h1 = mixer_out(o_list, st_list, proj, h,
                       w_dw[l].reshape(CONV_WIDTH, CONV_CH), b_dw[l].reshape(1, CONV_CH),
                       conv_ln_g[l].reshape(1, CONV_CH), conv_ln_b[l].reshape(1, CONV_CH),
                       w_o_attn[l].astype(BF16), w_o_conv[l].astype(BF16), w_out[l].astype(BF16),
                       ln1_g[l].reshape(1, D), ln1_b[l].reshape(1, D))
        h2 = h1.reshape(T, D)
        gates_pad, ids_pad, rank_pad, counts = router(h2, w_router[l], b_router[l])
        n_rows = T * TOP_K + N_EXPERTS * EXPERT_BLOCK
        pos_t, blk_e, blk_src, n_used = routing_layout(ids_pad[:, :TOP_K], rank_pad[:, :TOP_K],
                                                       counts[0, :N_EXPERTS], n_rows // EXPERT_BLOCK)
        xs = dispatch(h2, pos_t, n_rows)
        ys = experts(xs, blk_e, blk_src, n_used,
                     w_gate_up[l].astype(BF16), b_gate_up[l].reshape(N_EXPERTS, 1, 2 * D_FF),
                     w_down[l].astype(BF16), b_down[l].reshape(N_EXPERTS, 1, D))
        out = combine(ys, pos_t, gates_pad, h2, ln2_g[l].reshape(1, D), ln2_b[l].reshape(1, D))
        h = out.reshape(B, S, D)
    return h
```

```python
import functools
import math

import jax
import jax.numpy as jnp
from jax import lax
from jax.experimental import pallas as pl
from jax.experimental.pallas import tpu as pltpu

D_MODEL = 1024
ATT_GROUPS = ((128, 1), (512, 4), (2048, 16))
N_GROUPS = len(ATT_GROUPS)
HEADS_PER_GROUP = 8
HEAD_DIM = 64
GROUP_WIDTH = HEADS_PER_GROUP * HEAD_DIM
QKV_WIDTH = N_GROUPS * GROUP_WIDTH
ATT_BLOCK = 128
N_BUCKETS = 32
MAX_DISTANCE = 2048
CONV_CH = 768
CONV_WIDTH = 31
CONV_HALO = 32
N_EXPERTS = 32
TOP_K = 4
D_FF = 1024
SWIGLU_LIMIT = 7.0
SWIGLU_ALPHA = 1.702
IN_WIDTH = 3 * QKV_WIDTH + 2 * CONV_CH + 2 * D_MODEL
LN_EPS = 1e-5
NEG_INF = -1e30
DEPTH = 1
DEEPNORM_ALPHA = (2 * DEPTH) ** 0.25

LANES = 128
SUBLANES = 8
EXPERT_BLOCK = 512
VMEM_LIMIT = 56 * 1024 * 1024

F32 = jnp.float32
BF16 = jnp.bfloat16


def _params(sem, vmem=VMEM_LIMIT):
    return pltpu.CompilerParams(dimension_semantics=sem, vmem_limit_bytes=vmem)


def _sigmoid(x):
    return 1.0 / (1.0 + jnp.exp(-x))


def _layer_norm(x, g, b):
    mu = jnp.mean(x, axis=-1, keepdims=True)
    xc = x - mu
    var = jnp.mean(xc * xc, axis=-1, keepdims=True)
    return xc * lax.rsqrt(var + LN_EPS) * g + b


def _in_proj_body(x_ref, w_ref, o_ref):
    o_ref[...] = jnp.dot(x_ref[...], w_ref[...], preferred_element_type=F32).astype(o_ref.dtype)


def in_proj(xb, w_b, tn, name, tm=1024):
    T = xb.shape[0]
    N = w_b.shape[1]
    tm = min(tm, T)
    return pl.pallas_call(
        _in_proj_body,
        grid=(T // tm, N // tn),
        in_specs=[pl.BlockSpec((tm, D_MODEL), lambda m, n: (m, 0)),
                  pl.BlockSpec((D_MODEL, tn), lambda m, n: (0, n))],
        out_specs=pl.BlockSpec((tm, tn), lambda m, n: (m, n)),
        out_shape=jax.ShapeDtypeStruct((T, N), BF16),
        compiler_params=_params(("arbitrary", "arbitrary")),
        name=name,
    )(xb, w_b)


def _t5_bucket(dist):
    max_exact = N_BUCKETS // 2
    log_ratio = jnp.log(jnp.maximum(dist, max_exact).astype(F32) / max_exact) / math.log(MAX_DISTANCE / max_exact)
    large = jnp.minimum(max_exact + (log_ratio * (N_BUCKETS - max_exact)).astype(jnp.int32), N_BUCKETS - 1)
    return jnp.where(dist < max_exact, dist, large)


def _bias_body(tbl_ref, bucket_ref, band_ref, o_ref):
    g = pl.program_id(0)
    h = pl.program_id(1)
    col = g * HEADS_PER_GROUP + h
    bucket = bucket_ref[...]
    acc = jnp.zeros(bucket.shape, F32)
    for k in range(N_BUCKETS):
        acc = jnp.where(bucket == k, tbl_ref[k, col], acc)
    band = band_ref[...] != 0
    kj = lax.broadcasted_iota(jnp.int32, bucket.shape, 1)
    o_ref[0] = jnp.where(band, acc, NEG_INF)
    o_ref[1] = jnp.where(band & (kj >= ATT_BLOCK), acc, NEG_INF)


def bias_tables(rel_bias):
    qi = jnp.arange(ATT_BLOCK)[:, None]
    kj = jnp.arange(2 * ATT_BLOCK)[None, :]
    dist = qi - kj + ATT_BLOCK
    buckets, bands = [], []
    for window, dil in ATT_GROUPS:
        bands.append(((dist >= 0) & (dist <= window // dil)).astype(jnp.int32))
        buckets.append(_t5_bucket(jnp.maximum(dist, 0) * dil).astype(jnp.int32))
    buckets = jnp.stack(buckets)
    bands = jnp.stack(bands)
    blk = (None, ATT_BLOCK, 2 * ATT_BLOCK)
    return pl.pallas_call(
        _bias_body,
        grid=(N_GROUPS, HEADS_PER_GROUP),
        in_specs=[pl.BlockSpec(memory_space=pltpu.SMEM),
                  pl.BlockSpec(blk, lambda g, h: (g, 0, 0)),
                  pl.BlockSpec(blk, lambda g, h: (g, 0, 0))],
        out_specs=pl.BlockSpec((None, None, 2, ATT_BLOCK, 2 * ATT_BLOCK), lambda g, h: (g, h, 0, 0, 0)),
        out_shape=jax.ShapeDtypeStruct((N_GROUPS, HEADS_PER_GROUP, 2, ATT_BLOCK, 2 * ATT_BLOCK), F32),
        compiler_params=_params(("arbitrary", "arbitrary")),
        name="bias_tables",
    )(rel_bias.astype(F32), buckets, bands)


def _attn_body(q_ref, kc_ref, kp_ref, vc_ref, vp_ref, bm_ref, o_ref, st_ref, k_all, v_all, *, tq):
    i = pl.program_id(2)
    nsub = tq // ATT_BLOCK
    k_all[0:ATT_BLOCK] = kp_ref[...]
    k_all[ATT_BLOCK:] = kc_ref[...]
    v_all[0:ATT_BLOCK] = vp_ref[...]
    v_all[ATT_BLOCK:] = vc_ref[...]
    lane = lax.broadcasted_iota(jnp.int32, (ATT_BLOCK, LANES), 1)
    lo = lane < HEAD_DIM
    nt_dims = (((1,), (1,)), ((), ()))

    def sub_block(s, carry):
        r0 = pl.multiple_of(s * ATT_BLOCK, ATT_BLOCK)
        variant = jnp.where((i == 0) & (s == 0), 1, 0)
        st = jnp.zeros((ATT_BLOCK, LANES), F32)
        for j in range(HEADS_PER_GROUP // 2):
            cs = slice(j * LANES, (j + 1) * LANES)
            qp = q_ref[pl.ds(r0, ATT_BLOCK), cs] * jnp.asarray(HEAD_DIM ** -0.5, BF16)
            kp = k_all[pl.ds(r0, 2 * ATT_BLOCK), cs]
            vp = v_all[pl.ds(r0, 2 * ATT_BLOCK), cs]
            halves = []
            for hh in range(2):
                h = 2 * j + hh
                qh = jnp.where(lo if hh == 0 else ~lo, qp, jnp.zeros_like(qp))
                sc = lax.dot_general(qh, kp, nt_dims, preferred_element_type=F32) + bm_ref[h, variant]
                m = jnp.max(sc, axis=-1, keepdims=True)
                p = jnp.exp(sc - m)
                den = jnp.sum(p, axis=-1, keepdims=True)
                pv = jnp.dot(p.astype(BF16), vp, preferred_element_type=F32)
                halves.append(pv * (1.0 / den))
                st = jnp.where(lane == h, m, st)
                st = jnp.where(lane == HEADS_PER_GROUP + h, den, st)
            o_ref[pl.ds(r0, ATT_BLOCK), cs] = jnp.where(lo, halves[0], halves[1])
        st_ref[pl.ds(r0, ATT_BLOCK), :] = st
        return carry

    lax.fori_loop(0, nsub, sub_block, 0)


def attention_group(qkv, bm_g, gi, tq=512):
    B, dil, L, _ = qkv.shape
    tq = min(tq, L)
    sub = tq // ATT_BLOCK

    def cur(col):
        return pl.BlockSpec((None, None, tq, GROUP_WIDTH), lambda b, r, i: (b, r, i, col))

    def prev(col):
        return pl.BlockSpec((None, None, ATT_BLOCK, GROUP_WIDTH),
                            lambda b, r, i: (b, r, jnp.maximum(i * sub - 1, 0), col))

    return pl.pallas_call(
        functools.partial(_attn_body, tq=tq),
        grid=(B, dil, L // tq),
        in_specs=[cur(0), cur(1), prev(1), cur(2), prev(2),
                  pl.BlockSpec((HEADS_PER_GROUP, 2, ATT_BLOCK, 2 * ATT_BLOCK), lambda b, r, i: (0, 0, 0, 0))],
        out_specs=[pl.BlockSpec((None, None, tq, GROUP_WIDTH), lambda b, r, i: (b, r, i, 0)),
                   pl.BlockSpec((None, None, tq, LANES), lambda b, r, i: (b, r, i, 0))],
        out_shape=[jax.ShapeDtypeStruct((B, dil, L, GROUP_WIDTH), F32),
                   jax.ShapeDtypeStruct((B, dil, L, LANES), F32)],
        scratch_shapes=[pltpu.VMEM((ATT_BLOCK + tq, GROUP_WIDTH), BF16),
                        pltpu.VMEM((ATT_BLOCK + tq, GROUP_WIDTH), BF16)],
        compiler_params=_params(("arbitrary", "arbitrary", "arbitrary")),
        name=f"attention_g{gi}",
    )(qkv, qkv, qkv, qkv, qkv, bm_g)


def _split_bf16(x):
    hi = x.astype(BF16)
    lo = (x - hi.astype(F32)).astype(BF16)
    return hi, lo


def _to_token_order(blk_ref, tok_ref, dil, tm):
    n = tm // dil
    for r in range(dil):
        for c in range(tok_ref.shape[0]):
            tok_ref[c, pl.ds(r, n, stride=dil), :] = blk_ref[r, :, c * LANES:(c + 1) * LANES]


def _mixer_out_body(o0, o1, o2, s0, s1, s2, uv_ref, ug_ref, uvh_ref, ugh_ref, ga_ref, gc_ref, x_ref,
                    wdw_ref, bdw_ref, cg_ref, cb_ref, woa_ref, woc_ref, wout_ref, g1_ref, b1_ref,
                    h_ref, glu_ref, dw_ref, tok_o1, tok_s1, tok_o2, tok_s2, *, tm, chunk):
    i = pl.program_id(1)

    ncol = GROUP_WIDTH // LANES
    _to_token_order(o1, tok_o1, ATT_GROUPS[1][1], tm)
    _to_token_order(s1, tok_s1, ATT_GROUPS[1][1], tm)
    _to_token_order(o2, tok_o2, ATT_GROUPS[2][1], tm)
    _to_token_order(s2, tok_s2, ATT_GROUPS[2][1], tm)
    outs = [o0[0],
            jnp.concatenate([tok_o1[c] for c in range(ncol)], axis=1),
            jnp.concatenate([tok_o2[c] for c in range(ncol)], axis=1)]
    sts = [s0[0], tok_s1[0], tok_s2[0]]

    mx = jnp.maximum(jnp.maximum(sts[0], sts[1]), sts[2])
    wts = [pltpu.roll(st, LANES - HEADS_PER_GROUP, axis=1) * jnp.exp(st - mx) for st in sts]
    wsum = wts[0] + wts[1] + wts[2]
    row = lax.broadcasted_iota(jnp.int32, (LANES, GROUP_WIDTH), 0)
    colh = lax.broadcasted_iota(jnp.int32, (LANES, GROUP_WIDTH), 1) // HEAD_DIM
    expand = (row == colh).astype(BF16)
    attn = jnp.zeros((tm, GROUP_WIDTH), F32)
    head_lane = lax.broadcasted_iota(jnp.int32, (tm, LANES), 1) < HEADS_PER_GROUP
    for wt, o in zip(wts, outs):
        c_hi, c_lo = _split_bf16(jnp.where(head_lane, wt / wsum, 0.0))
        c = (jnp.dot(c_hi, expand, preferred_element_type=F32)
             + jnp.dot(c_lo, expand, preferred_element_type=F32))
        attn = attn + c * o
    a_out = jnp.dot(attn.astype(BF16), woa_ref[...], preferred_element_type=F32)

    gh = uvh_ref[...].astype(F32) * _sigmoid(ugh_ref[...].astype(F32))
    glu_ref[0:CONV_HALO] = jnp.where(i == 0, 0.0, gh)
    glu_ref[CONV_HALO:] = uv_ref[...].astype(F32) * _sigmoid(ug_ref[...].astype(F32))
    first_tap = CONV_HALO - (CONV_WIDTH - 1)

    for r0 in range(0, tm, chunk):
        acc = jnp.broadcast_to(bdw_ref[...], (chunk, CONV_CH))
        for j in range(CONV_WIDTH):
            acc = acc + wdw_ref[j:j + 1, :] * glu_ref[r0 + first_tap + j:r0 + first_tap + j + chunk, :]
        dw_ref[r0:r0 + chunk, :] = acc
    cn = _layer_norm(dw_ref[...], cg_ref[...], cb_ref[...])
    conv = cn * _sigmoid(cn)
    c_out = jnp.dot(conv.astype(BF16), woc_ref[...], preferred_element_type=F32)

    merged = (_sigmoid(ga_ref[...].astype(F32)) * a_out + _sigmoid(gc_ref[...].astype(F32)) * c_out)
    mix = jnp.dot(merged.astype(BF16), wout_ref[...], preferred_element_type=F32)
    h_ref[...] = _layer_norm(DEEPNORM_ALPHA * x_ref[...] + mix, g1_ref[...], b1_ref[...])


def mixer_out(o_list, st_list, proj3, x3, w_dw, b_dw, cg, cb, woa_b, woc_b, wout_b, g1, b1, tm=256, chunk=32):
    B, S, _ = x3.shape
    tm = min(tm, S)
    halo_blocks = tm // CONV_HALO
    uv_col = QKV_WIDTH // CONV_CH
    ga_col = (QKV_WIDTH + 2 * CONV_CH) // D_MODEL
    ncol = GROUP_WIDTH // LANES

    def tile(width, col=0):
        return pl.BlockSpec((None, tm, width), lambda b, i: (b, i, col))

    def dilated(gi, width):
        dil = ATT_GROUPS[gi][1]
        return pl.BlockSpec((None, dil, tm // dil, width), lambda b, i: (b, 0, i, 0))

    def halo(col):
        return pl.BlockSpec((None, CONV_HALO, CONV_CH), lambda b, i: (b, jnp.maximum(i * halo_blocks - 1, 0), col))

    def whole(shape):
        return pl.BlockSpec(shape, lambda b, i: (0,) * len(shape))

    return pl.pallas_call(
        functools.partial(_mixer_out_body, tm=tm, chunk=chunk),
        grid=(B, S // tm),
        in_specs=[dilated(g, GROUP_WIDTH) for g in range(N_GROUPS)] + [dilated(g, LANES) for g in range(N_GROUPS)]
        + [tile(CONV_CH, uv_col), tile(CONV_CH, uv_col + 1), halo(uv_col), halo(uv_col + 1),
           tile(D_MODEL, ga_col), tile(D_MODEL, ga_col + 1), tile(D_MODEL),
           whole((CONV_WIDTH, CONV_CH)), whole((1, CONV_CH)), whole((1, CONV_CH)), whole((1, CONV_CH)),
           whole((GROUP_WIDTH, D_MODEL)), whole((CONV_CH, D_MODEL)), whole((D_MODEL, D_MODEL)),
           whole((1, D_MODEL)), whole((1, D_MODEL))],
        out_specs=tile(D_MODEL),
        out_shape=jax.ShapeDtypeStruct((B, S, D_MODEL), F32),
        scratch_shapes=[pltpu.VMEM((CONV_HALO + tm, CONV_CH), F32), pltpu.VMEM((tm, CONV_CH), F32),
                        pltpu.VMEM((ncol, tm, LANES), F32), pltpu.VMEM((1, tm, LANES), F32),
                        pltpu.VMEM((ncol, tm, LANES), F32), pltpu.VMEM((1, tm, LANES), F32)],
        compiler_params=_params(("arbitrary", "arbitrary")),
        name="mixer_out",
    )(*o_list, *st_list, proj3, proj3, proj3, proj3, proj3, proj3, x3,
      w_dw, b_dw, cg, cb, woa_b, woc_b, wout_b, g1, b1)


def _router_body(h_ref, w_ref, b_ref, gates_ref, ids_ref, rank_ref, cnt_ref, carry_ref, *, tm):
    step = pl.program_id(0)

    @pl.when(step == 0)
    def _():
        carry_ref[...] = jnp.zeros_like(carry_ref)

    h_hi, h_lo = _split_bf16(h_ref[...])
    w_hi, w_lo = _split_bf16(w_ref[...])
    logits = (jnp.dot(h_hi, w_hi, preferred_element_type=F32)
              + (jnp.dot(h_lo, w_hi, preferred_element_type=F32) + jnp.dot(h_hi, w_lo, preferred_element_type=F32))
              + b_ref[...])
    lane = lax.broadcasted_iota(jnp.int32, (tm, LANES), 1)
    work = jnp.where(lane < N_EXPERTS, logits, -jnp.inf)

    vals, hots = [], []
    ids = jnp.zeros((tm, LANES), jnp.int32)
    for k in range(TOP_K):
        v = jnp.max(work, axis=-1, keepdims=True)
        idx = jnp.min(jnp.where(work == v, lane, LANES), axis=-1, keepdims=True)
        hot = lane == idx
        work = jnp.where(hot, -jnp.inf, work)
        ids = jnp.where(lane == k, idx, ids)
        vals.append(v)
        hots.append(hot)

    es = [jnp.exp(v - vals[0]) for v in vals]
    esum = es[0] + es[1] + es[2] + es[3]
    gates = jnp.zeros((tm, LANES), F32)
    for k in range(TOP_K):
        gates = jnp.where(lane == k, es[k] / esum, gates)

    sel = (hots[0] | hots[1] | hots[2] | hots[3])
    r_i = lax.broadcasted_iota(jnp.int32, (tm, tm), 0)
    c_i = lax.broadcasted_iota(jnp.int32, (tm, tm), 1)
    tri = (c_i < r_i).astype(BF16)
    before = jnp.dot(tri, sel.astype(BF16), preferred_element_type=F32) + carry_ref[...]
    rank = jnp.zeros((tm, LANES), jnp.int32)
    for k in range(TOP_K):
        rk = jnp.sum(jnp.where(hots[k], before, 0.0), axis=-1, keepdims=True)
        rank = jnp.where(lane == k, rk.astype(jnp.int32), rank)
    carry_ref[...] = carry_ref[...] + jnp.sum(sel.astype(F32), axis=0, keepdims=True)

    gates_ref[...] = gates
    ids_ref[...] = ids
    rank_ref[...] = rank
    cnt_ref[...] = carry_ref[...].astype(jnp.int32)


def router(h2, w_router, b_router, tm=512):
    T = h2.shape[0]
    tm = min(tm, T)
    w_pad = jnp.pad(w_router.astype(F32), ((0, 0), (0, LANES - N_EXPERTS)))
    b_pad = jnp.pad(b_router.astype(F32), (0, LANES - N_EXPERTS)).reshape(1, LANES)
    tile = pl.BlockSpec((tm, LANES), lambda i: (i, 0))
    return pl.pallas_call(
        functools.partial(_router_body, tm=tm),
        grid=(T // tm,),
        in_specs=[pl.BlockSpec((tm, D_MODEL), lambda i: (i, 0)),
                  pl.BlockSpec((D_MODEL, LANES), lambda i: (0, 0)),
                  pl.BlockSpec((1, LANES), lambda i: (0, 0))],
        out_specs=[tile, tile, tile, pl.BlockSpec((1, LANES), lambda i: (0, 0))],
        out_shape=[jax.ShapeDtypeStruct((T, LANES), F32), jax.ShapeDtypeStruct((T, LANES), jnp.int32),
                   jax.ShapeDtypeStruct((T, LANES), jnp.int32), jax.ShapeDtypeStruct((1, LANES), jnp.int32)],
        scratch_shapes=[pltpu.VMEM((1, LANES), F32)],
        compiler_params=_params(("arbitrary",)),
        name="router",
    )(h2, w_pad, b_pad)


def _row_copy(src, dst, s_row, d_row, sem):
    return pltpu.make_async_copy(src.at[pl.ds(s_row, 1), :], dst.at[pl.ds(d_row, 1), :], sem)


def _dispatch_body(zstart_ref, zcount_ref, pos_ref, h_ref, xs_ref, zeros_ref, sem, zsem, *, tm):
    bits = [1 << s for s in reversed(range(SUBLANES.bit_length() - 1, EXPERT_BLOCK.bit_length() - 1))]

    def zero_copy(start, size):
        return pltpu.make_async_copy(zeros_ref.at[pl.ds(0, size), :], xs_ref.at[pl.ds(start, size), :], zsem)

    def zero_fill(e, wait):
        start, count = zstart_ref[e], zcount_ref[e]
        head = jnp.minimum((-start) & (SUBLANES - 1), count)
        for j in range(SUBLANES - 1):
            @pl.when(j < head)
            def _():
                cp = zero_copy(0 if wait else start + j, 1)
                cp.wait() if wait else cp.start()
        body = count - head
        base = pl.multiple_of(start + head, SUBLANES)
        for bit in bits:
            @pl.when((body & bit) != 0)
            def _():
                off = pl.multiple_of(body & ~(2 * bit - 1), SUBLANES)
                cp = zero_copy(0 if wait else base + off, bit)
                cp.wait() if wait else cp.start()

    @pl.when(pl.program_id(0) == 0)
    def _():
        zeros_ref[...] = jnp.zeros_like(zeros_ref)
        lax.fori_loop(0, N_EXPERTS, lambda e, c: (zero_fill(e, False), c)[1], 0)

    def issue(t, c):
        for k in range(TOP_K):
            _row_copy(h_ref, xs_ref, t, pos_ref[k, t], sem).start(priority=k % 2)
        return c

    lax.fori_loop(0, tm, issue, 0)
    for k in range(TOP_K):
        pltpu.make_async_copy(h_ref, xs_ref.at[pl.ds(0, tm), :], sem).wait()

    @pl.when(pl.program_id(0) == 0)
    def _():
        lax.fori_loop(0, N_EXPERTS, lambda e, c: (zero_fill(e, True), c)[1], 0)


def dispatch(h2, pos_t, zstart, zcount, n_rows, tm=512):
    T = h2.shape[0]
    tm = min(tm, T)
    grid_spec = pltpu.PrefetchScalarGridSpec(
        num_scalar_prefetch=2,
        grid=(T // tm,),
        in_specs=[pl.BlockSpec((TOP_K, tm), lambda i, zs, zc: (0, i), memory_space=pltpu.SMEM),
                  pl.BlockSpec((tm, D_MODEL), lambda i, zs, zc: (i, 0))],
        out_specs=pl.BlockSpec(memory_space=pl.ANY),
        scratch_shapes=[pltpu.VMEM((EXPERT_BLOCK // 2, D_MODEL), F32), pltpu.SemaphoreType.DMA,
                        pltpu.SemaphoreType.DMA],
    )
    return pl.pallas_call(
        functools.partial(_dispatch_body, tm=tm),
        grid_spec=grid_spec,
        out_shape=jax.ShapeDtypeStruct((n_rows, D_MODEL), F32),
        compiler_params=_params(("arbitrary",)),
        name="dispatch",
    )(zstart, zcount, pos_t, h2)


def _combine_body(pos_ref, gates_ref, h_ref, g2_ref, b2_ref, ys_ref, o_ref, buf, sem, *, tm):
    def issue(t, c):
        for k in range(TOP_K):
            _row_copy(ys_ref, buf.at[k], pos_ref[k, t], t, sem).start(priority=k % 2)
        return c

    lax.fori_loop(0, tm, issue, 0)
    for k in range(TOP_K):
        pltpu.make_async_copy(ys_ref.at[pl.ds(0, tm), :], buf.at[k], sem).wait()

    gates = gates_ref[...]
    ffn = gates[:, 0:1] * buf[0]
    for k in range(1, TOP_K):
        ffn = ffn + gates[:, k:k + 1] * buf[k]
    o_ref[...] = _layer_norm(DEEPNORM_ALPHA * h_ref[...] + ffn, g2_ref[...], b2_ref[...])


def combine(ys, pos_t, gates_pad, h2, g2, b2, tm=256):
    T = h2.shape[0]
    tm = min(tm, T)
    return pl.pallas_call(
        functools.partial(_combine_body, tm=tm),
        grid=(T // tm,),
        in_specs=[pl.BlockSpec((TOP_K, tm), lambda i: (0, i), memory_space=pltpu.SMEM),
                  pl.BlockSpec((tm, LANES), lambda i: (i, 0)),
                  pl.BlockSpec((tm, D_MODEL), lambda i: (i, 0)),
                  pl.BlockSpec((1, D_MODEL), lambda i: (0, 0)),
                  pl.BlockSpec((1, D_MODEL), lambda i: (0, 0)),
                  pl.BlockSpec(memory_space=pl.ANY)],
        out_specs=pl.BlockSpec((tm, D_MODEL), lambda i: (i, 0)),
        out_shape=jax.ShapeDtypeStruct((T, D_MODEL), F32),
        scratch_shapes=[pltpu.VMEM((TOP_K, tm, D_MODEL), F32), pltpu.SemaphoreType.DMA],
        compiler_params=_params(("arbitrary",)),
        name="combine",
    )(pos_t, gates_pad, h2, g2, b2, ys)


def _experts_body(blk_e, blk_src, n_used, xs_ref, wgu_ref, bgu_ref, wdn_ref, bdn_ref, ys_ref):
    del blk_e, blk_src

    @pl.when(pl.program_id(0) < n_used[0])
    def _():
        hgu = jnp.dot(xs_ref[...].astype(BF16), wgu_ref[...], preferred_element_type=F32) + bgu_ref[...]
        gate = jnp.minimum(hgu[:, :D_FF], SWIGLU_LIMIT)
        up = jnp.clip(hgu[:, D_FF:], -SWIGLU_LIMIT, SWIGLU_LIMIT)
        act = (up + 1.0) * gate * _sigmoid(SWIGLU_ALPHA * gate)
        ys_ref[...] = jnp.dot(act.astype(BF16), wdn_ref[...], preferred_element_type=F32) + bdn_ref[...]


def experts(xs, blk_e, blk_src, n_used, wgu_b, bgu, wdn_b, bdn):
    n_rows = xs.shape[0]
    nb = n_rows // EXPERT_BLOCK
    grid_spec = pltpu.PrefetchScalarGridSpec(
        num_scalar_prefetch=3,
        grid=(nb,),
        in_specs=[pl.BlockSpec((EXPERT_BLOCK, D_MODEL), lambda i, e, s, n: (s[i], 0)),
                  pl.BlockSpec((None, D_MODEL, 2 * D_FF), lambda i, e, s, n: (e[i], 0, 0)),
                  pl.BlockSpec((None, 1, 2 * D_FF), lambda i, e, s, n: (e[i], 0, 0)),
                  pl.BlockSpec((None, D_FF, D_MODEL), lambda i, e, s, n: (e[i], 0, 0)),
                  pl.BlockSpec((None, 1, D_MODEL), lambda i, e, s, n: (e[i], 0, 0))],
        out_specs=pl.BlockSpec((EXPERT_BLOCK, D_MODEL), lambda i, e, s, n: (s[i], 0)),
    )
    return pl.pallas_call(
        _experts_body,
        grid_spec=grid_spec,
        out_shape=jax.ShapeDtypeStruct((n_rows, D_MODEL), F32),
        compiler_params=_params(("arbitrary",)),
        name="experts",
    )(blk_e, blk_src, n_used, xs, wgu_b, bgu, wdn_b, bdn)


def routing_layout(ids, rank, counts, n_blocks):
    padded = (counts + EXPERT_BLOCK - 1) // EXPERT_BLOCK * EXPERT_BLOCK
    pad_ends = jnp.cumsum(padded)
    pad_starts = pad_ends - padded
    pos_t = (pad_starts[ids] + rank).T.astype(jnp.int32)
    n_used = jnp.maximum(pad_ends[-1] // EXPERT_BLOCK, 1).astype(jnp.int32)
    blk = jnp.minimum(jnp.arange(n_blocks, dtype=jnp.int32), n_used - 1)
    blk_e = jnp.sum(pad_ends[None, :] <= (blk * EXPERT_BLOCK)[:, None], axis=1)
    blk_e = jnp.minimum(blk_e, N_EXPERTS - 1).astype(jnp.int32)
    zstart = (pad_starts + counts).astype(jnp.int32)
    zcount = (padded - counts).astype(jnp.int32)
    return pos_t, blk_e, blk, n_used.reshape(1), zstart, zcount


def _dilate(xb, B, S, dil):
    D = xb.shape[-1]
    return xb.reshape(B, S // dil, dil, D).transpose(0, 2, 1, 3).reshape(B * S, D)


def kernel(x, w_in, rel_bias, w_dw, b_dw, conv_ln_g, conv_ln_b, w_o_attn, w_o_conv, w_out, ln1_g, ln1_b,
           w_router, b_router, w_gate_up, b_gate_up, w_down, b_down, ln2_g, ln2_b):
    B, S, D = x.shape
    T = B * S
    h = x
    bm = bias_tables(rel_bias)
    q_off, k_off, v_off, rest = 0, QKV_WIDTH, 2 * QKV_WIDTH, 3 * QKV_WIDTH
    for l in range(DEPTH):
        wb = w_in[l].astype(BF16)

        def group_cols(gi):
            return [wb[:, off + gi * GROUP_WIDTH:off + (gi + 1) * GROUP_WIDTH] for off in (q_off, k_off, v_off)]

        xb = h.reshape(T, D).astype(BF16)
        w_main = jnp.concatenate(group_cols(0) + [wb[:, rest:]], axis=1)
        proj = in_proj(xb, w_main, 1024, "in_proj_main").reshape(B, S, w_main.shape[1])
        o_list, st_list = [], []
        for gi in range(N_GROUPS):
            dil = ATT_GROUPS[gi][1]
            if dil == 1:
                qkv = proj.reshape(B, 1, S, proj.shape[-1])
            else:
                w_g = jnp.concatenate(group_cols(gi), axis=1)
                qkv = in_proj(_dilate(xb, B, S, dil), w_g, w_g.shape[1], f"in_proj_g{gi}")
                qkv = qkv.reshape(B, dil, S // dil, w_g.shape[1])
            o_g, st_g = attention_group(qkv, bm[gi], gi)
            o_list.append(o_g)
            st_list.append(st_g)
        h1 = mixer_out(o_list, st_list, proj, h,
                       w_dw[l].reshape(CONV_WIDTH, CONV_CH), b_dw[l].reshape(1, CONV_CH),
                       conv_ln_g[l].reshape(1, CONV_CH), conv_ln_b[l].reshape(1, CONV_CH),
                       w_o_attn[l].astype(BF16), w_o_conv[l].astype(BF16), w_out[l].astype(BF16),
                       ln1_g[l].reshape(1, D), ln1_b[l].reshape(1, D))
        h2 = h1.reshape(T, D)
        gates_pad, ids_pad, rank_pad, counts = router(h2, w_router[l], b_router[l])
        n_rows = T * TOP_K + N_EXPERTS * EXPERT_BLOCK
        pos_t, blk_e, blk_src, n_used, zstart, zcount = routing_layout(
            ids_pad[:, :TOP_K], rank_pad[:, :TOP_K], counts[0, :N_EXPERTS], n_rows // EXPERT_BLOCK)
        xs = dispatch(h2, pos_t, zstart, zcount, n_rows)
        ys = experts(xs, blk_e, blk_src, n_used,
                     w_gate_up[l].astype(BF16), b_gate_up[l].reshape(N_EXPERTS, 1, 2 * D_FF),
                     w_down[l].astype(BF16), b_down[l].reshape(N_EXPERTS, 1, D))
        out = combine(ys, pos_t, gates_pad, h2, ln2_g[l].reshape(1, D), ln2_b[l].reshape(1, D))
        h = out.reshape(B, S, D)
    return h
```

```python
import functools
import math

import jax
import jax.numpy as jnp
from jax import lax
from jax.experimental import pallas as pl
from jax.experimental.pallas import tpu as pltpu

D_MODEL = 1024
ATT_GROUPS = ((128, 1), (512, 4), (2048, 16))
N_GROUPS = len(ATT_GROUPS)
HEADS_PER_GROUP = 8
HEAD_DIM = 64
GROUP_WIDTH = HEADS_PER_GROUP * HEAD_DIM
QKV_WIDTH = N_GROUPS * GROUP_WIDTH
ATT_BLOCK = 128
N_BUCKETS = 32
MAX_DISTANCE = 2048
CONV_CH = 768
CONV_WIDTH = 31
CONV_HALO = 32
N_EXPERTS = 32
TOP_K = 4
D_FF = 1024
SWIGLU_LIMIT = 7.0
SWIGLU_ALPHA = 1.702
IN_WIDTH = 3 * QKV_WIDTH + 2 * CONV_CH + 2 * D_MODEL
LN_EPS = 1e-5
NEG_INF = -1e30
DEPTH = 1
DEEPNORM_ALPHA = (2 * DEPTH) ** 0.25

LANES = 128
SUBLANES = 8
EXPERT_BLOCK = 512
VMEM_LIMIT = 56 * 1024 * 1024

F32 = jnp.float32
BF16 = jnp.bfloat16


def _params(sem, vmem=VMEM_LIMIT):
    return pltpu.CompilerParams(dimension_semantics=sem, vmem_limit_bytes=vmem)


def _sigmoid(x):
    return 1.0 / (1.0 + jnp.exp(-x))


def _layer_norm(x, g, b):
    mu = jnp.mean(x, axis=-1, keepdims=True)
    xc = x - mu
    var = jnp.mean(xc * xc, axis=-1, keepdims=True)
    return xc * lax.rsqrt(var + LN_EPS) * g + b


def _in_proj_body(x_ref, w_ref, o_ref):
    o_ref[...] = jnp.dot(x_ref[...], w_ref[...], preferred_element_type=F32).astype(o_ref.dtype)


def in_proj(xb, w_b, tn, name, tm=1024):
    T = xb.shape[0]
    N = w_b.shape[1]
    tm = min(tm, T)
    return pl.pallas_call(
        _in_proj_body,
        grid=(T // tm, N // tn),
        in_specs=[pl.BlockSpec((tm, D_MODEL), lambda m, n: (m, 0)),
                  pl.BlockSpec((D_MODEL, tn), lambda m, n: (0, n))],
        out_specs=pl.BlockSpec((tm, tn), lambda m, n: (m, n)),
        out_shape=jax.ShapeDtypeStruct((T, N), BF16),
        compiler_params=_params(("arbitrary", "arbitrary")),
        name=name,
    )(xb, w_b)


def _t5_bucket(dist):
    max_exact = N_BUCKETS // 2
    log_ratio = jnp.log(jnp.maximum(dist, max_exact).astype(F32) / max_exact) / math.log(MAX_DISTANCE / max_exact)
    large = jnp.minimum(max_exact + (log_ratio * (N_BUCKETS - max_exact)).astype(jnp.int32), N_BUCKETS - 1)
    return jnp.where(dist < max_exact, dist, large)


def _bias_body(tbl_ref, bucket_ref, band_ref, o_ref):
    g = pl.program_id(0)
    h = pl.program_id(1)
    col = g * HEADS_PER_GROUP + h
    bucket = bucket_ref[...]
    acc = jnp.zeros(bucket.shape, F32)
    for k in range(N_BUCKETS):
        acc = jnp.where(bucket == k, tbl_ref[k, col], acc)
    band = band_ref[...] != 0
    kj = lax.broadcasted_iota(jnp.int32, bucket.shape, 1)
    o_ref[0] = jnp.where(band, acc, NEG_INF)
    o_ref[1] = jnp.where(band & (kj >= ATT_BLOCK), acc, NEG_INF)


def bias_tables(rel_bias):
    qi = jnp.arange(ATT_BLOCK)[:, None]
    kj = jnp.arange(2 * ATT_BLOCK)[None, :]
    dist = qi - kj + ATT_BLOCK
    buckets, bands = [], []
    for window, dil in ATT_GROUPS:
        bands.append(((dist >= 0) & (dist <= window // dil)).astype(jnp.int32))
        buckets.append(_t5_bucket(jnp.maximum(dist, 0) * dil).astype(jnp.int32))
    buckets = jnp.stack(buckets)
    bands = jnp.stack(bands)
    blk = (None, ATT_BLOCK, 2 * ATT_BLOCK)
    return pl.pallas_call(
        _bias_body,
        grid=(N_GROUPS, HEADS_PER_GROUP),
        in_specs=[pl.BlockSpec(memory_space=pltpu.SMEM),
                  pl.BlockSpec(blk, lambda g, h: (g, 0, 0)),
                  pl.BlockSpec(blk, lambda g, h: (g, 0, 0))],
        out_specs=pl.BlockSpec((None, None, 2, ATT_BLOCK, 2 * ATT_BLOCK), lambda g, h: (g, h, 0, 0, 0)),
        out_shape=jax.ShapeDtypeStruct((N_GROUPS, HEADS_PER_GROUP, 2, ATT_BLOCK, 2 * ATT_BLOCK), F32),
        compiler_params=_params(("arbitrary", "arbitrary")),
        name="bias_tables",
    )(rel_bias.astype(F32), buckets, bands)


def _attn_body(q_ref, kc_ref, kp_ref, vc_ref, vp_ref, bm_ref, o_ref, st_ref, k_all, v_all, *, tq):
    i = pl.program_id(2)
    nsub = tq // ATT_BLOCK
    k_all[0:ATT_BLOCK] = kp_ref[...]
    k_all[ATT_BLOCK:] = kc_ref[...]
    v_all[0:ATT_BLOCK] = vp_ref[...]
    v_all[ATT_BLOCK:] = vc_ref[...]
    lane = lax.broadcasted_iota(jnp.int32, (ATT_BLOCK, LANES), 1)
    lo = lane < HEAD_DIM
    nt_dims = (((1,), (1,)), ((), ()))

    def sub_block(s, carry):
        r0 = pl.multiple_of(s * ATT_BLOCK, ATT_BLOCK)
        variant = jnp.where((i == 0) & (s == 0), 1, 0)
        st = jnp.zeros((ATT_BLOCK, LANES), F32)
        for j in range(HEADS_PER_GROUP // 2):
            cs = slice(j * LANES, (j + 1) * LANES)
            qp = q_ref[pl.ds(r0, ATT_BLOCK), cs] * jnp.asarray(HEAD_DIM ** -0.5, BF16)
            kp = k_all[pl.ds(r0, 2 * ATT_BLOCK), cs]
            vp = v_all[pl.ds(r0, 2 * ATT_BLOCK), cs]
            halves = []
            for hh in range(2):
                h = 2 * j + hh
                qh = jnp.where(lo if hh == 0 else ~lo, qp, jnp.zeros_like(qp))
                sc = lax.dot_general(qh, kp, nt_dims, preferred_element_type=F32) + bm_ref[h, variant]
                m = jnp.max(sc, axis=-1, keepdims=True)
                p = jnp.exp(sc - m)
                den = jnp.sum(p, axis=-1, keepdims=True)
                pv = jnp.dot(p.astype(BF16), vp, preferred_element_type=F32)
                halves.append(pv * (1.0 / den))
                st = jnp.where(lane == h, m, st)
                st = jnp.where(lane == HEADS_PER_GROUP + h, den, st)
            o_ref[pl.ds(r0, ATT_BLOCK), cs] = jnp.where(lo, halves[0], halves[1])
        st_ref[pl.ds(r0, ATT_BLOCK), :] = st
        return carry

    lax.fori_loop(0, nsub, sub_block, 0)


def attention_group(qkv, bm_g, gi, tq=512):
    B, dil, L, _ = qkv.shape
    tq = min(tq, L)
    sub = tq // ATT_BLOCK

    def cur(col):
        return pl.BlockSpec((None, None, tq, GROUP_WIDTH), lambda b, r, i: (b, r, i, col))

    def prev(col):
        return pl.BlockSpec((None, None, ATT_BLOCK, GROUP_WIDTH),
                            lambda b, r, i: (b, r, jnp.maximum(i * sub - 1, 0), col))

    return pl.pallas_call(
        functools.partial(_attn_body, tq=tq),
        grid=(B, dil, L // tq),
        in_specs=[cur(0), cur(1), prev(1), cur(2), prev(2),
                  pl.BlockSpec((HEADS_PER_GROUP, 2, ATT_BLOCK, 2 * ATT_BLOCK), lambda b, r, i: (0, 0, 0, 0))],
        out_specs=[pl.BlockSpec((None, None, tq, GROUP_WIDTH), lambda b, r, i: (b, r, i, 0)),
                   pl.BlockSpec((None, None, tq, LANES), lambda b, r, i: (b, r, i, 0))],
        out_shape=[jax.ShapeDtypeStruct((B, dil, L, GROUP_WIDTH), F32),
                   jax.ShapeDtypeStruct((B, dil, L, LANES), F32)],
        scratch_shapes=[pltpu.VMEM((ATT_BLOCK + tq, GROUP_WIDTH), BF16),
                        pltpu.VMEM((ATT_BLOCK + tq, GROUP_WIDTH), BF16)],
        compiler_params=_params(("arbitrary", "arbitrary", "arbitrary")),
        name=f"attention_g{gi}",
    )(qkv, qkv, qkv, qkv, qkv, bm_g)


def _split_bf16(x):
    hi = x.astype(BF16)
    lo = (x - hi.astype(F32)).astype(BF16)
    return hi, lo


def _load_token_tiles(ref, n):
    return jnp.concatenate([ref[pl.ds(c, n, stride=SUBLANES), :] for c in range(D_MODEL // LANES)], axis=1)


def _store_token_tiles(ref, x, n):
    for c in range(D_MODEL // LANES):
        ref[pl.ds(c, n, stride=SUBLANES), :] = x[:, c * LANES:(c + 1) * LANES]


def _to_token_order(blk_ref, tok_ref, dil, tm):
    n = tm // dil
    for r in range(dil):
        for c in range(tok_ref.shape[0]):
            tok_ref[c, pl.ds(r, n, stride=dil), :] = blk_ref[r, :, c * LANES:(c + 1) * LANES]


def _mixer_out_body(o0, o1, o2, s0, s1, s2, uv_ref, ug_ref, uvh_ref, ugh_ref, ga_ref, gc_ref, x_ref,
                    wdw_ref, bdw_ref, cg_ref, cb_ref, woa_ref, woc_ref, wout_ref, g1_ref, b1_ref,
                    h_ref, ht_ref, glu_ref, dw_ref, shift_ref, tok_o1, tok_s1, tok_o2, tok_s2, *, tm, chunk):
    i = pl.program_id(1)

    ncol = GROUP_WIDTH // LANES
    _to_token_order(o1, tok_o1, ATT_GROUPS[1][1], tm)
    _to_token_order(s1, tok_s1, ATT_GROUPS[1][1], tm)
    _to_token_order(o2, tok_o2, ATT_GROUPS[2][1], tm)
    _to_token_order(s2, tok_s2, ATT_GROUPS[2][1], tm)
    outs = [o0[0],
            jnp.concatenate([tok_o1[c] for c in range(ncol)], axis=1),
            jnp.concatenate([tok_o2[c] for c in range(ncol)], axis=1)]
    sts = [s0[0], tok_s1[0], tok_s2[0]]

    mx = jnp.maximum(jnp.maximum(sts[0], sts[1]), sts[2])
    wts = [pltpu.roll(st, LANES - HEADS_PER_GROUP, axis=1) * jnp.exp(st - mx) for st in sts]
    wsum = wts[0] + wts[1] + wts[2]
    row = lax.broadcasted_iota(jnp.int32, (LANES, GROUP_WIDTH), 0)
    colh = lax.broadcasted_iota(jnp.int32, (LANES, GROUP_WIDTH), 1) // HEAD_DIM
    expand = (row == colh).astype(BF16)
    attn = jnp.zeros((tm, GROUP_WIDTH), F32)
    head_lane = lax.broadcasted_iota(jnp.int32, (tm, LANES), 1) < HEADS_PER_GROUP
    for wt, o in zip(wts, outs):
        c_hi, c_lo = _split_bf16(jnp.where(head_lane, wt / wsum, 0.0))
        c = (jnp.dot(c_hi, expand, preferred_element_type=F32)
             + jnp.dot(c_lo, expand, preferred_element_type=F32))
        attn = attn + c * o
    a_out = jnp.dot(attn.astype(BF16), woa_ref[...], preferred_element_type=F32)

    gh = uvh_ref[...].astype(F32) * _sigmoid(ugh_ref[...].astype(F32))
    glu_ref[0:CONV_HALO] = jnp.where(i == 0, 0.0, gh)
    glu_ref[CONV_HALO:] = uv_ref[...].astype(F32) * _sigmoid(ug_ref[...].astype(F32))
    first_tap = CONV_HALO - (CONV_WIDTH - 1)
    for b in range(1, SUBLANES):
        shift_ref[b - 1] = glu_ref[b:b + shift_ref.shape[1], :]

    for r0 in range(0, tm, chunk):
        acc = jnp.broadcast_to(bdw_ref[...], (chunk, CONV_CH))
        for j in range(CONV_WIDTH):
            a, b = divmod(first_tap + j, SUBLANES)
            lo_row = r0 + a * SUBLANES
            rows = glu_ref[lo_row:lo_row + chunk, :] if b == 0 else shift_ref[b - 1, lo_row:lo_row + chunk, :]
            acc = acc + wdw_ref[j:j + 1, :] * rows
        dw_ref[r0:r0 + chunk, :] = acc
    cn = _layer_norm(dw_ref[...], cg_ref[...], cb_ref[...])
    conv = cn * _sigmoid(cn)
    c_out = jnp.dot(conv.astype(BF16), woc_ref[...], preferred_element_type=F32)

    merged = (_sigmoid(ga_ref[...].astype(F32)) * a_out + _sigmoid(gc_ref[...].astype(F32)) * c_out)
    mix = jnp.dot(merged.astype(BF16), wout_ref[...], preferred_element_type=F32)
    h = _layer_norm(DEEPNORM_ALPHA * x_ref[...] + mix, g1_ref[...], b1_ref[...])
    h_ref[...] = h
    _store_token_tiles(ht_ref, h, tm)


def mixer_out(o_list, st_list, proj3, x3, w_dw, b_dw, cg, cb, woa_b, woc_b, wout_b, g1, b1, tm=256, chunk=32):
    B, S, _ = x3.shape
    tm = min(tm, S)
    halo_blocks = tm // CONV_HALO
    uv_col = QKV_WIDTH // CONV_CH
    ga_col = (QKV_WIDTH + 2 * CONV_CH) // D_MODEL
    ncol = GROUP_WIDTH // LANES

    def tile(width, col=0):
        return pl.BlockSpec((None, tm, width), lambda b, i: (b, i, col))

    def dilated(gi, width):
        dil = ATT_GROUPS[gi][1]
        return pl.BlockSpec((None, dil, tm // dil, width), lambda b, i: (b, 0, i, 0))

    def halo(col):
        return pl.BlockSpec((None, CONV_HALO, CONV_CH), lambda b, i: (b, jnp.maximum(i * halo_blocks - 1, 0), col))

    def whole(shape):
        return pl.BlockSpec(shape, lambda b, i: (0,) * len(shape))

    return pl.pallas_call(
        functools.partial(_mixer_out_body, tm=tm, chunk=chunk),
        grid=(B, S // tm),
        in_specs=[dilated(g, GROUP_WIDTH) for g in range(N_GROUPS)] + [dilated(g, LANES) for g in range(N_GROUPS)]
        + [tile(CONV_CH, uv_col), tile(CONV_CH, uv_col + 1), halo(uv_col), halo(uv_col + 1),
           tile(D_MODEL, ga_col), tile(D_MODEL, ga_col + 1), tile(D_MODEL),
           whole((CONV_WIDTH, CONV_CH)), whole((1, CONV_CH)), whole((1, CONV_CH)), whole((1, CONV_CH)),
           whole((GROUP_WIDTH, D_MODEL)), whole((CONV_CH, D_MODEL)), whole((D_MODEL, D_MODEL)),
           whole((1, D_MODEL)), whole((1, D_MODEL))],
        out_specs=[tile(D_MODEL), pl.BlockSpec((tm * SUBLANES, LANES), lambda b, i: (b * (S // tm) + i, 0))],
        out_shape=[jax.ShapeDtypeStruct((B, S, D_MODEL), F32),
                   jax.ShapeDtypeStruct((B * S * SUBLANES, LANES), F32)],
        scratch_shapes=[pltpu.VMEM((CONV_HALO + tm, CONV_CH), F32), pltpu.VMEM((tm, CONV_CH), F32),
                        pltpu.VMEM((SUBLANES - 1, tm + CONV_HALO - SUBLANES, CONV_CH), F32),
                        pltpu.VMEM((ncol, tm, LANES), F32), pltpu.VMEM((1, tm, LANES), F32),
                        pltpu.VMEM((ncol, tm, LANES), F32), pltpu.VMEM((1, tm, LANES), F32)],
        compiler_params=_params(("arbitrary", "arbitrary")),
        name="mixer_out",
    )(*o_list, *st_list, proj3, proj3, proj3, proj3, proj3, proj3, x3,
      w_dw, b_dw, cg, cb, woa_b, woc_b, wout_b, g1, b1)


def _router_body(h_ref, w_ref, b_ref, gates_ref, ids_ref, rank_ref, cnt_ref, carry_ref, tri_ref, *, tm):
    step = pl.program_id(0)

    @pl.when(step == 0)
    def _():
        carry_ref[...] = jnp.zeros_like(carry_ref)
        r_i = lax.broadcasted_iota(jnp.int32, (tm, tm), 0)
        c_i = lax.broadcasted_iota(jnp.int32, (tm, tm), 1)
        tri_ref[...] = (c_i < r_i).astype(BF16)

    h_hi, h_lo = _split_bf16(h_ref[...])
    w_hi, w_lo = _split_bf16(w_ref[...])
    logits = (jnp.dot(h_hi, w_hi, preferred_element_type=F32)
              + (jnp.dot(h_lo, w_hi, preferred_element_type=F32) + jnp.dot(h_hi, w_lo, preferred_element_type=F32))
              + b_ref[...])
    lane = lax.broadcasted_iota(jnp.int32, (tm, LANES), 1)
    lane_f = lane.astype(F32)
    work = jnp.where(lane < N_EXPERTS, logits, -jnp.inf)

    vals, hots = [], []
    ids = jnp.zeros((tm, LANES), F32)
    for k in range(TOP_K):
        v = jnp.max(work, axis=-1, keepdims=True)
        idx = jnp.min(jnp.where(work == v, lane_f, float(LANES)), axis=-1, keepdims=True)
        hot = lane_f == idx
        work = jnp.where(hot, -jnp.inf, work)
        ids = jnp.where(lane == k, idx, ids)
        vals.append(v)
        hots.append(hot)

    es = [jnp.exp(v - vals[0]) for v in vals]
    esum = es[0] + es[1] + es[2] + es[3]
    gates = jnp.zeros((tm, LANES), F32)
    for k in range(TOP_K):
        gates = jnp.where(lane == k, es[k] / esum, gates)

    sel = (hots[0] | hots[1] | hots[2] | hots[3])
    before = jnp.dot(tri_ref[...], sel.astype(BF16), preferred_element_type=F32) + carry_ref[...]
    rank = jnp.zeros((tm, LANES), jnp.int32)
    for k in range(TOP_K):
        rk = jnp.sum(jnp.where(hots[k], before, 0.0), axis=-1, keepdims=True)
        rank = jnp.where(lane == k, rk.astype(jnp.int32), rank)
    carry_ref[...] = carry_ref[...] + jnp.sum(sel.astype(F32), axis=0, keepdims=True)

    gates_ref[...] = gates
    ids_ref[...] = ids.astype(jnp.int32)
    rank_ref[...] = rank
    cnt_ref[...] = carry_ref[...].astype(jnp.int32)


def router(h2, w_router, b_router, tm=512):
    T = h2.shape[0]
    tm = min(tm, T)
    w_pad = jnp.pad(w_router.astype(F32), ((0, 0), (0, LANES - N_EXPERTS)))
    b_pad = jnp.pad(b_router.astype(F32), (0, LANES - N_EXPERTS)).reshape(1, LANES)
    tile = pl.BlockSpec((tm, LANES), lambda i: (i, 0))
    return pl.pallas_call(
        functools.partial(_router_body, tm=tm),
        grid=(T // tm,),
        in_specs=[pl.BlockSpec((tm, D_MODEL), lambda i: (i, 0)),
                  pl.BlockSpec((D_MODEL, LANES), lambda i: (0, 0)),
                  pl.BlockSpec((1, LANES), lambda i: (0, 0))],
        out_specs=[tile, tile, tile, pl.BlockSpec((1, LANES), lambda i: (0, 0))],
        out_shape=[jax.ShapeDtypeStruct((T, LANES), F32), jax.ShapeDtypeStruct((T, LANES), jnp.int32),
                   jax.ShapeDtypeStruct((T, LANES), jnp.int32), jax.ShapeDtypeStruct((1, LANES), jnp.int32)],
        scratch_shapes=[pltpu.VMEM((1, LANES), F32), pltpu.VMEM((tm, tm), BF16)],
        compiler_params=_params(("arbitrary",)),
        name="router",
    )(h2, w_pad, b_pad)


def _token_copy(src, dst, s_tok, d_tok, sem, n=1):
    rows = n * SUBLANES
    return pltpu.make_async_copy(src.at[pl.ds(pl.multiple_of(s_tok * SUBLANES, SUBLANES), rows), :],
                                 dst.at[pl.ds(pl.multiple_of(d_tok * SUBLANES, SUBLANES), rows), :], sem)


def _dispatch_body(zstart_ref, zcount_ref, n_used_ref, pos_ref, h_ref, xs_ref, zeros_ref, sem, zsem, *, tm, nb):
    half = EXPERT_BLOCK // 2
    bits = [1 << s for s in reversed(range(half.bit_length()))]

    def zero_fill(e, wait):
        start, count = zstart_ref[e], zcount_ref[e]
        for bit in bits:
            @pl.when((count & bit) != 0)
            def _():
                cp = _token_copy(zeros_ref, xs_ref, 0, 0 if wait else start + (count & ~(2 * bit - 1)), zsem, bit)
                cp.wait() if wait else cp.start()

    def zero_tail(blk, wait):
        for part in range(2):
            cp = _token_copy(zeros_ref, xs_ref, 0, 0 if wait else blk * EXPERT_BLOCK + part * half, zsem, half)
            cp.wait() if wait else cp.start()

    @pl.when(pl.program_id(0) == 0)
    def _():
        zeros_ref[...] = jnp.zeros_like(zeros_ref)
        lax.fori_loop(0, N_EXPERTS, lambda e, c: (zero_fill(e, False), c)[1], 0)
        lax.fori_loop(n_used_ref[0], nb, lambda blk, c: (zero_tail(blk, False), c)[1], 0)

    def issue(t, c):
        for k in range(TOP_K):
            _token_copy(h_ref, xs_ref, t, pos_ref[k, t], sem).start(priority=k % 2)
        return c

    lax.fori_loop(0, tm, issue, 0)
    for k in range(TOP_K):
        _token_copy(h_ref, xs_ref, 0, 0, sem, tm).wait()

    @pl.when(pl.program_id(0) == 0)
    def _():
        lax.fori_loop(0, N_EXPERTS, lambda e, c: (zero_fill(e, True), c)[1], 0)
        lax.fori_loop(n_used_ref[0], nb, lambda blk, c: (zero_tail(blk, True), c)[1], 0)


def dispatch(ht, pos_t, zstart, zcount, n_used, n_rows, tm=512):
    T = ht.shape[0] // SUBLANES
    tm = min(tm, T)
    grid_spec = pltpu.PrefetchScalarGridSpec(
        num_scalar_prefetch=3,
        grid=(T // tm,),
        in_specs=[pl.BlockSpec((TOP_K, tm), lambda i, zs, zc, nu: (0, i), memory_space=pltpu.SMEM),
                  pl.BlockSpec((tm * SUBLANES, LANES), lambda i, zs, zc, nu: (i, 0))],
        out_specs=pl.BlockSpec(memory_space=pl.ANY),
        scratch_shapes=[pltpu.VMEM((EXPERT_BLOCK // 2 * SUBLANES, LANES), F32), pltpu.SemaphoreType.DMA,
                        pltpu.SemaphoreType.DMA],
    )
    return pl.pallas_call(
        functools.partial(_dispatch_body, tm=tm, nb=n_rows // EXPERT_BLOCK),
        grid_spec=grid_spec,
        out_shape=jax.ShapeDtypeStruct((n_rows * SUBLANES, LANES), F32),
        compiler_params=_params(("arbitrary",)),
        name="dispatch",
    )(zstart, zcount, n_used, pos_t, ht)


def _combine_body(pos_ref, gates_ref, h_ref, g2_ref, b2_ref, ys_ref, o_ref, buf, sem, *, tm):
    def issue(t, c):
        for k in range(TOP_K):
            _token_copy(ys_ref, buf.at[k], pos_ref[k, t], t, sem).start(priority=k % 2)
        return c

    lax.fori_loop(0, tm, issue, 0)
    for k in range(TOP_K):
        _token_copy(ys_ref, buf.at[k], 0, 0, sem, tm).wait()

    gates = gates_ref[...]
    ffn = gates[:, 0:1] * _load_token_tiles(buf.at[0], tm)
    for k in range(1, TOP_K):
        ffn = ffn + gates[:, k:k + 1] * _load_token_tiles(buf.at[k], tm)
    o_ref[...] = _layer_norm(DEEPNORM_ALPHA * h_ref[...] + ffn, g2_ref[...], b2_ref[...])


def combine(ys, pos_t, gates_pad, h2, g2, b2, tm=256):
    T = h2.shape[0]
    tm = min(tm, T)
    return pl.pallas_call(
        functools.partial(_combine_body, tm=tm),
        grid=(T // tm,),
        in_specs=[pl.BlockSpec((TOP_K, tm), lambda i: (0, i), memory_space=pltpu.SMEM),
                  pl.BlockSpec((tm, LANES), lambda i: (i, 0)),
                  pl.BlockSpec((tm, D_MODEL), lambda i: (i, 0)),
                  pl.BlockSpec((1, D_MODEL), lambda i: (0, 0)),
                  pl.BlockSpec((1, D_MODEL), lambda i: (0, 0)),
                  pl.BlockSpec(memory_space=pl.ANY)],
        out_specs=pl.BlockSpec((tm, D_MODEL), lambda i: (i, 0)),
        out_shape=jax.ShapeDtypeStruct((T, D_MODEL), F32),
        scratch_shapes=[pltpu.VMEM((TOP_K, tm * SUBLANES, LANES), F32), pltpu.SemaphoreType.DMA],
        compiler_params=_params(("arbitrary",)),
        name="combine",
    )(pos_t, gates_pad, h2, g2, b2, ys)


def _experts_body(blk_e, blk_src, n_used, xs_ref, wgu_ref, bgu_ref, wdn_ref, bdn_ref, ys_ref):
    del blk_e, blk_src
    used = pl.program_id(0) < n_used[0]

    @pl.when(used)
    def _():
        x = _load_token_tiles(xs_ref, EXPERT_BLOCK).astype(BF16)
        hgu = jnp.dot(x, wgu_ref[...], preferred_element_type=F32) + bgu_ref[...]
        gate = jnp.minimum(hgu[:, :D_FF], SWIGLU_LIMIT)
        up = jnp.clip(hgu[:, D_FF:], -SWIGLU_LIMIT, SWIGLU_LIMIT)
        act = (up + 1.0) * gate * _sigmoid(SWIGLU_ALPHA * gate)
        y = jnp.dot(act.astype(BF16), wdn_ref[...], preferred_element_type=F32) + bdn_ref[...]
        _store_token_tiles(ys_ref, y, EXPERT_BLOCK)

    @pl.when(jnp.logical_not(used))
    def _():
        ys_ref[...] = jnp.zeros_like(ys_ref)


def experts(xs, blk_e, blk_src, n_used, wgu_b, bgu, wdn_b, bdn):
    n_rows = xs.shape[0] // SUBLANES
    nb = n_rows // EXPERT_BLOCK
    rows = EXPERT_BLOCK * SUBLANES
    grid_spec = pltpu.PrefetchScalarGridSpec(
        num_scalar_prefetch=3,
        grid=(nb,),
        in_specs=[pl.BlockSpec((rows, LANES), lambda i, e, s, n: (s[i], 0)),
                  pl.BlockSpec((None, D_MODEL, 2 * D_FF), lambda i, e, s, n: (e[i], 0, 0)),
                  pl.BlockSpec((None, 1, 2 * D_FF), lambda i, e, s, n: (e[i], 0, 0)),
                  pl.BlockSpec((None, D_FF, D_MODEL), lambda i, e, s, n: (e[i], 0, 0)),
                  pl.BlockSpec((None, 1, D_MODEL), lambda i, e, s, n: (e[i], 0, 0))],
        out_specs=pl.BlockSpec((rows, LANES), lambda i, e, s, n: (i, 0)),
    )
    return pl.pallas_call(
        _experts_body,
        grid_spec=grid_spec,
        out_shape=jax.ShapeDtypeStruct((n_rows * SUBLANES, LANES), F32),
        compiler_params=_params(("arbitrary",)),
        name="experts",
    )(blk_e, blk_src, n_used, xs, wgu_b, bgu, wdn_b, bdn)


def routing_layout(ids, rank, counts, n_blocks):
    padded = (counts + EXPERT_BLOCK - 1) // EXPERT_BLOCK * EXPERT_BLOCK
    pad_ends = jnp.cumsum(padded)
    pad_starts = pad_ends - padded
    pos_t = (pad_starts[ids] + rank).T.astype(jnp.int32)
    n_used = jnp.maximum(pad_ends[-1] // EXPERT_BLOCK, 1).astype(jnp.int32)
    blk = jnp.minimum(jnp.arange(n_blocks, dtype=jnp.int32), n_used - 1)
    blk_e = jnp.sum(pad_ends[None, :] <= (blk * EXPERT_BLOCK)[:, None], axis=1)
    blk_e = jnp.minimum(blk_e, N_EXPERTS - 1).astype(jnp.int32)
    zstart = (pad_starts + counts).astype(jnp.int32)
    zcount = (padded - counts).astype(jnp.int32)
    return pos_t, blk_e, blk, n_used.reshape(1), zstart, zcount


def _dilate(xb, B, S, dil):
    D = xb.shape[-1]
    return xb.reshape(B, S // dil, dil, D).transpose(0, 2, 1, 3).reshape(B * S, D)


def kernel(x, w_in, rel_bias, w_dw, b_dw, conv_ln_g, conv_ln_b, w_o_attn, w_o_conv, w_out, ln1_g, ln1_b,
           w_router, b_router, w_gate_up, b_gate_up, w_down, b_down, ln2_g, ln2_b):
    B, S, D = x.shape
    T = B * S
    h = x
    bm = bias_tables(rel_bias)
    q_off, k_off, v_off, rest = 0, QKV_WIDTH, 2 * QKV_WIDTH, 3 * QKV_WIDTH
    for l in range(DEPTH):
        wb = w_in[l].astype(BF16)

        def group_cols(gi):
            return [wb[:, off + gi * GROUP_WIDTH:off + (gi + 1) * GROUP_WIDTH] for off in (q_off, k_off, v_off)]

        xb = h.reshape(T, D).astype(BF16)
        w_main = jnp.concatenate(group_cols(0) + [wb[:, rest:]], axis=1)
        proj = in_proj(xb, w_main, 1024, "in_proj_main").reshape(B, S, w_main.shape[1])
        o_list, st_list = [], []
        for gi in range(N_GROUPS):
            dil = ATT_GROUPS[gi][1]
            if dil == 1:
                qkv = proj.reshape(B, 1, S, proj.shape[-1])
            else:
                w_g = jnp.concatenate(group_cols(gi), axis=1)
                qkv = in_proj(_dilate(xb, B, S, dil), w_g, w_g.shape[1], f"in_proj_g{gi}")
                qkv = qkv.reshape(B, dil, S // dil, w_g.shape[1])
            o_g, st_g = attention_group(qkv, bm[gi], gi)
            o_list.append(o_g)
            st_list.append(st_g)
        h1, h1_tiles = mixer_out(o_list, st_list, proj, h,
                       w_dw[l].reshape(CONV_WIDTH, CONV_CH), b_dw[l].reshape(1, CONV_CH),
                       conv_ln_g[l].reshape(1, CONV_CH), conv_ln_b[l].reshape(1, CONV_CH),
                       w_o_attn[l].astype(BF16), w_o_conv[l].astype(BF16), w_out[l].astype(BF16),
                       ln1_g[l].reshape(1, D), ln1_b[l].reshape(1, D))
        h2 = h1.reshape(T, D)
        gates_pad, ids_pad, rank_pad, counts = router(h2, w_router[l], b_router[l])
        n_rows = T * TOP_K + N_EXPERTS * EXPERT_BLOCK
        pos_t, blk_e, blk_src, n_used, zstart, zcount = routing_layout(
            ids_pad[:, :TOP_K], rank_pad[:, :TOP_K], counts[0, :N_EXPERTS], n_rows // EXPERT_BLOCK)
        xs = dispatch(h1_tiles, pos_t, zstart, zcount, n_used, n_rows)
        ys = experts(xs, blk_e, blk_src, n_used,
                     w_gate_up[l].astype(BF16), b_gate_up[l].reshape(N_EXPERTS, 1, 2 * D_FF),
                     w_down[l].astype(BF16), b_down[l].reshape(N_EXPERTS, 1, D))
        out = combine(ys, pos_t, gates_pad, h2, ln2_g[l].reshape(1, D), ln2_b[l].reshape(1, D))
        h = out.reshape(B, S, D)
    return h
```

```python
import functools
import math

import jax
import jax.numpy as jnp
from jax import lax
from jax.experimental import pallas as pl
from jax.experimental.pallas import tpu as pltpu

D_MODEL = 1024
ATT_GROUPS = ((128, 1), (512, 4), (2048, 16))
N_GROUPS = len(ATT_GROUPS)
HEADS_PER_GROUP = 8
HEAD_DIM = 64
GROUP_WIDTH = HEADS_PER_GROUP * HEAD_DIM
QKV_WIDTH = N_GROUPS * GROUP_WIDTH
ATT_BLOCK = 128
N_BUCKETS = 32
MAX_DISTANCE = 2048
CONV_CH = 768
CONV_WIDTH = 31
CONV_HALO = 32
N_EXPERTS = 32
TOP_K = 4
D_FF = 1024
SWIGLU_LIMIT = 7.0
SWIGLU_ALPHA = 1.702
IN_WIDTH = 3 * QKV_WIDTH + 2 * CONV_CH + 2 * D_MODEL
LN_EPS = 1e-5
NEG_INF = -1e30
DEPTH = 1
DEEPNORM_ALPHA = (2 * DEPTH) ** 0.25

LANES = 128
SUBLANES = 8
EXPERT_BLOCK = 512
VMEM_LIMIT = 56 * 1024 * 1024

F32 = jnp.float32
BF16 = jnp.bfloat16


def _params(sem, vmem=VMEM_LIMIT):
    return pltpu.CompilerParams(dimension_semantics=sem, vmem_limit_bytes=vmem)


def _sigmoid(x):
    return 1.0 / (1.0 + jnp.exp(-x))


def _layer_norm(x, g, b):
    mu = jnp.mean(x, axis=-1, keepdims=True)
    xc = x - mu
    var = jnp.mean(xc * xc, axis=-1, keepdims=True)
    return xc * lax.rsqrt(var + LN_EPS) * g + b


def _in_proj_body(x_ref, w_ref, o_ref):
    o_ref[...] = jnp.dot(x_ref[...], w_ref[...], preferred_element_type=F32).astype(o_ref.dtype)


def in_proj(xb, w_b, tn, name, tm=1024):
    T = xb.shape[0]
    N = w_b.shape[1]
    tm = min(tm, T)
    return pl.pallas_call(
        _in_proj_body,
        grid=(N // tn, T // tm),
        in_specs=[pl.BlockSpec((tm, D_MODEL), lambda n, m: (m, 0)),
                  pl.BlockSpec((D_MODEL, tn), lambda n, m: (0, n))],
        out_specs=pl.BlockSpec((tm, tn), lambda n, m: (m, n)),
        out_shape=jax.ShapeDtypeStruct((T, N), BF16),
        compiler_params=_params(("arbitrary", "arbitrary")),
        name=name,
    )(xb, w_b)


def _t5_bucket(dist):
    max_exact = N_BUCKETS // 2
    log_ratio = jnp.log(jnp.maximum(dist, max_exact).astype(F32) / max_exact) / math.log(MAX_DISTANCE / max_exact)
    large = jnp.minimum(max_exact + (log_ratio * (N_BUCKETS - max_exact)).astype(jnp.int32), N_BUCKETS - 1)
    return jnp.where(dist < max_exact, dist, large)


def _bias_body(tbl_ref, bucket_ref, band_ref, o_ref):
    g = pl.program_id(0)
    h = pl.program_id(1)
    col = g * HEADS_PER_GROUP + h
    bucket = bucket_ref[...]
    acc = jnp.zeros(bucket.shape, F32)
    for k in range(N_BUCKETS):
        acc = jnp.where(bucket == k, tbl_ref[k, col], acc)
    band = band_ref[...] != 0
    kj = lax.broadcasted_iota(jnp.int32, bucket.shape, 1)
    o_ref[0] = jnp.where(band, acc, NEG_INF)
    o_ref[1] = jnp.where(band & (kj >= ATT_BLOCK), acc, NEG_INF)


def bias_tables(rel_bias):
    qi = jnp.arange(ATT_BLOCK)[:, None]
    kj = jnp.arange(2 * ATT_BLOCK)[None, :]
    dist = qi - kj + ATT_BLOCK
    buckets, bands = [], []
    for window, dil in ATT_GROUPS:
        bands.append(((dist >= 0) & (dist <= window // dil)).astype(jnp.int32))
        buckets.append(_t5_bucket(jnp.maximum(dist, 0) * dil).astype(jnp.int32))
    buckets = jnp.stack(buckets)
    bands = jnp.stack(bands)
    blk = (None, ATT_BLOCK, 2 * ATT_BLOCK)
    return pl.pallas_call(
        _bias_body,
        grid=(N_GROUPS, HEADS_PER_GROUP),
        in_specs=[pl.BlockSpec(memory_space=pltpu.SMEM),
                  pl.BlockSpec(blk, lambda g, h: (g, 0, 0)),
                  pl.BlockSpec(blk, lambda g, h: (g, 0, 0))],
        out_specs=pl.BlockSpec((None, None, 2, ATT_BLOCK, 2 * ATT_BLOCK), lambda g, h: (g, h, 0, 0, 0)),
        out_shape=jax.ShapeDtypeStruct((N_GROUPS, HEADS_PER_GROUP, 2, ATT_BLOCK, 2 * ATT_BLOCK), F32),
        compiler_params=_params(("arbitrary", "arbitrary")),
        name="bias_tables",
    )(rel_bias.astype(F32), buckets, bands)


def _attn_body(q_ref, kc_ref, kp_ref, vc_ref, vp_ref, bm_ref, o_ref, st_ref, k_all, v_all, *, tq):
    i = pl.program_id(2)
    nsub = tq // ATT_BLOCK
    k_all[0:ATT_BLOCK] = kp_ref[...]
    k_all[ATT_BLOCK:] = kc_ref[...]
    v_all[0:ATT_BLOCK] = vp_ref[...]
    v_all[ATT_BLOCK:] = vc_ref[...]
    lane = lax.broadcasted_iota(jnp.int32, (ATT_BLOCK, LANES), 1)
    lo = lane < HEAD_DIM
    nt_dims = (((1,), (1,)), ((), ()))

    def sub_block(s, carry):
        r0 = pl.multiple_of(s * ATT_BLOCK, ATT_BLOCK)
        variant = jnp.where((i == 0) & (s == 0), 1, 0)
        st = jnp.zeros((ATT_BLOCK, LANES), F32)
        for j in range(HEADS_PER_GROUP // 2):
            cs = slice(j * LANES, (j + 1) * LANES)
            qp = q_ref[pl.ds(r0, ATT_BLOCK), cs] * jnp.asarray(HEAD_DIM ** -0.5, BF16)
            kp = k_all[pl.ds(r0, 2 * ATT_BLOCK), cs]
            vp = v_all[pl.ds(r0, 2 * ATT_BLOCK), cs]
            halves = []
            for hh in range(2):
                h = 2 * j + hh
                qh = jnp.where(lo if hh == 0 else ~lo, qp, jnp.zeros_like(qp))
                sc = lax.dot_general(qh, kp, nt_dims, preferred_element_type=F32) + bm_ref[h, variant]
                m = jnp.max(sc, axis=-1, keepdims=True)
                p = jnp.exp(sc - m)
                den = jnp.sum(p, axis=-1, keepdims=True)
                pv = jnp.dot(p.astype(BF16), vp, preferred_element_type=F32)
                halves.append(pv * (1.0 / den))
                st = jnp.where(lane == h, m, st)
                st = jnp.where(lane == HEADS_PER_GROUP + h, den, st)
            o_ref[pl.ds(r0, ATT_BLOCK), cs] = jnp.where(lo, halves[0], halves[1])
        st_ref[pl.ds(r0, ATT_BLOCK), :] = st
        return carry

    lax.fori_loop(0, nsub, sub_block, 0)


def attention_group(qkv, bm_g, gi, tq=512):
    B, dil, L, _ = qkv.shape
    tq = min(tq, L)
    sub = tq // ATT_BLOCK

    def cur(col):
        return pl.BlockSpec((None, None, tq, GROUP_WIDTH), lambda b, r, i: (b, r, i, col))

    def prev(col):
        return pl.BlockSpec((None, None, ATT_BLOCK, GROUP_WIDTH),
                            lambda b, r, i: (b, r, jnp.maximum(i * sub - 1, 0), col))

    return pl.pallas_call(
        functools.partial(_attn_body, tq=tq),
        grid=(B, dil, L // tq),
        in_specs=[cur(0), cur(1), prev(1), cur(2), prev(2),
                  pl.BlockSpec((HEADS_PER_GROUP, 2, ATT_BLOCK, 2 * ATT_BLOCK), lambda b, r, i: (0, 0, 0, 0))],
        out_specs=[pl.BlockSpec((None, None, tq, GROUP_WIDTH), lambda b, r, i: (b, r, i, 0)),
                   pl.BlockSpec((None, None, tq, LANES), lambda b, r, i: (b, r, i, 0))],
        out_shape=[jax.ShapeDtypeStruct((B, dil, L, GROUP_WIDTH), F32),
                   jax.ShapeDtypeStruct((B, dil, L, LANES), F32)],
        scratch_shapes=[pltpu.VMEM((ATT_BLOCK + tq, GROUP_WIDTH), BF16),
                        pltpu.VMEM((ATT_BLOCK + tq, GROUP_WIDTH), BF16)],
        compiler_params=_params(("arbitrary", "arbitrary", "arbitrary")),
        name=f"attention_g{gi}",
    )(qkv, qkv, qkv, qkv, qkv, bm_g)


def _split_bf16(x):
    hi = x.astype(BF16)
    lo = (x - hi.astype(F32)).astype(BF16)
    return hi, lo


def _load_token_tiles(ref, n):
    return jnp.concatenate([ref[pl.ds(c, n, stride=SUBLANES), :] for c in range(D_MODEL // LANES)], axis=1)


def _store_token_tiles(ref, x, n):
    for c in range(D_MODEL // LANES):
        ref[pl.ds(c, n, stride=SUBLANES), :] = x[:, c * LANES:(c + 1) * LANES]


def _to_token_order(blk_ref, tok_ref, dil, tm):
    n = tm // dil
    for r in range(dil):
        for c in range(tok_ref.shape[0]):
            tok_ref[c, pl.ds(r, n, stride=dil), :] = blk_ref[r, :, c * LANES:(c + 1) * LANES]


def _mixer_out_body(o0, o1, o2, s0, s1, s2, uv_ref, ug_ref, uvh_ref, ugh_ref, ga_ref, gc_ref, x_ref,
                    wdw_ref, bdw_ref, cg_ref, cb_ref, woa_ref, woc_ref, wout_ref, g1_ref, b1_ref,
                    h_ref, ht_ref, glu_ref, dw_ref, shift_ref, tok_o1, tok_s1, tok_o2, tok_s2, *, tm, chunk):
    i = pl.program_id(1)

    ncol = GROUP_WIDTH // LANES
    _to_token_order(o1, tok_o1, ATT_GROUPS[1][1], tm)
    _to_token_order(s1, tok_s1, ATT_GROUPS[1][1], tm)
    _to_token_order(o2, tok_o2, ATT_GROUPS[2][1], tm)
    _to_token_order(s2, tok_s2, ATT_GROUPS[2][1], tm)
    outs = [o0[0],
            jnp.concatenate([tok_o1[c] for c in range(ncol)], axis=1),
            jnp.concatenate([tok_o2[c] for c in range(ncol)], axis=1)]
    sts = [s0[0], tok_s1[0], tok_s2[0]]

    mx = jnp.maximum(jnp.maximum(sts[0], sts[1]), sts[2])
    wts = [pltpu.roll(st, LANES - HEADS_PER_GROUP, axis=1) * jnp.exp(st - mx) for st in sts]
    wsum = wts[0] + wts[1] + wts[2]
    row = lax.broadcasted_iota(jnp.int32, (LANES, GROUP_WIDTH), 0)
    colh = lax.broadcasted_iota(jnp.int32, (LANES, GROUP_WIDTH), 1) // HEAD_DIM
    expand = (row == colh).astype(BF16)
    attn = jnp.zeros((tm, GROUP_WIDTH), F32)
    head_lane = lax.broadcasted_iota(jnp.int32, (tm, LANES), 1) < HEADS_PER_GROUP
    for wt, o in zip(wts, outs):
        c_hi, c_lo = _split_bf16(jnp.where(head_lane, wt / wsum, 0.0))
        c = (jnp.dot(c_hi, expand, preferred_element_type=F32)
             + jnp.dot(c_lo, expand, preferred_element_type=F32))
        attn = attn + c * o
    a_out = jnp.dot(attn.astype(BF16), woa_ref[...], preferred_element_type=F32)

    gh = uvh_ref[...].astype(F32) * _sigmoid(ugh_ref[...].astype(F32))
    glu_ref[0:CONV_HALO] = jnp.where(i == 0, 0.0, gh)
    glu_ref[CONV_HALO:] = uv_ref[...].astype(F32) * _sigmoid(ug_ref[...].astype(F32))
    first_tap = CONV_HALO - (CONV_WIDTH - 1)
    for b in range(1, SUBLANES):
        shift_ref[b - 1] = glu_ref[b:b + shift_ref.shape[1], :]

    for r0 in range(0, tm, chunk):
        acc = jnp.broadcast_to(bdw_ref[...], (chunk, CONV_CH))
        for j in range(CONV_WIDTH):
            a, b = divmod(first_tap + j, SUBLANES)
            lo_row = r0 + a * SUBLANES
            rows = glu_ref[lo_row:lo_row + chunk, :] if b == 0 else shift_ref[b - 1, lo_row:lo_row + chunk, :]
            acc = acc + wdw_ref[j:j + 1, :] * rows
        dw_ref[r0:r0 + chunk, :] = acc
    cn = _layer_norm(dw_ref[...], cg_ref[...], cb_ref[...])
    conv = cn * _sigmoid(cn)
    c_out = jnp.dot(conv.astype(BF16), woc_ref[...], preferred_element_type=F32)

    merged = (_sigmoid(ga_ref[...].astype(F32)) * a_out + _sigmoid(gc_ref[...].astype(F32)) * c_out)
    mix = jnp.dot(merged.astype(BF16), wout_ref[...], preferred_element_type=F32)
    h = _layer_norm(DEEPNORM_ALPHA * x_ref[...] + mix, g1_ref[...], b1_ref[...])
    h_ref[...] = h
    _store_token_tiles(ht_ref, h, tm)


def mixer_out(o_list, st_list, proj3, x3, w_dw, b_dw, cg, cb, woa_b, woc_b, wout_b, g1, b1, tm=256, chunk=32):
    B, S, _ = x3.shape
    tm = min(tm, S)
    halo_blocks = tm // CONV_HALO
    uv_col = QKV_WIDTH // CONV_CH
    ga_col = (QKV_WIDTH + 2 * CONV_CH) // D_MODEL
    ncol = GROUP_WIDTH // LANES

    def tile(width, col=0):
        return pl.BlockSpec((None, tm, width), lambda b, i: (b, i, col))

    def dilated(gi, width):
        dil = ATT_GROUPS[gi][1]
        return pl.BlockSpec((None, dil, tm // dil, width), lambda b, i: (b, 0, i, 0))

    def halo(col):
        return pl.BlockSpec((None, CONV_HALO, CONV_CH), lambda b, i: (b, jnp.maximum(i * halo_blocks - 1, 0), col))

    def whole(shape):
        return pl.BlockSpec(shape, lambda b, i: (0,) * len(shape))

    return pl.pallas_call(
        functools.partial(_mixer_out_body, tm=tm, chunk=chunk),
        grid=(B, S // tm),
        in_specs=[dilated(g, GROUP_WIDTH) for g in range(N_GROUPS)] + [dilated(g, LANES) for g in range(N_GROUPS)]
        + [tile(CONV_CH, uv_col), tile(CONV_CH, uv_col + 1), halo(uv_col), halo(uv_col + 1),
           tile(D_MODEL, ga_col), tile(D_MODEL, ga_col + 1), tile(D_MODEL),
           whole((CONV_WIDTH, CONV_CH)), whole((1, CONV_CH)), whole((1, CONV_CH)), whole((1, CONV_CH)),
           whole((GROUP_WIDTH, D_MODEL)), whole((CONV_CH, D_MODEL)), whole((D_MODEL, D_MODEL)),
           whole((1, D_MODEL)), whole((1, D_MODEL))],
        out_specs=[tile(D_MODEL), pl.BlockSpec((tm * SUBLANES, LANES), lambda b, i: (b * (S // tm) + i, 0))],
        out_shape=[jax.ShapeDtypeStruct((B, S, D_MODEL), F32),
                   jax.ShapeDtypeStruct((B * S * SUBLANES, LANES), F32)],
        scratch_shapes=[pltpu.VMEM((CONV_HALO + tm, CONV_CH), F32), pltpu.VMEM((tm, CONV_CH), F32),
                        pltpu.VMEM((SUBLANES - 1, tm + CONV_HALO - SUBLANES, CONV_CH), F32),
                        pltpu.VMEM((ncol, tm, LANES), F32), pltpu.VMEM((1, tm, LANES), F32),
                        pltpu.VMEM((ncol, tm, LANES), F32), pltpu.VMEM((1, tm, LANES), F32)],
        compiler_params=_params(("arbitrary", "arbitrary")),
        name="mixer_out",
    )(*o_list, *st_list, proj3, proj3, proj3, proj3, proj3, proj3, x3,
      w_dw, b_dw, cg, cb, woa_b, woc_b, wout_b, g1, b1)


def _router_body(h_ref, w_ref, b_ref, gates_ref, ids_ref, rank_ref, cnt_ref, carry_ref, tri_ref, *, tm):
    step = pl.program_id(0)

    @pl.when(step == 0)
    def _():
        carry_ref[...] = jnp.zeros_like(carry_ref)
        r_i = lax.broadcasted_iota(jnp.int32, (tm, tm), 0)
        c_i = lax.broadcasted_iota(jnp.int32, (tm, tm), 1)
        tri_ref[...] = (c_i < r_i).astype(BF16)

    h_hi, h_lo = _split_bf16(h_ref[...])
    w_hi, w_lo = _split_bf16(w_ref[...])
    logits = (jnp.dot(h_hi, w_hi, preferred_element_type=F32)
              + (jnp.dot(h_lo, w_hi, preferred_element_type=F32) + jnp.dot(h_hi, w_lo, preferred_element_type=F32))
              + b_ref[...])
    lane = lax.broadcasted_iota(jnp.int32, (tm, LANES), 1)
    lane_f = lane.astype(F32)
    work = jnp.where(lane < N_EXPERTS, logits, -jnp.inf)

    vals, hots = [], []
    ids = jnp.zeros((tm, LANES), F32)
    for k in range(TOP_K):
        v = jnp.max(work, axis=-1, keepdims=True)
        idx = jnp.min(jnp.where(work == v, lane_f, float(LANES)), axis=-1, keepdims=True)
        hot = lane_f == idx
        work = jnp.where(hot, -jnp.inf, work)
        ids = jnp.where(lane == k, idx, ids)
        vals.append(v)
        hots.append(hot)

    es = [jnp.exp(v - vals[0]) for v in vals]
    esum = es[0] + es[1] + es[2] + es[3]
    gates = jnp.zeros((tm, LANES), F32)
    for k in range(TOP_K):
        gates = jnp.where(lane == k, es[k] / esum, gates)

    sel = (hots[0] | hots[1] | hots[2] | hots[3])
    before = jnp.dot(tri_ref[...], sel.astype(BF16), preferred_element_type=F32) + carry_ref[...]
    rank = jnp.zeros((tm, LANES), jnp.int32)
    for k in range(TOP_K):
        rk = jnp.sum(jnp.where(hots[k], before, 0.0), axis=-1, keepdims=True)
        rank = jnp.where(lane == k, rk.astype(jnp.int32), rank)
    carry_ref[...] = carry_ref[...] + jnp.sum(sel.astype(F32), axis=0, keepdims=True)

    gates_ref[...] = gates
    ids_ref[...] = ids.astype(jnp.int32)
    rank_ref[...] = rank
    cnt_ref[...] = carry_ref[...].astype(jnp.int32)


def router(h2, w_router, b_router, tm=512):
    T = h2.shape[0]
    tm = min(tm, T)
    w_pad = jnp.pad(w_router.astype(F32), ((0, 0), (0, LANES - N_EXPERTS)))
    b_pad = jnp.pad(b_router.astype(F32), (0, LANES - N_EXPERTS)).reshape(1, LANES)
    tile = pl.BlockSpec((tm, LANES), lambda i: (i, 0))
    return pl.pallas_call(
        functools.partial(_router_body, tm=tm),
        grid=(T // tm,),
        in_specs=[pl.BlockSpec((tm, D_MODEL), lambda i: (i, 0)),
                  pl.BlockSpec((D_MODEL, LANES), lambda i: (0, 0)),
                  pl.BlockSpec((1, LANES), lambda i: (0, 0))],
        out_specs=[tile, tile, tile, pl.BlockSpec((1, LANES), lambda i: (0, 0))],
        out_shape=[jax.ShapeDtypeStruct((T, LANES), F32), jax.ShapeDtypeStruct((T, LANES), jnp.int32),
                   jax.ShapeDtypeStruct((T, LANES), jnp.int32), jax.ShapeDtypeStruct((1, LANES), jnp.int32)],
        scratch_shapes=[pltpu.VMEM((1, LANES), F32), pltpu.VMEM((tm, tm), BF16)],
        compiler_params=_params(("arbitrary",)),
        name="router",
    )(h2, w_pad, b_pad)


def _token_copy(src, dst, s_tok, d_tok, sem, n=1):
    rows = n * SUBLANES
    return pltpu.make_async_copy(src.at[pl.ds(pl.multiple_of(s_tok * SUBLANES, SUBLANES), rows), :],
                                 dst.at[pl.ds(pl.multiple_of(d_tok * SUBLANES, SUBLANES), rows), :], sem)


def _dispatch_body(zstart_ref, zcount_ref, n_used_ref, pos_ref, h_ref, xs_ref, zeros_ref, sem, zsem, *, tm, nb):
    half = EXPERT_BLOCK // 2
    bits = [1 << s for s in reversed(range(half.bit_length()))]

    def zero_fill(e, wait):
        start, count = zstart_ref[e], zcount_ref[e]
        for bit in bits:
            @pl.when((count & bit) != 0)
            def _():
                cp = _token_copy(zeros_ref, xs_ref, 0, 0 if wait else start + (count & ~(2 * bit - 1)), zsem, bit)
                cp.wait() if wait else cp.start()

    def zero_tail(blk, wait):
        for part in range(2):
            cp = _token_copy(zeros_ref, xs_ref, 0, 0 if wait else blk * EXPERT_BLOCK + part * half, zsem, half)
            cp.wait() if wait else cp.start()

    @pl.when(pl.program_id(0) == 0)
    def _():
        zeros_ref[...] = jnp.zeros_like(zeros_ref)
        lax.fori_loop(0, N_EXPERTS, lambda e, c: (zero_fill(e, False), c)[1], 0)
        lax.fori_loop(n_used_ref[0], nb, lambda blk, c: (zero_tail(blk, False), c)[1], 0)

    def issue(t, c):
        for k in range(TOP_K):
            _token_copy(h_ref, xs_ref, t, pos_ref[k, t], sem).start(priority=k % 2)
        return c

    lax.fori_loop(0, tm, issue, 0)
    for k in range(TOP_K):
        _token_copy(h_ref, xs_ref, 0, 0, sem, tm).wait()

    @pl.when(pl.program_id(0) == 0)
    def _():
        lax.fori_loop(0, N_EXPERTS, lambda e, c: (zero_fill(e, True), c)[1], 0)
        lax.fori_loop(n_used_ref[0], nb, lambda blk, c: (zero_tail(blk, True), c)[1], 0)


def dispatch(ht, pos_t, zstart, zcount, n_used, n_rows, tm=512):
    T = ht.shape[0] // SUBLANES
    tm = min(tm, T)
    grid_spec = pltpu.PrefetchScalarGridSpec(
        num_scalar_prefetch=3,
        grid=(T // tm,),
        in_specs=[pl.BlockSpec((TOP_K, tm), lambda i, zs, zc, nu: (0, i), memory_space=pltpu.SMEM),
                  pl.BlockSpec((tm * SUBLANES, LANES), lambda i, zs, zc, nu: (i, 0))],
        out_specs=pl.BlockSpec(memory_space=pl.ANY),
        scratch_shapes=[pltpu.VMEM((EXPERT_BLOCK // 2 * SUBLANES, LANES), F32), pltpu.SemaphoreType.DMA,
                        pltpu.SemaphoreType.DMA],
    )
    return pl.pallas_call(
        functools.partial(_dispatch_body, tm=tm, nb=n_rows // EXPERT_BLOCK),
        grid_spec=grid_spec,
        out_shape=jax.ShapeDtypeStruct((n_rows * SUBLANES, LANES), F32),
        compiler_params=_params(("arbitrary",)),
        name="dispatch",
    )(zstart, zcount, n_used, pos_t, ht)


def _combine_body(pos_ref, pos_next_ref, gates_ref, h_ref, g2_ref, b2_ref, ys_ref, o_ref, buf, sems, *, tm):
    i = pl.program_id(0)
    slot = i % 2

    def issue(p_ref, s):
        def body(t, c):
            for k in range(TOP_K):
                _token_copy(ys_ref, buf.at[s, k], p_ref[k, t], t, sems.at[s]).start(priority=k % 2)
            return c

        lax.fori_loop(0, tm, body, 0)

    @pl.when(i == 0)
    def _():
        issue(pos_ref, 0)

    @pl.when(i + 1 < pl.num_programs(0))
    def _():
        issue(pos_next_ref, 1 - slot)

    for k in range(TOP_K):
        _token_copy(ys_ref, buf.at[slot, k], 0, 0, sems.at[slot], tm).wait()

    gates = gates_ref[...]
    ffn = gates[:, 0:1] * _load_token_tiles(buf.at[slot, 0], tm)
    for k in range(1, TOP_K):
        ffn = ffn + gates[:, k:k + 1] * _load_token_tiles(buf.at[slot, k], tm)
    o_ref[...] = _layer_norm(DEEPNORM_ALPHA * h_ref[...] + ffn, g2_ref[...], b2_ref[...])


def combine(ys, pos_t, gates_pad, h2, g2, b2, tm=256):
    T = h2.shape[0]
    tm = min(tm, T)
    last = T // tm - 1
    return pl.pallas_call(
        functools.partial(_combine_body, tm=tm),
        grid=(T // tm,),
        in_specs=[pl.BlockSpec((TOP_K, tm), lambda i: (0, i), memory_space=pltpu.SMEM),
                  pl.BlockSpec((TOP_K, tm), lambda i: (0, jnp.minimum(i + 1, last)), memory_space=pltpu.SMEM),
                  pl.BlockSpec((tm, LANES), lambda i: (i, 0)),
                  pl.BlockSpec((tm, D_MODEL), lambda i: (i, 0)),
                  pl.BlockSpec((1, D_MODEL), lambda i: (0, 0)),
                  pl.BlockSpec((1, D_MODEL), lambda i: (0, 0)),
                  pl.BlockSpec(memory_space=pl.ANY)],
        out_specs=pl.BlockSpec((tm, D_MODEL), lambda i: (i, 0)),
        out_shape=jax.ShapeDtypeStruct((T, D_MODEL), F32),
        scratch_shapes=[pltpu.VMEM((2, TOP_K, tm * SUBLANES, LANES), F32), pltpu.SemaphoreType.DMA((2,))],
        compiler_params=_params(("arbitrary",)),
        name="combine",
    )(pos_t, pos_t, gates_pad, h2, g2, b2, ys)


def _experts_body(blk_e, blk_src, n_used, xs_ref, wgu_ref, bgu_ref, wdn_ref, bdn_ref, ys_ref, wgu_b, wdn_b):
    del blk_src
    i = pl.program_id(0)
    used = i < n_used[0]

    @pl.when((i == 0) | (blk_e[i] != blk_e[jnp.maximum(i - 1, 0)]))
    def _():
        wgu_b[...] = wgu_ref[...].astype(BF16)
        wdn_b[...] = wdn_ref[...].astype(BF16)

    @pl.when(used)
    def _():
        x = _load_token_tiles(xs_ref, EXPERT_BLOCK).astype(BF16)
        hgu = jnp.dot(x, wgu_b[...], preferred_element_type=F32) + bgu_ref[...]
        gate = jnp.minimum(hgu[:, :D_FF], SWIGLU_LIMIT)
        up = jnp.clip(hgu[:, D_FF:], -SWIGLU_LIMIT, SWIGLU_LIMIT)
        act = (up + 1.0) * gate * _sigmoid(SWIGLU_ALPHA * gate)
        y = jnp.dot(act.astype(BF16), wdn_b[...], preferred_element_type=F32) + bdn_ref[...]
        _store_token_tiles(ys_ref, y, EXPERT_BLOCK)

    @pl.when(jnp.logical_not(used))
    def _():
        ys_ref[...] = jnp.zeros_like(ys_ref)


def experts(xs, blk_e, blk_src, n_used, wgu, bgu, wdn, bdn):
    n_rows = xs.shape[0] // SUBLANES
    nb = n_rows // EXPERT_BLOCK
    rows = EXPERT_BLOCK * SUBLANES
    grid_spec = pltpu.PrefetchScalarGridSpec(
        num_scalar_prefetch=3,
        grid=(nb,),
        in_specs=[pl.BlockSpec((rows, LANES), lambda i, e, s, n: (s[i], 0)),
                  pl.BlockSpec((None, D_MODEL, 2 * D_FF), lambda i, e, s, n: (e[i], 0, 0)),
                  pl.BlockSpec((None, 1, 2 * D_FF), lambda i, e, s, n: (e[i], 0, 0)),
                  pl.BlockSpec((None, D_FF, D_MODEL), lambda i, e, s, n: (e[i], 0, 0)),
                  pl.BlockSpec((None, 1, D_MODEL), lambda i, e, s, n: (e[i], 0, 0))],
        out_specs=pl.BlockSpec((rows, LANES), lambda i, e, s, n: (i, 0)),
        scratch_shapes=[pltpu.VMEM((D_MODEL, 2 * D_FF), BF16), pltpu.VMEM((D_FF, D_MODEL), BF16)],
    )
    return pl.pallas_call(
        _experts_body,
        grid_spec=grid_spec,
        out_shape=jax.ShapeDtypeStruct((n_rows * SUBLANES, LANES), F32),
        compiler_params=_params(("arbitrary",)),
        name="experts",
    )(blk_e, blk_src, n_used, xs, wgu, bgu, wdn, bdn)


def routing_layout(ids, rank, counts, n_blocks):
    padded = (counts + EXPERT_BLOCK - 1) // EXPERT_BLOCK * EXPERT_BLOCK
    pad_ends = jnp.cumsum(padded)
    pad_starts = pad_ends - padded
    pos_t = (pad_starts[ids] + rank).T.astype(jnp.int32)
    n_used = jnp.maximum(pad_ends[-1] // EXPERT_BLOCK, 1).astype(jnp.int32)
    blk = jnp.minimum(jnp.arange(n_blocks, dtype=jnp.int32), n_used - 1)
    blk_e = jnp.sum(pad_ends[None, :] <= (blk * EXPERT_BLOCK)[:, None], axis=1)
    blk_e = jnp.minimum(blk_e, N_EXPERTS - 1).astype(jnp.int32)
    zstart = (pad_starts + counts).astype(jnp.int32)
    zcount = (padded - counts).astype(jnp.int32)
    return pos_t, blk_e, blk, n_used.reshape(1), zstart, zcount


def _dilate(xb, B, S, dil):
    D = xb.shape[-1]
    return xb.reshape(B, S // dil, dil, D).transpose(0, 2, 1, 3).reshape(B * S, D)


def kernel(x, w_in, rel_bias, w_dw, b_dw, conv_ln_g, conv_ln_b, w_o_attn, w_o_conv, w_out, ln1_g, ln1_b,
           w_router, b_router, w_gate_up, b_gate_up, w_down, b_down, ln2_g, ln2_b):
    B, S, D = x.shape
    T = B * S
    h = x
    bm = bias_tables(rel_bias)
    q_off, k_off, v_off, rest = 0, QKV_WIDTH, 2 * QKV_WIDTH, 3 * QKV_WIDTH
    for l in range(DEPTH):
        wb = w_in[l].astype(BF16)

        def group_cols(gi):
            return [wb[:, off + gi * GROUP_WIDTH:off + (gi + 1) * GROUP_WIDTH] for off in (q_off, k_off, v_off)]

        xb = h.reshape(T, D).astype(BF16)
        w_main = jnp.concatenate(group_cols(0) + [wb[:, rest:]], axis=1)
        proj = in_proj(xb, w_main, w_main.shape[1] // 2, "in_proj_main").reshape(B, S, w_main.shape[1])
        o_list, st_list = [], []
        for gi in range(N_GROUPS):
            dil = ATT_GROUPS[gi][1]
            if dil == 1:
                qkv = proj.reshape(B, 1, S, proj.shape[-1])
            else:
                w_g = jnp.concatenate(group_cols(gi), axis=1)
                qkv = in_proj(_dilate(xb, B, S, dil), w_g, w_g.shape[1], f"in_proj_g{gi}")
                qkv = qkv.reshape(B, dil, S // dil, w_g.shape[1])
            o_g, st_g = attention_group(qkv, bm[gi], gi)
            o_list.append(o_g)
            st_list.append(st_g)
        h1, h1_tiles = mixer_out(o_list, st_list, proj, h,
                       w_dw[l].reshape(CONV_WIDTH, CONV_CH), b_dw[l].reshape(1, CONV_CH),
                       conv_ln_g[l].reshape(1, CONV_CH), conv_ln_b[l].reshape(1, CONV_CH),
                       w_o_attn[l].astype(BF16), w_o_conv[l].astype(BF16), w_out[l].astype(BF16),
                       ln1_g[l].reshape(1, D), ln1_b[l].reshape(1, D))
        h2 = h1.reshape(T, D)
        gates_pad, ids_pad, rank_pad, counts = router(h2, w_router[l], b_router[l])
        n_rows = T * TOP_K + N_EXPERTS * EXPERT_BLOCK
        pos_t, blk_e, blk_src, n_used, zstart, zcount = routing_layout(
            ids_pad[:, :TOP_K], rank_pad[:, :TOP_K], counts[0, :N_EXPERTS], n_rows // EXPERT_BLOCK)
        xs = dispatch(h1_tiles, pos_t, zstart, zcount, n_used, n_rows)
        ys = experts(xs, blk_e, blk_src, n_used,
                     w_gate_up[l], b_gate_up[l].reshape(N_EXPERTS, 1, 2 * D_FF),
                     w_down[l], b_down[l].reshape(N_EXPERTS, 1, D))
        out = combine(ys, pos_t, gates_pad, h2, ln2_g[l].reshape(1, D), ln2_b[l].reshape(1, D))
        h = out.reshape(B, S, D)
    return h
```

```python
import functools
import math

import jax
import jax.numpy as jnp
from jax import lax
from jax.experimental import pallas as pl
from jax.experimental.pallas import tpu as pltpu

D_MODEL = 1024
ATT_GROUPS = ((128, 1), (512, 4), (2048, 16))
N_GROUPS = len(ATT_GROUPS)
HEADS_PER_GROUP = 8
HEAD_DIM = 64
GROUP_WIDTH = HEADS_PER_GROUP * HEAD_DIM
QKV_WIDTH = N_GROUPS * GROUP_WIDTH
ATT_BLOCK = 128
N_BUCKETS = 32
MAX_DISTANCE = 2048
CONV_CH = 768
CONV_WIDTH = 31
CONV_HALO = 32
N_EXPERTS = 32
TOP_K = 4
D_FF = 1024
SWIGLU_LIMIT = 7.0
SWIGLU_ALPHA = 1.702
IN_WIDTH = 3 * QKV_WIDTH + 2 * CONV_CH + 2 * D_MODEL
LN_EPS = 1e-5
NEG_INF = -1e30
DEPTH = 1
DEEPNORM_ALPHA = (2 * DEPTH) ** 0.25

LANES = 128
SUBLANES = 8
EXPERT_BLOCK = 512
VMEM_LIMIT = 56 * 1024 * 1024

F32 = jnp.float32
BF16 = jnp.bfloat16


def _params(sem, vmem=VMEM_LIMIT):
    return pltpu.CompilerParams(dimension_semantics=sem, vmem_limit_bytes=vmem)


def _sigmoid(x):
    return 0.5 * jnp.tanh(0.5 * x) + 0.5


def _layer_norm(x, g, b):
    mu = jnp.mean(x, axis=-1, keepdims=True)
    xc = x - mu
    var = jnp.mean(xc * xc, axis=-1, keepdims=True)
    return xc * lax.rsqrt(var + LN_EPS) * g + b


def _in_proj_body(x_ref, w_ref, o_ref):
    o_ref[...] = jnp.dot(x_ref[...], w_ref[...], preferred_element_type=F32).astype(o_ref.dtype)


def in_proj(xb, w_b, tn, name, tm=1024):
    T = xb.shape[0]
    N = w_b.shape[1]
    tm = min(tm, T)
    return pl.pallas_call(
        _in_proj_body,
        grid=(N // tn, T // tm),
        in_specs=[pl.BlockSpec((tm, D_MODEL), lambda n, m: (m, 0)),
                  pl.BlockSpec((D_MODEL, tn), lambda n, m: (0, n))],
        out_specs=pl.BlockSpec((tm, tn), lambda n, m: (m, n)),
        out_shape=jax.ShapeDtypeStruct((T, N), BF16),
        compiler_params=_params(("arbitrary", "arbitrary")),
        name=name,
    )(xb, w_b)


def _t5_bucket(dist):
    max_exact = N_BUCKETS // 2
    log_ratio = jnp.log(jnp.maximum(dist, max_exact).astype(F32) / max_exact) / math.log(MAX_DISTANCE / max_exact)
    large = jnp.minimum(max_exact + (log_ratio * (N_BUCKETS - max_exact)).astype(jnp.int32), N_BUCKETS - 1)
    return jnp.where(dist < max_exact, dist, large)


def _bias_body(tbl_ref, bucket_ref, band_ref, o_ref):
    g = pl.program_id(0)
    h = pl.program_id(1)
    col = g * HEADS_PER_GROUP + h
    bucket = bucket_ref[...]
    acc = jnp.zeros(bucket.shape, F32)
    for k in range(N_BUCKETS):
        acc = jnp.where(bucket == k, tbl_ref[k, col], acc)
    band = band_ref[...] != 0
    kj = lax.broadcasted_iota(jnp.int32, bucket.shape, 1)
    o_ref[0] = jnp.where(band, acc, NEG_INF)
    o_ref[1] = jnp.where(band & (kj >= ATT_BLOCK), acc, NEG_INF)


def bias_tables(rel_bias):
    qi = jnp.arange(ATT_BLOCK)[:, None]
    kj = jnp.arange(2 * ATT_BLOCK)[None, :]
    dist = qi - kj + ATT_BLOCK
    buckets, bands = [], []
    for window, dil in ATT_GROUPS:
        bands.append(((dist >= 0) & (dist <= window // dil)).astype(jnp.int32))
        buckets.append(_t5_bucket(jnp.maximum(dist, 0) * dil).astype(jnp.int32))
    buckets = jnp.stack(buckets)
    bands = jnp.stack(bands)
    blk = (None, ATT_BLOCK, 2 * ATT_BLOCK)
    return pl.pallas_call(
        _bias_body,
        grid=(N_GROUPS, HEADS_PER_GROUP),
        in_specs=[pl.BlockSpec(memory_space=pltpu.SMEM),
                  pl.BlockSpec(blk, lambda g, h: (g, 0, 0)),
                  pl.BlockSpec(blk, lambda g, h: (g, 0, 0))],
        out_specs=pl.BlockSpec((None, None, 2, ATT_BLOCK, 2 * ATT_BLOCK), lambda g, h: (g, h, 0, 0, 0)),
        out_shape=jax.ShapeDtypeStruct((N_GROUPS, HEADS_PER_GROUP, 2, ATT_BLOCK, 2 * ATT_BLOCK), F32),
        compiler_params=_params(("arbitrary", "arbitrary")),
        name="bias_tables",
    )(rel_bias.astype(F32), buckets, bands)


def _attn_body(q_ref, kc_ref, kp_ref, vc_ref, vp_ref, bm_ref, o_ref, st_ref, k_all, v_all, s_buf, *, tq):
    i = pl.program_id(2)
    nsub = tq // ATT_BLOCK
    k_all[0:ATT_BLOCK] = kp_ref[...]
    k_all[ATT_BLOCK:] = kc_ref[...]
    v_all[0:ATT_BLOCK] = vp_ref[...]
    v_all[ATT_BLOCK:] = vc_ref[...]
    lane = lax.broadcasted_iota(jnp.int32, (ATT_BLOCK, LANES), 1)
    lo = lane < HEAD_DIM
    nt_dims = (((1,), (1,)), ((), ()))
    first = jnp.where(i == 0, 1, 0)

    def scores(s, slot):
        r0 = s * ATT_BLOCK
        for j in range(HEADS_PER_GROUP // 2):
            cs = slice(j * LANES, (j + 1) * LANES)
            qp = q_ref[r0:r0 + ATT_BLOCK, cs] * jnp.asarray(HEAD_DIM ** -0.5, BF16)
            kp = k_all[r0:r0 + 2 * ATT_BLOCK, cs]
            for hh in range(2):
                h = 2 * j + hh
                qh = jnp.where(lo if hh == 0 else ~lo, qp, jnp.zeros_like(qp))
                bias = bm_ref[h, first] if s == 0 else bm_ref[h, 0]
                s_buf[slot, h] = lax.dot_general(qh, kp, nt_dims, preferred_element_type=F32) + bias

    def softmax_pv(s, slot):
        r0 = s * ATT_BLOCK
        st = jnp.zeros((ATT_BLOCK, LANES), F32)
        for j in range(HEADS_PER_GROUP // 2):
            cs = slice(j * LANES, (j + 1) * LANES)
            vp = v_all[r0:r0 + 2 * ATT_BLOCK, cs]
            halves = []
            for hh in range(2):
                h = 2 * j + hh
                sc = s_buf[slot, h]
                m = jnp.max(sc, axis=-1, keepdims=True)
                p = jnp.exp(sc - m)
                den = jnp.sum(p, axis=-1, keepdims=True)
                pv = jnp.dot(p.astype(BF16), vp, preferred_element_type=F32)
                halves.append(pv * (1.0 / den))
                st = jnp.where(lane == h, m, st)
                st = jnp.where(lane == HEADS_PER_GROUP + h, den, st)
            o_ref[r0:r0 + ATT_BLOCK, cs] = jnp.where(lo, halves[0], halves[1])
        st_ref[r0:r0 + ATT_BLOCK, :] = st

    scores(0, 0)
    for s in range(nsub):
        if s + 1 < nsub:
            scores(s + 1, (s + 1) % 2)
        softmax_pv(s, s % 2)


def attention_group(qkv, bm_g, gi, tq=512):
    B, dil, L, _ = qkv.shape
    tq = min(tq, L)
    sub = tq // ATT_BLOCK

    def cur(col):
        return pl.BlockSpec((None, None, tq, GROUP_WIDTH), lambda b, r, i: (b, r, i, col))

    def prev(col):
        return pl.BlockSpec((None, None, ATT_BLOCK, GROUP_WIDTH),
                            lambda b, r, i: (b, r, jnp.maximum(i * sub - 1, 0), col))

    return pl.pallas_call(
        functools.partial(_attn_body, tq=tq),
        grid=(B, dil, L // tq),
        in_specs=[cur(0), cur(1), prev(1), cur(2), prev(2),
                  pl.BlockSpec((HEADS_PER_GROUP, 2, ATT_BLOCK, 2 * ATT_BLOCK), lambda b, r, i: (0, 0, 0, 0))],
        out_specs=[pl.BlockSpec((None, None, tq, GROUP_WIDTH), lambda b, r, i: (b, r, i, 0)),
                   pl.BlockSpec((None, None, tq, LANES), lambda b, r, i: (b, r, i, 0))],
        out_shape=[jax.ShapeDtypeStruct((B, dil, L, GROUP_WIDTH), F32),
                   jax.ShapeDtypeStruct((B, dil, L, LANES), F32)],
        scratch_shapes=[pltpu.VMEM((ATT_BLOCK + tq, GROUP_WIDTH), BF16),
                        pltpu.VMEM((ATT_BLOCK + tq, GROUP_WIDTH), BF16),
                        pltpu.VMEM((2, HEADS_PER_GROUP, ATT_BLOCK, 2 * ATT_BLOCK), F32)],
        compiler_params=_params(("arbitrary", "arbitrary", "arbitrary")),
        name=f"attention_g{gi}",
    )(qkv, qkv, qkv, qkv, qkv, bm_g)


def _split_bf16(x):
    hi = x.astype(BF16)
    lo = (x - hi.astype(F32)).astype(BF16)
    return hi, lo


def _load_token_tiles(ref, n):
    return jnp.concatenate([ref[pl.ds(c, n, stride=SUBLANES), :] for c in range(D_MODEL // LANES)], axis=1)


def _store_token_tiles(ref, x, n):
    for c in range(D_MODEL // LANES):
        ref[pl.ds(c, n, stride=SUBLANES), :] = x[:, c * LANES:(c + 1) * LANES]


def _to_token_order(blk_ref, tok_ref, dil, tm):
    n = tm // dil
    for r in range(dil):
        for c in range(tok_ref.shape[0]):
            tok_ref[c, pl.ds(r, n, stride=dil), :] = blk_ref[r, :, c * LANES:(c + 1) * LANES]


def _mixer_out_body(o0, o1, o2, s0, s1, s2, uv_ref, ug_ref, uvh_ref, ugh_ref, ga_ref, gc_ref, x_ref,
                    wdw_ref, bdw_ref, cg_ref, cb_ref, woa_ref, woc_ref, wout_ref, g1_ref, b1_ref,
                    h_ref, ht_ref, glu_ref, dw_ref, shift_ref, tok_o1, tok_s1, tok_o2, tok_s2, *, tm, chunk):
    i = pl.program_id(1)

    ncol = GROUP_WIDTH // LANES
    _to_token_order(o1, tok_o1, ATT_GROUPS[1][1], tm)
    _to_token_order(s1, tok_s1, ATT_GROUPS[1][1], tm)
    _to_token_order(o2, tok_o2, ATT_GROUPS[2][1], tm)
    _to_token_order(s2, tok_s2, ATT_GROUPS[2][1], tm)
    outs = [o0[0],
            jnp.concatenate([tok_o1[c] for c in range(ncol)], axis=1),
            jnp.concatenate([tok_o2[c] for c in range(ncol)], axis=1)]
    sts = [s0[0], tok_s1[0], tok_s2[0]]

    mx = jnp.maximum(jnp.maximum(sts[0], sts[1]), sts[2])
    wts = [pltpu.roll(st, LANES - HEADS_PER_GROUP, axis=1) * jnp.exp(st - mx) for st in sts]
    wsum = wts[0] + wts[1] + wts[2]
    row = lax.broadcasted_iota(jnp.int32, (LANES, GROUP_WIDTH), 0)
    colh = lax.broadcasted_iota(jnp.int32, (LANES, GROUP_WIDTH), 1) // HEAD_DIM
    expand = (row == colh).astype(BF16)
    attn = jnp.zeros((tm, GROUP_WIDTH), F32)
    head_lane = lax.broadcasted_iota(jnp.int32, (tm, LANES), 1) < HEADS_PER_GROUP
    for wt, o in zip(wts, outs):
        c_hi, c_lo = _split_bf16(jnp.where(head_lane, wt / wsum, 0.0))
        c = (jnp.dot(c_hi, expand, preferred_element_type=F32)
             + jnp.dot(c_lo, expand, preferred_element_type=F32))
        attn = attn + c * o
    a_out = jnp.dot(attn.astype(BF16), woa_ref[...], preferred_element_type=F32)

    gh = uvh_ref[...].astype(F32) * _sigmoid(ugh_ref[...].astype(F32))
    glu_ref[0:CONV_HALO] = jnp.where(i == 0, 0.0, gh)
    glu_ref[CONV_HALO:] = uv_ref[...].astype(F32) * _sigmoid(ug_ref[...].astype(F32))
    first_tap = CONV_HALO - (CONV_WIDTH - 1)
    for b in range(1, SUBLANES):
        shift_ref[b - 1] = glu_ref[b:b + shift_ref.shape[1], :]

    for r0 in range(0, tm, chunk):
        acc = jnp.broadcast_to(bdw_ref[...], (chunk, CONV_CH))
        for j in range(CONV_WIDTH):
            a, b = divmod(first_tap + j, SUBLANES)
            lo_row = r0 + a * SUBLANES
            rows = glu_ref[lo_row:lo_row + chunk, :] if b == 0 else shift_ref[b - 1, lo_row:lo_row + chunk, :]
            acc = acc + wdw_ref[j:j + 1, :] * rows
        dw_ref[r0:r0 + chunk, :] = acc
    cn = _layer_norm(dw_ref[...], cg_ref[...], cb_ref[...])
    conv = cn * _sigmoid(cn)
    c_out = jnp.dot(conv.astype(BF16), woc_ref[...], preferred_element_type=F32)

    merged = (_sigmoid(ga_ref[...].astype(F32)) * a_out + _sigmoid(gc_ref[...].astype(F32)) * c_out)
    mix = jnp.dot(merged.astype(BF16), wout_ref[...], preferred_element_type=F32)
    h = _layer_norm(DEEPNORM_ALPHA * x_ref[...] + mix, g1_ref[...], b1_ref[...])
    h_ref[...] = h
    _store_token_tiles(ht_ref, h, tm)


def mixer_out(o_list, st_list, proj3, x3, w_dw, b_dw, cg, cb, woa_b, woc_b, wout_b, g1, b1, tm=256, chunk=32):
    B, S, _ = x3.shape
    tm = min(tm, S)
    halo_blocks = tm // CONV_HALO
    uv_col = QKV_WIDTH // CONV_CH
    ga_col = (QKV_WIDTH + 2 * CONV_CH) // D_MODEL
    ncol = GROUP_WIDTH // LANES

    def tile(width, col=0):
        return pl.BlockSpec((None, tm, width), lambda b, i: (b, i, col))

    def dilated(gi, width):
        dil = ATT_GROUPS[gi][1]
        return pl.BlockSpec((None, dil, tm // dil, width), lambda b, i: (b, 0, i, 0))

    def halo(col):
        return pl.BlockSpec((None, CONV_HALO, CONV_CH), lambda b, i: (b, jnp.maximum(i * halo_blocks - 1, 0), col))

    def whole(shape):
        return pl.BlockSpec(shape, lambda b, i: (0,) * len(shape))

    return pl.pallas_call(
        functools.partial(_mixer_out_body, tm=tm, chunk=chunk),
        grid=(B, S // tm),
        in_specs=[dilated(g, GROUP_WIDTH) for g in range(N_GROUPS)] + [dilated(g, LANES) for g in range(N_GROUPS)]
        + [tile(CONV_CH, uv_col), tile(CONV_CH, uv_col + 1), halo(uv_col), halo(uv_col + 1),
           tile(D_MODEL, ga_col), tile(D_MODEL, ga_col + 1), tile(D_MODEL),
           whole((CONV_WIDTH, CONV_CH)), whole((1, CONV_CH)), whole((1, CONV_CH)), whole((1, CONV_CH)),
           whole((GROUP_WIDTH, D_MODEL)), whole((CONV_CH, D_MODEL)), whole((D_MODEL, D_MODEL)),
           whole((1, D_MODEL)), whole((1, D_MODEL))],
        out_specs=[tile(D_MODEL), pl.BlockSpec((tm * SUBLANES, LANES), lambda b, i: (b * (S // tm) + i, 0))],
        out_shape=[jax.ShapeDtypeStruct((B, S, D_MODEL), F32),
                   jax.ShapeDtypeStruct((B * S * SUBLANES, LANES), F32)],
        scratch_shapes=[pltpu.VMEM((CONV_HALO + tm, CONV_CH), F32), pltpu.VMEM((tm, CONV_CH), F32),
                        pltpu.VMEM((SUBLANES - 1, tm + CONV_HALO - SUBLANES, CONV_CH), F32),
                        pltpu.VMEM((ncol, tm, LANES), F32), pltpu.VMEM((1, tm, LANES), F32),
                        pltpu.VMEM((ncol, tm, LANES), F32), pltpu.VMEM((1, tm, LANES), F32)],
        compiler_params=_params(("arbitrary", "arbitrary")),
        name="mixer_out",
    )(*o_list, *st_list, proj3, proj3, proj3, proj3, proj3, proj3, x3,
      w_dw, b_dw, cg, cb, woa_b, woc_b, wout_b, g1, b1)


def _router_body(h_ref, w_ref, b_ref, gates_ref, ids_ref, rank_ref, cnt_ref, carry_ref, tri_ref, *, tm):
    step = pl.program_id(0)

    @pl.when(step == 0)
    def _():
        carry_ref[...] = jnp.zeros_like(carry_ref)
        r_i = lax.broadcasted_iota(jnp.int32, (tm, tm), 0)
        c_i = lax.broadcasted_iota(jnp.int32, (tm, tm), 1)
        tri_ref[...] = (c_i < r_i).astype(BF16)

    h_hi, h_lo = _split_bf16(h_ref[...])
    w_hi, w_lo = _split_bf16(w_ref[...])
    logits = (jnp.dot(h_hi, w_hi, preferred_element_type=F32)
              + (jnp.dot(h_lo, w_hi, preferred_element_type=F32) + jnp.dot(h_hi, w_lo, preferred_element_type=F32))
              + b_ref[...])
    lane = lax.broadcasted_iota(jnp.int32, (tm, LANES), 1)
    lane_f = lane.astype(F32)
    work = jnp.where(lane < N_EXPERTS, logits, -jnp.inf)

    vals, hots = [], []
    ids = jnp.zeros((tm, LANES), F32)
    for k in range(TOP_K):
        v = jnp.max(work, axis=-1, keepdims=True)
        idx = jnp.min(jnp.where(work == v, lane_f, float(LANES)), axis=-1, keepdims=True)
        hot = lane_f == idx
        work = jnp.where(hot, -jnp.inf, work)
        ids = jnp.where(lane == k, idx, ids)
        vals.append(v)
        hots.append(hot)

    es = [jnp.exp(v - vals[0]) for v in vals]
    esum = es[0] + es[1] + es[2] + es[3]
    gates = jnp.zeros((tm, LANES), F32)
    for k in range(TOP_K):
        gates = jnp.where(lane == k, es[k] / esum, gates)

    sel = (hots[0] | hots[1] | hots[2] | hots[3])
    before = jnp.dot(tri_ref[...], sel.astype(BF16), preferred_element_type=F32) + carry_ref[...]
    rank = jnp.zeros((tm, LANES), jnp.int32)
    for k in range(TOP_K):
        rk = jnp.sum(jnp.where(hots[k], before, 0.0), axis=-1, keepdims=True)
        rank = jnp.where(lane == k, rk.astype(jnp.int32), rank)
    carry_ref[...] = carry_ref[...] + jnp.sum(sel.astype(F32), axis=0, keepdims=True)

    gates_ref[...] = gates
    ids_ref[...] = ids.astype(jnp.int32)
    rank_ref[...] = rank
    cnt_ref[...] = carry_ref[...].astype(jnp.int32)


def router(h2, w_router, b_router, tm=512):
    T = h2.shape[0]
    tm = min(tm, T)
    w_pad = jnp.pad(w_router.astype(F32), ((0, 0), (0, LANES - N_EXPERTS)))
    b_pad = jnp.pad(b_router.astype(F32), (0, LANES - N_EXPERTS)).reshape(1, LANES)
    tile = pl.BlockSpec((tm, LANES), lambda i: (i, 0))
    return pl.pallas_call(
        functools.partial(_router_body, tm=tm),
        grid=(T // tm,),
        in_specs=[pl.BlockSpec((tm, D_MODEL), lambda i: (i, 0)),
                  pl.BlockSpec((D_MODEL, LANES), lambda i: (0, 0)),
                  pl.BlockSpec((1, LANES), lambda i: (0, 0))],
        out_specs=[tile, tile, tile, pl.BlockSpec((1, LANES), lambda i: (0, 0))],
        out_shape=[jax.ShapeDtypeStruct((T, LANES), F32), jax.ShapeDtypeStruct((T, LANES), jnp.int32),
                   jax.ShapeDtypeStruct((T, LANES), jnp.int32), jax.ShapeDtypeStruct((1, LANES), jnp.int32)],
        scratch_shapes=[pltpu.VMEM((1, LANES), F32), pltpu.VMEM((tm, tm), BF16)],
        compiler_params=_params(("arbitrary",)),
        name="router",
    )(h2, w_pad, b_pad)


def _token_copy(src, dst, s_tok, d_tok, sem, n=1):
    rows = n * SUBLANES
    return pltpu.make_async_copy(src.at[pl.ds(pl.multiple_of(s_tok * SUBLANES, SUBLANES), rows), :],
                                 dst.at[pl.ds(pl.multiple_of(d_tok * SUBLANES, SUBLANES), rows), :], sem)


def _dispatch_body(zstart_ref, zcount_ref, n_used_ref, pos_ref, h_ref, xs_ref, zeros_ref, sem, zsem, *, tm, nb):
    half = EXPERT_BLOCK // 2
    bits = [1 << s for s in reversed(range(half.bit_length()))]

    def zero_fill(e, wait):
        start, count = zstart_ref[e], zcount_ref[e]
        for bit in bits:
            @pl.when((count & bit) != 0)
            def _():
                cp = _token_copy(zeros_ref, xs_ref, 0, 0 if wait else start + (count & ~(2 * bit - 1)), zsem, bit)
                cp.wait() if wait else cp.start()

    def zero_tail(blk, wait):
        for part in range(2):
            cp = _token_copy(zeros_ref, xs_ref, 0, 0 if wait else blk * EXPERT_BLOCK + part * half, zsem, half)
            cp.wait() if wait else cp.start()

    @pl.when(pl.program_id(0) == 0)
    def _():
        zeros_ref[...] = jnp.zeros_like(zeros_ref)
        lax.fori_loop(0, N_EXPERTS, lambda e, c: (zero_fill(e, False), c)[1], 0)
        lax.fori_loop(n_used_ref[0], nb, lambda blk, c: (zero_tail(blk, False), c)[1], 0)

    def issue(t, c):
        for k in range(TOP_K):
            _token_copy(h_ref, xs_ref, t, pos_ref[k, t], sem).start(priority=k % 2)
        return c

    lax.fori_loop(0, tm, issue, 0)
    for k in range(TOP_K):
        _token_copy(h_ref, xs_ref, 0, 0, sem, tm).wait()

    @pl.when(pl.program_id(0) == 0)
    def _():
        lax.fori_loop(0, N_EXPERTS, lambda e, c: (zero_fill(e, True), c)[1], 0)
        lax.fori_loop(n_used_ref[0], nb, lambda blk, c: (zero_tail(blk, True), c)[1], 0)


def dispatch(ht, pos_t, zstart, zcount, n_used, n_rows, tm=512):
    T = ht.shape[0] // SUBLANES
    tm = min(tm, T)
    grid_spec = pltpu.PrefetchScalarGridSpec(
        num_scalar_prefetch=3,
        grid=(T // tm,),
        in_specs=[pl.BlockSpec((TOP_K, tm), lambda i, zs, zc, nu: (0, i), memory_space=pltpu.SMEM),
                  pl.BlockSpec((tm * SUBLANES, LANES), lambda i, zs, zc, nu: (i, 0))],
        out_specs=pl.BlockSpec(memory_space=pl.ANY),
        scratch_shapes=[pltpu.VMEM((EXPERT_BLOCK // 2 * SUBLANES, LANES), F32), pltpu.SemaphoreType.DMA,
                        pltpu.SemaphoreType.DMA],
    )
    return pl.pallas_call(
        functools.partial(_dispatch_body, tm=tm, nb=n_rows // EXPERT_BLOCK),
        grid_spec=grid_spec,
        out_shape=jax.ShapeDtypeStruct((n_rows * SUBLANES, LANES), F32),
        compiler_params=_params(("arbitrary",)),
        name="dispatch",
    )(zstart, zcount, n_used, pos_t, ht)


def _combine_body(pos_ref, pos_next_ref, gates_ref, h_ref, g2_ref, b2_ref, ys_ref, o_ref, buf, sems, *, tm):
    i = pl.program_id(0)
    slot = i % 2

    def issue(p_ref, s):
        def body(t, c):
            for k in range(TOP_K):
                _token_copy(ys_ref, buf.at[s, k], p_ref[k, t], t, sems.at[s]).start(priority=k % 2)
            return c

        lax.fori_loop(0, tm, body, 0)

    @pl.when(i == 0)
    def _():
        issue(pos_ref, 0)

    @pl.when(i + 1 < pl.num_programs(0))
    def _():
        issue(pos_next_ref, 1 - slot)

    for k in range(TOP_K):
        _token_copy(ys_ref, buf.at[slot, k], 0, 0, sems.at[slot], tm).wait()

    gates = gates_ref[...]
    ffn = gates[:, 0:1] * _load_token_tiles(buf.at[slot, 0], tm)
    for k in range(1, TOP_K):
        ffn = ffn + gates[:, k:k + 1] * _load_token_tiles(buf.at[slot, k], tm)
    o_ref[...] = _layer_norm(DEEPNORM_ALPHA * h_ref[...] + ffn, g2_ref[...], b2_ref[...])


def combine(ys, pos_t, gates_pad, h2, g2, b2, tm=256):
    T = h2.shape[0]
    tm = min(tm, T)
    last = T // tm - 1
    return pl.pallas_call(
        functools.partial(_combine_body, tm=tm),
        grid=(T // tm,),
        in_specs=[pl.BlockSpec((TOP_K, tm), lambda i: (0, i), memory_space=pltpu.SMEM),
                  pl.BlockSpec((TOP_K, tm), lambda i: (0, jnp.minimum(i + 1, last)), memory_space=pltpu.SMEM),
                  pl.BlockSpec((tm, LANES), lambda i: (i, 0)),
                  pl.BlockSpec((tm, D_MODEL), lambda i: (i, 0)),
                  pl.BlockSpec((1, D_MODEL), lambda i: (0, 0)),
                  pl.BlockSpec((1, D_MODEL), lambda i: (0, 0)),
                  pl.BlockSpec(memory_space=pl.ANY)],
        out_specs=pl.BlockSpec((tm, D_MODEL), lambda i: (i, 0)),
        out_shape=jax.ShapeDtypeStruct((T, D_MODEL), F32),
        scratch_shapes=[pltpu.VMEM((2, TOP_K, tm * SUBLANES, LANES), F32), pltpu.SemaphoreType.DMA((2,))],
        compiler_params=_params(("arbitrary",)),
        name="combine",
    )(pos_t, pos_t, gates_pad, h2, g2, b2, ys)


def _experts_body(blk_e, blk_src, n_used, xs_ref, wgu_ref, bgu_ref, wdn_ref, bdn_ref, ys_ref, wgu_b, wdn_b):
    del blk_src
    i = pl.program_id(0)
    used = i < n_used[0]

    @pl.when((i == 0) | (blk_e[i] != blk_e[jnp.maximum(i - 1, 0)]))
    def _():
        wgu_b[...] = wgu_ref[...].astype(BF16)
        wdn_b[...] = wdn_ref[...].astype(BF16)

    @pl.when(used)
    def _():
        x = _load_token_tiles(xs_ref, EXPERT_BLOCK).astype(BF16)
        hgu = jnp.dot(x, wgu_b[...], preferred_element_type=F32) + bgu_ref[...]
        gate = jnp.minimum(hgu[:, :D_FF], SWIGLU_LIMIT)
        up = jnp.clip(hgu[:, D_FF:], -SWIGLU_LIMIT, SWIGLU_LIMIT)
        act = (up + 1.0) * gate * _sigmoid(SWIGLU_ALPHA * gate)
        y = jnp.dot(act.astype(BF16), wdn_b[...], preferred_element_type=F32) + bdn_ref[...]
        _store_token_tiles(ys_ref, y, EXPERT_BLOCK)

    @pl.when(jnp.logical_not(used))
    def _():
        ys_ref[...] = jnp.zeros_like(ys_ref)


def experts(xs, blk_e, blk_src, n_used, wgu, bgu, wdn, bdn):
    n_rows = xs.shape[0] // SUBLANES
    nb = n_rows // EXPERT_BLOCK
    rows = EXPERT_BLOCK * SUBLANES
    grid_spec = pltpu.PrefetchScalarGridSpec(
        num_scalar_prefetch=3,
        grid=(nb,),
        in_specs=[pl.BlockSpec((rows, LANES), lambda i, e, s, n: (s[i], 0)),
                  pl.BlockSpec((None, D_MODEL, 2 * D_FF), lambda i, e, s, n: (e[i], 0, 0)),
                  pl.BlockSpec((None, 1, 2 * D_FF), lambda i, e, s, n: (e[i], 0, 0)),
                  pl.BlockSpec((None, D_FF, D_MODEL), lambda i, e, s, n: (e[i], 0, 0)),
                  pl.BlockSpec((None, 1, D_MODEL), lambda i, e, s, n: (e[i], 0, 0))],
        out_specs=pl.BlockSpec((rows, LANES), lambda i, e, s, n: (i, 0)),
        scratch_shapes=[pltpu.VMEM((D_MODEL, 2 * D_FF), BF16), pltpu.VMEM((D_FF, D_MODEL), BF16)],
    )
    return pl.pallas_call(
        _experts_body,
        grid_spec=grid_spec,
        out_shape=jax.ShapeDtypeStruct((n_rows * SUBLANES, LANES), F32),
        compiler_params=_params(("arbitrary",)),
        name="experts",
    )(blk_e, blk_src, n_used, xs, wgu, bgu, wdn, bdn)


def routing_layout(ids, rank, counts, n_blocks):
    padded = (counts + EXPERT_BLOCK - 1) // EXPERT_BLOCK * EXPERT_BLOCK
    pad_ends = jnp.cumsum(padded)
    pad_starts = pad_ends - padded
    pos_t = (pad_starts[ids] + rank).T.astype(jnp.int32)
    n_used = jnp.maximum(pad_ends[-1] // EXPERT_BLOCK, 1).astype(jnp.int32)
    blk = jnp.minimum(jnp.arange(n_blocks, dtype=jnp.int32), n_used - 1)
    blk_e = jnp.sum(pad_ends[None, :] <= (blk * EXPERT_BLOCK)[:, None], axis=1)
    blk_e = jnp.minimum(blk_e, N_EXPERTS - 1).astype(jnp.int32)
    zstart = (pad_starts + counts).astype(jnp.int32)
    zcount = (padded - counts).astype(jnp.int32)
    return pos_t, blk_e, blk, n_used.reshape(1), zstart, zcount


def _dilate_body(x_ref, *o_refs, n):
    for (_, dil), o_ref in zip(ATT_GROUPS, o_refs):
        for r in range(dil):
            o_ref[r] = x_ref[pl.ds(r, n // dil, stride=dil), :].astype(BF16)


def dilate_cast(x3, n=2048):
    B, S, D = x3.shape
    n = min(n, S)
    dils = [dil for _, dil in ATT_GROUPS]
    return pl.pallas_call(
        functools.partial(_dilate_body, n=n),
        grid=(B, S // n, D // LANES),
        in_specs=[pl.BlockSpec((None, n, LANES), lambda b, i, c: (b, i, c))],
        out_specs=[pl.BlockSpec((None, dil, n // dil, LANES), lambda b, i, c: (b, 0, i, c)) for dil in dils],
        out_shape=[jax.ShapeDtypeStruct((B, dil, S // dil, D), BF16) for dil in dils],
        compiler_params=_params(("arbitrary", "arbitrary", "arbitrary")),
        name="dilate_cast",
    )(x3)


def kernel(x, w_in, rel_bias, w_dw, b_dw, conv_ln_g, conv_ln_b, w_o_attn, w_o_conv, w_out, ln1_g, ln1_b,
           w_router, b_router, w_gate_up, b_gate_up, w_down, b_down, ln2_g, ln2_b):
    B, S, D = x.shape
    T = B * S
    h = x
    bm = bias_tables(rel_bias)
    q_off, k_off, v_off, rest = 0, QKV_WIDTH, 2 * QKV_WIDTH, 3 * QKV_WIDTH
    for l in range(DEPTH):
        wb = w_in[l].astype(BF16)

        def group_cols(gi):
            return [wb[:, off + gi * GROUP_WIDTH:off + (gi + 1) * GROUP_WIDTH] for off in (q_off, k_off, v_off)]

        x_dil = dilate_cast(h)
        xb = x_dil[0].reshape(T, D)
        w_main = jnp.concatenate(group_cols(0) + [wb[:, rest:]], axis=1)
        proj = in_proj(xb, w_main, w_main.shape[1] // 2, "in_proj_main").reshape(B, S, w_main.shape[1])
        o_list, st_list = [], []
        for gi in range(N_GROUPS):
            dil = ATT_GROUPS[gi][1]
            if dil == 1:
                qkv = proj.reshape(B, 1, S, proj.shape[-1])
            else:
                w_g = jnp.concatenate(group_cols(gi), axis=1)
                qkv = in_proj(x_dil[gi].reshape(T, D), w_g, w_g.shape[1], f"in_proj_g{gi}")
                qkv = qkv.reshape(B, dil, S // dil, w_g.shape[1])
            o_g, st_g = attention_group(qkv, bm[gi], gi)
            o_list.append(o_g)
            st_list.append(st_g)
        h1, h1_tiles = mixer_out(o_list, st_list, proj, h,
                       w_dw[l].reshape(CONV_WIDTH, CONV_CH), b_dw[l].reshape(1, CONV_CH),
                       conv_ln_g[l].reshape(1, CONV_CH), conv_ln_b[l].reshape(1, CONV_CH),
                       w_o_attn[l].astype(BF16), w_o_conv[l].astype(BF16), w_out[l].astype(BF16),
                       ln1_g[l].reshape(1, D), ln1_b[l].reshape(1, D))
        h2 = h1.reshape(T, D)
        gates_pad, ids_pad, rank_pad, counts = router(h2, w_router[l], b_router[l])
        n_rows = T * TOP_K + N_EXPERTS * EXPERT_BLOCK
        pos_t, blk_e, blk_src, n_used, zstart, zcount = routing_layout(
            ids_pad[:, :TOP_K], rank_pad[:, :TOP_K], counts[0, :N_EXPERTS], n_rows // EXPERT_BLOCK)
        xs = dispatch(h1_tiles, pos_t, zstart, zcount, n_used, n_rows)
        ys = experts(xs, blk_e, blk_src, n_used,
                     w_gate_up[l], b_gate_up[l].reshape(N_EXPERTS, 1, 2 * D_FF),
                     w_down[l], b_down[l].reshape(N_EXPERTS, 1, D))
        out = combine(ys, pos_t, gates_pad, h2, ln2_g[l].reshape(1, D), ln2_b[l].reshape(1, D))
        h = out.reshape(B, S, D)
    return h
```

```python
import functools
import math

import jax
import jax.numpy as jnp
from jax import lax
from jax.experimental import pallas as pl
from jax.experimental.pallas import tpu as pltpu

D_MODEL = 1024
ATT_GROUPS = ((128, 1), (512, 4), (2048, 16))
N_GROUPS = len(ATT_GROUPS)
HEADS_PER_GROUP = 8
HEAD_DIM = 64
GROUP_WIDTH = HEADS_PER_GROUP * HEAD_DIM
QKV_WIDTH = N_GROUPS * GROUP_WIDTH
ATT_BLOCK = 128
N_BUCKETS = 32
MAX_DISTANCE = 2048
CONV_CH = 768
CONV_WIDTH = 31
CONV_HALO = 32
N_EXPERTS = 32
TOP_K = 4
D_FF = 1024
SWIGLU_LIMIT = 7.0
SWIGLU_ALPHA = 1.702
IN_WIDTH = 3 * QKV_WIDTH + 2 * CONV_CH + 2 * D_MODEL
LN_EPS = 1e-5
NEG_INF = -1e30
DEPTH = 1
DEEPNORM_ALPHA = (2 * DEPTH) ** 0.25

LANES = 128
SUBLANES = 8
EXPERT_BLOCK = 512
VMEM_LIMIT = 56 * 1024 * 1024

F32 = jnp.float32
BF16 = jnp.bfloat16


def _params(sem, vmem=VMEM_LIMIT):
    return pltpu.CompilerParams(dimension_semantics=sem, vmem_limit_bytes=vmem)


def _sigmoid(x):
    return 0.5 * jnp.tanh(0.5 * x) + 0.5


def _layer_norm(x, g, b):
    mu = jnp.mean(x, axis=-1, keepdims=True)
    xc = x - mu
    var = jnp.mean(xc * xc, axis=-1, keepdims=True)
    return xc * lax.rsqrt(var + LN_EPS) * g + b


def _in_proj_body(x_ref, w_ref, o_ref):
    o_ref[...] = jnp.dot(x_ref[...], w_ref[...], preferred_element_type=F32).astype(o_ref.dtype)


def in_proj(xb, w_b, tn, name, tm=1024):
    T = xb.shape[0]
    N = w_b.shape[1]
    tm = min(tm, T)
    return pl.pallas_call(
        _in_proj_body,
        grid=(N // tn, T // tm),
        in_specs=[pl.BlockSpec((tm, D_MODEL), lambda n, m: (m, 0)),
                  pl.BlockSpec((D_MODEL, tn), lambda n, m: (0, n))],
        out_specs=pl.BlockSpec((tm, tn), lambda n, m: (m, n)),
        out_shape=jax.ShapeDtypeStruct((T, N), BF16),
        compiler_params=_params(("arbitrary", "arbitrary")),
        name=name,
    )(xb, w_b)


def _conv_module(uv_ref, ug_ref, uvh_ref, ugh_ref, wdw_ref, bdw_ref, cg_ref, cb_ref, out_ref,
                 glu_ref, dw_ref, shift_ref, first_of_batch, tc, sub, chunk):
    first_tap = CONV_HALO - (CONV_WIDTH - 1)
    for s0 in range(0, tc, sub):
        if s0 == 0:
            gh = uvh_ref[...].astype(F32) * _sigmoid(ugh_ref[...].astype(F32))
            gh = jnp.where(first_of_batch, 0.0, gh)
        else:
            gh = uv_ref[s0 - CONV_HALO:s0, :].astype(F32) * _sigmoid(ug_ref[s0 - CONV_HALO:s0, :].astype(F32))
        glu_ref[0:CONV_HALO] = gh
        glu_ref[CONV_HALO:] = uv_ref[s0:s0 + sub, :].astype(F32) * _sigmoid(ug_ref[s0:s0 + sub, :].astype(F32))
        for b in range(1, SUBLANES):
            shift_ref[b - 1] = glu_ref[b:b + shift_ref.shape[1], :]
        for r0 in range(0, sub, chunk):
            acc = jnp.broadcast_to(bdw_ref[...], (chunk, CONV_CH))
            for j in range(CONV_WIDTH):
                a, b = divmod(first_tap + j, SUBLANES)
                lo_row = r0 + a * SUBLANES
                rows = glu_ref[lo_row:lo_row + chunk, :] if b == 0 else shift_ref[b - 1, lo_row:lo_row + chunk, :]
                acc = acc + wdw_ref[j:j + 1, :] * rows
            dw_ref[r0:r0 + chunk, :] = acc
        cn = _layer_norm(dw_ref[...], cg_ref[...], cb_ref[...])
        out_ref[s0:s0 + sub, :] = (cn * _sigmoid(cn)).astype(out_ref.dtype)


def _in_proj_conv_body(x_ref, w_ref, uv_ref, ug_ref, uvh_ref, ugh_ref, wdw_ref, bdw_ref, cg_ref, cb_ref,
                       o_ref, conv_ref, glu_ref, dw_ref, shift_ref, *, tile0, tiles_per_seq, tc, sub, chunk):
    o_ref[...] = jnp.dot(x_ref[...], w_ref[...], preferred_element_type=F32).astype(o_ref.dtype)
    tile = tile0 + pl.program_id(0)
    _conv_module(uv_ref, ug_ref, uvh_ref, ugh_ref, wdw_ref, bdw_ref, cg_ref, cb_ref, conv_ref,
                 glu_ref, dw_ref, shift_ref, tile % tiles_per_seq == 0, tc, sub, chunk)


def in_proj_conv(xb, w_b, proj2, S, half, w_dw, b_dw, cg, cb, name, tm=1024, sub=256, chunk=32):
    T = xb.shape[0]
    N = w_b.shape[1]
    tm = min(tm, T)
    steps = T // tm
    tc = T // (2 * steps)
    sub = min(sub, tc)
    tile0 = half * steps
    uv_col = QKV_WIDTH // CONV_CH
    halo_blocks = tc // CONV_HALO

    def tile(col):
        return pl.BlockSpec((tc, CONV_CH), lambda m: (tile0 + m, col))

    def halo(col):
        return pl.BlockSpec((CONV_HALO, CONV_CH), lambda m: (jnp.maximum((tile0 + m) * halo_blocks - 1, 0), col))

    def whole(shape):
        return pl.BlockSpec(shape, lambda m: (0,) * len(shape))

    return pl.pallas_call(
        functools.partial(_in_proj_conv_body, tile0=tile0, tiles_per_seq=S // tc, tc=tc, sub=sub, chunk=chunk),
        grid=(steps,),
        in_specs=[pl.BlockSpec((tm, D_MODEL), lambda m: (m, 0)), whole((D_MODEL, N)),
                  tile(uv_col), tile(uv_col + 1), halo(uv_col), halo(uv_col + 1),
                  whole((CONV_WIDTH, CONV_CH)), whole((1, CONV_CH)), whole((1, CONV_CH)), whole((1, CONV_CH))],
        out_specs=[pl.BlockSpec((tm, N), lambda m: (m, 0)), pl.BlockSpec((tc, CONV_CH), lambda m: (m, 0))],
        out_shape=[jax.ShapeDtypeStruct((T, N), BF16), jax.ShapeDtypeStruct((T // 2, CONV_CH), BF16)],
        scratch_shapes=[pltpu.VMEM((CONV_HALO + sub, CONV_CH), F32), pltpu.VMEM((sub, CONV_CH), F32),
                        pltpu.VMEM((SUBLANES - 1, sub + CONV_HALO - SUBLANES, CONV_CH), F32)],
        compiler_params=_params(("arbitrary",)),
        name=name,
    )(xb, w_b, proj2, proj2, proj2, proj2, w_dw, b_dw, cg, cb)


def _t5_bucket(dist):
    max_exact = N_BUCKETS // 2
    log_ratio = jnp.log(jnp.maximum(dist, max_exact).astype(F32) / max_exact) / math.log(MAX_DISTANCE / max_exact)
    large = jnp.minimum(max_exact + (log_ratio * (N_BUCKETS - max_exact)).astype(jnp.int32), N_BUCKETS - 1)
    return jnp.where(dist < max_exact, dist, large)


def _bias_body(tbl_ref, bucket_ref, band_ref, o_ref):
    g = pl.program_id(0)
    h = pl.program_id(1)
    col = g * HEADS_PER_GROUP + h
    bucket = bucket_ref[...]
    acc = jnp.zeros(bucket.shape, F32)
    for k in range(N_BUCKETS):
        acc = jnp.where(bucket == k, tbl_ref[k, col], acc)
    band = band_ref[...] != 0
    kj = lax.broadcasted_iota(jnp.int32, bucket.shape, 1)
    o_ref[0] = jnp.where(band, acc, NEG_INF)
    o_ref[1] = jnp.where(band & (kj >= ATT_BLOCK), acc, NEG_INF)


def bias_tables(rel_bias):
    qi = jnp.arange(ATT_BLOCK)[:, None]
    kj = jnp.arange(2 * ATT_BLOCK)[None, :]
    dist = qi - kj + ATT_BLOCK
    buckets, bands = [], []
    for window, dil in ATT_GROUPS:
        bands.append(((dist >= 0) & (dist <= window // dil)).astype(jnp.int32))
        buckets.append(_t5_bucket(jnp.maximum(dist, 0) * dil).astype(jnp.int32))
    buckets = jnp.stack(buckets)
    bands = jnp.stack(bands)
    blk = (None, ATT_BLOCK, 2 * ATT_BLOCK)
    return pl.pallas_call(
        _bias_body,
        grid=(N_GROUPS, HEADS_PER_GROUP),
        in_specs=[pl.BlockSpec(memory_space=pltpu.SMEM),
                  pl.BlockSpec(blk, lambda g, h: (g, 0, 0)),
                  pl.BlockSpec(blk, lambda g, h: (g, 0, 0))],
        out_specs=pl.BlockSpec((None, None, 2, ATT_BLOCK, 2 * ATT_BLOCK), lambda g, h: (g, h, 0, 0, 0)),
        out_shape=jax.ShapeDtypeStruct((N_GROUPS, HEADS_PER_GROUP, 2, ATT_BLOCK, 2 * ATT_BLOCK), F32),
        compiler_params=_params(("arbitrary", "arbitrary")),
        name="bias_tables",
    )(rel_bias.astype(F32), buckets, bands)


def _attn_body(q_ref, kc_ref, kp_ref, vc_ref, vp_ref, bm_ref, o_ref, st_ref, k_all, v_all, s_buf, *, tq):
    i = pl.program_id(2)
    nsub = tq // ATT_BLOCK
    k_all[0:ATT_BLOCK] = kp_ref[...]
    k_all[ATT_BLOCK:] = kc_ref[...]
    v_all[0:ATT_BLOCK] = vp_ref[...]
    v_all[ATT_BLOCK:] = vc_ref[...]
    lane = lax.broadcasted_iota(jnp.int32, (ATT_BLOCK, LANES), 1)
    lo = lane < HEAD_DIM
    nt_dims = (((1,), (1,)), ((), ()))
    first = jnp.where(i == 0, 1, 0)

    def scores(s, slot):
        r0 = s * ATT_BLOCK
        for j in range(HEADS_PER_GROUP // 2):
            cs = slice(j * LANES, (j + 1) * LANES)
            qp = q_ref[r0:r0 + ATT_BLOCK, cs] * jnp.asarray(HEAD_DIM ** -0.5, BF16)
            kp = k_all[r0:r0 + 2 * ATT_BLOCK, cs]
            for hh in range(2):
                h = 2 * j + hh
                qh = jnp.where(lo if hh == 0 else ~lo, qp, jnp.zeros_like(qp))
                bias = bm_ref[h, first] if s == 0 else bm_ref[h, 0]
                s_buf[slot, h] = lax.dot_general(qh, kp, nt_dims, preferred_element_type=F32) + bias

    def softmax_pv(s, slot):
        r0 = s * ATT_BLOCK
        st = jnp.zeros((ATT_BLOCK, LANES), F32)
        for j in range(HEADS_PER_GROUP // 2):
            cs = slice(j * LANES, (j + 1) * LANES)
            vp = v_all[r0:r0 + 2 * ATT_BLOCK, cs]
            halves = []
            for hh in range(2):
                h = 2 * j + hh
                sc = s_buf[slot, h]
                m = jnp.max(sc, axis=-1, keepdims=True)
                p = jnp.exp(sc - m)
                den = jnp.sum(p, axis=-1, keepdims=True)
                pv = jnp.dot(p.astype(BF16), vp, preferred_element_type=F32)
                halves.append(pv * (1.0 / den))
                st = jnp.where(lane == h, m, st)
                st = jnp.where(lane == HEADS_PER_GROUP + h, den, st)
            o_ref[r0:r0 + ATT_BLOCK, cs] = jnp.where(lo, halves[0], halves[1])
        st_ref[r0:r0 + ATT_BLOCK, :] = st

    scores(0, 0)
    for s in range(nsub):
        if s + 1 < nsub:
            scores(s + 1, (s + 1) % 2)
        softmax_pv(s, s % 2)


def attention_group(qkv, bm_g, gi, tq=512):
    B, dil, L, _ = qkv.shape
    tq = min(tq, L)
    sub = tq // ATT_BLOCK

    def cur(col):
        return pl.BlockSpec((None, None, tq, GROUP_WIDTH), lambda b, r, i: (b, r, i, col))

    def prev(col):
        return pl.BlockSpec((None, None, ATT_BLOCK, GROUP_WIDTH),
                            lambda b, r, i: (b, r, jnp.maximum(i * sub - 1, 0), col))

    return pl.pallas_call(
        functools.partial(_attn_body, tq=tq),
        grid=(B, dil, L // tq),
        in_specs=[cur(0), cur(1), prev(1), cur(2), prev(2),
                  pl.BlockSpec((HEADS_PER_GROUP, 2, ATT_BLOCK, 2 * ATT_BLOCK), lambda b, r, i: (0, 0, 0, 0))],
        out_specs=[pl.BlockSpec((None, None, tq, GROUP_WIDTH), lambda b, r, i: (b, r, i, 0)),
                   pl.BlockSpec((None, None, tq, LANES), lambda b, r, i: (b, r, i, 0))],
        out_shape=[jax.ShapeDtypeStruct((B, dil, L, GROUP_WIDTH), F32),
                   jax.ShapeDtypeStruct((B, dil, L, LANES), F32)],
        scratch_shapes=[pltpu.VMEM((ATT_BLOCK + tq, GROUP_WIDTH), BF16),
                        pltpu.VMEM((ATT_BLOCK + tq, GROUP_WIDTH), BF16),
                        pltpu.VMEM((2, HEADS_PER_GROUP, ATT_BLOCK, 2 * ATT_BLOCK), F32)],
        compiler_params=_params(("arbitrary", "arbitrary", "arbitrary")),
        name=f"attention_g{gi}",
    )(qkv, qkv, qkv, qkv, qkv, bm_g)


def _split_bf16(x):
    hi = x.astype(BF16)
    lo = (x - hi.astype(F32)).astype(BF16)
    return hi, lo


def _load_token_tiles(ref, n):
    return jnp.concatenate([ref[pl.ds(c, n, stride=SUBLANES), :] for c in range(D_MODEL // LANES)], axis=1)


def _store_token_tiles(ref, x, n):
    for c in range(D_MODEL // LANES):
        ref[pl.ds(c, n, stride=SUBLANES), :] = x[:, c * LANES:(c + 1) * LANES]


def _to_token_order(blk_ref, tok_ref, dil, tm):
    n = tm // dil
    for r in range(dil):
        for c in range(tok_ref.shape[0]):
            tok_ref[c, pl.ds(r, n, stride=dil), :] = blk_ref[r, :, c * LANES:(c + 1) * LANES]


def _mixer_out_body(o0, o1, o2, s0, s1, s2, conv_a, conv_b, ga_ref, gc_ref, x_ref,
                    woa_ref, woc_ref, wout_ref, g1_ref, b1_ref,
                    h_ref, ht_ref, tok_o1, tok_s1, tok_o2, tok_s2, *, tm, half_tiles):
    ncol = GROUP_WIDTH // LANES
    _to_token_order(o1, tok_o1, ATT_GROUPS[1][1], tm)
    _to_token_order(s1, tok_s1, ATT_GROUPS[1][1], tm)
    _to_token_order(o2, tok_o2, ATT_GROUPS[2][1], tm)
    _to_token_order(s2, tok_s2, ATT_GROUPS[2][1], tm)
    outs = [o0[0],
            jnp.concatenate([tok_o1[c] for c in range(ncol)], axis=1),
            jnp.concatenate([tok_o2[c] for c in range(ncol)], axis=1)]
    sts = [s0[0], tok_s1[0], tok_s2[0]]

    mx = jnp.maximum(jnp.maximum(sts[0], sts[1]), sts[2])
    wts = [pltpu.roll(st, LANES - HEADS_PER_GROUP, axis=1) * jnp.exp(st - mx) for st in sts]
    wsum = wts[0] + wts[1] + wts[2]
    row = lax.broadcasted_iota(jnp.int32, (LANES, GROUP_WIDTH), 0)
    colh = lax.broadcasted_iota(jnp.int32, (LANES, GROUP_WIDTH), 1) // HEAD_DIM
    expand = (row == colh).astype(BF16)
    attn = jnp.zeros((tm, GROUP_WIDTH), F32)
    head_lane = lax.broadcasted_iota(jnp.int32, (tm, LANES), 1) < HEADS_PER_GROUP
    for wt, o in zip(wts, outs):
        c_hi, c_lo = _split_bf16(jnp.where(head_lane, wt / wsum, 0.0))
        c = (jnp.dot(c_hi, expand, preferred_element_type=F32)
             + jnp.dot(c_lo, expand, preferred_element_type=F32))
        attn = attn + c * o
    a_out = jnp.dot(attn.astype(BF16), woa_ref[...], preferred_element_type=F32)

    lin = pl.program_id(0) * pl.num_programs(1) + pl.program_id(1)
    conv = jnp.where(lin < half_tiles, conv_a[...], conv_b[...])
    c_out = jnp.dot(conv, woc_ref[...], preferred_element_type=F32)

    merged = (_sigmoid(ga_ref[...].astype(F32)) * a_out + _sigmoid(gc_ref[...].astype(F32)) * c_out)
    mix = jnp.dot(merged.astype(BF16), wout_ref[...], preferred_element_type=F32)
    h = _layer_norm(DEEPNORM_ALPHA * x_ref[...] + mix, g1_ref[...], b1_ref[...])
    h_ref[...] = h
    _store_token_tiles(ht_ref, h, tm)


def mixer_out(o_list, st_list, conv_halves, proj3, x3, woa_b, woc_b, wout_b, g1, b1, tm=256):
    B, S, _ = x3.shape
    tm = min(tm, S)
    ga_col = (QKV_WIDTH + 2 * CONV_CH) // D_MODEL
    ncol = GROUP_WIDTH // LANES
    per_seq = S // tm
    half_tiles = B * per_seq // 2

    def tile(width, col=0):
        return pl.BlockSpec((None, tm, width), lambda b, i: (b, i, col))

    def dilated(gi, width):
        dil = ATT_GROUPS[gi][1]
        return pl.BlockSpec((None, dil, tm // dil, width), lambda b, i: (b, 0, i, 0))

    def whole(shape):
        return pl.BlockSpec(shape, lambda b, i: (0,) * len(shape))

    conv_specs = [
        pl.BlockSpec((tm, CONV_CH), lambda b, i: (jnp.minimum(b * per_seq + i, half_tiles - 1), 0)),
        pl.BlockSpec((tm, CONV_CH), lambda b, i: (jnp.maximum(b * per_seq + i - half_tiles, 0), 0))]

    return pl.pallas_call(
        functools.partial(_mixer_out_body, tm=tm, half_tiles=half_tiles),
        grid=(B, per_seq),
        in_specs=[dilated(g, GROUP_WIDTH) for g in range(N_GROUPS)] + [dilated(g, LANES) for g in range(N_GROUPS)]
        + conv_specs
        + [tile(D_MODEL, ga_col), tile(D_MODEL, ga_col + 1), tile(D_MODEL),
           whole((GROUP_WIDTH, D_MODEL)), whole((CONV_CH, D_MODEL)), whole((D_MODEL, D_MODEL)),
           whole((1, D_MODEL)), whole((1, D_MODEL))],
        out_specs=[tile(D_MODEL), pl.BlockSpec((tm * SUBLANES, LANES), lambda b, i: (b * per_seq + i, 0))],
        out_shape=[jax.ShapeDtypeStruct((B, S, D_MODEL), F32),
                   jax.ShapeDtypeStruct((B * S * SUBLANES, LANES), F32)],
        scratch_shapes=[pltpu.VMEM((ncol, tm, LANES), F32), pltpu.VMEM((1, tm, LANES), F32),
                        pltpu.VMEM((ncol, tm, LANES), F32), pltpu.VMEM((1, tm, LANES), F32)],
        compiler_params=_params(("arbitrary", "arbitrary")),
        name="mixer_out",
    )(*o_list, *st_list, *conv_halves, proj3, proj3, x3, woa_b, woc_b, wout_b, g1, b1)


def _router_body(h_ref, w_ref, b_ref, gates_ref, ids_ref, rank_ref, cnt_ref, carry_ref, tri_ref, *, tm):
    step = pl.program_id(0)

    @pl.when(step == 0)
    def _():
        carry_ref[...] = jnp.zeros_like(carry_ref)
        r_i = lax.broadcasted_iota(jnp.int32, (tm, tm), 0)
        c_i = lax.broadcasted_iota(jnp.int32, (tm, tm), 1)
        tri_ref[...] = (c_i < r_i).astype(BF16)

    h_hi, h_lo = _split_bf16(h_ref[...])
    w_hi, w_lo = _split_bf16(w_ref[...])
    logits = (jnp.dot(h_hi, w_hi, preferred_element_type=F32)
              + (jnp.dot(h_lo, w_hi, preferred_element_type=F32) + jnp.dot(h_hi, w_lo, preferred_element_type=F32))
              + b_ref[...])
    lane = lax.broadcasted_iota(jnp.int32, (tm, LANES), 1)
    lane_f = lane.astype(F32)
    work = jnp.where(lane < N_EXPERTS, logits, -jnp.inf)

    vals, hots = [], []
    ids = jnp.zeros((tm, LANES), F32)
    for k in range(TOP_K):
        v = jnp.max(work, axis=-1, keepdims=True)
        idx = jnp.min(jnp.where(work == v, lane_f, float(LANES)), axis=-1, keepdims=True)
        hot = lane_f == idx
        work = jnp.where(hot, -jnp.inf, work)
        ids = jnp.where(lane == k, idx, ids)
        vals.append(v)
        hots.append(hot)

    es = [jnp.exp(v - vals[0]) for v in vals]
    esum = es[0] + es[1] + es[2] + es[3]
    gates = jnp.zeros((tm, LANES), F32)
    for k in range(TOP_K):
        gates = jnp.where(lane == k, es[k] / esum, gates)

    sel = (hots[0] | hots[1] | hots[2] | hots[3])
    before = jnp.dot(tri_ref[...], sel.astype(BF16), preferred_element_type=F32) + carry_ref[...]
    rank = jnp.zeros((tm, LANES), jnp.int32)
    for k in range(TOP_K):
        rk = jnp.sum(jnp.where(hots[k], before, 0.0), axis=-1, keepdims=True)
        rank = jnp.where(lane == k, rk.astype(jnp.int32), rank)
    carry_ref[...] = carry_ref[...] + jnp.sum(sel.astype(F32), axis=0, keepdims=True)

    gates_ref[...] = gates
    ids_ref[...] = ids.astype(jnp.int32)
    rank_ref[...] = rank
    cnt_ref[...] = carry_ref[...].astype(jnp.int32)


def router(h2, w_router, b_router, tm=512):
    T = h2.shape[0]
    tm = min(tm, T)
    w_pad = jnp.pad(w_router.astype(F32), ((0, 0), (0, LANES - N_EXPERTS)))
    b_pad = jnp.pad(b_router.astype(F32), (0, LANES - N_EXPERTS)).reshape(1, LANES)
    tile = pl.BlockSpec((tm, LANES), lambda i: (i, 0))
    return pl.pallas_call(
        functools.partial(_router_body, tm=tm),
        grid=(T // tm,),
        in_specs=[pl.BlockSpec((tm, D_MODEL), lambda i: (i, 0)),
                  pl.BlockSpec((D_MODEL, LANES), lambda i: (0, 0)),
                  pl.BlockSpec((1, LANES), lambda i: (0, 0))],
        out_specs=[tile, tile, tile, pl.BlockSpec((1, LANES), lambda i: (0, 0))],
        out_shape=[jax.ShapeDtypeStruct((T, LANES), F32), jax.ShapeDtypeStruct((T, LANES), jnp.int32),
                   jax.ShapeDtypeStruct((T, LANES), jnp.int32), jax.ShapeDtypeStruct((1, LANES), jnp.int32)],
        scratch_shapes=[pltpu.VMEM((1, LANES), F32), pltpu.VMEM((tm, tm), BF16)],
        compiler_params=_params(("arbitrary",)),
        name="router",
    )(h2, w_pad, b_pad)


def _token_copy(src, dst, s_tok, d_tok, sem, n=1):
    rows = n * SUBLANES
    return pltpu.make_async_copy(src.at[pl.ds(pl.multiple_of(s_tok * SUBLANES, SUBLANES), rows), :],
                                 dst.at[pl.ds(pl.multiple_of(d_tok * SUBLANES, SUBLANES), rows), :], sem)


def _dispatch_body(zstart_ref, zcount_ref, n_used_ref, pos_ref, h_ref, xs_ref, zeros_ref, sem, zsem, *, tm, nb):
    half = EXPERT_BLOCK // 2
    bits = [1 << s for s in reversed(range(half.bit_length()))]

    def zero_fill(e, wait):
        start, count = zstart_ref[e], zcount_ref[e]
        for bit in bits:
            @pl.when((count & bit) != 0)
            def _():
                cp = _token_copy(zeros_ref, xs_ref, 0, 0 if wait else start + (count & ~(2 * bit - 1)), zsem, bit)
                cp.wait() if wait else cp.start()

    def zero_tail(blk, wait):
        for part in range(2):
            cp = _token_copy(zeros_ref, xs_ref, 0, 0 if wait else blk * EXPERT_BLOCK + part * half, zsem, half)
            cp.wait() if wait else cp.start()

    @pl.when(pl.program_id(0) == 0)
    def _():
        zeros_ref[...] = jnp.zeros_like(zeros_ref)
        lax.fori_loop(0, N_EXPERTS, lambda e, c: (zero_fill(e, False), c)[1], 0)
        lax.fori_loop(n_used_ref[0], nb, lambda blk, c: (zero_tail(blk, False), c)[1], 0)

    def issue(t, c):
        for k in range(TOP_K):
            _token_copy(h_ref, xs_ref, t, pos_ref[k, t], sem).start(priority=k % 2)
        return c

    lax.fori_loop(0, tm, issue, 0)
    for k in range(TOP_K):
        _token_copy(h_ref, xs_ref, 0, 0, sem, tm).wait()

    @pl.when(pl.program_id(0) == 0)
    def _():
        lax.fori_loop(0, N_EXPERTS, lambda e, c: (zero_fill(e, True), c)[1], 0)
        lax.fori_loop(n_used_ref[0], nb, lambda blk, c: (zero_tail(blk, True), c)[1], 0)


def dispatch(ht, pos_t, zstart, zcount, n_used, n_rows, tm=512):
    T = ht.shape[0] // SUBLANES
    tm = min(tm, T)
    grid_spec = pltpu.PrefetchScalarGridSpec(
        num_scalar_prefetch=3,
        grid=(T // tm,),
        in_specs=[pl.BlockSpec((TOP_K, tm), lambda i, zs, zc, nu: (0, i), memory_space=pltpu.SMEM),
                  pl.BlockSpec((tm * SUBLANES, LANES), lambda i, zs, zc, nu: (i, 0))],
        out_specs=pl.BlockSpec(memory_space=pl.ANY),
        scratch_shapes=[pltpu.VMEM((EXPERT_BLOCK // 2 * SUBLANES, LANES), F32), pltpu.SemaphoreType.DMA,
                        pltpu.SemaphoreType.DMA],
    )
    return pl.pallas_call(
        functools.partial(_dispatch_body, tm=tm, nb=n_rows // EXPERT_BLOCK),
        grid_spec=grid_spec,
        out_shape=jax.ShapeDtypeStruct((n_rows * SUBLANES, LANES), F32),
        compiler_params=_params(("arbitrary",)),
        name="dispatch",
    )(zstart, zcount, n_used, pos_t, ht)


def _combine_body(pos_ref, pos_next_ref, gates_ref, h_ref, g2_ref, b2_ref, ys_ref, o_ref, buf, sems, *, tm):
    i = pl.program_id(0)
    slot = i % 2

    def issue(p_ref, s):
        def body(t, c):
            for k in range(TOP_K):
                _token_copy(ys_ref, buf.at[s, k], p_ref[k, t], t, sems.at[s]).start(priority=k % 2)
            return c

        lax.fori_loop(0, tm, body, 0)

    @pl.when(i == 0)
    def _():
        issue(pos_ref, 0)

    @pl.when(i + 1 < pl.num_programs(0))
    def _():
        issue(pos_next_ref, 1 - slot)

    for k in range(TOP_K):
        _token_copy(ys_ref, buf.at[slot, k], 0, 0, sems.at[slot], tm).wait()

    gates = gates_ref[...]
    ffn = gates[:, 0:1] * _load_token_tiles(buf.at[slot, 0], tm)
    for k in range(1, TOP_K):
        ffn = ffn + gates[:, k:k + 1] * _load_token_tiles(buf.at[slot, k], tm)
    o_ref[...] = _layer_norm(DEEPNORM_ALPHA * h_ref[...] + ffn, g2_ref[...], b2_ref[...])


def combine(ys, pos_t, gates_pad, h2, g2, b2, tm=256):
    T = h2.shape[0]
    tm = min(tm, T)
    last = T // tm - 1
    return pl.pallas_call(
        functools.partial(_combine_body, tm=tm),
        grid=(T // tm,),
        in_specs=[pl.BlockSpec((TOP_K, tm), lambda i: (0, i), memory_space=pltpu.SMEM),
                  pl.BlockSpec((TOP_K, tm), lambda i: (0, jnp.minimum(i + 1, last)), memory_space=pltpu.SMEM),
                  pl.BlockSpec((tm, LANES), lambda i: (i, 0)),
                  pl.BlockSpec((tm, D_MODEL), lambda i: (i, 0)),
                  pl.BlockSpec((1, D_MODEL), lambda i: (0, 0)),
                  pl.BlockSpec((1, D_MODEL), lambda i: (0, 0)),
                  pl.BlockSpec(memory_space=pl.ANY)],
        out_specs=pl.BlockSpec((tm, D_MODEL), lambda i: (i, 0)),
        out_shape=jax.ShapeDtypeStruct((T, D_MODEL), F32),
        scratch_shapes=[pltpu.VMEM((2, TOP_K, tm * SUBLANES, LANES), F32), pltpu.SemaphoreType.DMA((2,))],
        compiler_params=_params(("arbitrary",)),
        name="combine",
    )(pos_t, pos_t, gates_pad, h2, g2, b2, ys)


def _experts_body(blk_e, blk_src, n_used, xs_ref, wgu_ref, bgu_ref, wdn_ref, bdn_ref, ys_ref, wgu_b, wdn_b):
    del blk_src
    i = pl.program_id(0)
    used = i < n_used[0]

    @pl.when((i == 0) | (blk_e[i] != blk_e[jnp.maximum(i - 1, 0)]))
    def _():
        wgu_b[...] = wgu_ref[...].astype(BF16)
        wdn_b[...] = wdn_ref[...].astype(BF16)

    @pl.when(used)
    def _():
        x = _load_token_tiles(xs_ref, EXPERT_BLOCK).astype(BF16)
        hgu = jnp.dot(x, wgu_b[...], preferred_element_type=F32) + bgu_ref[...]
        gate = jnp.minimum(hgu[:, :D_FF], SWIGLU_LIMIT)
        up = jnp.clip(hgu[:, D_FF:], -SWIGLU_LIMIT, SWIGLU_LIMIT)
        act = (up + 1.0) * gate * _sigmoid(SWIGLU_ALPHA * gate)
        y = jnp.dot(act.astype(BF16), wdn_b[...], preferred_element_type=F32) + bdn_ref[...]
        _store_token_tiles(ys_ref, y, EXPERT_BLOCK)

    @pl.when(jnp.logical_not(used))
    def _():
        ys_ref[...] = jnp.zeros_like(ys_ref)


def experts(xs, blk_e, blk_src, n_used, wgu, bgu, wdn, bdn):
    n_rows = xs.shape[0] // SUBLANES
    nb = n_rows // EXPERT_BLOCK
    rows = EXPERT_BLOCK * SUBLANES
    grid_spec = pltpu.PrefetchScalarGridSpec(
        num_scalar_prefetch=3,
        grid=(nb,),
        in_specs=[pl.BlockSpec((rows, LANES), lambda i, e, s, n: (s[i], 0)),
                  pl.BlockSpec((None, D_MODEL, 2 * D_FF), lambda i, e, s, n: (e[i], 0, 0)),
                  pl.BlockSpec((None, 1, 2 * D_FF), lambda i, e, s, n: (e[i], 0, 0)),
                  pl.BlockSpec((None, D_FF, D_MODEL), lambda i, e, s, n: (e[i], 0, 0)),
                  pl.BlockSpec((None, 1, D_MODEL), lambda i, e, s, n: (e[i], 0, 0))],
        out_specs=pl.BlockSpec((rows, LANES), lambda i, e, s, n: (i, 0)),
        scratch_shapes=[pltpu.VMEM((D_MODEL, 2 * D_FF), BF16), pltpu.VMEM((D_FF, D_MODEL), BF16)],
    )
    return pl.pallas_call(
        _experts_body,
        grid_spec=grid_spec,
        out_shape=jax.ShapeDtypeStruct((n_rows * SUBLANES, LANES), F32),
        compiler_params=_params(("arbitrary",)),
        name="experts",
    )(blk_e, blk_src, n_used, xs, wgu, bgu, wdn, bdn)


def routing_layout(ids, rank, counts, n_blocks):
    padded = (counts + EXPERT_BLOCK - 1) // EXPERT_BLOCK * EXPERT_BLOCK
    pad_ends = jnp.cumsum(padded)
    pad_starts = pad_ends - padded
    pos_t = (pad_starts[ids] + rank).T.astype(jnp.int32)
    n_used = jnp.maximum(pad_ends[-1] // EXPERT_BLOCK, 1).astype(jnp.int32)
    blk = jnp.minimum(jnp.arange(n_blocks, dtype=jnp.int32), n_used - 1)
    blk_e = jnp.sum(pad_ends[None, :] <= (blk * EXPERT_BLOCK)[:, None], axis=1)
    blk_e = jnp.minimum(blk_e, N_EXPERTS - 1).astype(jnp.int32)
    zstart = (pad_starts + counts).astype(jnp.int32)
    zcount = (padded - counts).astype(jnp.int32)
    return pos_t, blk_e, blk, n_used.reshape(1), zstart, zcount


def _dilate_body(x_ref, *o_refs, n):
    for (_, dil), o_ref in zip(ATT_GROUPS, o_refs):
        for r in range(dil):
            o_ref[r] = x_ref[pl.ds(r, n // dil, stride=dil), :].astype(BF16)


def dilate_cast(x3, n=2048):
    B, S, D = x3.shape
    n = min(n, S)
    dils = [dil for _, dil in ATT_GROUPS]
    return pl.pallas_call(
        functools.partial(_dilate_body, n=n),
        grid=(B, S // n, D // LANES),
        in_specs=[pl.BlockSpec((None, n, LANES), lambda b, i, c: (b, i, c))],
        out_specs=[pl.BlockSpec((None, dil, n // dil, LANES), lambda b, i, c: (b, 0, i, c)) for dil in dils],
        out_shape=[jax.ShapeDtypeStruct((B, dil, S // dil, D), BF16) for dil in dils],
        compiler_params=_params(("arbitrary", "arbitrary", "arbitrary")),
        name="dilate_cast",
    )(x3)


def kernel(x, w_in, rel_bias, w_dw, b_dw, conv_ln_g, conv_ln_b, w_o_attn, w_o_conv, w_out, ln1_g, ln1_b,
           w_router, b_router, w_gate_up, b_gate_up, w_down, b_down, ln2_g, ln2_b):
    B, S, D = x.shape
    T = B * S
    h = x
    bm = bias_tables(rel_bias)
    q_off, k_off, v_off, rest = 0, QKV_WIDTH, 2 * QKV_WIDTH, 3 * QKV_WIDTH
    for l in range(DEPTH):
        wb = w_in[l].astype(BF16)

        def group_cols(gi):
            return [wb[:, off + gi * GROUP_WIDTH:off + (gi + 1) * GROUP_WIDTH] for off in (q_off, k_off, v_off)]

        x_dil = dilate_cast(h)
        xb = x_dil[0].reshape(T, D)
        w_main = jnp.concatenate(group_cols(0) + [wb[:, rest:]], axis=1)
        proj = in_proj(xb, w_main, w_main.shape[1] // 2, "in_proj_main").reshape(B, S, w_main.shape[1])
        conv_params = (w_dw[l].reshape(CONV_WIDTH, CONV_CH), b_dw[l].reshape(1, CONV_CH),
                       conv_ln_g[l].reshape(1, CONV_CH), conv_ln_b[l].reshape(1, CONV_CH))
        o_list, st_list, conv_halves = [], [], []
        for gi in range(N_GROUPS):
            dil = ATT_GROUPS[gi][1]
            if dil == 1:
                qkv = proj.reshape(B, 1, S, proj.shape[-1])
            else:
                w_g = jnp.concatenate(group_cols(gi), axis=1)
                qkv, conv_half = in_proj_conv(x_dil[gi].reshape(T, D), w_g, proj.reshape(T, proj.shape[-1]), S,
                                              len(conv_halves), *conv_params, f"in_proj_g{gi}")
                conv_halves.append(conv_half)
                qkv = qkv.reshape(B, dil, S // dil, w_g.shape[1])
            o_g, st_g = attention_group(qkv, bm[gi], gi)
            o_list.append(o_g)
            st_list.append(st_g)
        h1, h1_tiles = mixer_out(o_list, st_list, conv_halves, proj, h,
                                 w_o_attn[l].astype(BF16), w_o_conv[l].astype(BF16), w_out[l].astype(BF16),
                                 ln1_g[l].reshape(1, D), ln1_b[l].reshape(1, D))
        h2 = h1.reshape(T, D)
        gates_pad, ids_pad, rank_pad, counts = router(h2, w_router[l], b_router[l])
        n_rows = T * TOP_K + N_EXPERTS * EXPERT_BLOCK
        pos_t, blk_e, blk_src, n_used, zstart, zcount = routing_layout(
            ids_pad[:, :TOP_K], rank_pad[:, :TOP_K], counts[0, :N_EXPERTS], n_rows // EXPERT_BLOCK)
        xs = dispatch(h1_tiles, pos_t, zstart, zcount, n_used, n_rows)
        ys = experts(xs, blk_e, blk_src, n_used,
                     w_gate_up[l], b_gate_up[l].reshape(N_EXPERTS, 1, 2 * D_FF),
                     w_down[l], b_down[l].reshape(N_EXPERTS, 1, D))
        out = combine(ys, pos_t, gates_pad, h2, ln2_g[l].reshape(1, D), ln2_b[l].reshape(1, D))
        h = out.reshape(B, S, D)
    return h
```

```python
import functools
import math

import jax
import jax.numpy as jnp
from jax import lax
from jax.experimental import pallas as pl
from jax.experimental.pallas import tpu as pltpu

D_MODEL = 1024
ATT_GROUPS = ((128, 1), (512, 4), (2048, 16))
N_GROUPS = len(ATT_GROUPS)
HEADS_PER_GROUP = 8
HEAD_DIM = 64
GROUP_WIDTH = HEADS_PER_GROUP * HEAD_DIM
QKV_WIDTH = N_GROUPS * GROUP_WIDTH
ATT_BLOCK = 128
N_BUCKETS = 32
MAX_DISTANCE = 2048
CONV_CH = 768
CONV_WIDTH = 31
CONV_HALO = 32
N_EXPERTS = 32
TOP_K = 4
D_FF = 1024
SWIGLU_LIMIT = 7.0
SWIGLU_ALPHA = 1.702
IN_WIDTH = 3 * QKV_WIDTH + 2 * CONV_CH + 2 * D_MODEL
LN_EPS = 1e-5
NEG_INF = -1e30
DEPTH = 1
DEEPNORM_ALPHA = (2 * DEPTH) ** 0.25

LANES = 128
SUBLANES = 8
EXPERT_BLOCK = 512
VMEM_LIMIT = 56 * 1024 * 1024

F32 = jnp.float32
BF16 = jnp.bfloat16


def _params(sem, vmem=VMEM_LIMIT):
    return pltpu.CompilerParams(dimension_semantics=sem, vmem_limit_bytes=vmem)


def _sigmoid(x):
    return 0.5 * jnp.tanh(0.5 * x) + 0.5


def _layer_norm(x, g, b):
    mu = jnp.mean(x, axis=-1, keepdims=True)
    xc = x - mu
    var = jnp.mean(xc * xc, axis=-1, keepdims=True)
    return xc * lax.rsqrt(var + LN_EPS) * g + b


def _in_proj_body(x_ref, w_ref, o_ref):
    o_ref[...] = jnp.dot(x_ref[...], w_ref[...], preferred_element_type=F32).astype(o_ref.dtype)


def in_proj(xb, w_b, tn, name, tm=1024):
    T = xb.shape[0]
    N = w_b.shape[1]
    tm = min(tm, T)
    return pl.pallas_call(
        _in_proj_body,
        grid=(N // tn, T // tm),
        in_specs=[pl.BlockSpec((tm, D_MODEL), lambda n, m: (m, 0)),
                  pl.BlockSpec((D_MODEL, tn), lambda n, m: (0, n))],
        out_specs=pl.BlockSpec((tm, tn), lambda n, m: (m, n)),
        out_shape=jax.ShapeDtypeStruct((T, N), BF16),
        compiler_params=_params(("arbitrary", "arbitrary")),
        name=name,
    )(xb, w_b)


def _t5_bucket(dist):
    max_exact = N_BUCKETS // 2
    log_ratio = jnp.log(jnp.maximum(dist, max_exact).astype(F32) / max_exact) / math.log(MAX_DISTANCE / max_exact)
    large = jnp.minimum(max_exact + (log_ratio * (N_BUCKETS - max_exact)).astype(jnp.int32), N_BUCKETS - 1)
    return jnp.where(dist < max_exact, dist, large)


def _bias_body(tbl_ref, bucket_ref, band_ref, o_ref):
    g = pl.program_id(0)
    h = pl.program_id(1)
    col = g * HEADS_PER_GROUP + h
    bucket = bucket_ref[...]
    acc = jnp.zeros(bucket.shape, F32)
    for k in range(N_BUCKETS):
        acc = jnp.where(bucket == k, tbl_ref[k, col], acc)
    band = band_ref[...] != 0
    kj = lax.broadcasted_iota(jnp.int32, bucket.shape, 1)
    o_ref[0] = jnp.where(band, acc, NEG_INF)
    o_ref[1] = jnp.where(band & (kj >= ATT_BLOCK), acc, NEG_INF)


def bias_tables(rel_bias):
    qi = jnp.arange(ATT_BLOCK)[:, None]
    kj = jnp.arange(2 * ATT_BLOCK)[None, :]
    dist = qi - kj + ATT_BLOCK
    buckets, bands = [], []
    for window, dil in ATT_GROUPS:
        bands.append(((dist >= 0) & (dist <= window // dil)).astype(jnp.int32))
        buckets.append(_t5_bucket(jnp.maximum(dist, 0) * dil).astype(jnp.int32))
    buckets = jnp.stack(buckets)
    bands = jnp.stack(bands)
    blk = (None, ATT_BLOCK, 2 * ATT_BLOCK)
    return pl.pallas_call(
        _bias_body,
        grid=(N_GROUPS, HEADS_PER_GROUP),
        in_specs=[pl.BlockSpec(memory_space=pltpu.SMEM),
                  pl.BlockSpec(blk, lambda g, h: (g, 0, 0)),
                  pl.BlockSpec(blk, lambda g, h: (g, 0, 0))],
        out_specs=pl.BlockSpec((None, None, 2, ATT_BLOCK, 2 * ATT_BLOCK), lambda g, h: (g, h, 0, 0, 0)),
        out_shape=jax.ShapeDtypeStruct((N_GROUPS, HEADS_PER_GROUP, 2, ATT_BLOCK, 2 * ATT_BLOCK), F32),
        compiler_params=_params(("arbitrary", "arbitrary")),
        name="bias_tables",
    )(rel_bias.astype(F32), buckets, bands)


def _attn_body(q_ref, kc_ref, kp_ref, vc_ref, vp_ref, bm_ref, o_ref, st_ref, k_all, v_all, s_buf, *, tq):
    i = pl.program_id(2)
    nsub = tq // ATT_BLOCK
    k_all[0:ATT_BLOCK] = kp_ref[...]
    k_all[ATT_BLOCK:] = kc_ref[...]
    v_all[0:ATT_BLOCK] = vp_ref[...]
    v_all[ATT_BLOCK:] = vc_ref[...]
    lo = lax.broadcasted_iota(jnp.int32, (ATT_BLOCK, LANES), 1) < HEAD_DIM
    nt_dims = (((1,), (1,)), ((), ()))
    first = jnp.where(i == 0, 1, 0)

    def scores(s, slot):
        r0 = s * ATT_BLOCK
        for j in range(HEADS_PER_GROUP // 2):
            cs = slice(j * LANES, (j + 1) * LANES)
            qp = q_ref[r0:r0 + ATT_BLOCK, cs] * jnp.asarray(HEAD_DIM ** -0.5, BF16)
            kp = k_all[r0:r0 + 2 * ATT_BLOCK, cs]
            for hh in range(2):
                h = 2 * j + hh
                qh = jnp.where(lo if hh == 0 else ~lo, qp, jnp.zeros_like(qp))
                bias = bm_ref[h, first] if s == 0 else bm_ref[h, 0]
                s_buf[slot, h] = lax.dot_general(qh, kp, nt_dims, preferred_element_type=F32) + bias

    def softmax_pv(s, slot):
        rows = slice(s * ATT_BLOCK, (s + 1) * ATT_BLOCK)
        st_ref[rows, :] = jnp.zeros((ATT_BLOCK, LANES), F32)
        for j in range(HEADS_PER_GROUP // 2):
            vp = v_all[s * ATT_BLOCK:(s + 2) * ATT_BLOCK, j * LANES:(j + 1) * LANES]
            for hh in range(2):
                h = 2 * j + hh
                sc = s_buf[slot, h]
                m = jnp.max(sc, axis=-1, keepdims=True)
                p = jnp.exp(sc - m)
                den = jnp.sum(p, axis=-1, keepdims=True)
                pv = jnp.dot(p.astype(BF16), vp, preferred_element_type=F32) * (1.0 / den)
                c0 = h * HEAD_DIM
                o_ref[rows, c0:c0 + HEAD_DIM] = pv[:, hh * HEAD_DIM:(hh + 1) * HEAD_DIM]
                st_ref[rows, h:h + 1] = m
                st_ref[rows, HEADS_PER_GROUP + h:HEADS_PER_GROUP + h + 1] = den

    scores(0, 0)
    for s in range(nsub):
        if s + 1 < nsub:
            scores(s + 1, (s + 1) % 2)
        softmax_pv(s, s % 2)


def attention_group(qkv, bm_g, gi, tq=512):
    B, dil, L, _ = qkv.shape
    tq = min(tq, L)
    sub = tq // ATT_BLOCK

    def cur(col):
        return pl.BlockSpec((None, None, tq, GROUP_WIDTH), lambda b, r, i: (b, r, i, col))

    def prev(col):
        return pl.BlockSpec((None, None, ATT_BLOCK, GROUP_WIDTH),
                            lambda b, r, i: (b, r, jnp.maximum(i * sub - 1, 0), col))

    return pl.pallas_call(
        functools.partial(_attn_body, tq=tq),
        grid=(B, dil, L // tq),
        in_specs=[cur(0), cur(1), prev(1), cur(2), prev(2),
                  pl.BlockSpec((HEADS_PER_GROUP, 2, ATT_BLOCK, 2 * ATT_BLOCK), lambda b, r, i: (0, 0, 0, 0))],
        out_specs=[pl.BlockSpec((None, None, tq, GROUP_WIDTH), lambda b, r, i: (b, r, i, 0)),
                   pl.BlockSpec((None, None, tq, LANES), lambda b, r, i: (b, r, i, 0))],
        out_shape=[jax.ShapeDtypeStruct((B, dil, L, GROUP_WIDTH), F32),
                   jax.ShapeDtypeStruct((B, dil, L, LANES), F32)],
        scratch_shapes=[pltpu.VMEM((ATT_BLOCK + tq, GROUP_WIDTH), BF16),
                        pltpu.VMEM((ATT_BLOCK + tq, GROUP_WIDTH), BF16),
                        pltpu.VMEM((2, HEADS_PER_GROUP, ATT_BLOCK, 2 * ATT_BLOCK), F32)],
        compiler_params=_params(("arbitrary", "arbitrary", "arbitrary")),
        name=f"attention_g{gi}",
    )(qkv, qkv, qkv, qkv, qkv, bm_g)


def _split_bf16(x):
    hi = x.astype(BF16)
    lo = (x - hi.astype(F32)).astype(BF16)
    return hi, lo


def _load_token_tiles(ref, n):
    return jnp.concatenate([ref[pl.ds(c, n, stride=SUBLANES), :] for c in range(D_MODEL // LANES)], axis=1)


def _store_token_tiles(ref, x, n):
    for c in range(D_MODEL // LANES):
        ref[pl.ds(c, n, stride=SUBLANES), :] = x[:, c * LANES:(c + 1) * LANES]


def _to_token_order(blk_ref, tok_ref, dil, tm):
    n = tm // dil
    for r in range(dil):
        for c in range(tok_ref.shape[0]):
            tok_ref[c, pl.ds(r, n, stride=dil), :] = blk_ref[r, :, c * LANES:(c + 1) * LANES]


def _mixer_out_body(o0, o1, o2, s0, s1, s2, uv_ref, ug_ref, uvh_ref, ugh_ref, ga_ref, gc_ref, x_ref,
                    wdw_ref, bdw_ref, cg_ref, cb_ref, woa_ref, woc_ref, wout_ref, g1_ref, b1_ref,
                    h_ref, ht_ref, glu_ref, dw_ref, shift_ref, tok_o1, tok_s1, tok_o2, tok_s2, *, tm, chunk):
    i = pl.program_id(1)

    ncol = GROUP_WIDTH // LANES
    _to_token_order(o1, tok_o1, ATT_GROUPS[1][1], tm)
    _to_token_order(s1, tok_s1, ATT_GROUPS[1][1], tm)
    _to_token_order(o2, tok_o2, ATT_GROUPS[2][1], tm)
    _to_token_order(s2, tok_s2, ATT_GROUPS[2][1], tm)
    outs = [o0[0],
            jnp.concatenate([tok_o1[c] for c in range(ncol)], axis=1),
            jnp.concatenate([tok_o2[c] for c in range(ncol)], axis=1)]
    sts = [s0[0], tok_s1[0], tok_s2[0]]

    mx = jnp.maximum(jnp.maximum(sts[0], sts[1]), sts[2])
    wts = [pltpu.roll(st, LANES - HEADS_PER_GROUP, axis=1) * jnp.exp(st - mx) for st in sts]
    wsum = wts[0] + wts[1] + wts[2]
    row = lax.broadcasted_iota(jnp.int32, (LANES, GROUP_WIDTH), 0)
    colh = lax.broadcasted_iota(jnp.int32, (LANES, GROUP_WIDTH), 1) // HEAD_DIM
    expand = (row == colh).astype(BF16)
    attn = jnp.zeros((tm, GROUP_WIDTH), F32)
    head_lane = lax.broadcasted_iota(jnp.int32, (tm, LANES), 1) < HEADS_PER_GROUP
    for wt, o in zip(wts, outs):
        c_hi, c_lo = _split_bf16(jnp.where(head_lane, wt / wsum, 0.0))
        c = (jnp.dot(c_hi, expand, preferred_element_type=F32)
             + jnp.dot(c_lo, expand, preferred_element_type=F32))
        attn = attn + c * o
    a_out = jnp.dot(attn.astype(BF16), woa_ref[...], preferred_element_type=F32)

    gh = uvh_ref[...].astype(F32) * _sigmoid(ugh_ref[...].astype(F32))
    glu_ref[0:CONV_HALO] = jnp.where(i == 0, 0.0, gh)
    glu_ref[CONV_HALO:] = uv_ref[...].astype(F32) * _sigmoid(ug_ref[...].astype(F32))
    first_tap = CONV_HALO - (CONV_WIDTH - 1)
    for b in range(1, SUBLANES):
        shift_ref[b - 1] = glu_ref[b:b + shift_ref.shape[1], :]

    for c0 in range(0, CONV_CH, LANES):
        cs = slice(c0, c0 + LANES)
        bias = jnp.broadcast_to(bdw_ref[:, cs], (chunk, LANES))
        for r0 in range(0, tm, chunk):
            acc = bias
            for j in range(CONV_WIDTH):
                a, b = divmod(first_tap + j, SUBLANES)
                lo_row = r0 + a * SUBLANES
                rows = glu_ref[lo_row:lo_row + chunk, cs] if b == 0 else shift_ref[b - 1, lo_row:lo_row + chunk, cs]
                acc = acc + wdw_ref[j:j + 1, cs] * rows
            dw_ref[r0:r0 + chunk, cs] = acc
    cn = _layer_norm(dw_ref[...], cg_ref[...], cb_ref[...])
    conv = cn * _sigmoid(cn)
    c_out = jnp.dot(conv.astype(BF16), woc_ref[...], preferred_element_type=F32)

    merged = (_sigmoid(ga_ref[...].astype(F32)) * a_out + _sigmoid(gc_ref[...].astype(F32)) * c_out)
    mix = jnp.dot(merged.astype(BF16), wout_ref[...], preferred_element_type=F32)
    h = _layer_norm(DEEPNORM_ALPHA * x_ref[...] + mix, g1_ref[...], b1_ref[...])
    h_ref[...] = h
    _store_token_tiles(ht_ref, h, tm)


def mixer_out(o_list, st_list, proj3, x3, w_dw, b_dw, cg, cb, woa_b, woc_b, wout_b, g1, b1, tm=256, chunk=32):
    B, S, _ = x3.shape
    tm = min(tm, S)
    halo_blocks = tm // CONV_HALO
    uv_col = QKV_WIDTH // CONV_CH
    ga_col = (QKV_WIDTH + 2 * CONV_CH) // D_MODEL
    ncol = GROUP_WIDTH // LANES

    def tile(width, col=0):
        return pl.BlockSpec((None, tm, width), lambda b, i: (b, i, col))

    def dilated(gi, width):
        dil = ATT_GROUPS[gi][1]
        return pl.BlockSpec((None, dil, tm // dil, width), lambda b, i: (b, 0, i, 0))

    def halo(col):
        return pl.BlockSpec((None, CONV_HALO, CONV_CH), lambda b, i: (b, jnp.maximum(i * halo_blocks - 1, 0), col))

    def whole(shape):
        return pl.BlockSpec(shape, lambda b, i: (0,) * len(shape))

    return pl.pallas_call(
        functools.partial(_mixer_out_body, tm=tm, chunk=chunk),
        grid=(B, S // tm),
        in_specs=[dilated(g, GROUP_WIDTH) for g in range(N_GROUPS)] + [dilated(g, LANES) for g in range(N_GROUPS)]
        + [tile(CONV_CH, uv_col), tile(CONV_CH, uv_col + 1), halo(uv_col), halo(uv_col + 1),
           tile(D_MODEL, ga_col), tile(D_MODEL, ga_col + 1), tile(D_MODEL),
           whole((CONV_WIDTH, CONV_CH)), whole((1, CONV_CH)), whole((1, CONV_CH)), whole((1, CONV_CH)),
           whole((GROUP_WIDTH, D_MODEL)), whole((CONV_CH, D_MODEL)), whole((D_MODEL, D_MODEL)),
           whole((1, D_MODEL)), whole((1, D_MODEL))],
        out_specs=[tile(D_MODEL), pl.BlockSpec((tm * SUBLANES, LANES), lambda b, i: (b * (S // tm) + i, 0))],
        out_shape=[jax.ShapeDtypeStruct((B, S, D_MODEL), F32),
                   jax.ShapeDtypeStruct((B * S * SUBLANES, LANES), F32)],
        scratch_shapes=[pltpu.VMEM((CONV_HALO + tm, CONV_CH), F32), pltpu.VMEM((tm, CONV_CH), F32),
                        pltpu.VMEM((SUBLANES - 1, tm + CONV_HALO - SUBLANES, CONV_CH), F32),
                        pltpu.VMEM((ncol, tm, LANES), F32), pltpu.VMEM((1, tm, LANES), F32),
                        pltpu.VMEM((ncol, tm, LANES), F32), pltpu.VMEM((1, tm, LANES), F32)],
        compiler_params=_params(("arbitrary", "arbitrary")),
        name="mixer_out",
    )(*o_list, *st_list, proj3, proj3, proj3, proj3, proj3, proj3, x3,
      w_dw, b_dw, cg, cb, woa_b, woc_b, wout_b, g1, b1)


def _router_body(h_ref, wt_ref, b_ref, gates_ref, ids_ref, rank_ref, cnt_ref, carry_ref, tri_ref, *, tm):
    step = pl.program_id(0)

    @pl.when(step == 0)
    def _():
        carry_ref[...] = jnp.zeros_like(carry_ref)
        r_i = lax.broadcasted_iota(jnp.int32, (tm, tm), 0)
        c_i = lax.broadcasted_iota(jnp.int32, (tm, tm), 1)
        tri_ref[...] = (r_i < c_i).astype(BF16)

    nt_dims = (((1,), (1,)), ((), ()))
    h_hi, h_lo = _split_bf16(h_ref[...])
    w_hi, w_lo = _split_bf16(wt_ref[...])
    logits = (lax.dot_general(w_hi, h_hi, nt_dims, preferred_element_type=F32)
              + (lax.dot_general(w_hi, h_lo, nt_dims, preferred_element_type=F32)
                 + lax.dot_general(w_lo, h_hi, nt_dims, preferred_element_type=F32))
              + b_ref[...])
    expert = lax.broadcasted_iota(jnp.int32, (N_EXPERTS, tm), 0).astype(F32)
    row = lax.broadcasted_iota(jnp.int32, (SUBLANES, tm), 0)

    work = logits
    vals, hots = [], []
    ids = jnp.zeros((SUBLANES, tm), F32)
    for k in range(TOP_K):
        v = jnp.max(work, axis=0, keepdims=True)
        idx = jnp.min(jnp.where(work == v, expert, float(N_EXPERTS)), axis=0, keepdims=True)
        hot = expert == idx
        work = jnp.where(hot, -jnp.inf, work)
        ids = jnp.where(row == k, idx, ids)
        vals.append(v)
        hots.append(hot)

    es = [jnp.exp(v - vals[0]) for v in vals]
    esum = es[0] + es[1] + es[2] + es[3]
    gates = jnp.zeros((SUBLANES, tm), F32)
    for k in range(TOP_K):
        gates = jnp.where(row == k, es[k] / esum, gates)

    sel = (hots[0] | hots[1] | hots[2] | hots[3])
    before = jnp.dot(sel.astype(BF16), tri_ref[...], preferred_element_type=F32) + carry_ref[:, 0:1]
    rank = jnp.zeros((SUBLANES, tm), F32)
    for k in range(TOP_K):
        rk = jnp.sum(jnp.where(hots[k], before, 0.0), axis=0, keepdims=True)
        rank = jnp.where(row == k, rk, rank)
    carry_ref[...] = carry_ref[...] + jnp.sum(sel.astype(F32), axis=1, keepdims=True)

    gates_ref[...] = gates
    ids_ref[...] = ids.astype(jnp.int32)
    rank_ref[...] = rank.astype(jnp.int32)
    cnt_ref[...] = carry_ref[...].astype(jnp.int32)


def router(h2, w_router, b_router, tm=512):
    T = h2.shape[0]
    tm = min(tm, T)
    tile = pl.BlockSpec((SUBLANES, tm), lambda i: (0, i))
    return pl.pallas_call(
        functools.partial(_router_body, tm=tm),
        grid=(T // tm,),
        in_specs=[pl.BlockSpec((tm, D_MODEL), lambda i: (i, 0)),
                  pl.BlockSpec((N_EXPERTS, D_MODEL), lambda i: (0, 0)),
                  pl.BlockSpec((N_EXPERTS, 1), lambda i: (0, 0))],
        out_specs=[tile, tile, tile, pl.BlockSpec((N_EXPERTS, LANES), lambda i: (0, 0))],
        out_shape=[jax.ShapeDtypeStruct((SUBLANES, T), F32), jax.ShapeDtypeStruct((SUBLANES, T), jnp.int32),
                   jax.ShapeDtypeStruct((SUBLANES, T), jnp.int32), jax.ShapeDtypeStruct((N_EXPERTS, LANES), jnp.int32)],
        scratch_shapes=[pltpu.VMEM((N_EXPERTS, LANES), F32), pltpu.VMEM((tm, tm), BF16)],
        compiler_params=_params(("arbitrary",)),
        name="router",
    )(h2, w_router.astype(F32).T, b_router.astype(F32).reshape(N_EXPERTS, 1))


def _token_copy(src, dst, s_tok, d_tok, sem, n=1):
    rows = n * SUBLANES
    return pltpu.make_async_copy(src.at[pl.ds(pl.multiple_of(s_tok * SUBLANES, SUBLANES), rows), :],
                                 dst.at[pl.ds(pl.multiple_of(d_tok * SUBLANES, SUBLANES), rows), :], sem)


def _dispatch_body(zstart_ref, zcount_ref, n_used_ref, pos_ref, h_ref, xs_ref, zeros_ref, sem, zsem, *, tm, nb):
    half = EXPERT_BLOCK // 2
    bits = [1 << s for s in reversed(range(half.bit_length()))]

    def zero_fill(e, wait):
        start, count = zstart_ref[e], zcount_ref[e]
        for bit in bits:
            @pl.when((count & bit) != 0)
            def _():
                cp = _token_copy(zeros_ref, xs_ref, 0, 0 if wait else start + (count & ~(2 * bit - 1)), zsem, bit)
                cp.wait() if wait else cp.start()

    def zero_tail(blk, wait):
        for part in range(2):
            cp = _token_copy(zeros_ref, xs_ref, 0, 0 if wait else blk * EXPERT_BLOCK + part * half, zsem, half)
            cp.wait() if wait else cp.start()

    @pl.when(pl.program_id(0) == 0)
    def _():
        zeros_ref[...] = jnp.zeros_like(zeros_ref)
        lax.fori_loop(0, N_EXPERTS, lambda e, c: (zero_fill(e, False), c)[1], 0)
        lax.fori_loop(n_used_ref[0], nb, lambda blk, c: (zero_tail(blk, False), c)[1], 0)

    def issue(t, c):
        for k in range(TOP_K):
            _token_copy(h_ref, xs_ref, t, pos_ref[k, t], sem).start(priority=k % 2)
        return c

    lax.fori_loop(0, tm, issue, 0)
    for k in range(TOP_K):
        _token_copy(h_ref, xs_ref, 0, 0, sem, tm).wait()

    @pl.when(pl.program_id(0) == 0)
    def _():
        lax.fori_loop(0, N_EXPERTS, lambda e, c: (zero_fill(e, True), c)[1], 0)
        lax.fori_loop(n_used_ref[0], nb, lambda blk, c: (zero_tail(blk, True), c)[1], 0)


def dispatch(ht, pos_t, zstart, zcount, n_used, n_rows, tm=512):
    T = ht.shape[0] // SUBLANES
    tm = min(tm, T)
    grid_spec = pltpu.PrefetchScalarGridSpec(
        num_scalar_prefetch=3,
        grid=(T // tm,),
        in_specs=[pl.BlockSpec((TOP_K, tm), lambda i, zs, zc, nu: (0, i), memory_space=pltpu.SMEM),
                  pl.BlockSpec((tm * SUBLANES, LANES), lambda i, zs, zc, nu: (i, 0))],
        out_specs=pl.BlockSpec(memory_space=pl.ANY),
        scratch_shapes=[pltpu.VMEM((EXPERT_BLOCK // 2 * SUBLANES, LANES), F32), pltpu.SemaphoreType.DMA,
                        pltpu.SemaphoreType.DMA],
    )
    return pl.pallas_call(
        functools.partial(_dispatch_body, tm=tm, nb=n_rows // EXPERT_BLOCK),
        grid_spec=grid_spec,
        out_shape=jax.ShapeDtypeStruct((n_rows * SUBLANES, LANES), F32),
        compiler_params=_params(("arbitrary",)),
        name="dispatch",
    )(zstart, zcount, n_used, pos_t, ht)


def _combine_body(pos_ref, pos_next_ref, gates_ref, h_ref, g2_ref, b2_ref, ys_ref, o_ref, buf, sems, *, tm):
    i = pl.program_id(0)
    slot = i % 2

    def issue(p_ref, s):
        def body(t, c):
            for k in range(TOP_K):
                _token_copy(ys_ref, buf.at[s, k], p_ref[k, t], t, sems.at[s]).start(priority=k % 2)
            return c

        lax.fori_loop(0, tm, body, 0)

    @pl.when(i == 0)
    def _():
        issue(pos_ref, 0)

    @pl.when(i + 1 < pl.num_programs(0))
    def _():
        issue(pos_next_ref, 1 - slot)

    for k in range(TOP_K):
        _token_copy(ys_ref, buf.at[slot, k], 0, 0, sems.at[slot], tm).wait()

    gates = gates_ref[...].T
    ffn = gates[:, 0:1] * _load_token_tiles(buf.at[slot, 0], tm)
    for k in range(1, TOP_K):
        ffn = ffn + gates[:, k:k + 1] * _load_token_tiles(buf.at[slot, k], tm)
    o_ref[...] = _layer_norm(DEEPNORM_ALPHA * h_ref[...] + ffn, g2_ref[...], b2_ref[...])


def combine(ys, pos_t, gates_t, h2, g2, b2, tm=256):
    T = h2.shape[0]
    tm = min(tm, T)
    last = T // tm - 1
    return pl.pallas_call(
        functools.partial(_combine_body, tm=tm),
        grid=(T // tm,),
        in_specs=[pl.BlockSpec((TOP_K, tm), lambda i: (0, i), memory_space=pltpu.SMEM),
                  pl.BlockSpec((TOP_K, tm), lambda i: (0, jnp.minimum(i + 1, last)), memory_space=pltpu.SMEM),
                  pl.BlockSpec((SUBLANES, tm), lambda i: (0, i)),
                  pl.BlockSpec((tm, D_MODEL), lambda i: (i, 0)),
                  pl.BlockSpec((1, D_MODEL), lambda i: (0, 0)),
                  pl.BlockSpec((1, D_MODEL), lambda i: (0, 0)),
                  pl.BlockSpec(memory_space=pl.ANY)],
        out_specs=pl.BlockSpec((tm, D_MODEL), lambda i: (i, 0)),
        out_shape=jax.ShapeDtypeStruct((T, D_MODEL), F32),
        scratch_shapes=[pltpu.VMEM((2, TOP_K, tm * SUBLANES, LANES), F32), pltpu.SemaphoreType.DMA((2,))],
        compiler_params=_params(("arbitrary",)),
        name="combine",
    )(pos_t, pos_t, gates_t, h2, g2, b2, ys)


def _experts_body(blk_e, blk_src, n_used, xs_ref, wgu_ref, bgu_ref, wdn_ref, bdn_ref, ys_ref, wgu_b, wdn_b):
    del blk_src
    i = pl.program_id(0)
    used = i < n_used[0]

    @pl.when((i == 0) | (blk_e[i] != blk_e[jnp.maximum(i - 1, 0)]))
    def _():
        wgu_b[...] = wgu_ref[...].astype(BF16)
        wdn_b[...] = wdn_ref[...].astype(BF16)

    @pl.when(used)
    def _():
        x = _load_token_tiles(xs_ref, EXPERT_BLOCK).astype(BF16)
        hgu = jnp.dot(x, wgu_b[...], preferred_element_type=F32) + bgu_ref[...]
        gate = jnp.minimum(hgu[:, :D_FF], SWIGLU_LIMIT)
        up = jnp.clip(hgu[:, D_FF:], -SWIGLU_LIMIT, SWIGLU_LIMIT)
        act = (up + 1.0) * gate * _sigmoid(SWIGLU_ALPHA * gate)
        y = jnp.dot(act.astype(BF16), wdn_b[...], preferred_element_type=F32) + bdn_ref[...]
        _store_token_tiles(ys_ref, y, EXPERT_BLOCK)

    @pl.when(jnp.logical_not(used))
    def _():
        ys_ref[...] = jnp.zeros_like(ys_ref)


def experts(xs, blk_e, blk_src, n_used, wgu, bgu, wdn, bdn):
    n_rows = xs.shape[0] // SUBLANES
    nb = n_rows // EXPERT_BLOCK
    rows = EXPERT_BLOCK * SUBLANES
    grid_spec = pltpu.PrefetchScalarGridSpec(
        num_scalar_prefetch=3,
        grid=(nb,),
        in_specs=[pl.BlockSpec((rows, LANES), lambda i, e, s, n: (s[i], 0)),
                  pl.BlockSpec((None, D_MODEL, 2 * D_FF), lambda i, e, s, n: (e[i], 0, 0)),
                  pl.BlockSpec((None, 1, 2 * D_FF), lambda i, e, s, n: (e[i], 0, 0)),
                  pl.BlockSpec((None, D_FF, D_MODEL), lambda i, e, s, n: (e[i], 0, 0)),
                  pl.BlockSpec((None, 1, D_MODEL), lambda i, e, s, n: (e[i], 0, 0))],
        out_specs=pl.BlockSpec((rows, LANES), lambda i, e, s, n: (i, 0)),
        scratch_shapes=[pltpu.VMEM((D_MODEL, 2 * D_FF), BF16), pltpu.VMEM((D_FF, D_MODEL), BF16)],
    )
    return pl.pallas_call(
        _experts_body,
        grid_spec=grid_spec,
        out_shape=jax.ShapeDtypeStruct((n_rows * SUBLANES, LANES), F32),
        compiler_params=_params(("arbitrary",)),
        name="experts",
    )(blk_e, blk_src, n_used, xs, wgu, bgu, wdn, bdn)


def routing_layout(ids, rank, counts, n_blocks):
    padded = (counts + EXPERT_BLOCK - 1) // EXPERT_BLOCK * EXPERT_BLOCK
    pad_ends = jnp.cumsum(padded)
    pad_starts = pad_ends - padded
    pos_t = (pad_starts[ids] + rank).astype(jnp.int32)
    n_used = jnp.maximum(pad_ends[-1] // EXPERT_BLOCK, 1).astype(jnp.int32)
    blk = jnp.minimum(jnp.arange(n_blocks, dtype=jnp.int32), n_used - 1)
    blk_e = jnp.sum(pad_ends[None, :] <= (blk * EXPERT_BLOCK)[:, None], axis=1)
    blk_e = jnp.minimum(blk_e, N_EXPERTS - 1).astype(jnp.int32)
    zstart = (pad_starts + counts).astype(jnp.int32)
    zcount = (padded - counts).astype(jnp.int32)
    return pos_t, blk_e, blk, n_used.reshape(1), zstart, zcount


DILATE_WIDTH = 4 * LANES


def _dilate_body(x_ref, *refs, n):
    o_refs, slab = refs[:-1], refs[-1]
    for c0 in range(0, DILATE_WIDTH, LANES):
        slab[...] = x_ref[:, c0:c0 + LANES]
        for (_, dil), o_ref in zip(ATT_GROUPS, o_refs):
            for r in range(dil):
                o_ref[r, :, c0:c0 + LANES] = slab[pl.ds(r, n // dil, stride=dil), :].astype(BF16)


def dilate_cast(x3, n=2048):
    B, S, D = x3.shape
    n = min(n, S)
    dils = [dil for _, dil in ATT_GROUPS]
    return pl.pallas_call(
        functools.partial(_dilate_body, n=n),
        grid=(B, S // n, D // DILATE_WIDTH),
        in_specs=[pl.BlockSpec((None, n, DILATE_WIDTH), lambda b, i, c: (b, i, c))],
        out_specs=[pl.BlockSpec((None, dil, n // dil, DILATE_WIDTH), lambda b, i, c: (b, 0, i, c)) for dil in dils],
        out_shape=[jax.ShapeDtypeStruct((B, dil, S // dil, D), BF16) for dil in dils],
        scratch_shapes=[pltpu.VMEM((n, LANES), F32)],
        compiler_params=_params(("arbitrary", "arbitrary", "arbitrary")),
        name="dilate_cast",
    )(x3)


def kernel(x, w_in, rel_bias, w_dw, b_dw, conv_ln_g, conv_ln_b, w_o_attn, w_o_conv, w_out, ln1_g, ln1_b,
           w_router, b_router, w_gate_up, b_gate_up, w_down, b_down, ln2_g, ln2_b):
    B, S, D = x.shape
    T = B * S
    h = x
    bm = bias_tables(rel_bias)
    q_off, k_off, v_off, rest = 0, QKV_WIDTH, 2 * QKV_WIDTH, 3 * QKV_WIDTH
    for l in range(DEPTH):
        wb = w_in[l].astype(BF16)

        def group_cols(gi):
            return [wb[:, off + gi * GROUP_WIDTH:off + (gi + 1) * GROUP_WIDTH] for off in (q_off, k_off, v_off)]

        x_dil = dilate_cast(h)
        xb = x_dil[0].reshape(T, D)
        w_main = jnp.concatenate(group_cols(0) + [wb[:, rest:]], axis=1)
        proj = in_proj(xb, w_main, w_main.shape[1] // 2, "in_proj_main").reshape(B, S, w_main.shape[1])
        o_list, st_list = [], []
        for gi in range(N_GROUPS):
            dil = ATT_GROUPS[gi][1]
            if dil == 1:
                qkv = proj.reshape(B, 1, S, proj.shape[-1])
            else:
                w_g = jnp.concatenate(group_cols(gi), axis=1)
                qkv = in_proj(x_dil[gi].reshape(T, D), w_g, w_g.shape[1], f"in_proj_g{gi}")
                qkv = qkv.reshape(B, dil, S // dil, w_g.shape[1])
            o_g, st_g = attention_group(qkv, bm[gi], gi)
            o_list.append(o_g)
            st_list.append(st_g)
        h1, h1_tiles = mixer_out(o_list, st_list, proj, h,
                       w_dw[l].reshape(CONV_WIDTH, CONV_CH), b_dw[l].reshape(1, CONV_CH),
                       conv_ln_g[l].reshape(1, CONV_CH), conv_ln_b[l].reshape(1, CONV_CH),
                       w_o_attn[l].astype(BF16), w_o_conv[l].astype(BF16), w_out[l].astype(BF16),
                       ln1_g[l].reshape(1, D), ln1_b[l].reshape(1, D))
        h2 = h1.reshape(T, D)
        gates_t, ids_t, rank_t, counts = router(h2, w_router[l], b_router[l])
        n_rows = T * TOP_K + N_EXPERTS * EXPERT_BLOCK
        pos_t, blk_e, blk_src, n_used, zstart, zcount = routing_layout(
            ids_t[:TOP_K], rank_t[:TOP_K], counts[:, 0], n_rows // EXPERT_BLOCK)
        xs = dispatch(h1_tiles, pos_t, zstart, zcount, n_used, n_rows)
        ys = experts(xs, blk_e, blk_src, n_used,
                     w_gate_up[l], b_gate_up[l].reshape(N_EXPERTS, 1, 2 * D_FF),
                     w_down[l], b_down[l].reshape(N_EXPERTS, 1, D))
        out = combine(ys, pos_t, gates_t, h2, ln2_g[l].reshape(1, D), ln2_b[l].reshape(1, D))
        h = out.reshape(B, S, D)
    return h
```

```python
import functools
import math

import jax
import jax.numpy as jnp
from jax import lax
from jax.experimental import pallas as pl
from jax.experimental.pallas import tpu as pltpu

D_MODEL = 1024
ATT_GROUPS = ((128, 1), (512, 4), (2048, 16))
N_GROUPS = len(ATT_GROUPS)
HEADS_PER_GROUP = 8
HEAD_DIM = 64
GROUP_WIDTH = HEADS_PER_GROUP * HEAD_DIM
QKV_WIDTH = N_GROUPS * GROUP_WIDTH
ATT_BLOCK = 128
N_BUCKETS = 32
MAX_DISTANCE = 2048
CONV_CH = 768
CONV_WIDTH = 31
CONV_HALO = 32
N_EXPERTS = 32
TOP_K = 4
D_FF = 1024
SWIGLU_LIMIT = 7.0
SWIGLU_ALPHA = 1.702
IN_WIDTH = 3 * QKV_WIDTH + 2 * CONV_CH + 2 * D_MODEL
LN_EPS = 1e-5
NEG_INF = -1e30
DEPTH = 1
DEEPNORM_ALPHA = (2 * DEPTH) ** 0.25

LANES = 128
SUBLANES = 8
EXPERT_BLOCK = 512
VMEM_LIMIT = 56 * 1024 * 1024

F32 = jnp.float32
BF16 = jnp.bfloat16


def _params(sem, vmem=VMEM_LIMIT):
    return pltpu.CompilerParams(dimension_semantics=sem, vmem_limit_bytes=vmem)


def _sigmoid(x):
    return 0.5 * jnp.tanh(0.5 * x) + 0.5


def _layer_norm(x, g, b):
    mu = jnp.mean(x, axis=-1, keepdims=True)
    xc = x - mu
    var = jnp.mean(xc * xc, axis=-1, keepdims=True)
    return xc * lax.rsqrt(var + LN_EPS) * g + b


def _in_proj_body(x_ref, w_ref, o_ref):
    o_ref[...] = jnp.dot(x_ref[...], w_ref[...], preferred_element_type=F32).astype(o_ref.dtype)


def in_proj(xb, w_b, tn, name, tm=1024):
    T = xb.shape[0]
    N = w_b.shape[1]
    tm = min(tm, T)
    return pl.pallas_call(
        _in_proj_body,
        grid=(N // tn, T // tm),
        in_specs=[pl.BlockSpec((tm, D_MODEL), lambda n, m: (m, 0)),
                  pl.BlockSpec((D_MODEL, tn), lambda n, m: (0, n))],
        out_specs=pl.BlockSpec((tm, tn), lambda n, m: (m, n)),
        out_shape=jax.ShapeDtypeStruct((T, N), BF16),
        compiler_params=_params(("arbitrary", "arbitrary")),
        name=name,
    )(xb, w_b)


def _t5_bucket(dist):
    max_exact = N_BUCKETS // 2
    log_ratio = jnp.log(jnp.maximum(dist, max_exact).astype(F32) / max_exact) / math.log(MAX_DISTANCE / max_exact)
    large = jnp.minimum(max_exact + (log_ratio * (N_BUCKETS - max_exact)).astype(jnp.int32), N_BUCKETS - 1)
    return jnp.where(dist < max_exact, dist, large)


def _bias_body(tbl_ref, bucket_ref, band_ref, o_ref):
    g = pl.program_id(0)
    h = pl.program_id(1)
    col = g * HEADS_PER_GROUP + h
    bucket = bucket_ref[...]
    acc = jnp.zeros(bucket.shape, F32)
    for k in range(N_BUCKETS):
        acc = jnp.where(bucket == k, tbl_ref[k, col], acc)
    band = band_ref[...] != 0
    kj = lax.broadcasted_iota(jnp.int32, bucket.shape, 1)
    o_ref[0] = jnp.where(band, acc, NEG_INF)
    o_ref[1] = jnp.where(band & (kj >= ATT_BLOCK), acc, NEG_INF)


def bias_tables(rel_bias):
    qi = jnp.arange(ATT_BLOCK)[:, None]
    kj = jnp.arange(2 * ATT_BLOCK)[None, :]
    dist = qi - kj + ATT_BLOCK
    buckets, bands = [], []
    for window, dil in ATT_GROUPS:
        bands.append(((dist >= 0) & (dist <= window // dil)).astype(jnp.int32))
        buckets.append(_t5_bucket(jnp.maximum(dist, 0) * dil).astype(jnp.int32))
    buckets = jnp.stack(buckets)
    bands = jnp.stack(bands)
    blk = (None, ATT_BLOCK, 2 * ATT_BLOCK)
    return pl.pallas_call(
        _bias_body,
        grid=(N_GROUPS, HEADS_PER_GROUP),
        in_specs=[pl.BlockSpec(memory_space=pltpu.SMEM),
                  pl.BlockSpec(blk, lambda g, h: (g, 0, 0)),
                  pl.BlockSpec(blk, lambda g, h: (g, 0, 0))],
        out_specs=pl.BlockSpec((None, None, 2, ATT_BLOCK, 2 * ATT_BLOCK), lambda g, h: (g, h, 0, 0, 0)),
        out_shape=jax.ShapeDtypeStruct((N_GROUPS, HEADS_PER_GROUP, 2, ATT_BLOCK, 2 * ATT_BLOCK), F32),
        compiler_params=_params(("arbitrary", "arbitrary")),
        name="bias_tables",
    )(rel_bias.astype(F32), buckets, bands)


def _attn_body(q_ref, kc_ref, kp_ref, vc_ref, vp_ref, bm_ref, o_ref, st_ref, k_all, v_all, s_buf, *, tq):
    i = pl.program_id(2)
    nsub = tq // ATT_BLOCK
    k_all[0:ATT_BLOCK] = kp_ref[...]
    k_all[ATT_BLOCK:] = kc_ref[...]
    v_all[0:ATT_BLOCK] = vp_ref[...]
    v_all[ATT_BLOCK:] = vc_ref[...]
    lo = lax.broadcasted_iota(jnp.int32, (ATT_BLOCK, LANES), 1) < HEAD_DIM
    nt_dims = (((1,), (1,)), ((), ()))
    first = jnp.where(i == 0, 1, 0)

    def scores(s, slot):
        r0 = s * ATT_BLOCK
        for j in range(HEADS_PER_GROUP // 2):
            cs = slice(j * LANES, (j + 1) * LANES)
            qp = q_ref[r0:r0 + ATT_BLOCK, cs] * jnp.asarray(HEAD_DIM ** -0.5, BF16)
            kp = k_all[r0:r0 + 2 * ATT_BLOCK, cs]
            for hh in range(2):
                h = 2 * j + hh
                qh = jnp.where(lo if hh == 0 else ~lo, qp, jnp.zeros_like(qp))
                bias = bm_ref[h, first] if s == 0 else bm_ref[h, 0]
                s_buf[slot, h] = lax.dot_general(qh, kp, nt_dims, preferred_element_type=F32) + bias

    def softmax_pv(s, slot):
        rows = slice(s * ATT_BLOCK, (s + 1) * ATT_BLOCK)
        st_ref[rows, :] = jnp.zeros((ATT_BLOCK, LANES), F32)
        for j in range(HEADS_PER_GROUP // 2):
            vp = v_all[s * ATT_BLOCK:(s + 2) * ATT_BLOCK, j * LANES:(j + 1) * LANES]
            for hh in range(2):
                h = 2 * j + hh
                sc = s_buf[slot, h]
                m = jnp.max(sc, axis=-1, keepdims=True)
                p = jnp.exp(sc - m)
                den = jnp.sum(p, axis=-1, keepdims=True)
                pv = jnp.dot(p.astype(BF16), vp, preferred_element_type=F32) * (1.0 / den)
                c0 = h * HEAD_DIM
                o_ref[rows, c0:c0 + HEAD_DIM] = pv[:, hh * HEAD_DIM:(hh + 1) * HEAD_DIM]
                st_ref[rows, h:h + 1] = m
                st_ref[rows, HEADS_PER_GROUP + h:HEADS_PER_GROUP + h + 1] = den

    scores(0, 0)
    for s in range(nsub):
        if s + 1 < nsub:
            scores(s + 1, (s + 1) % 2)
        softmax_pv(s, s % 2)


def attention_group(qkv, bm_g, gi, tq=512):
    B, dil, L, _ = qkv.shape
    tq = min(tq, L)
    sub = tq // ATT_BLOCK

    def cur(col):
        return pl.BlockSpec((None, None, tq, GROUP_WIDTH), lambda b, r, i: (b, r, i, col))

    def prev(col):
        return pl.BlockSpec((None, None, ATT_BLOCK, GROUP_WIDTH),
                            lambda b, r, i: (b, r, jnp.maximum(i * sub - 1, 0), col))

    return pl.pallas_call(
        functools.partial(_attn_body, tq=tq),
        grid=(B, dil, L // tq),
        in_specs=[cur(0), cur(1), prev(1), cur(2), prev(2),
                  pl.BlockSpec((HEADS_PER_GROUP, 2, ATT_BLOCK, 2 * ATT_BLOCK), lambda b, r, i: (0, 0, 0, 0))],
        out_specs=[pl.BlockSpec((None, None, tq, GROUP_WIDTH), lambda b, r, i: (b, r, i, 0)),
                   pl.BlockSpec((None, None, tq, LANES), lambda b, r, i: (b, r, i, 0))],
        out_shape=[jax.ShapeDtypeStruct((B, dil, L, GROUP_WIDTH), F32),
                   jax.ShapeDtypeStruct((B, dil, L, LANES), F32)],
        scratch_shapes=[pltpu.VMEM((ATT_BLOCK + tq, GROUP_WIDTH), BF16),
                        pltpu.VMEM((ATT_BLOCK + tq, GROUP_WIDTH), BF16),
                        pltpu.VMEM((2, HEADS_PER_GROUP, ATT_BLOCK, 2 * ATT_BLOCK), F32)],
        compiler_params=_params(("arbitrary", "arbitrary", "arbitrary")),
        name=f"attention_g{gi}",
    )(qkv, qkv, qkv, qkv, qkv, bm_g)


def _split_bf16(x):
    hi = x.astype(BF16)
    lo = (x - hi.astype(F32)).astype(BF16)
    return hi, lo


def _load_token_tiles(ref, n):
    return jnp.concatenate([ref[pl.ds(c, n, stride=SUBLANES), :] for c in range(D_MODEL // LANES)], axis=1)


def _store_token_tiles(ref, x, n):
    for c in range(D_MODEL // LANES):
        ref[pl.ds(c, n, stride=SUBLANES), :] = x[:, c * LANES:(c + 1) * LANES]


def _to_token_order(blk_ref, tok_ref, dil, tm):
    n = tm // dil
    for r in range(dil):
        for c in range(tok_ref.shape[0]):
            tok_ref[c, pl.ds(r, n, stride=dil), :] = blk_ref[r, :, c * LANES:(c + 1) * LANES]


def _mixer_out_body(o0, o1, o2, s0, s1, s2, uv_ref, ug_ref, uvh_ref, ugh_ref, ga_ref, gc_ref, x_ref,
                    wdw_ref, bdw_ref, cg_ref, cb_ref, woa_ref, woc_ref, wout_ref, g1_ref, b1_ref,
                    h_ref, ht_ref, glu_ref, dw_ref, shift_ref, tok_o1, tok_s1, tok_o2, tok_s2, *, tm, chunk):
    i = pl.program_id(1)

    ncol = GROUP_WIDTH // LANES
    _to_token_order(o1, tok_o1, ATT_GROUPS[1][1], tm)
    _to_token_order(s1, tok_s1, ATT_GROUPS[1][1], tm)
    _to_token_order(o2, tok_o2, ATT_GROUPS[2][1], tm)
    _to_token_order(s2, tok_s2, ATT_GROUPS[2][1], tm)
    outs = [o0[0],
            jnp.concatenate([tok_o1[c] for c in range(ncol)], axis=1),
            jnp.concatenate([tok_o2[c] for c in range(ncol)], axis=1)]
    sts = [s0[0], tok_s1[0], tok_s2[0]]

    mx = jnp.maximum(jnp.maximum(sts[0], sts[1]), sts[2])
    wts = [pltpu.roll(st, LANES - HEADS_PER_GROUP, axis=1) * jnp.exp(st - mx) for st in sts]
    wsum = wts[0] + wts[1] + wts[2]
    row = lax.broadcasted_iota(jnp.int32, (LANES, GROUP_WIDTH), 0)
    colh = lax.broadcasted_iota(jnp.int32, (LANES, GROUP_WIDTH), 1) // HEAD_DIM
    expand = (row == colh).astype(BF16)
    attn = jnp.zeros((tm, GROUP_WIDTH), F32)
    head_lane = lax.broadcasted_iota(jnp.int32, (tm, LANES), 1) < HEADS_PER_GROUP
    for wt, o in zip(wts, outs):
        c_hi, c_lo = _split_bf16(jnp.where(head_lane, wt / wsum, 0.0))
        c = (jnp.dot(c_hi, expand, preferred_element_type=F32)
             + jnp.dot(c_lo, expand, preferred_element_type=F32))
        attn = attn + c * o
    a_out = jnp.dot(attn.astype(BF16), woa_ref[...], preferred_element_type=F32)

    gh = uvh_ref[...].astype(F32) * _sigmoid(ugh_ref[...].astype(F32))
    glu_ref[0:CONV_HALO] = jnp.where(i == 0, 0.0, gh)
    glu_ref[CONV_HALO:] = uv_ref[...].astype(F32) * _sigmoid(ug_ref[...].astype(F32))
    first_tap = CONV_HALO - (CONV_WIDTH - 1)
    for b in range(1, SUBLANES):
        shift_ref[b - 1] = glu_ref[b:b + shift_ref.shape[1], :]

    for c0 in range(0, CONV_CH, LANES):
        cs = slice(c0, c0 + LANES)
        bias = jnp.broadcast_to(bdw_ref[:, cs], (chunk, LANES))
        for r0 in range(0, tm, chunk):
            acc = bias
            for j in range(CONV_WIDTH):
                a, b = divmod(first_tap + j, SUBLANES)
                lo_row = r0 + a * SUBLANES
                rows = glu_ref[lo_row:lo_row + chunk, cs] if b == 0 else shift_ref[b - 1, lo_row:lo_row + chunk, cs]
                acc = acc + wdw_ref[j:j + 1, cs] * rows
            dw_ref[r0:r0 + chunk, cs] = acc
    cn = _layer_norm(dw_ref[...], cg_ref[...], cb_ref[...])
    conv = cn * _sigmoid(cn)
    c_out = jnp.dot(conv.astype(BF16), woc_ref[...], preferred_element_type=F32)

    merged = (_sigmoid(ga_ref[...].astype(F32)) * a_out + _sigmoid(gc_ref[...].astype(F32)) * c_out)
    mix = jnp.dot(merged.astype(BF16), wout_ref[...], preferred_element_type=F32)
    h = _layer_norm(DEEPNORM_ALPHA * x_ref[...] + mix, g1_ref[...], b1_ref[...])
    h_ref[...] = h
    _store_token_tiles(ht_ref, h, tm)


def mixer_out(o_list, st_list, proj3, x3, w_dw, b_dw, cg, cb, woa_b, woc_b, wout_b, g1, b1, tm=256, chunk=32):
    B, S, _ = x3.shape
    tm = min(tm, S)
    halo_blocks = tm // CONV_HALO
    uv_col = QKV_WIDTH // CONV_CH
    ga_col = (QKV_WIDTH + 2 * CONV_CH) // D_MODEL
    ncol = GROUP_WIDTH // LANES

    def tile(width, col=0):
        return pl.BlockSpec((None, tm, width), lambda b, i: (b, i, col))

    def dilated(gi, width):
        dil = ATT_GROUPS[gi][1]
        return pl.BlockSpec((None, dil, tm // dil, width), lambda b, i: (b, 0, i, 0))

    def halo(col):
        return pl.BlockSpec((None, CONV_HALO, CONV_CH), lambda b, i: (b, jnp.maximum(i * halo_blocks - 1, 0), col))

    def whole(shape):
        return pl.BlockSpec(shape, lambda b, i: (0,) * len(shape))

    return pl.pallas_call(
        functools.partial(_mixer_out_body, tm=tm, chunk=chunk),
        grid=(B, S // tm),
        in_specs=[dilated(g, GROUP_WIDTH) for g in range(N_GROUPS)] + [dilated(g, LANES) for g in range(N_GROUPS)]
        + [tile(CONV_CH, uv_col), tile(CONV_CH, uv_col + 1), halo(uv_col), halo(uv_col + 1),
           tile(D_MODEL, ga_col), tile(D_MODEL, ga_col + 1), tile(D_MODEL),
           whole((CONV_WIDTH, CONV_CH)), whole((1, CONV_CH)), whole((1, CONV_CH)), whole((1, CONV_CH)),
           whole((GROUP_WIDTH, D_MODEL)), whole((CONV_CH, D_MODEL)), whole((D_MODEL, D_MODEL)),
           whole((1, D_MODEL)), whole((1, D_MODEL))],
        out_specs=[tile(D_MODEL), pl.BlockSpec((tm * SUBLANES, LANES), lambda b, i: (b * (S // tm) + i, 0))],
        out_shape=[jax.ShapeDtypeStruct((B, S, D_MODEL), F32),
                   jax.ShapeDtypeStruct((B * S * SUBLANES, LANES), F32)],
        scratch_shapes=[pltpu.VMEM((CONV_HALO + tm, CONV_CH), F32), pltpu.VMEM((tm, CONV_CH), F32),
                        pltpu.VMEM((SUBLANES - 1, tm + CONV_HALO - SUBLANES, CONV_CH), F32),
                        pltpu.VMEM((ncol, tm, LANES), F32), pltpu.VMEM((1, tm, LANES), F32),
                        pltpu.VMEM((ncol, tm, LANES), F32), pltpu.VMEM((1, tm, LANES), F32)],
        compiler_params=_params(("arbitrary", "arbitrary")),
        name="mixer_out",
    )(*o_list, *st_list, proj3, proj3, proj3, proj3, proj3, proj3, x3,
      w_dw, b_dw, cg, cb, woa_b, woc_b, wout_b, g1, b1)


def _router_body(h_ref, wt_ref, b_ref, gates_ref, ids_ref, rank_ref, cnt_ref, carry_ref, tri_ref, *, tm):
    step = pl.program_id(0)

    @pl.when(step == 0)
    def _():
        carry_ref[...] = jnp.zeros_like(carry_ref)
        r_i = lax.broadcasted_iota(jnp.int32, (tm, tm), 0)
        c_i = lax.broadcasted_iota(jnp.int32, (tm, tm), 1)
        tri_ref[...] = (r_i < c_i).astype(BF16)

    nt_dims = (((1,), (1,)), ((), ()))
    h_hi, h_lo = _split_bf16(h_ref[...])
    w_hi, w_lo = _split_bf16(wt_ref[...])
    logits = (lax.dot_general(w_hi, h_hi, nt_dims, preferred_element_type=F32)
              + (lax.dot_general(w_hi, h_lo, nt_dims, preferred_element_type=F32)
                 + lax.dot_general(w_lo, h_hi, nt_dims, preferred_element_type=F32))
              + b_ref[...])
    expert = lax.broadcasted_iota(jnp.int32, (N_EXPERTS, tm), 0).astype(F32)
    row = lax.broadcasted_iota(jnp.int32, (SUBLANES, tm), 0)

    work = logits
    vals, hots = [], []
    ids = jnp.zeros((SUBLANES, tm), F32)
    for k in range(TOP_K):
        v = jnp.max(work, axis=0, keepdims=True)
        idx = jnp.min(jnp.where(work == v, expert, float(N_EXPERTS)), axis=0, keepdims=True)
        hot = expert == idx
        work = jnp.where(hot, -jnp.inf, work)
        ids = jnp.where(row == k, idx, ids)
        vals.append(v)
        hots.append(hot)

    es = [jnp.exp(v - vals[0]) for v in vals]
    esum = es[0] + es[1] + es[2] + es[3]
    gates = jnp.zeros((SUBLANES, tm), F32)
    for k in range(TOP_K):
        gates = jnp.where(row == k, es[k] / esum, gates)

    sel = (hots[0] | hots[1] | hots[2] | hots[3])
    before = jnp.dot(sel.astype(BF16), tri_ref[...], preferred_element_type=F32) + carry_ref[:, 0:1]
    rank = jnp.zeros((SUBLANES, tm), F32)
    for k in range(TOP_K):
        rk = jnp.sum(jnp.where(hots[k], before, 0.0), axis=0, keepdims=True)
        rank = jnp.where(row == k, rk, rank)
    carry_ref[...] = carry_ref[...] + jnp.sum(sel.astype(F32), axis=1, keepdims=True)

    gates_ref[...] = gates
    ids_ref[...] = ids.astype(jnp.int32)
    rank_ref[...] = rank.astype(jnp.int32)
    cnt_ref[...] = carry_ref[...].astype(jnp.int32)


def router(h2, w_router, b_router, tm=512):
    T = h2.shape[0]
    tm = min(tm, T)
    tile = pl.BlockSpec((SUBLANES, tm), lambda i: (0, i))
    return pl.pallas_call(
        functools.partial(_router_body, tm=tm),
        grid=(T // tm,),
        in_specs=[pl.BlockSpec((tm, D_MODEL), lambda i: (i, 0)),
                  pl.BlockSpec((N_EXPERTS, D_MODEL), lambda i: (0, 0)),
                  pl.BlockSpec((N_EXPERTS, 1), lambda i: (0, 0))],
        out_specs=[tile, tile, tile, pl.BlockSpec((N_EXPERTS, LANES), lambda i: (0, 0))],
        out_shape=[jax.ShapeDtypeStruct((SUBLANES, T), F32), jax.ShapeDtypeStruct((SUBLANES, T), jnp.int32),
                   jax.ShapeDtypeStruct((SUBLANES, T), jnp.int32), jax.ShapeDtypeStruct((N_EXPERTS, LANES), jnp.int32)],
        scratch_shapes=[pltpu.VMEM((N_EXPERTS, LANES), F32), pltpu.VMEM((tm, tm), BF16)],
        compiler_params=_params(("arbitrary",)),
        name="router",
    )(h2, w_router.astype(F32).T, b_router.astype(F32).reshape(N_EXPERTS, 1))


def _token_copy(src, dst, s_tok, d_tok, sem, n=1):
    rows = n * SUBLANES
    return pltpu.make_async_copy(src.at[pl.ds(pl.multiple_of(s_tok * SUBLANES, SUBLANES), rows), :],
                                 dst.at[pl.ds(pl.multiple_of(d_tok * SUBLANES, SUBLANES), rows), :], sem)


def _dispatch_body(zstart_ref, zcount_ref, n_used_ref, pos_ref, h_ref, xs_ref, zeros_ref, sem, zsem, *, tm, nb):
    half = EXPERT_BLOCK // 2
    bits = [1 << s for s in reversed(range(half.bit_length()))]

    def zero_fill(e, wait):
        start, count = zstart_ref[e], zcount_ref[e]
        for bit in bits:
            @pl.when((count & bit) != 0)
            def _():
                cp = _token_copy(zeros_ref, xs_ref, 0, 0 if wait else start + (count & ~(2 * bit - 1)), zsem, bit)
                cp.wait() if wait else cp.start()

    def zero_tail(blk, wait):
        for part in range(2):
            cp = _token_copy(zeros_ref, xs_ref, 0, 0 if wait else blk * EXPERT_BLOCK + part * half, zsem, half)
            cp.wait() if wait else cp.start()

    @pl.when(pl.program_id(0) == 0)
    def _():
        zeros_ref[...] = jnp.zeros_like(zeros_ref)
        lax.fori_loop(0, N_EXPERTS, lambda e, c: (zero_fill(e, False), c)[1], 0)
        lax.fori_loop(n_used_ref[0], nb, lambda blk, c: (zero_tail(blk, False), c)[1], 0)

    def issue(t, c):
        for k in range(TOP_K):
            _token_copy(h_ref, xs_ref, t, pos_ref[k, t], sem).start(priority=k % 2)
        return c

    lax.fori_loop(0, tm, issue, 0)
    for k in range(TOP_K):
        _token_copy(h_ref, xs_ref, 0, 0, sem, tm).wait()

    @pl.when(pl.program_id(0) == 0)
    def _():
        lax.fori_loop(0, N_EXPERTS, lambda e, c: (zero_fill(e, True), c)[1], 0)
        lax.fori_loop(n_used_ref[0], nb, lambda blk, c: (zero_tail(blk, True), c)[1], 0)


def dispatch(ht, pos_t, zstart, zcount, n_used, n_rows, tm=512):
    T = ht.shape[0] // SUBLANES
    tm = min(tm, T)
    grid_spec = pltpu.PrefetchScalarGridSpec(
        num_scalar_prefetch=3,
        grid=(T // tm,),
        in_specs=[pl.BlockSpec((TOP_K, tm), lambda i, zs, zc, nu: (0, i), memory_space=pltpu.SMEM),
                  pl.BlockSpec((tm * SUBLANES, LANES), lambda i, zs, zc, nu: (i, 0))],
        out_specs=pl.BlockSpec(memory_space=pl.ANY),
        scratch_shapes=[pltpu.VMEM((EXPERT_BLOCK // 2 * SUBLANES, LANES), F32), pltpu.SemaphoreType.DMA,
                        pltpu.SemaphoreType.DMA],
    )
    return pl.pallas_call(
        functools.partial(_dispatch_body, tm=tm, nb=n_rows // EXPERT_BLOCK),
        grid_spec=grid_spec,
        out_shape=jax.ShapeDtypeStruct((n_rows * SUBLANES, LANES), F32),
        compiler_params=_params(("arbitrary",)),
        name="dispatch",
    )(zstart, zcount, n_used, pos_t, ht)


def _combine_body(pos_ref, pos_next_ref, gates_ref, h_ref, g2_ref, b2_ref, ys_ref, o_ref, buf, sems, *, tm):
    i = pl.program_id(0)
    slot = i % 2

    def issue(p_ref, s):
        def body(t, c):
            for k in range(TOP_K):
                _token_copy(ys_ref, buf.at[s, k], p_ref[k, t], t, sems.at[s]).start(priority=k % 2)
            return c

        lax.fori_loop(0, tm, body, 0)

    @pl.when(i == 0)
    def _():
        issue(pos_ref, 0)

    @pl.when(i + 1 < pl.num_programs(0))
    def _():
        issue(pos_next_ref, 1 - slot)

    for k in range(TOP_K):
        _token_copy(ys_ref, buf.at[slot, k], 0, 0, sems.at[slot], tm).wait()

    gates = gates_ref[...].T
    ffn = gates[:, 0:1] * _load_token_tiles(buf.at[slot, 0], tm)
    for k in range(1, TOP_K):
        ffn = ffn + gates[:, k:k + 1] * _load_token_tiles(buf.at[slot, k], tm)
    o_ref[...] = _layer_norm(DEEPNORM_ALPHA * h_ref[...] + ffn, g2_ref[...], b2_ref[...])


def combine(ys, pos_t, gates_t, h2, g2, b2, tm=256):
    T = h2.shape[0]
    tm = min(tm, T)
    last = T // tm - 1
    return pl.pallas_call(
        functools.partial(_combine_body, tm=tm),
        grid=(T // tm,),
        in_specs=[pl.BlockSpec((TOP_K, tm), lambda i: (0, i), memory_space=pltpu.SMEM),
                  pl.BlockSpec((TOP_K, tm), lambda i: (0, jnp.minimum(i + 1, last)), memory_space=pltpu.SMEM),
                  pl.BlockSpec((SUBLANES, tm), lambda i: (0, i)),
                  pl.BlockSpec((tm, D_MODEL), lambda i: (i, 0)),
                  pl.BlockSpec((1, D_MODEL), lambda i: (0, 0)),
                  pl.BlockSpec((1, D_MODEL), lambda i: (0, 0)),
                  pl.BlockSpec(memory_space=pl.ANY)],
        out_specs=pl.BlockSpec((tm, D_MODEL), lambda i: (i, 0)),
        out_shape=jax.ShapeDtypeStruct((T, D_MODEL), F32),
        scratch_shapes=[pltpu.VMEM((2, TOP_K, tm * SUBLANES, LANES), F32), pltpu.SemaphoreType.DMA((2,))],
        compiler_params=_params(("arbitrary",)),
        name="combine",
    )(pos_t, pos_t, gates_t, h2, g2, b2, ys)


def _experts_body(blk_e, blk_src, n_used, xs_ref, wgu_ref, bgu_ref, wdn_ref, bdn_ref, ys_ref, wgu_b, wdn_b):
    del blk_src
    i = pl.program_id(0)
    used = i < n_used[0]

    @pl.when((i == 0) | (blk_e[i] != blk_e[jnp.maximum(i - 1, 0)]))
    def _():
        wgu_b[...] = wgu_ref[...].astype(BF16)
        wdn_b[...] = wdn_ref[...].astype(BF16)

    @pl.when(used)
    def _():
        x = _load_token_tiles(xs_ref, EXPERT_BLOCK).astype(BF16)
        hgu = jnp.dot(x, wgu_b[...], preferred_element_type=F32) + bgu_ref[...]
        gate = jnp.minimum(hgu[:, :D_FF], SWIGLU_LIMIT)
        up = jnp.clip(hgu[:, D_FF:], -SWIGLU_LIMIT, SWIGLU_LIMIT)
        act = (up + 1.0) * gate * _sigmoid(SWIGLU_ALPHA * gate)
        y = jnp.dot(act.astype(BF16), wdn_b[...], preferred_element_type=F32) + bdn_ref[...]
        _store_token_tiles(ys_ref, y, EXPERT_BLOCK)

    @pl.when(jnp.logical_not(used))
    def _():
        ys_ref[...] = jnp.zeros_like(ys_ref)


def experts(xs, blk_e, blk_src, n_used, wgu, bgu, wdn, bdn):
    n_rows = xs.shape[0] // SUBLANES
    nb = n_rows // EXPERT_BLOCK
    rows = EXPERT_BLOCK * SUBLANES
    grid_spec = pltpu.PrefetchScalarGridSpec(
        num_scalar_prefetch=3,
        grid=(nb,),
        in_specs=[pl.BlockSpec((rows, LANES), lambda i, e, s, n: (s[i], 0)),
                  pl.BlockSpec((None, D_MODEL, 2 * D_FF), lambda i, e, s, n: (e[i], 0, 0)),
                  pl.BlockSpec((None, 1, 2 * D_FF), lambda i, e, s, n: (e[i], 0, 0)),
                  pl.BlockSpec((None, D_FF, D_MODEL), lambda i, e, s, n: (e[i], 0, 0)),
                  pl.BlockSpec((None, 1, D_MODEL), lambda i, e, s, n: (e[i], 0, 0))],
        out_specs=pl.BlockSpec((rows, LANES), lambda i, e, s, n: (i, 0)),
        scratch_shapes=[pltpu.VMEM((D_MODEL, 2 * D_FF), BF16), pltpu.VMEM((D_FF, D_MODEL), BF16)],
    )
    return pl.pallas_call(
        _experts_body,
        grid_spec=grid_spec,
        out_shape=jax.ShapeDtypeStruct((n_rows * SUBLANES, LANES), F32),
        compiler_params=_params(("arbitrary",)),
        name="experts",
    )(blk_e, blk_src, n_used, xs, wgu, bgu, wdn, bdn)


def routing_layout(ids, rank, counts, n_blocks):
    padded = (counts + EXPERT_BLOCK - 1) // EXPERT_BLOCK * EXPERT_BLOCK
    pad_ends = jnp.cumsum(padded)
    pad_starts = pad_ends - padded
    start_of = jnp.zeros_like(ids)
    for e in range(N_EXPERTS):
        start_of = jnp.where(ids == e, pad_starts[e], start_of)
    pos_t = (start_of + rank).astype(jnp.int32)
    n_used = jnp.maximum(pad_ends[-1] // EXPERT_BLOCK, 1).astype(jnp.int32)
    blk = jnp.minimum(jnp.arange(n_blocks, dtype=jnp.int32), n_used - 1)
    blk_e = jnp.sum(pad_ends[None, :] <= (blk * EXPERT_BLOCK)[:, None], axis=1)
    blk_e = jnp.minimum(blk_e, N_EXPERTS - 1).astype(jnp.int32)
    zstart = (pad_starts + counts).astype(jnp.int32)
    zcount = (padded - counts).astype(jnp.int32)
    return pos_t, blk_e, blk, n_used.reshape(1), zstart, zcount


DILATE_WIDTH = 4 * LANES


def _dilate_body(x_ref, *refs, n):
    o_refs, slab = refs[:-1], refs[-1]
    for c0 in range(0, DILATE_WIDTH, LANES):
        slab[...] = x_ref[:, c0:c0 + LANES]
        for (_, dil), o_ref in zip(ATT_GROUPS, o_refs):
            for r in range(dil):
                o_ref[r, :, c0:c0 + LANES] = slab[pl.ds(r, n // dil, stride=dil), :].astype(BF16)


def dilate_cast(x3, n=2048):
    B, S, D = x3.shape
    n = min(n, S)
    dils = [dil for _, dil in ATT_GROUPS]
    return pl.pallas_call(
        functools.partial(_dilate_body, n=n),
        grid=(B, S // n, D // DILATE_WIDTH),
        in_specs=[pl.BlockSpec((None, n, DILATE_WIDTH), lambda b, i, c: (b, i, c))],
        out_specs=[pl.BlockSpec((None, dil, n // dil, DILATE_WIDTH), lambda b, i, c: (b, 0, i, c)) for dil in dils],
        out_shape=[jax.ShapeDtypeStruct((B, dil, S // dil, D), BF16) for dil in dils],
        scratch_shapes=[pltpu.VMEM((n, LANES), F32)],
        compiler_params=_params(("arbitrary", "arbitrary", "arbitrary")),
        name="dilate_cast",
    )(x3)


def kernel(x, w_in, rel_bias, w_dw, b_dw, conv_ln_g, conv_ln_b, w_o_attn, w_o_conv, w_out, ln1_g, ln1_b,
           w_router, b_router, w_gate_up, b_gate_up, w_down, b_down, ln2_g, ln2_b):
    B, S, D = x.shape
    T = B * S
    h = x
    bm = bias_tables(rel_bias)
    q_off, k_off, v_off, rest = 0, QKV_WIDTH, 2 * QKV_WIDTH, 3 * QKV_WIDTH
    for l in range(DEPTH):
        wb = w_in[l].astype(BF16)

        def group_cols(gi):
            return [wb[:, off + gi * GROUP_WIDTH:off + (gi + 1) * GROUP_WIDTH] for off in (q_off, k_off, v_off)]

        x_dil = dilate_cast(h)
        xb = x_dil[0].reshape(T, D)
        w_main = jnp.concatenate(group_cols(0) + [wb[:, rest:]], axis=1)
        proj = in_proj(xb, w_main, w_main.shape[1] // 2, "in_proj_main").reshape(B, S, w_main.shape[1])
        o_list, st_list = [], []
        for gi in range(N_GROUPS):
            dil = ATT_GROUPS[gi][1]
            if dil == 1:
                qkv = proj.reshape(B, 1, S, proj.shape[-1])
            else:
                w_g = jnp.concatenate(group_cols(gi), axis=1)
                qkv = in_proj(x_dil[gi].reshape(T, D), w_g, w_g.shape[1], f"in_proj_g{gi}")
                qkv = qkv.reshape(B, dil, S // dil, w_g.shape[1])
            o_g, st_g = attention_group(qkv, bm[gi], gi)
            o_list.append(o_g)
            st_list.append(st_g)
        h1, h1_tiles = mixer_out(o_list, st_list, proj, h,
                       w_dw[l].reshape(CONV_WIDTH, CONV_CH), b_dw[l].reshape(1, CONV_CH),
                       conv_ln_g[l].reshape(1, CONV_CH), conv_ln_b[l].reshape(1, CONV_CH),
                       w_o_attn[l].astype(BF16), w_o_conv[l].astype(BF16), w_out[l].astype(BF16),
                       ln1_g[l].reshape(1, D), ln1_b[l].reshape(1, D))
        h2 = h1.reshape(T, D)
        gates_t, ids_t, rank_t, counts = router(h2, w_router[l], b_router[l])
        n_rows = T * TOP_K + N_EXPERTS * EXPERT_BLOCK
        pos_t, blk_e, blk_src, n_used, zstart, zcount = routing_layout(
            ids_t[:TOP_K], rank_t[:TOP_K], counts[:, 0], n_rows // EXPERT_BLOCK)
        xs = dispatch(h1_tiles, pos_t, zstart, zcount, n_used, n_rows)
        ys = experts(xs, blk_e, blk_src, n_used,
                     w_gate_up[l], b_gate_up[l].reshape(N_EXPERTS, 1, 2 * D_FF),
                     w_down[l], b_down[l].reshape(N_EXPERTS, 1, D))
        out = combine(ys, pos_t, gates_t, h2, ln2_g[l].reshape(1, D), ln2_b[l].reshape(1, D))
        h = out.reshape(B, S, D)
    return h
```

```python
import functools
import math

import jax
import jax.numpy as jnp
from jax import lax
from jax.experimental import pallas as pl
from jax.experimental.pallas import tpu as pltpu

D_MODEL = 1024
ATT_GROUPS = ((128, 1), (512, 4), (2048, 16))
N_GROUPS = len(ATT_GROUPS)
HEADS_PER_GROUP = 8
HEAD_DIM = 64
GROUP_WIDTH = HEADS_PER_GROUP * HEAD_DIM
QKV_WIDTH = N_GROUPS * GROUP_WIDTH
ATT_BLOCK = 128
N_BUCKETS = 32
MAX_DISTANCE = 2048
CONV_CH = 768
CONV_WIDTH = 31
CONV_HALO = 32
N_EXPERTS = 32
TOP_K = 4
D_FF = 1024
SWIGLU_LIMIT = 7.0
SWIGLU_ALPHA = 1.702
IN_WIDTH = 3 * QKV_WIDTH + 2 * CONV_CH + 2 * D_MODEL
LN_EPS = 1e-5
NEG_INF = -1e30
DEPTH = 1
DEEPNORM_ALPHA = (2 * DEPTH) ** 0.25

LANES = 128
SUBLANES = 8
EXPERT_BLOCK = 512
VMEM_LIMIT = 56 * 1024 * 1024

F32 = jnp.float32
BF16 = jnp.bfloat16


def _params(sem, vmem=VMEM_LIMIT):
    return pltpu.CompilerParams(dimension_semantics=sem, vmem_limit_bytes=vmem)


def _sigmoid(x):
    return 0.5 * jnp.tanh(0.5 * x) + 0.5


def _layer_norm(x, g, b):
    mu = jnp.mean(x, axis=-1, keepdims=True)
    xc = x - mu
    var = jnp.mean(xc * xc, axis=-1, keepdims=True)
    return xc * lax.rsqrt(var + LN_EPS) * g + b


def _in_proj_body(x_ref, w_ref, o_ref):
    o_ref[...] = jnp.dot(x_ref[...], w_ref[...], preferred_element_type=F32).astype(o_ref.dtype)


def in_proj(xb, w_b, tn, name, tm=1024):
    T = xb.shape[0]
    N = w_b.shape[1]
    tm = min(tm, T)
    return pl.pallas_call(
        _in_proj_body,
        grid=(N // tn, T // tm),
        in_specs=[pl.BlockSpec((tm, D_MODEL), lambda n, m: (m, 0)),
                  pl.BlockSpec((D_MODEL, tn), lambda n, m: (0, n))],
        out_specs=pl.BlockSpec((tm, tn), lambda n, m: (m, n)),
        out_shape=jax.ShapeDtypeStruct((T, N), BF16),
        compiler_params=_params(("arbitrary", "arbitrary")),
        name=name,
    )(xb, w_b)


def _t5_bucket(dist):
    max_exact = N_BUCKETS // 2
    log_ratio = jnp.log(jnp.maximum(dist, max_exact).astype(F32) / max_exact) / math.log(MAX_DISTANCE / max_exact)
    large = jnp.minimum(max_exact + (log_ratio * (N_BUCKETS - max_exact)).astype(jnp.int32), N_BUCKETS - 1)
    return jnp.where(dist < max_exact, dist, large)


def _bias_body(tbl_ref, bucket_ref, band_ref, o_ref):
    g = pl.program_id(0)
    h = pl.program_id(1)
    col = g * HEADS_PER_GROUP + h
    bucket = bucket_ref[...]
    acc = jnp.zeros(bucket.shape, F32)
    for k in range(N_BUCKETS):
        acc = jnp.where(bucket == k, tbl_ref[k, col], acc)
    band = band_ref[...] != 0
    kj = lax.broadcasted_iota(jnp.int32, bucket.shape, 1)
    o_ref[0] = jnp.where(band, acc, NEG_INF)
    o_ref[1] = jnp.where(band & (kj >= ATT_BLOCK), acc, NEG_INF)


def bias_tables(rel_bias):
    qi = jnp.arange(ATT_BLOCK)[:, None]
    kj = jnp.arange(2 * ATT_BLOCK)[None, :]
    dist = qi - kj + ATT_BLOCK
    buckets, bands = [], []
    for window, dil in ATT_GROUPS:
        bands.append(((dist >= 0) & (dist <= window // dil)).astype(jnp.int32))
        buckets.append(_t5_bucket(jnp.maximum(dist, 0) * dil).astype(jnp.int32))
    buckets = jnp.stack(buckets)
    bands = jnp.stack(bands)
    blk = (None, ATT_BLOCK, 2 * ATT_BLOCK)
    return pl.pallas_call(
        _bias_body,
        grid=(N_GROUPS, HEADS_PER_GROUP),
        in_specs=[pl.BlockSpec(memory_space=pltpu.SMEM),
                  pl.BlockSpec(blk, lambda g, h: (g, 0, 0)),
                  pl.BlockSpec(blk, lambda g, h: (g, 0, 0))],
        out_specs=pl.BlockSpec((None, None, 2, ATT_BLOCK, 2 * ATT_BLOCK), lambda g, h: (g, h, 0, 0, 0)),
        out_shape=jax.ShapeDtypeStruct((N_GROUPS, HEADS_PER_GROUP, 2, ATT_BLOCK, 2 * ATT_BLOCK), F32),
        compiler_params=_params(("arbitrary", "arbitrary")),
        name="bias_tables",
    )(rel_bias.astype(F32), buckets, bands)


def _attn_body(q_ref, kc_ref, kp_ref, vc_ref, vp_ref, bm_ref, o_ref, st_ref, k_all, v_all, s_buf, *, tq):
    i = pl.program_id(2)
    nsub = tq // ATT_BLOCK
    k_all[0:ATT_BLOCK] = kp_ref[...]
    k_all[ATT_BLOCK:] = kc_ref[...]
    v_all[0:ATT_BLOCK] = vp_ref[...]
    v_all[ATT_BLOCK:] = vc_ref[...]
    lo = lax.broadcasted_iota(jnp.int32, (ATT_BLOCK, LANES), 1) < HEAD_DIM
    nt_dims = (((1,), (1,)), ((), ()))
    first = jnp.where(i == 0, 1, 0)

    def scores(s, slot):
        r0 = s * ATT_BLOCK
        for j in range(HEADS_PER_GROUP // 2):
            cs = slice(j * LANES, (j + 1) * LANES)
            qp = q_ref[r0:r0 + ATT_BLOCK, cs] * jnp.asarray(HEAD_DIM ** -0.5, BF16)
            kp = k_all[r0:r0 + 2 * ATT_BLOCK, cs]
            for hh in range(2):
                h = 2 * j + hh
                qh = jnp.where(lo if hh == 0 else ~lo, qp, jnp.zeros_like(qp))
                bias = bm_ref[h, first] if s == 0 else bm_ref[h, 0]
                s_buf[slot, h] = lax.dot_general(qh, kp, nt_dims, preferred_element_type=F32) + bias

    def softmax_pv(s, slot):
        rows = slice(s * ATT_BLOCK, (s + 1) * ATT_BLOCK)
        st_ref[rows, :] = jnp.zeros((ATT_BLOCK, LANES), F32)
        for j in range(HEADS_PER_GROUP // 2):
            vp = v_all[s * ATT_BLOCK:(s + 2) * ATT_BLOCK, j * LANES:(j + 1) * LANES]
            for hh in range(2):
                h = 2 * j + hh
                sc = s_buf[slot, h]
                m = jnp.max(sc, axis=-1, keepdims=True)
                p = jnp.exp(sc - m)
                den = jnp.sum(p, axis=-1, keepdims=True)
                pv = jnp.dot(p.astype(BF16), vp, preferred_element_type=F32) * (1.0 / den)
                c0 = h * HEAD_DIM
                o_ref[rows, c0:c0 + HEAD_DIM] = pv[:, hh * HEAD_DIM:(hh + 1) * HEAD_DIM]
                st_ref[rows, h:h + 1] = m
                st_ref[rows, HEADS_PER_GROUP + h:HEADS_PER_GROUP + h + 1] = den

    scores(0, 0)
    for s in range(nsub):
        if s + 1 < nsub:
            scores(s + 1, (s + 1) % 2)
        softmax_pv(s, s % 2)


def attention_group(qkv, bm_g, gi, tq=512):
    B, dil, L, _ = qkv.shape
    tq = min(tq, L)
    sub = tq // ATT_BLOCK

    def cur(col):
        return pl.BlockSpec((None, None, tq, GROUP_WIDTH), lambda b, r, i: (b, r, i, col))

    def prev(col):
        return pl.BlockSpec((None, None, ATT_BLOCK, GROUP_WIDTH),
                            lambda b, r, i: (b, r, jnp.maximum(i * sub - 1, 0), col))

    return pl.pallas_call(
        functools.partial(_attn_body, tq=tq),
        grid=(B, dil, L // tq),
        in_specs=[cur(0), cur(1), prev(1), cur(2), prev(2),
                  pl.BlockSpec((HEADS_PER_GROUP, 2, ATT_BLOCK, 2 * ATT_BLOCK), lambda b, r, i: (0, 0, 0, 0))],
        out_specs=[pl.BlockSpec((None, None, tq, GROUP_WIDTH), lambda b, r, i: (b, r, i, 0)),
                   pl.BlockSpec((None, None, tq, LANES), lambda b, r, i: (b, r, i, 0))],
        out_shape=[jax.ShapeDtypeStruct((B, dil, L, GROUP_WIDTH), F32),
                   jax.ShapeDtypeStruct((B, dil, L, LANES), F32)],
        scratch_shapes=[pltpu.VMEM((ATT_BLOCK + tq, GROUP_WIDTH), BF16),
                        pltpu.VMEM((ATT_BLOCK + tq, GROUP_WIDTH), BF16),
                        pltpu.VMEM((2, HEADS_PER_GROUP, ATT_BLOCK, 2 * ATT_BLOCK), F32)],
        compiler_params=_params(("arbitrary", "arbitrary", "arbitrary")),
        name=f"attention_g{gi}",
    )(qkv, qkv, qkv, qkv, qkv, bm_g)


def _split_bf16(x):
    hi = x.astype(BF16)
    lo = (x - hi.astype(F32)).astype(BF16)
    return hi, lo


def _load_token_tiles(ref, n):
    return jnp.concatenate([ref[pl.ds(c, n, stride=SUBLANES), :] for c in range(D_MODEL // LANES)], axis=1)


def _store_token_tiles(ref, x, n):
    for c in range(D_MODEL // LANES):
        ref[pl.ds(c, n, stride=SUBLANES), :] = x[:, c * LANES:(c + 1) * LANES]


def _to_token_order(blk_ref, tok_ref, dil, tm):
    n = tm // dil
    for r in range(dil):
        for c in range(tok_ref.shape[0]):
            tok_ref[c, pl.ds(r, n, stride=dil), :] = blk_ref[r, :, c * LANES:(c + 1) * LANES]


def _mixer_out_body(o0, o1, o2, s0, s1, s2, uv_ref, ug_ref, uvh_ref, ugh_ref, ga_ref, gc_ref, x_ref,
                    wdw_ref, bdw_ref, cg_ref, cb_ref, woa_ref, woc_ref, wout_ref, g1_ref, b1_ref,
                    h_ref, ht_ref, glu_ref, dw_ref, shift_ref, tok_o1, tok_s1, tok_o2, tok_s2, *, tm, chunk):
    i = pl.program_id(1)

    ncol = GROUP_WIDTH // LANES
    _to_token_order(o1, tok_o1, ATT_GROUPS[1][1], tm)
    _to_token_order(s1, tok_s1, ATT_GROUPS[1][1], tm)
    _to_token_order(o2, tok_o2, ATT_GROUPS[2][1], tm)
    _to_token_order(s2, tok_s2, ATT_GROUPS[2][1], tm)
    outs = [o0[0],
            jnp.concatenate([tok_o1[c] for c in range(ncol)], axis=1),
            jnp.concatenate([tok_o2[c] for c in range(ncol)], axis=1)]
    sts = [s0[0], tok_s1[0], tok_s2[0]]

    mx = jnp.maximum(jnp.maximum(sts[0], sts[1]), sts[2])
    wts = [pltpu.roll(st, LANES - HEADS_PER_GROUP, axis=1) * jnp.exp(st - mx) for st in sts]
    wsum = wts[0] + wts[1] + wts[2]
    row = lax.broadcasted_iota(jnp.int32, (LANES, GROUP_WIDTH), 0)
    colh = lax.broadcasted_iota(jnp.int32, (LANES, GROUP_WIDTH), 1) // HEAD_DIM
    expand = (row == colh).astype(BF16)
    attn = jnp.zeros((tm, GROUP_WIDTH), F32)
    head_lane = lax.broadcasted_iota(jnp.int32, (tm, LANES), 1) < HEADS_PER_GROUP
    for wt, o in zip(wts, outs):
        c_hi, c_lo = _split_bf16(jnp.where(head_lane, wt / wsum, 0.0))
        c = (jnp.dot(c_hi, expand, preferred_element_type=F32)
             + jnp.dot(c_lo, expand, preferred_element_type=F32))
        attn = attn + c * o
    a_out = jnp.dot(attn.astype(BF16), woa_ref[...], preferred_element_type=F32)

    gh = uvh_ref[...].astype(F32) * _sigmoid(ugh_ref[...].astype(F32))
    glu_ref[0:CONV_HALO] = jnp.where(i == 0, 0.0, gh)
    glu_ref[CONV_HALO:] = uv_ref[...].astype(F32) * _sigmoid(ug_ref[...].astype(F32))
    first_tap = CONV_HALO - (CONV_WIDTH - 1)
    for b in range(1, SUBLANES):
        shift_ref[b - 1] = glu_ref[b:b + shift_ref.shape[1], :]

    for c0 in range(0, CONV_CH, LANES):
        cs = slice(c0, c0 + LANES)
        bias = jnp.broadcast_to(bdw_ref[:, cs], (chunk, LANES))
        for r0 in range(0, tm, chunk):
            acc = bias
            for j in range(CONV_WIDTH):
                a, b = divmod(first_tap + j, SUBLANES)
                lo_row = r0 + a * SUBLANES
                rows = glu_ref[lo_row:lo_row + chunk, cs] if b == 0 else shift_ref[b - 1, lo_row:lo_row + chunk, cs]
                acc = acc + wdw_ref[j:j + 1, cs] * rows
            dw_ref[r0:r0 + chunk, cs] = acc
    cn = _layer_norm(dw_ref[...], cg_ref[...], cb_ref[...])
    conv = cn * _sigmoid(cn)
    c_out = jnp.dot(conv.astype(BF16), woc_ref[...], preferred_element_type=F32)

    merged = (_sigmoid(ga_ref[...].astype(F32)) * a_out + _sigmoid(gc_ref[...].astype(F32)) * c_out)
    mix = jnp.dot(merged.astype(BF16), wout_ref[...], preferred_element_type=F32)
    h = _layer_norm(DEEPNORM_ALPHA * x_ref[...] + mix, g1_ref[...], b1_ref[...])
    h_ref[...] = h
    _store_token_tiles(ht_ref, h, tm)


def mixer_out(o_list, st_list, proj3, x3, w_dw, b_dw, cg, cb, woa_b, woc_b, wout_b, g1, b1, tm=256, chunk=32):
    B, S, _ = x3.shape
    tm = min(tm, S)
    halo_blocks = tm // CONV_HALO
    uv_col = QKV_WIDTH // CONV_CH
    ga_col = (QKV_WIDTH + 2 * CONV_CH) // D_MODEL
    ncol = GROUP_WIDTH // LANES

    def tile(width, col=0):
        return pl.BlockSpec((None, tm, width), lambda b, i: (b, i, col))

    def dilated(gi, width):
        dil = ATT_GROUPS[gi][1]
        return pl.BlockSpec((None, dil, tm // dil, width), lambda b, i: (b, 0, i, 0))

    def halo(col):
        return pl.BlockSpec((None, CONV_HALO, CONV_CH), lambda b, i: (b, jnp.maximum(i * halo_blocks - 1, 0), col))

    def whole(shape):
        return pl.BlockSpec(shape, lambda b, i: (0,) * len(shape))

    return pl.pallas_call(
        functools.partial(_mixer_out_body, tm=tm, chunk=chunk),
        grid=(B, S // tm),
        in_specs=[dilated(g, GROUP_WIDTH) for g in range(N_GROUPS)] + [dilated(g, LANES) for g in range(N_GROUPS)]
        + [tile(CONV_CH, uv_col), tile(CONV_CH, uv_col + 1), halo(uv_col), halo(uv_col + 1),
           tile(D_MODEL, ga_col), tile(D_MODEL, ga_col + 1), tile(D_MODEL),
           whole((CONV_WIDTH, CONV_CH)), whole((1, CONV_CH)), whole((1, CONV_CH)), whole((1, CONV_CH)),
           whole((GROUP_WIDTH, D_MODEL)), whole((CONV_CH, D_MODEL)), whole((D_MODEL, D_MODEL)),
           whole((1, D_MODEL)), whole((1, D_MODEL))],
        out_specs=[tile(D_MODEL), pl.BlockSpec((tm * SUBLANES, LANES), lambda b, i: (b * (S // tm) + i, 0))],
        out_shape=[jax.ShapeDtypeStruct((B, S, D_MODEL), F32),
                   jax.ShapeDtypeStruct((B * S * SUBLANES, LANES), F32)],
        scratch_shapes=[pltpu.VMEM((CONV_HALO + tm, CONV_CH), F32), pltpu.VMEM((tm, CONV_CH), F32),
                        pltpu.VMEM((SUBLANES - 1, tm + CONV_HALO - SUBLANES, CONV_CH), F32),
                        pltpu.VMEM((ncol, tm, LANES), F32), pltpu.VMEM((1, tm, LANES), F32),
                        pltpu.VMEM((ncol, tm, LANES), F32), pltpu.VMEM((1, tm, LANES), F32)],
        compiler_params=_params(("arbitrary", "arbitrary")),
        name="mixer_out",
    )(*o_list, *st_list, proj3, proj3, proj3, proj3, proj3, proj3, x3,
      w_dw, b_dw, cg, cb, woa_b, woc_b, wout_b, g1, b1)


def _router_body(h_ref, wt_ref, b_ref, gates_ref, ids_ref, rank_ref, cnt_ref, carry_ref, tri_ref, *, tm):
    step = pl.program_id(0)

    @pl.when(step == 0)
    def _():
        carry_ref[...] = jnp.zeros_like(carry_ref)
        r_i = lax.broadcasted_iota(jnp.int32, (tm, tm), 0)
        c_i = lax.broadcasted_iota(jnp.int32, (tm, tm), 1)
        tri_ref[...] = (r_i < c_i).astype(BF16)

    nt_dims = (((1,), (1,)), ((), ()))
    h_hi, h_lo = _split_bf16(h_ref[...])
    w_hi, w_lo = _split_bf16(wt_ref[...])
    logits = (lax.dot_general(w_hi, h_hi, nt_dims, preferred_element_type=F32)
              + (lax.dot_general(w_hi, h_lo, nt_dims, preferred_element_type=F32)
                 + lax.dot_general(w_lo, h_hi, nt_dims, preferred_element_type=F32))
              + b_ref[...])
    expert = lax.broadcasted_iota(jnp.int32, (N_EXPERTS, tm), 0).astype(F32)
    row = lax.broadcasted_iota(jnp.int32, (SUBLANES, tm), 0)

    work = logits
    vals, hots = [], []
    ids = jnp.zeros((SUBLANES, tm), F32)
    for k in range(TOP_K):
        v = jnp.max(work, axis=0, keepdims=True)
        idx = jnp.min(jnp.where(work == v, expert, float(N_EXPERTS)), axis=0, keepdims=True)
        hot = expert == idx
        work = jnp.where(hot, -jnp.inf, work)
        ids = jnp.where(row == k, idx, ids)
        vals.append(v)
        hots.append(hot)

    es = [jnp.exp(v - vals[0]) for v in vals]
    esum = es[0] + es[1] + es[2] + es[3]
    gates = jnp.zeros((SUBLANES, tm), F32)
    for k in range(TOP_K):
        gates = jnp.where(row == k, es[k] / esum, gates)

    sel = (hots[0] | hots[1] | hots[2] | hots[3])
    before = jnp.dot(sel.astype(BF16), tri_ref[...], preferred_element_type=F32) + carry_ref[:, 0:1]
    rank = jnp.zeros((SUBLANES, tm), F32)
    for k in range(TOP_K):
        rk = jnp.sum(jnp.where(hots[k], before, 0.0), axis=0, keepdims=True)
        rank = jnp.where(row == k, rk, rank)
    carry_ref[...] = carry_ref[...] + jnp.sum(sel.astype(F32), axis=1, keepdims=True)

    gates_ref[...] = gates
    ids_ref[...] = ids.astype(jnp.int32)
    rank_ref[...] = rank.astype(jnp.int32)
    cnt_ref[...] = carry_ref[...].astype(jnp.int32)


def router(h2, w_router, b_router, tm=512):
    T = h2.shape[0]
    tm = min(tm, T)
    tile = pl.BlockSpec((SUBLANES, tm), lambda i: (0, i))
    return pl.pallas_call(
        functools.partial(_router_body, tm=tm),
        grid=(T // tm,),
        in_specs=[pl.BlockSpec((tm, D_MODEL), lambda i: (i, 0)),
                  pl.BlockSpec((N_EXPERTS, D_MODEL), lambda i: (0, 0)),
                  pl.BlockSpec((N_EXPERTS, 1), lambda i: (0, 0))],
        out_specs=[tile, tile, tile, pl.BlockSpec((N_EXPERTS, LANES), lambda i: (0, 0))],
        out_shape=[jax.ShapeDtypeStruct((SUBLANES, T), F32), jax.ShapeDtypeStruct((SUBLANES, T), jnp.int32),
                   jax.ShapeDtypeStruct((SUBLANES, T), jnp.int32), jax.ShapeDtypeStruct((N_EXPERTS, LANES), jnp.int32)],
        scratch_shapes=[pltpu.VMEM((N_EXPERTS, LANES), F32), pltpu.VMEM((tm, tm), BF16)],
        compiler_params=_params(("arbitrary",)),
        name="router",
    )(h2, w_router.astype(F32).T, b_router.astype(F32).reshape(N_EXPERTS, 1))


def _token_copy(src, dst, s_tok, d_tok, sem, n=1):
    rows = n * SUBLANES
    return pltpu.make_async_copy(src.at[pl.ds(pl.multiple_of(s_tok * SUBLANES, SUBLANES), rows), :],
                                 dst.at[pl.ds(pl.multiple_of(d_tok * SUBLANES, SUBLANES), rows), :], sem)


def _dispatch_body(zstart_ref, zcount_ref, n_used_ref, pos_ref, h_ref, xs_ref, zeros_ref, sem, zsem, *, tm, nb):
    half = EXPERT_BLOCK // 2
    bits = [1 << s for s in reversed(range(half.bit_length()))]

    def zero_fill(e, wait):
        start, count = zstart_ref[e], zcount_ref[e]
        for bit in bits:
            @pl.when((count & bit) != 0)
            def _():
                cp = _token_copy(zeros_ref, xs_ref, 0, 0 if wait else start + (count & ~(2 * bit - 1)), zsem, bit)
                cp.wait() if wait else cp.start()

    def zero_tail(blk, wait):
        for part in range(2):
            cp = _token_copy(zeros_ref, xs_ref, 0, 0 if wait else blk * EXPERT_BLOCK + part * half, zsem, half)
            cp.wait() if wait else cp.start()

    @pl.when(pl.program_id(0) == 0)
    def _():
        zeros_ref[...] = jnp.zeros_like(zeros_ref)
        lax.fori_loop(0, N_EXPERTS, lambda e, c: (zero_fill(e, False), c)[1], 0)
        lax.fori_loop(n_used_ref[0], nb, lambda blk, c: (zero_tail(blk, False), c)[1], 0)

    def issue(t, c):
        for k in range(TOP_K):
            _token_copy(h_ref, xs_ref, t, pos_ref[k * tm + t], sem).start(priority=k % 2)
        return c

    lax.fori_loop(0, tm, issue, 0)
    for k in range(TOP_K):
        _token_copy(h_ref, xs_ref, 0, 0, sem, tm).wait()

    @pl.when(pl.program_id(0) == 0)
    def _():
        lax.fori_loop(0, N_EXPERTS, lambda e, c: (zero_fill(e, True), c)[1], 0)
        lax.fori_loop(n_used_ref[0], nb, lambda blk, c: (zero_tail(blk, True), c)[1], 0)


def _tile_major(pos_t, tm):
    T = pos_t.shape[1]
    return pos_t.reshape(TOP_K, T // tm, tm).transpose(1, 0, 2).reshape(-1)


def dispatch(ht, pos_t, zstart, zcount, n_used, n_rows, tm=512):
    T = ht.shape[0] // SUBLANES
    tm = min(tm, T)
    grid_spec = pltpu.PrefetchScalarGridSpec(
        num_scalar_prefetch=3,
        grid=(T // tm,),
        in_specs=[pl.BlockSpec((TOP_K * tm,), lambda i, zs, zc, nu: (i,), memory_space=pltpu.SMEM),
                  pl.BlockSpec((tm * SUBLANES, LANES), lambda i, zs, zc, nu: (i, 0))],
        out_specs=pl.BlockSpec(memory_space=pl.ANY),
        scratch_shapes=[pltpu.VMEM((EXPERT_BLOCK // 2 * SUBLANES, LANES), F32), pltpu.SemaphoreType.DMA,
                        pltpu.SemaphoreType.DMA],
    )
    return pl.pallas_call(
        functools.partial(_dispatch_body, tm=tm, nb=n_rows // EXPERT_BLOCK),
        grid_spec=grid_spec,
        out_shape=jax.ShapeDtypeStruct((n_rows * SUBLANES, LANES), F32),
        compiler_params=_params(("arbitrary",)),
        name="dispatch",
    )(zstart, zcount, n_used, _tile_major(pos_t, tm), ht)


def _combine_body(pos_ref, pos_next_ref, gates_ref, h_ref, g2_ref, b2_ref, ys_ref, o_ref, buf, sems, *, tm):
    i = pl.program_id(0)

    def issue(p_ref, s):
        def body(t, c):
            for k in range(TOP_K):
                _token_copy(ys_ref, buf.at[s, k], p_ref[k * tm + t], t, sems.at[s]).start(priority=k % 2)
            return c

        lax.fori_loop(0, tm, body, 0)

    @pl.when(i == 0)
    def _():
        issue(pos_ref, 0)

    def step(slot):
        @pl.when(i + 1 < pl.num_programs(0))
        def _():
            issue(pos_next_ref, 1 - slot)

        for k in range(TOP_K):
            _token_copy(ys_ref, buf.at[slot, k], 0, 0, sems.at[slot], tm).wait()

        gates = gates_ref[...].T
        ffn = gates[:, 0:1] * _load_token_tiles(buf.at[slot, 0], tm)
        for k in range(1, TOP_K):
            ffn = ffn + gates[:, k:k + 1] * _load_token_tiles(buf.at[slot, k], tm)
        o_ref[...] = _layer_norm(DEEPNORM_ALPHA * h_ref[...] + ffn, g2_ref[...], b2_ref[...])

    for slot in range(2):
        pl.when(i % 2 == slot)(functools.partial(step, slot))


def combine(ys, pos_t, gates_t, h2, g2, b2, tm=256):
    T = h2.shape[0]
    tm = min(tm, T)
    last = T // tm - 1
    pos_flat = _tile_major(pos_t, tm)
    return pl.pallas_call(
        functools.partial(_combine_body, tm=tm),
        grid=(T // tm,),
        in_specs=[pl.BlockSpec((TOP_K * tm,), lambda i: (i,), memory_space=pltpu.SMEM),
                  pl.BlockSpec((TOP_K * tm,), lambda i: (jnp.minimum(i + 1, last),), memory_space=pltpu.SMEM),
                  pl.BlockSpec((SUBLANES, tm), lambda i: (0, i)),
                  pl.BlockSpec((tm, D_MODEL), lambda i: (i, 0)),
                  pl.BlockSpec((1, D_MODEL), lambda i: (0, 0)),
                  pl.BlockSpec((1, D_MODEL), lambda i: (0, 0)),
                  pl.BlockSpec(memory_space=pl.ANY)],
        out_specs=pl.BlockSpec((tm, D_MODEL), lambda i: (i, 0)),
        out_shape=jax.ShapeDtypeStruct((T, D_MODEL), F32),
        scratch_shapes=[pltpu.VMEM((2, TOP_K, tm * SUBLANES, LANES), F32), pltpu.SemaphoreType.DMA((2,))],
        compiler_params=_params(("arbitrary",)),
        name="combine",
    )(pos_flat, pos_flat, gates_t, h2, g2, b2, ys)


def _experts_body(blk_e, blk_src, n_used, xs_ref, wgu_ref, bgu_ref, wdn_ref, bdn_ref, ys_ref, wgu_b, wdn_b):
    del blk_src
    i = pl.program_id(0)
    used = i < n_used[0]

    @pl.when((i == 0) | (blk_e[i] != blk_e[jnp.maximum(i - 1, 0)]))
    def _():
        wgu_b[...] = wgu_ref[...].astype(BF16)
        wdn_b[...] = wdn_ref[...].astype(BF16)

    @pl.when(used)
    def _():
        x = _load_token_tiles(xs_ref, EXPERT_BLOCK).astype(BF16)
        hgu = jnp.dot(x, wgu_b[...], preferred_element_type=F32) + bgu_ref[...]
        gate = jnp.minimum(hgu[:, :D_FF], SWIGLU_LIMIT)
        up = jnp.clip(hgu[:, D_FF:], -SWIGLU_LIMIT, SWIGLU_LIMIT)
        act = (up + 1.0) * gate * _sigmoid(SWIGLU_ALPHA * gate)
        y = jnp.dot(act.astype(BF16), wdn_b[...], preferred_element_type=F32) + bdn_ref[...]
        _store_token_tiles(ys_ref, y, EXPERT_BLOCK)

    @pl.when(jnp.logical_not(used))
    def _():
        ys_ref[...] = jnp.zeros_like(ys_ref)


def experts(xs, blk_e, blk_src, n_used, wgu, bgu, wdn, bdn):
    n_rows = xs.shape[0] // SUBLANES
    nb = n_rows // EXPERT_BLOCK
    rows = EXPERT_BLOCK * SUBLANES
    grid_spec = pltpu.PrefetchScalarGridSpec(
        num_scalar_prefetch=3,
        grid=(nb,),
        in_specs=[pl.BlockSpec((rows, LANES), lambda i, e, s, n: (s[i], 0)),
                  pl.BlockSpec((None, D_MODEL, 2 * D_FF), lambda i, e, s, n: (e[i], 0, 0)),
                  pl.BlockSpec((None, 1, 2 * D_FF), lambda i, e, s, n: (e[i], 0, 0)),
                  pl.BlockSpec((None, D_FF, D_MODEL), lambda i, e, s, n: (e[i], 0, 0)),
                  pl.BlockSpec((None, 1, D_MODEL), lambda i, e, s, n: (e[i], 0, 0))],
        out_specs=pl.BlockSpec((rows, LANES), lambda i, e, s, n: (i, 0)),
        scratch_shapes=[pltpu.VMEM((D_MODEL, 2 * D_FF), BF16), pltpu.VMEM((D_FF, D_MODEL), BF16)],
    )
    return pl.pallas_call(
        _experts_body,
        grid_spec=grid_spec,
        out_shape=jax.ShapeDtypeStruct((n_rows * SUBLANES, LANES), F32),
        compiler_params=_params(("arbitrary",)),
        name="experts",
    )(blk_e, blk_src, n_used, xs, wgu, bgu, wdn, bdn)


def routing_layout(ids, rank, counts, n_blocks):
    padded = (counts + EXPERT_BLOCK - 1) // EXPERT_BLOCK * EXPERT_BLOCK
    pad_ends = jnp.cumsum(padded)
    pad_starts = pad_ends - padded
    start_of = jnp.zeros_like(ids)
    for e in range(N_EXPERTS):
        start_of = jnp.where(ids == e, pad_starts[e], start_of)
    pos_t = (start_of + rank).astype(jnp.int32)
    n_used = jnp.maximum(pad_ends[-1] // EXPERT_BLOCK, 1).astype(jnp.int32)
    blk = jnp.minimum(jnp.arange(n_blocks, dtype=jnp.int32), n_used - 1)
    blk_e = jnp.sum(pad_ends[None, :] <= (blk * EXPERT_BLOCK)[:, None], axis=1)
    blk_e = jnp.minimum(blk_e, N_EXPERTS - 1).astype(jnp.int32)
    zstart = (pad_starts + counts).astype(jnp.int32)
    zcount = (padded - counts).astype(jnp.int32)
    return pos_t, blk_e, blk, n_used.reshape(1), zstart, zcount


DILATE_WIDTH = 4 * LANES


def _dilate_body(x_ref, *refs, n):
    o_refs, slab = refs[:-1], refs[-1]
    for c0 in range(0, DILATE_WIDTH, LANES):
        slab[...] = x_ref[:, c0:c0 + LANES]
        for (_, dil), o_ref in zip(ATT_GROUPS, o_refs):
            for r in range(dil):
                o_ref[r, :, c0:c0 + LANES] = slab[pl.ds(r, n // dil, stride=dil), :].astype(BF16)


def dilate_cast(x3, n=2048):
    B, S, D = x3.shape
    n = min(n, S)
    dils = [dil for _, dil in ATT_GROUPS]
    return pl.pallas_call(
        functools.partial(_dilate_body, n=n),
        grid=(B, S // n, D // DILATE_WIDTH),
        in_specs=[pl.BlockSpec((None, n, DILATE_WIDTH), lambda b, i, c: (b, i, c))],
        out_specs=[pl.BlockSpec((None, dil, n // dil, DILATE_WIDTH), lambda b, i, c: (b, 0, i, c)) for dil in dils],
        out_shape=[jax.ShapeDtypeStruct((B, dil, S // dil, D), BF16) for dil in dils],
        scratch_shapes=[pltpu.VMEM((n, LANES), F32)],
        compiler_params=_params(("arbitrary", "arbitrary", "arbitrary")),
        name="dilate_cast",
    )(x3)


def kernel(x, w_in, rel_bias, w_dw, b_dw, conv_ln_g, conv_ln_b, w_o_attn, w_o_conv, w_out, ln1_g, ln1_b,
           w_router, b_router, w_gate_up, b_gate_up, w_down, b_down, ln2_g, ln2_b):
    B, S, D = x.shape
    T = B * S
    h = x
    bm = bias_tables(rel_bias)
    q_off, k_off, v_off, rest = 0, QKV_WIDTH, 2 * QKV_WIDTH, 3 * QKV_WIDTH
    for l in range(DEPTH):
        wb = w_in[l].astype(BF16)

        def group_cols(gi):
            return [wb[:, off + gi * GROUP_WIDTH:off + (gi + 1) * GROUP_WIDTH] for off in (q_off, k_off, v_off)]

        x_dil = dilate_cast(h)
        xb = x_dil[0].reshape(T, D)
        w_main = jnp.concatenate(group_cols(0) + [wb[:, rest:]], axis=1)
        proj = in_proj(xb, w_main, w_main.shape[1] // 2, "in_proj_main").reshape(B, S, w_main.shape[1])
        o_list, st_list = [], []
        for gi in range(N_GROUPS):
            dil = ATT_GROUPS[gi][1]
            if dil == 1:
                qkv = proj.reshape(B, 1, S, proj.shape[-1])
            else:
                w_g = jnp.concatenate(group_cols(gi), axis=1)
                qkv = in_proj(x_dil[gi].reshape(T, D), w_g, w_g.shape[1], f"in_proj_g{gi}")
                qkv = qkv.reshape(B, dil, S // dil, w_g.shape[1])
            o_g, st_g = attention_group(qkv, bm[gi], gi)
            o_list.append(o_g)
            st_list.append(st_g)
        h1, h1_tiles = mixer_out(o_list, st_list, proj, h,
                       w_dw[l].reshape(CONV_WIDTH, CONV_CH), b_dw[l].reshape(1, CONV_CH),
                       conv_ln_g[l].reshape(1, CONV_CH), conv_ln_b[l].reshape(1, CONV_CH),
                       w_o_attn[l].astype(BF16), w_o_conv[l].astype(BF16), w_out[l].astype(BF16),
                       ln1_g[l].reshape(1, D), ln1_b[l].reshape(1, D))
        h2 = h1.reshape(T, D)
        gates_t, ids_t, rank_t, counts = router(h2, w_router[l], b_router[l])
        n_rows = T * TOP_K + N_EXPERTS * EXPERT_BLOCK
        pos_t, blk_e, blk_src, n_used, zstart, zcount = routing_layout(
            ids_t[:TOP_K], rank_t[:TOP_K], counts[:, 0], n_rows // EXPERT_BLOCK)
        xs = dispatch(h1_tiles, pos_t, zstart, zcount, n_used, n_rows)
        ys = experts(xs, blk_e, blk_src, n_used,
                     w_gate_up[l], b_gate_up[l].reshape(N_EXPERTS, 1, 2 * D_FF),
                     w_down[l], b_down[l].reshape(N_EXPERTS, 1, D))
        out = combine(ys, pos_t, gates_t, h2, ln2_g[l].reshape(1, D), ln2_b[l].reshape(1, D))
        h = out.reshape(B, S, D)
    return h
```

```python
import functools
import math

import jax
import jax.numpy as jnp
from jax import lax
from jax.experimental import pallas as pl
from jax.experimental.pallas import tpu as pltpu

D_MODEL = 1024
ATT_GROUPS = ((128, 1), (512, 4), (2048, 16))
N_GROUPS = len(ATT_GROUPS)
HEADS_PER_GROUP = 8
HEAD_DIM = 64
GROUP_WIDTH = HEADS_PER_GROUP * HEAD_DIM
QKV_WIDTH = N_GROUPS * GROUP_WIDTH
ATT_BLOCK = 128
N_BUCKETS = 32
MAX_DISTANCE = 2048
CONV_CH = 768
CONV_WIDTH = 31
CONV_HALO = 32
N_EXPERTS = 32
TOP_K = 4
D_FF = 1024
SWIGLU_LIMIT = 7.0
SWIGLU_ALPHA = 1.702
IN_WIDTH = 3 * QKV_WIDTH + 2 * CONV_CH + 2 * D_MODEL
LN_EPS = 1e-5
NEG_INF = -1e30
DEPTH = 1
DEEPNORM_ALPHA = (2 * DEPTH) ** 0.25

LANES = 128
SUBLANES = 8
EXPERT_BLOCK = 512
VMEM_LIMIT = 56 * 1024 * 1024

F32 = jnp.float32
BF16 = jnp.bfloat16


def _params(sem, vmem=VMEM_LIMIT):
    return pltpu.CompilerParams(dimension_semantics=sem, vmem_limit_bytes=vmem)


def _sigmoid(x):
    return 0.5 * jnp.tanh(0.5 * x) + 0.5


def _layer_norm(x, g, b):
    mu = jnp.mean(x, axis=-1, keepdims=True)
    xc = x - mu
    var = jnp.mean(xc * xc, axis=-1, keepdims=True)
    return xc * lax.rsqrt(var + LN_EPS) * g + b


def _in_proj_body(x_ref, w_ref, o_ref):
    o_ref[...] = jnp.dot(x_ref[...], w_ref[...], preferred_element_type=F32).astype(o_ref.dtype)


def in_proj(xb, w_b, tn, name, tm=1024):
    T = xb.shape[0]
    N = w_b.shape[1]
    tm = min(tm, T)
    return pl.pallas_call(
        _in_proj_body,
        grid=(N // tn, T // tm),
        in_specs=[pl.BlockSpec((tm, D_MODEL), lambda n, m: (m, 0)),
                  pl.BlockSpec((D_MODEL, tn), lambda n, m: (0, n))],
        out_specs=pl.BlockSpec((tm, tn), lambda n, m: (m, n)),
        out_shape=jax.ShapeDtypeStruct((T, N), BF16),
        compiler_params=_params(("arbitrary", "arbitrary")),
        name=name,
    )(xb, w_b)


def _t5_bucket(dist):
    max_exact = N_BUCKETS // 2
    log_ratio = jnp.log(jnp.maximum(dist, max_exact).astype(F32) / max_exact) / math.log(MAX_DISTANCE / max_exact)
    large = jnp.minimum(max_exact + (log_ratio * (N_BUCKETS - max_exact)).astype(jnp.int32), N_BUCKETS - 1)
    return jnp.where(dist < max_exact, dist, large)


def _bias_body(tbl_ref, bucket_ref, band_ref, o_ref):
    g = pl.program_id(0)
    h = pl.program_id(1)
    col = g * HEADS_PER_GROUP + h
    bucket = bucket_ref[...]
    acc = jnp.zeros(bucket.shape, F32)
    for k in range(N_BUCKETS):
        acc = jnp.where(bucket == k, tbl_ref[k, col], acc)
    band = band_ref[...] != 0
    kj = lax.broadcasted_iota(jnp.int32, bucket.shape, 1)
    o_ref[0] = jnp.where(band, acc, NEG_INF)
    o_ref[1] = jnp.where(band & (kj >= ATT_BLOCK), acc, NEG_INF)


def bias_tables(rel_bias):
    qi = jnp.arange(ATT_BLOCK)[:, None]
    kj = jnp.arange(2 * ATT_BLOCK)[None, :]
    dist = qi - kj + ATT_BLOCK
    buckets, bands = [], []
    for window, dil in ATT_GROUPS:
        bands.append(((dist >= 0) & (dist <= window // dil)).astype(jnp.int32))
        buckets.append(_t5_bucket(jnp.maximum(dist, 0) * dil).astype(jnp.int32))
    buckets = jnp.stack(buckets)
    bands = jnp.stack(bands)
    blk = (None, ATT_BLOCK, 2 * ATT_BLOCK)
    return pl.pallas_call(
        _bias_body,
        grid=(N_GROUPS, HEADS_PER_GROUP),
        in_specs=[pl.BlockSpec(memory_space=pltpu.SMEM),
                  pl.BlockSpec(blk, lambda g, h: (g, 0, 0)),
                  pl.BlockSpec(blk, lambda g, h: (g, 0, 0))],
        out_specs=pl.BlockSpec((None, None, 2, ATT_BLOCK, 2 * ATT_BLOCK), lambda g, h: (g, h, 0, 0, 0)),
        out_shape=jax.ShapeDtypeStruct((N_GROUPS, HEADS_PER_GROUP, 2, ATT_BLOCK, 2 * ATT_BLOCK), F32),
        compiler_params=_params(("arbitrary", "arbitrary")),
        name="bias_tables",
    )(rel_bias.astype(F32), buckets, bands)


def _attn_body(q_ref, kc_ref, kp_ref, vc_ref, vp_ref, bm_ref, o_ref, st_ref, k_all, v_all, s_buf, *, tq):
    i = pl.program_id(2)
    nsub = tq // ATT_BLOCK
    k_all[0:ATT_BLOCK] = kp_ref[...]
    k_all[ATT_BLOCK:] = kc_ref[...]
    v_all[0:ATT_BLOCK] = vp_ref[...]
    v_all[ATT_BLOCK:] = vc_ref[...]
    lo = lax.broadcasted_iota(jnp.int32, (ATT_BLOCK, LANES), 1) < HEAD_DIM
    nt_dims = (((1,), (1,)), ((), ()))
    first = jnp.where(i == 0, 1, 0)

    def scores(s, slot):
        r0 = s * ATT_BLOCK
        for j in range(HEADS_PER_GROUP // 2):
            cs = slice(j * LANES, (j + 1) * LANES)
            qp = q_ref[r0:r0 + ATT_BLOCK, cs] * jnp.asarray(HEAD_DIM ** -0.5, BF16)
            kp = k_all[r0:r0 + 2 * ATT_BLOCK, cs]
            for hh in range(2):
                h = 2 * j + hh
                qh = jnp.where(lo if hh == 0 else ~lo, qp, jnp.zeros_like(qp))
                bias = bm_ref[h, first] if s == 0 else bm_ref[h, 0]
                s_buf[slot, h] = lax.dot_general(qh, kp, nt_dims, preferred_element_type=F32) + bias

    def softmax_pv(s, slot):
        rows = slice(s * ATT_BLOCK, (s + 1) * ATT_BLOCK)
        st_ref[rows, :] = jnp.zeros((ATT_BLOCK, LANES), F32)
        for j in range(HEADS_PER_GROUP // 2):
            vp = v_all[s * ATT_BLOCK:(s + 2) * ATT_BLOCK, j * LANES:(j + 1) * LANES]
            for hh in range(2):
                h = 2 * j + hh
                sc = s_buf[slot, h]
                m = jnp.max(sc, axis=-1, keepdims=True)
                p = jnp.exp(sc - m)
                den = jnp.sum(p, axis=-1, keepdims=True)
                pv = jnp.dot(p.astype(BF16), vp, preferred_element_type=F32) * (1.0 / den)
                c0 = h * HEAD_DIM
                o_ref[rows, c0:c0 + HEAD_DIM] = pv[:, hh * HEAD_DIM:(hh + 1) * HEAD_DIM]
                st_ref[rows, h:h + 1] = m
                st_ref[rows, HEADS_PER_GROUP + h:HEADS_PER_GROUP + h + 1] = den

    scores(0, 0)
    for s in range(nsub):
        if s + 1 < nsub:
            scores(s + 1, (s + 1) % 2)
        softmax_pv(s, s % 2)


def attention_group(qkv, bm_g, gi, tq=512):
    B, dil, L, _ = qkv.shape
    tq = min(tq, L)
    sub = tq // ATT_BLOCK

    def cur(col):
        return pl.BlockSpec((None, None, tq, GROUP_WIDTH), lambda b, r, i: (b, r, i, col))

    def prev(col):
        return pl.BlockSpec((None, None, ATT_BLOCK, GROUP_WIDTH),
                            lambda b, r, i: (b, r, jnp.maximum(i * sub - 1, 0), col))

    return pl.pallas_call(
        functools.partial(_attn_body, tq=tq),
        grid=(B, dil, L // tq),
        in_specs=[cur(0), cur(1), prev(1), cur(2), prev(2),
                  pl.BlockSpec((HEADS_PER_GROUP, 2, ATT_BLOCK, 2 * ATT_BLOCK), lambda b, r, i: (0, 0, 0, 0))],
        out_specs=[pl.BlockSpec((None, None, tq, GROUP_WIDTH), lambda b, r, i: (b, r, i, 0)),
                   pl.BlockSpec((None, None, tq, LANES), lambda b, r, i: (b, r, i, 0))],
        out_shape=[jax.ShapeDtypeStruct((B, dil, L, GROUP_WIDTH), F32),
                   jax.ShapeDtypeStruct((B, dil, L, LANES), F32)],
        scratch_shapes=[pltpu.VMEM((ATT_BLOCK + tq, GROUP_WIDTH), BF16),
                        pltpu.VMEM((ATT_BLOCK + tq, GROUP_WIDTH), BF16),
                        pltpu.VMEM((2, HEADS_PER_GROUP, ATT_BLOCK, 2 * ATT_BLOCK), F32)],
        compiler_params=_params(("arbitrary", "arbitrary", "arbitrary")),
        name=f"attention_g{gi}",
    )(qkv, qkv, qkv, qkv, qkv, bm_g)


def _split_bf16(x):
    hi = x.astype(BF16)
    lo = (x - hi.astype(F32)).astype(BF16)
    return hi, lo


def _load_token_tiles(ref, n):
    return jnp.concatenate([ref[pl.ds(c, n, stride=SUBLANES), :] for c in range(D_MODEL // LANES)], axis=1)


def _store_token_tiles(ref, x, n):
    for c in range(D_MODEL // LANES):
        ref[pl.ds(c, n, stride=SUBLANES), :] = x[:, c * LANES:(c + 1) * LANES]


def _to_token_order(blk_ref, tok_ref, dil, tm):
    n = tm // dil
    for r in range(dil):
        for c in range(tok_ref.shape[0]):
            tok_ref[c, pl.ds(r, n, stride=dil), :] = blk_ref[r, :, c * LANES:(c + 1) * LANES]


def _mixer_out_body(o0, o1, o2, s0, s1, s2, uv_ref, ug_ref, uvh_ref, ugh_ref, ga_ref, gc_ref, x_ref,
                    wdw_ref, bdw_ref, cg_ref, cb_ref, woa_ref, woc_ref, wout_ref, g1_ref, b1_ref,
                    h_ref, ht_ref, glu_ref, dw_ref, shift_ref, tok_o1, tok_s1, tok_o2, tok_s2, *, tm, chunk):
    i = pl.program_id(1)

    ncol = GROUP_WIDTH // LANES
    _to_token_order(o1, tok_o1, ATT_GROUPS[1][1], tm)
    _to_token_order(s1, tok_s1, ATT_GROUPS[1][1], tm)
    _to_token_order(o2, tok_o2, ATT_GROUPS[2][1], tm)
    _to_token_order(s2, tok_s2, ATT_GROUPS[2][1], tm)
    outs = [o0[0],
            jnp.concatenate([tok_o1[c] for c in range(ncol)], axis=1),
            jnp.concatenate([tok_o2[c] for c in range(ncol)], axis=1)]
    sts = [s0[0], tok_s1[0], tok_s2[0]]

    mx = jnp.maximum(jnp.maximum(sts[0], sts[1]), sts[2])
    wts = [pltpu.roll(st, LANES - HEADS_PER_GROUP, axis=1) * jnp.exp(st - mx) for st in sts]
    wsum = wts[0] + wts[1] + wts[2]
    row = lax.broadcasted_iota(jnp.int32, (LANES, GROUP_WIDTH), 0)
    colh = lax.broadcasted_iota(jnp.int32, (LANES, GROUP_WIDTH), 1) // HEAD_DIM
    expand = (row == colh).astype(BF16)
    attn = jnp.zeros((tm, GROUP_WIDTH), F32)
    head_lane = lax.broadcasted_iota(jnp.int32, (tm, LANES), 1) < HEADS_PER_GROUP
    for wt, o in zip(wts, outs):
        c_hi, c_lo = _split_bf16(jnp.where(head_lane, wt / wsum, 0.0))
        c = (jnp.dot(c_hi, expand, preferred_element_type=F32)
             + jnp.dot(c_lo, expand, preferred_element_type=F32))
        attn = attn + c * o
    a_out = jnp.dot(attn.astype(BF16), woa_ref[...], preferred_element_type=F32)

    gh = uvh_ref[...].astype(F32) * _sigmoid(ugh_ref[...].astype(F32))
    glu_ref[0:CONV_HALO] = jnp.where(i == 0, 0.0, gh)
    glu_ref[CONV_HALO:] = uv_ref[...].astype(F32) * _sigmoid(ug_ref[...].astype(F32))
    first_tap = CONV_HALO - (CONV_WIDTH - 1)
    for b in range(1, SUBLANES):
        shift_ref[b - 1] = glu_ref[b:b + shift_ref.shape[1], :]

    for c0 in range(0, CONV_CH, LANES):
        cs = slice(c0, c0 + LANES)
        bias = jnp.broadcast_to(bdw_ref[:, cs], (chunk, LANES))
        for r0 in range(0, tm, chunk):
            acc = bias
            for j in range(CONV_WIDTH):
                a, b = divmod(first_tap + j, SUBLANES)
                lo_row = r0 + a * SUBLANES
                rows = glu_ref[lo_row:lo_row + chunk, cs] if b == 0 else shift_ref[b - 1, lo_row:lo_row + chunk, cs]
                acc = acc + wdw_ref[j:j + 1, cs] * rows
            dw_ref[r0:r0 + chunk, cs] = acc
    cn = _layer_norm(dw_ref[...], cg_ref[...], cb_ref[...])
    conv = cn * _sigmoid(cn)
    c_out = jnp.dot(conv.astype(BF16), woc_ref[...], preferred_element_type=F32)

    merged = (_sigmoid(ga_ref[...].astype(F32)) * a_out + _sigmoid(gc_ref[...].astype(F32)) * c_out)
    mix = jnp.dot(merged.astype(BF16), wout_ref[...], preferred_element_type=F32)
    h = _layer_norm(DEEPNORM_ALPHA * x_ref[...] + mix, g1_ref[...], b1_ref[...])
    h_ref[...] = h
    _store_token_tiles(ht_ref, h, tm)


def mixer_out(o_list, st_list, proj3, x3, w_dw, b_dw, cg, cb, woa_b, woc_b, wout_b, g1, b1, tm=512, chunk=32):
    B, S, _ = x3.shape
    tm = min(tm, S)
    halo_blocks = tm // CONV_HALO
    uv_col = QKV_WIDTH // CONV_CH
    ga_col = (QKV_WIDTH + 2 * CONV_CH) // D_MODEL
    ncol = GROUP_WIDTH // LANES

    def tile(width, col=0):
        return pl.BlockSpec((None, tm, width), lambda b, i: (b, i, col))

    def dilated(gi, width):
        dil = ATT_GROUPS[gi][1]
        return pl.BlockSpec((None, dil, tm // dil, width), lambda b, i: (b, 0, i, 0))

    def halo(col):
        return pl.BlockSpec((None, CONV_HALO, CONV_CH), lambda b, i: (b, jnp.maximum(i * halo_blocks - 1, 0), col))

    def whole(shape):
        return pl.BlockSpec(shape, lambda b, i: (0,) * len(shape))

    return pl.pallas_call(
        functools.partial(_mixer_out_body, tm=tm, chunk=chunk),
        grid=(B, S // tm),
        in_specs=[dilated(g, GROUP_WIDTH) for g in range(N_GROUPS)] + [dilated(g, LANES) for g in range(N_GROUPS)]
        + [tile(CONV_CH, uv_col), tile(CONV_CH, uv_col + 1), halo(uv_col), halo(uv_col + 1),
           tile(D_MODEL, ga_col), tile(D_MODEL, ga_col + 1), tile(D_MODEL),
           whole((CONV_WIDTH, CONV_CH)), whole((1, CONV_CH)), whole((1, CONV_CH)), whole((1, CONV_CH)),
           whole((GROUP_WIDTH, D_MODEL)), whole((CONV_CH, D_MODEL)), whole((D_MODEL, D_MODEL)),
           whole((1, D_MODEL)), whole((1, D_MODEL))],
        out_specs=[tile(D_MODEL), pl.BlockSpec((tm * SUBLANES, LANES), lambda b, i: (b * (S // tm) + i, 0))],
        out_shape=[jax.ShapeDtypeStruct((B, S, D_MODEL), F32),
                   jax.ShapeDtypeStruct((B * S * SUBLANES, LANES), F32)],
        scratch_shapes=[pltpu.VMEM((CONV_HALO + tm, CONV_CH), F32), pltpu.VMEM((tm, CONV_CH), F32),
                        pltpu.VMEM((SUBLANES - 1, tm + CONV_HALO - SUBLANES, CONV_CH), F32),
                        pltpu.VMEM((ncol, tm, LANES), F32), pltpu.VMEM((1, tm, LANES), F32),
                        pltpu.VMEM((ncol, tm, LANES), F32), pltpu.VMEM((1, tm, LANES), F32)],
        compiler_params=_params(("arbitrary", "arbitrary")),
        name="mixer_out",
    )(*o_list, *st_list, proj3, proj3, proj3, proj3, proj3, proj3, x3,
      w_dw, b_dw, cg, cb, woa_b, woc_b, wout_b, g1, b1)


def _router_body(h_ref, wt_ref, b_ref, gates_ref, ids_ref, rank_ref, cnt_ref, carry_ref, tri_ref, *, tm):
    step = pl.program_id(0)

    @pl.when(step == 0)
    def _():
        carry_ref[...] = jnp.zeros_like(carry_ref)
        r_i = lax.broadcasted_iota(jnp.int32, (tm, tm), 0)
        c_i = lax.broadcasted_iota(jnp.int32, (tm, tm), 1)
        tri_ref[...] = (r_i < c_i).astype(BF16)

    nt_dims = (((1,), (1,)), ((), ()))
    h_hi, h_lo = _split_bf16(h_ref[...])
    w_hi, w_lo = _split_bf16(wt_ref[...])
    logits = (lax.dot_general(w_hi, h_hi, nt_dims, preferred_element_type=F32)
              + (lax.dot_general(w_hi, h_lo, nt_dims, preferred_element_type=F32)
                 + lax.dot_general(w_lo, h_hi, nt_dims, preferred_element_type=F32))
              + b_ref[...])
    expert = lax.broadcasted_iota(jnp.int32, (N_EXPERTS, tm), 0).astype(F32)
    row = lax.broadcasted_iota(jnp.int32, (SUBLANES, tm), 0)

    work = logits
    vals, hots = [], []
    ids = jnp.zeros((SUBLANES, tm), F32)
    for k in range(TOP_K):
        v = jnp.max(work, axis=0, keepdims=True)
        idx = jnp.min(jnp.where(work == v, expert, float(N_EXPERTS)), axis=0, keepdims=True)
        hot = expert == idx
        work = jnp.where(hot, -jnp.inf, work)
        ids = jnp.where(row == k, idx, ids)
        vals.append(v)
        hots.append(hot)

    es = [jnp.exp(v - vals[0]) for v in vals]
    esum = es[0] + es[1] + es[2] + es[3]
    gates = jnp.zeros((SUBLANES, tm), F32)
    for k in range(TOP_K):
        gates = jnp.where(row == k, es[k] / esum, gates)

    sel = (hots[0] | hots[1] | hots[2] | hots[3])
    before = jnp.dot(sel.astype(BF16), tri_ref[...], preferred_element_type=F32) + carry_ref[:, 0:1]
    rank = jnp.zeros((SUBLANES, tm), F32)
    for k in range(TOP_K):
        rk = jnp.sum(jnp.where(hots[k], before, 0.0), axis=0, keepdims=True)
        rank = jnp.where(row == k, rk, rank)
    carry_ref[...] = carry_ref[...] + jnp.sum(sel.astype(F32), axis=1, keepdims=True)

    gates_ref[...] = gates
    ids_ref[...] = ids.astype(jnp.int32)
    rank_ref[...] = rank.astype(jnp.int32)
    cnt_ref[...] = carry_ref[...].astype(jnp.int32)


def router(h2, w_router, b_router, tm=512):
    T = h2.shape[0]
    tm = min(tm, T)
    tile = pl.BlockSpec((SUBLANES, tm), lambda i: (0, i))
    return pl.pallas_call(
        functools.partial(_router_body, tm=tm),
        grid=(T // tm,),
        in_specs=[pl.BlockSpec((tm, D_MODEL), lambda i: (i, 0)),
                  pl.BlockSpec((N_EXPERTS, D_MODEL), lambda i: (0, 0)),
                  pl.BlockSpec((N_EXPERTS, 1), lambda i: (0, 0))],
        out_specs=[tile, tile, tile, pl.BlockSpec((N_EXPERTS, LANES), lambda i: (0, 0))],
        out_shape=[jax.ShapeDtypeStruct((SUBLANES, T), F32), jax.ShapeDtypeStruct((SUBLANES, T), jnp.int32),
                   jax.ShapeDtypeStruct((SUBLANES, T), jnp.int32), jax.ShapeDtypeStruct((N_EXPERTS, LANES), jnp.int32)],
        scratch_shapes=[pltpu.VMEM((N_EXPERTS, LANES), F32), pltpu.VMEM((tm, tm), BF16)],
        compiler_params=_params(("arbitrary",)),
        name="router",
    )(h2, w_router.astype(F32).T, b_router.astype(F32).reshape(N_EXPERTS, 1))


def _token_copy(src, dst, s_tok, d_tok, sem, n=1):
    rows = n * SUBLANES
    return pltpu.make_async_copy(src.at[pl.ds(pl.multiple_of(s_tok * SUBLANES, SUBLANES), rows), :],
                                 dst.at[pl.ds(pl.multiple_of(d_tok * SUBLANES, SUBLANES), rows), :], sem)


def _dispatch_body(zstart_ref, zcount_ref, n_used_ref, pos_ref, h_ref, xs_ref, zeros_ref, sem, zsem, *, tm, nb):
    half = EXPERT_BLOCK // 2
    bits = [1 << s for s in reversed(range(half.bit_length()))]

    def zero_fill(e, wait):
        start, count = zstart_ref[e], zcount_ref[e]
        for bit in bits:
            @pl.when((count & bit) != 0)
            def _():
                cp = _token_copy(zeros_ref, xs_ref, 0, 0 if wait else start + (count & ~(2 * bit - 1)), zsem, bit)
                cp.wait() if wait else cp.start()

    def zero_tail(blk, wait):
        for part in range(2):
            cp = _token_copy(zeros_ref, xs_ref, 0, 0 if wait else blk * EXPERT_BLOCK + part * half, zsem, half)
            cp.wait() if wait else cp.start()

    @pl.when(pl.program_id(0) == 0)
    def _():
        zeros_ref[...] = jnp.zeros_like(zeros_ref)
        lax.fori_loop(0, N_EXPERTS, lambda e, c: (zero_fill(e, False), c)[1], 0)
        lax.fori_loop(n_used_ref[0], nb, lambda blk, c: (zero_tail(blk, False), c)[1], 0)

    def issue(t, c):
        for k in range(TOP_K):
            _token_copy(h_ref, xs_ref, t, pos_ref[k * tm + t], sem).start(priority=k % 2)
        return c

    lax.fori_loop(0, tm, issue, 0)
    for k in range(TOP_K):
        _token_copy(h_ref, xs_ref, 0, 0, sem, tm).wait()

    @pl.when(pl.program_id(0) == 0)
    def _():
        lax.fori_loop(0, N_EXPERTS, lambda e, c: (zero_fill(e, True), c)[1], 0)
        lax.fori_loop(n_used_ref[0], nb, lambda blk, c: (zero_tail(blk, True), c)[1], 0)


def _tile_major(pos_t, tm):
    T = pos_t.shape[1]
    return pos_t.reshape(TOP_K, T // tm, tm).transpose(1, 0, 2).reshape(-1)


def dispatch(ht, pos_t, zstart, zcount, n_used, n_rows, tm=512):
    T = ht.shape[0] // SUBLANES
    tm = min(tm, T)
    grid_spec = pltpu.PrefetchScalarGridSpec(
        num_scalar_prefetch=3,
        grid=(T // tm,),
        in_specs=[pl.BlockSpec((TOP_K * tm,), lambda i, zs, zc, nu: (i,), memory_space=pltpu.SMEM),
                  pl.BlockSpec((tm * SUBLANES, LANES), lambda i, zs, zc, nu: (i, 0))],
        out_specs=pl.BlockSpec(memory_space=pl.ANY),
        scratch_shapes=[pltpu.VMEM((EXPERT_BLOCK // 2 * SUBLANES, LANES), F32), pltpu.SemaphoreType.DMA,
                        pltpu.SemaphoreType.DMA],
    )
    return pl.pallas_call(
        functools.partial(_dispatch_body, tm=tm, nb=n_rows // EXPERT_BLOCK),
        grid_spec=grid_spec,
        out_shape=jax.ShapeDtypeStruct((n_rows * SUBLANES, LANES), F32),
        compiler_params=_params(("arbitrary",)),
        name="dispatch",
    )(zstart, zcount, n_used, _tile_major(pos_t, tm), ht)


def _combine_body(pos_ref, pos_next_ref, gates_ref, h_ref, g2_ref, b2_ref, ys_ref, o_ref, buf, sems, *, tm):
    i = pl.program_id(0)

    def issue(p_ref, s):
        def body(t, c):
            for k in range(TOP_K):
                _token_copy(ys_ref, buf.at[s, k], p_ref[k * tm + t], t, sems.at[s]).start(priority=k % 2)
            return c

        lax.fori_loop(0, tm, body, 0)

    @pl.when(i == 0)
    def _():
        issue(pos_ref, 0)

    def step(slot):
        @pl.when(i + 1 < pl.num_programs(0))
        def _():
            issue(pos_next_ref, 1 - slot)

        for k in range(TOP_K):
            _token_copy(ys_ref, buf.at[slot, k], 0, 0, sems.at[slot], tm).wait()

        gates = gates_ref[...].T
        ffn = gates[:, 0:1] * _load_token_tiles(buf.at[slot, 0], tm)
        for k in range(1, TOP_K):
            ffn = ffn + gates[:, k:k + 1] * _load_token_tiles(buf.at[slot, k], tm)
        o_ref[...] = _layer_norm(DEEPNORM_ALPHA * h_ref[...] + ffn, g2_ref[...], b2_ref[...])

    for slot in range(2):
        pl.when(i % 2 == slot)(functools.partial(step, slot))


def combine(ys, pos_t, gates_t, h2, g2, b2, tm=256):
    T = h2.shape[0]
    tm = min(tm, T)
    last = T // tm - 1
    pos_flat = _tile_major(pos_t, tm)
    return pl.pallas_call(
        functools.partial(_combine_body, tm=tm),
        grid=(T // tm,),
        in_specs=[pl.BlockSpec((TOP_K * tm,), lambda i: (i,), memory_space=pltpu.SMEM),
                  pl.BlockSpec((TOP_K * tm,), lambda i: (jnp.minimum(i + 1, last),), memory_space=pltpu.SMEM),
                  pl.BlockSpec((SUBLANES, tm), lambda i: (0, i)),
                  pl.BlockSpec((tm, D_MODEL), lambda i: (i, 0)),
                  pl.BlockSpec((1, D_MODEL), lambda i: (0, 0)),
                  pl.BlockSpec((1, D_MODEL), lambda i: (0, 0)),
                  pl.BlockSpec(memory_space=pl.ANY)],
        out_specs=pl.BlockSpec((tm, D_MODEL), lambda i: (i, 0)),
        out_shape=jax.ShapeDtypeStruct((T, D_MODEL), F32),
        scratch_shapes=[pltpu.VMEM((2, TOP_K, tm * SUBLANES, LANES), F32), pltpu.SemaphoreType.DMA((2,))],
        compiler_params=_params(("arbitrary",)),
        name="combine",
    )(pos_flat, pos_flat, gates_t, h2, g2, b2, ys)


def _experts_body(blk_e, blk_src, n_used, next_e, wslot, xs_ref, wgu_hbm, bgu_ref, wdn_hbm, bdn_ref, ys_ref,
                  wgu_f, wdn_f, wgu_b, wdn_b, sems):
    del blk_src
    i = pl.program_id(0)
    used = i < n_used[0]
    e = blk_e[i]

    def weight_copies(expert, slot):
        return (pltpu.make_async_copy(wgu_hbm.at[expert], wgu_f.at[slot], sems.at[slot]),
                pltpu.make_async_copy(wdn_hbm.at[expert], wdn_f.at[slot], sems.at[slot]))

    @pl.when(i == 0)
    def _():
        for cp in weight_copies(e, 0):
            cp.start()

    @pl.when((i == 0) | (e != blk_e[jnp.maximum(i - 1, 0)]))
    def _():
        nxt = next_e[e]
        for slot in range(2):
            @pl.when(wslot[e] == slot)
            def _():
                for cp in weight_copies(e, slot):
                    cp.wait()
                wgu_b[...] = wgu_f[slot].astype(BF16)
                wdn_b[...] = wdn_f[slot].astype(BF16)

                @pl.when(nxt < N_EXPERTS)
                def _():
                    for cp in weight_copies(nxt, 1 - slot):
                        cp.start()

    @pl.when(used)
    def _():
        x = _load_token_tiles(xs_ref, EXPERT_BLOCK).astype(BF16)
        hgu = jnp.dot(x, wgu_b[...], preferred_element_type=F32) + bgu_ref[...]
        gate = jnp.minimum(hgu[:, :D_FF], SWIGLU_LIMIT)
        up = jnp.clip(hgu[:, D_FF:], -SWIGLU_LIMIT, SWIGLU_LIMIT)
        act = (up + 1.0) * gate * _sigmoid(SWIGLU_ALPHA * gate)
        y = jnp.dot(act.astype(BF16), wdn_b[...], preferred_element_type=F32) + bdn_ref[...]
        _store_token_tiles(ys_ref, y, EXPERT_BLOCK)

    @pl.when(jnp.logical_not(used))
    def _():
        ys_ref[...] = jnp.zeros_like(ys_ref)


def experts(xs, blk_e, blk_src, n_used, next_e, wslot, wgu, bgu, wdn, bdn):
    n_rows = xs.shape[0] // SUBLANES
    nb = n_rows // EXPERT_BLOCK
    rows = EXPERT_BLOCK * SUBLANES
    grid_spec = pltpu.PrefetchScalarGridSpec(
        num_scalar_prefetch=5,
        grid=(nb,),
        in_specs=[pl.BlockSpec((rows, LANES), lambda i, e, s, n, ne, ws: (s[i], 0)),
                  pl.BlockSpec(memory_space=pl.ANY),
                  pl.BlockSpec((None, 1, 2 * D_FF), lambda i, e, s, n, ne, ws: (e[i], 0, 0)),
                  pl.BlockSpec(memory_space=pl.ANY),
                  pl.BlockSpec((None, 1, D_MODEL), lambda i, e, s, n, ne, ws: (e[i], 0, 0))],
        out_specs=pl.BlockSpec((rows, LANES), lambda i, e, s, n, ne, ws: (i, 0)),
        scratch_shapes=[pltpu.VMEM((2, D_MODEL, 2 * D_FF), F32), pltpu.VMEM((2, D_FF, D_MODEL), F32),
                        pltpu.VMEM((D_MODEL, 2 * D_FF), BF16), pltpu.VMEM((D_FF, D_MODEL), BF16),
                        pltpu.SemaphoreType.DMA((2,))],
    )
    return pl.pallas_call(
        _experts_body,
        grid_spec=grid_spec,
        out_shape=jax.ShapeDtypeStruct((n_rows * SUBLANES, LANES), F32),
        compiler_params=_params(("arbitrary",)),
        name="experts",
    )(blk_e, blk_src, n_used, next_e, wslot, xs, wgu, bgu, wdn, bdn)


def routing_layout(ids, rank, counts, n_blocks):
    padded = (counts + EXPERT_BLOCK - 1) // EXPERT_BLOCK * EXPERT_BLOCK
    pad_ends = jnp.cumsum(padded)
    pad_starts = pad_ends - padded
    start_of = jnp.zeros_like(ids)
    for e in range(N_EXPERTS):
        start_of = jnp.where(ids == e, pad_starts[e], start_of)
    pos_t = (start_of + rank).astype(jnp.int32)
    n_used = jnp.maximum(pad_ends[-1] // EXPERT_BLOCK, 1).astype(jnp.int32)
    blk = jnp.minimum(jnp.arange(n_blocks, dtype=jnp.int32), n_used - 1)
    blk_e = jnp.sum(pad_ends[None, :] <= (blk * EXPERT_BLOCK)[:, None], axis=1)
    blk_e = jnp.minimum(blk_e, N_EXPERTS - 1).astype(jnp.int32)
    zstart = (pad_starts + counts).astype(jnp.int32)
    zcount = (padded - counts).astype(jnp.int32)
    nonempty = counts > 0
    expert_ids = jnp.arange(N_EXPERTS, dtype=jnp.int32)
    later = jnp.where(nonempty[None, :] & (expert_ids[None, :] > expert_ids[:, None]), expert_ids[None, :], N_EXPERTS)
    next_e = jnp.min(later, axis=1).astype(jnp.int32)
    wslot = ((jnp.cumsum(nonempty) - 1) % 2).astype(jnp.int32)
    return pos_t, blk_e, blk, n_used.reshape(1), zstart, zcount, next_e, wslot


DILATE_WIDTH = 4 * LANES


def _dilate_body(x_ref, *refs, n):
    o_refs, slab = refs[:-1], refs[-1]
    for c0 in range(0, DILATE_WIDTH, LANES):
        slab[...] = x_ref[:, c0:c0 + LANES]
        for (_, dil), o_ref in zip(ATT_GROUPS, o_refs):
            for r in range(dil):
                o_ref[r, :, c0:c0 + LANES] = slab[pl.ds(r, n // dil, stride=dil), :].astype(BF16)


def dilate_cast(x3, n=2048):
    B, S, D = x3.shape
    n = min(n, S)
    dils = [dil for _, dil in ATT_GROUPS]
    return pl.pallas_call(
        functools.partial(_dilate_body, n=n),
        grid=(B, S // n, D // DILATE_WIDTH),
        in_specs=[pl.BlockSpec((None, n, DILATE_WIDTH), lambda b, i, c: (b, i, c))],
        out_specs=[pl.BlockSpec((None, dil, n // dil, DILATE_WIDTH), lambda b, i, c: (b, 0, i, c)) for dil in dils],
        out_shape=[jax.ShapeDtypeStruct((B, dil, S // dil, D), BF16) for dil in dils],
        scratch_shapes=[pltpu.VMEM((n, LANES), F32)],
        compiler_params=_params(("arbitrary", "arbitrary", "arbitrary")),
        name="dilate_cast",
    )(x3)


def kernel(x, w_in, rel_bias, w_dw, b_dw, conv_ln_g, conv_ln_b, w_o_attn, w_o_conv, w_out, ln1_g, ln1_b,
           w_router, b_router, w_gate_up, b_gate_up, w_down, b_down, ln2_g, ln2_b):
    B, S, D = x.shape
    T = B * S
    h = x
    bm = bias_tables(rel_bias)
    q_off, k_off, v_off, rest = 0, QKV_WIDTH, 2 * QKV_WIDTH, 3 * QKV_WIDTH
    for l in range(DEPTH):
        wb = w_in[l].astype(BF16)

        def group_cols(gi):
            return [wb[:, off + gi * GROUP_WIDTH:off + (gi + 1) * GROUP_WIDTH] for off in (q_off, k_off, v_off)]

        x_dil = dilate_cast(h)
        xb = x_dil[0].reshape(T, D)
        w_main = jnp.concatenate(group_cols(0) + [wb[:, rest:]], axis=1)
        proj = in_proj(xb, w_main, w_main.shape[1] // 2, "in_proj_main").reshape(B, S, w_main.shape[1])
        o_list, st_list = [], []
        for gi in range(N_GROUPS):
            dil = ATT_GROUPS[gi][1]
            if dil == 1:
                qkv = proj.reshape(B, 1, S, proj.shape[-1])
            else:
                w_g = jnp.concatenate(group_cols(gi), axis=1)
                qkv = in_proj(x_dil[gi].reshape(T, D), w_g, w_g.shape[1], f"in_proj_g{gi}")
                qkv = qkv.reshape(B, dil, S // dil, w_g.shape[1])
            o_g, st_g = attention_group(qkv, bm[gi], gi)
            o_list.append(o_g)
            st_list.append(st_g)
        h1, h1_tiles = mixer_out(o_list, st_list, proj, h,
                       w_dw[l].reshape(CONV_WIDTH, CONV_CH), b_dw[l].reshape(1, CONV_CH),
                       conv_ln_g[l].reshape(1, CONV_CH), conv_ln_b[l].reshape(1, CONV_CH),
                       w_o_attn[l].astype(BF16), w_o_conv[l].astype(BF16), w_out[l].astype(BF16),
                       ln1_g[l].reshape(1, D), ln1_b[l].reshape(1, D))
        h2 = h1.reshape(T, D)
        gates_t, ids_t, rank_t, counts = router(h2, w_router[l], b_router[l])
        n_rows = T * TOP_K + N_EXPERTS * EXPERT_BLOCK
        pos_t, blk_e, blk_src, n_used, zstart, zcount, next_e, wslot = routing_layout(
            ids_t[:TOP_K], rank_t[:TOP_K], counts[:, 0], n_rows // EXPERT_BLOCK)
        xs = dispatch(h1_tiles, pos_t, zstart, zcount, n_used, n_rows)
        ys = experts(xs, blk_e, blk_src, n_used, next_e, wslot,
                     w_gate_up[l], b_gate_up[l].reshape(N_EXPERTS, 1, 2 * D_FF),
                     w_down[l], b_down[l].reshape(N_EXPERTS, 1, D))
        out = combine(ys, pos_t, gates_t, h2, ln2_g[l].reshape(1, D), ln2_b[l].reshape(1, D))
        h = out.reshape(B, S, D)
    return h
```

```python
import functools
import math

import jax
import jax.numpy as jnp
from jax import lax
from jax.experimental import pallas as pl
from jax.experimental.pallas import tpu as pltpu

D_MODEL = 1024
ATT_GROUPS = ((128, 1), (512, 4), (2048, 16))
N_GROUPS = len(ATT_GROUPS)
HEADS_PER_GROUP = 8
HEAD_DIM = 64
GROUP_WIDTH = HEADS_PER_GROUP * HEAD_DIM
QKV_WIDTH = N_GROUPS * GROUP_WIDTH
ATT_BLOCK = 128
N_BUCKETS = 32
MAX_DISTANCE = 2048
CONV_CH = 768
CONV_WIDTH = 31
CONV_HALO = 32
N_EXPERTS = 32
TOP_K = 4
D_FF = 1024
SWIGLU_LIMIT = 7.0
SWIGLU_ALPHA = 1.702
IN_WIDTH = 3 * QKV_WIDTH + 2 * CONV_CH + 2 * D_MODEL
LN_EPS = 1e-5
NEG_INF = -1e30
DEPTH = 1
DEEPNORM_ALPHA = (2 * DEPTH) ** 0.25

LANES = 128
SUBLANES = 8
EXPERT_BLOCK = 512
VMEM_LIMIT = 56 * 1024 * 1024

F32 = jnp.float32
BF16 = jnp.bfloat16


def _params(sem, vmem=VMEM_LIMIT):
    return pltpu.CompilerParams(dimension_semantics=sem, vmem_limit_bytes=vmem)


def _sigmoid(x):
    return 0.5 * jnp.tanh(0.5 * x) + 0.5


def _layer_norm(x, g, b):
    mu = jnp.mean(x, axis=-1, keepdims=True)
    xc = x - mu
    var = jnp.mean(xc * xc, axis=-1, keepdims=True)
    return xc * lax.rsqrt(var + LN_EPS) * g + b


def _in_proj_body(x_ref, w_ref, o_ref):
    o_ref[...] = jnp.dot(x_ref[...], w_ref[...], preferred_element_type=F32).astype(o_ref.dtype)


def in_proj(xb, w_b, tn, name, tm=1024):
    T = xb.shape[0]
    N = w_b.shape[1]
    tm = min(tm, T)
    return pl.pallas_call(
        _in_proj_body,
        grid=(N // tn, T // tm),
        in_specs=[pl.BlockSpec((tm, D_MODEL), lambda n, m: (m, 0)),
                  pl.BlockSpec((D_MODEL, tn), lambda n, m: (0, n))],
        out_specs=pl.BlockSpec((tm, tn), lambda n, m: (m, n)),
        out_shape=jax.ShapeDtypeStruct((T, N), BF16),
        compiler_params=_params(("arbitrary", "arbitrary")),
        name=name,
    )(xb, w_b)


def _t5_bucket(dist):
    max_exact = N_BUCKETS // 2
    log_ratio = jnp.log(jnp.maximum(dist, max_exact).astype(F32) / max_exact) / math.log(MAX_DISTANCE / max_exact)
    large = jnp.minimum(max_exact + (log_ratio * (N_BUCKETS - max_exact)).astype(jnp.int32), N_BUCKETS - 1)
    return jnp.where(dist < max_exact, dist, large)


def _bias_body(tbl_ref, bucket_ref, band_ref, o_ref):
    g = pl.program_id(0)
    h = pl.program_id(1)
    col = g * HEADS_PER_GROUP + h
    bucket = bucket_ref[...]
    acc = jnp.zeros(bucket.shape, F32)
    for k in range(N_BUCKETS):
        acc = jnp.where(bucket == k, tbl_ref[k, col], acc)
    band = band_ref[...] != 0
    kj = lax.broadcasted_iota(jnp.int32, bucket.shape, 1)
    o_ref[0] = jnp.where(band, acc, NEG_INF)
    o_ref[1] = jnp.where(band & (kj >= ATT_BLOCK), acc, NEG_INF)


def bias_tables(rel_bias):
    qi = jnp.arange(ATT_BLOCK)[:, None]
    kj = jnp.arange(2 * ATT_BLOCK)[None, :]
    dist = qi - kj + ATT_BLOCK
    buckets, bands = [], []
    for window, dil in ATT_GROUPS:
        bands.append(((dist >= 0) & (dist <= window // dil)).astype(jnp.int32))
        buckets.append(_t5_bucket(jnp.maximum(dist, 0) * dil).astype(jnp.int32))
    buckets = jnp.stack(buckets)
    bands = jnp.stack(bands)
    blk = (None, ATT_BLOCK, 2 * ATT_BLOCK)
    return pl.pallas_call(
        _bias_body,
        grid=(N_GROUPS, HEADS_PER_GROUP),
        in_specs=[pl.BlockSpec(memory_space=pltpu.SMEM),
                  pl.BlockSpec(blk, lambda g, h: (g, 0, 0)),
                  pl.BlockSpec(blk, lambda g, h: (g, 0, 0))],
        out_specs=pl.BlockSpec((None, None, 2, ATT_BLOCK, 2 * ATT_BLOCK), lambda g, h: (g, h, 0, 0, 0)),
        out_shape=jax.ShapeDtypeStruct((N_GROUPS, HEADS_PER_GROUP, 2, ATT_BLOCK, 2 * ATT_BLOCK), F32),
        compiler_params=_params(("arbitrary", "arbitrary")),
        name="bias_tables",
    )(rel_bias.astype(F32), buckets, bands)


def _attn_body(q_ref, kc_ref, kp_ref, vc_ref, vp_ref, bm_ref, o_ref, st_ref, k_all, v_all, s_buf, *, tq):
    i = pl.program_id(2)
    nsub = tq // ATT_BLOCK
    k_all[0:ATT_BLOCK] = kp_ref[...]
    k_all[ATT_BLOCK:] = kc_ref[...]
    v_all[0:ATT_BLOCK] = vp_ref[...]
    v_all[ATT_BLOCK:] = vc_ref[...]
    lo = lax.broadcasted_iota(jnp.int32, (ATT_BLOCK, LANES), 1) < HEAD_DIM
    nt_dims = (((1,), (1,)), ((), ()))
    first = jnp.where(i == 0, 1, 0)

    def scores(s, slot):
        r0 = s * ATT_BLOCK
        for j in range(HEADS_PER_GROUP // 2):
            cs = slice(j * LANES, (j + 1) * LANES)
            qp = q_ref[r0:r0 + ATT_BLOCK, cs] * jnp.asarray(HEAD_DIM ** -0.5, BF16)
            kp = k_all[r0:r0 + 2 * ATT_BLOCK, cs]
            for hh in range(2):
                h = 2 * j + hh
                qh = jnp.where(lo if hh == 0 else ~lo, qp, jnp.zeros_like(qp))
                bias = bm_ref[h, first] if s == 0 else bm_ref[h, 0]
                s_buf[slot, h] = lax.dot_general(qh, kp, nt_dims, preferred_element_type=F32) + bias

    def softmax_pv(s, slot):
        rows = slice(s * ATT_BLOCK, (s + 1) * ATT_BLOCK)
        st_ref[rows, :] = jnp.zeros((ATT_BLOCK, LANES), F32)
        for j in range(HEADS_PER_GROUP // 2):
            vp = v_all[s * ATT_BLOCK:(s + 2) * ATT_BLOCK, j * LANES:(j + 1) * LANES]
            for hh in range(2):
                h = 2 * j + hh
                sc = s_buf[slot, h]
                m = jnp.max(sc, axis=-1, keepdims=True)
                p = jnp.exp(sc - m)
                den = jnp.sum(p, axis=-1, keepdims=True)
                pv = jnp.dot(p.astype(BF16), vp, preferred_element_type=F32) * (1.0 / den)
                c0 = h * HEAD_DIM
                o_ref[rows, c0:c0 + HEAD_DIM] = pv[:, hh * HEAD_DIM:(hh + 1) * HEAD_DIM]
                st_ref[rows, h:h + 1] = m
                st_ref[rows, HEADS_PER_GROUP + h:HEADS_PER_GROUP + h + 1] = den

    scores(0, 0)
    for s in range(nsub):
        softmax_pv(s, s % 2)
        if s + 1 < nsub:
            scores(s + 1, (s + 1) % 2)


def attention_group(qkv, bm_g, gi, tq=512):
    B, dil, L, _ = qkv.shape
    tq = min(tq, L)
    sub = tq // ATT_BLOCK

    def cur(col):
        return pl.BlockSpec((None, None, tq, GROUP_WIDTH), lambda b, r, i: (b, r, i, col))

    def prev(col):
        return pl.BlockSpec((None, None, ATT_BLOCK, GROUP_WIDTH),
                            lambda b, r, i: (b, r, jnp.maximum(i * sub - 1, 0), col))

    return pl.pallas_call(
        functools.partial(_attn_body, tq=tq),
        grid=(B, dil, L // tq),
        in_specs=[cur(0), cur(1), prev(1), cur(2), prev(2),
                  pl.BlockSpec((HEADS_PER_GROUP, 2, ATT_BLOCK, 2 * ATT_BLOCK), lambda b, r, i: (0, 0, 0, 0))],
        out_specs=[pl.BlockSpec((None, None, tq, GROUP_WIDTH), lambda b, r, i: (b, r, i, 0)),
                   pl.BlockSpec((None, None, tq, LANES), lambda b, r, i: (b, r, i, 0))],
        out_shape=[jax.ShapeDtypeStruct((B, dil, L, GROUP_WIDTH), F32),
                   jax.ShapeDtypeStruct((B, dil, L, LANES), F32)],
        scratch_shapes=[pltpu.VMEM((ATT_BLOCK + tq, GROUP_WIDTH), BF16),
                        pltpu.VMEM((ATT_BLOCK + tq, GROUP_WIDTH), BF16),
                        pltpu.VMEM((2, HEADS_PER_GROUP, ATT_BLOCK, 2 * ATT_BLOCK), F32)],
        compiler_params=_params(("arbitrary", "arbitrary", "arbitrary")),
        name=f"attention_g{gi}",
    )(qkv, qkv, qkv, qkv, qkv, bm_g)


def _split_bf16(x):
    hi = x.astype(BF16)
    lo = (x - hi.astype(F32)).astype(BF16)
    return hi, lo


def _load_token_tiles(ref, n):
    return jnp.concatenate([ref[pl.ds(c, n, stride=SUBLANES), :] for c in range(D_MODEL // LANES)], axis=1)


def _store_token_tiles(ref, x, n):
    for c in range(D_MODEL // LANES):
        ref[pl.ds(c, n, stride=SUBLANES), :] = x[:, c * LANES:(c + 1) * LANES]


def _to_token_order(blk_ref, tok_ref, dil, tm):
    n = tm // dil
    for r in range(dil):
        for c in range(tok_ref.shape[0]):
            tok_ref[c, pl.ds(r, n, stride=dil), :] = blk_ref[r, :, c * LANES:(c + 1) * LANES]


def _mixer_out_body(o0, o1, o2, s0, s1, s2, uv_ref, ug_ref, uvh_ref, ugh_ref, ga_ref, gc_ref, x_ref,
                    wdw_ref, bdw_ref, cg_ref, cb_ref, woa_ref, woc_ref, wout_ref, g1_ref, b1_ref,
                    h_ref, ht_ref, glu_ref, dw_ref, shift_ref, tok_o1, tok_s1, tok_o2, tok_s2, *, tm, chunk):
    i = pl.program_id(1)

    ncol = GROUP_WIDTH // LANES
    _to_token_order(o1, tok_o1, ATT_GROUPS[1][1], tm)
    _to_token_order(s1, tok_s1, ATT_GROUPS[1][1], tm)
    _to_token_order(o2, tok_o2, ATT_GROUPS[2][1], tm)
    _to_token_order(s2, tok_s2, ATT_GROUPS[2][1], tm)
    outs = [o0[0],
            jnp.concatenate([tok_o1[c] for c in range(ncol)], axis=1),
            jnp.concatenate([tok_o2[c] for c in range(ncol)], axis=1)]
    sts = [s0[0], tok_s1[0], tok_s2[0]]

    mx = jnp.maximum(jnp.maximum(sts[0], sts[1]), sts[2])
    wts = [pltpu.roll(st, LANES - HEADS_PER_GROUP, axis=1) * jnp.exp(st - mx) for st in sts]
    wsum = wts[0] + wts[1] + wts[2]
    row = lax.broadcasted_iota(jnp.int32, (LANES, GROUP_WIDTH), 0)
    colh = lax.broadcasted_iota(jnp.int32, (LANES, GROUP_WIDTH), 1) // HEAD_DIM
    expand = (row == colh).astype(BF16)
    attn = jnp.zeros((tm, GROUP_WIDTH), F32)
    head_lane = lax.broadcasted_iota(jnp.int32, (tm, LANES), 1) < HEADS_PER_GROUP
    for wt, o in zip(wts, outs):
        c_hi, c_lo = _split_bf16(jnp.where(head_lane, wt / wsum, 0.0))
        c = (jnp.dot(c_hi, expand, preferred_element_type=F32)
             + jnp.dot(c_lo, expand, preferred_element_type=F32))
        attn = attn + c * o
    a_out = jnp.dot(attn.astype(BF16), woa_ref[...], preferred_element_type=F32)

    gh = uvh_ref[...].astype(F32) * _sigmoid(ugh_ref[...].astype(F32))
    glu_ref[0:CONV_HALO] = jnp.where(i == 0, 0.0, gh)
    glu_ref[CONV_HALO:] = uv_ref[...].astype(F32) * _sigmoid(ug_ref[...].astype(F32))
    first_tap = CONV_HALO - (CONV_WIDTH - 1)
    for b in range(1, SUBLANES):
        shift_ref[b - 1] = glu_ref[b:b + shift_ref.shape[1], :]

    for c0 in range(0, CONV_CH, LANES):
        cs = slice(c0, c0 + LANES)
        bias = jnp.broadcast_to(bdw_ref[:, cs], (chunk, LANES))
        for r0 in range(0, tm, chunk):
            acc = bias
            for j in range(CONV_WIDTH):
                a, b = divmod(first_tap + j, SUBLANES)
                lo_row = r0 + a * SUBLANES
                rows = glu_ref[lo_row:lo_row + chunk, cs] if b == 0 else shift_ref[b - 1, lo_row:lo_row + chunk, cs]
                acc = acc + wdw_ref[j:j + 1, cs] * rows
            dw_ref[r0:r0 + chunk, cs] = acc
    cn = _layer_norm(dw_ref[...], cg_ref[...], cb_ref[...])
    conv = cn * _sigmoid(cn)
    c_out = jnp.dot(conv.astype(BF16), woc_ref[...], preferred_element_type=F32)

    merged = (_sigmoid(ga_ref[...].astype(F32)) * a_out + _sigmoid(gc_ref[...].astype(F32)) * c_out)
    mix = jnp.dot(merged.astype(BF16), wout_ref[...], preferred_element_type=F32)
    h = _layer_norm(DEEPNORM_ALPHA * x_ref[...] + mix, g1_ref[...], b1_ref[...])
    h_ref[...] = h
    _store_token_tiles(ht_ref, h, tm)


def mixer_out(o_list, st_list, proj3, x3, w_dw, b_dw, cg, cb, woa_b, woc_b, wout_b, g1, b1, tm=512, chunk=32):
    B, S, _ = x3.shape
    tm = min(tm, S)
    halo_blocks = tm // CONV_HALO
    uv_col = QKV_WIDTH // CONV_CH
    ga_col = (QKV_WIDTH + 2 * CONV_CH) // D_MODEL
    ncol = GROUP_WIDTH // LANES

    def tile(width, col=0):
        return pl.BlockSpec((None, tm, width), lambda b, i: (b, i, col))

    def dilated(gi, width):
        dil = ATT_GROUPS[gi][1]
        return pl.BlockSpec((None, dil, tm // dil, width), lambda b, i: (b, 0, i, 0))

    def halo(col):
        return pl.BlockSpec((None, CONV_HALO, CONV_CH), lambda b, i: (b, jnp.maximum(i * halo_blocks - 1, 0), col))

    def whole(shape):
        return pl.BlockSpec(shape, lambda b, i: (0,) * len(shape))

    return pl.pallas_call(
        functools.partial(_mixer_out_body, tm=tm, chunk=chunk),
        grid=(B, S // tm),
        in_specs=[dilated(g, GROUP_WIDTH) for g in range(N_GROUPS)] + [dilated(g, LANES) for g in range(N_GROUPS)]
        + [tile(CONV_CH, uv_col), tile(CONV_CH, uv_col + 1), halo(uv_col), halo(uv_col + 1),
           tile(D_MODEL, ga_col), tile(D_MODEL, ga_col + 1), tile(D_MODEL),
           whole((CONV_WIDTH, CONV_CH)), whole((1, CONV_CH)), whole((1, CONV_CH)), whole((1, CONV_CH)),
           whole((GROUP_WIDTH, D_MODEL)), whole((CONV_CH, D_MODEL)), whole((D_MODEL, D_MODEL)),
           whole((1, D_MODEL)), whole((1, D_MODEL))],
        out_specs=[tile(D_MODEL), pl.BlockSpec((tm * SUBLANES, LANES), lambda b, i: (b * (S // tm) + i, 0))],
        out_shape=[jax.ShapeDtypeStruct((B, S, D_MODEL), F32),
                   jax.ShapeDtypeStruct((B * S * SUBLANES, LANES), F32)],
        scratch_shapes=[pltpu.VMEM((CONV_HALO + tm, CONV_CH), F32), pltpu.VMEM((tm, CONV_CH), F32),
                        pltpu.VMEM((SUBLANES - 1, tm + CONV_HALO - SUBLANES, CONV_CH), F32),
                        pltpu.VMEM((ncol, tm, LANES), F32), pltpu.VMEM((1, tm, LANES), F32),
                        pltpu.VMEM((ncol, tm, LANES), F32), pltpu.VMEM((1, tm, LANES), F32)],
        compiler_params=_params(("arbitrary", "arbitrary")),
        name="mixer_out",
    )(*o_list, *st_list, proj3, proj3, proj3, proj3, proj3, proj3, x3,
      w_dw, b_dw, cg, cb, woa_b, woc_b, wout_b, g1, b1)


def _router_body(h_ref, wt_ref, b_ref, gates_ref, ids_ref, rank_ref, cnt_ref, carry_ref, tri_ref, *, tm):
    step = pl.program_id(0)

    @pl.when(step == 0)
    def _():
        carry_ref[...] = jnp.zeros_like(carry_ref)
        r_i = lax.broadcasted_iota(jnp.int32, (tm, tm), 0)
        c_i = lax.broadcasted_iota(jnp.int32, (tm, tm), 1)
        tri_ref[...] = (r_i < c_i).astype(BF16)

    nt_dims = (((1,), (1,)), ((), ()))
    h_hi, h_lo = _split_bf16(h_ref[...])
    w_hi, w_lo = _split_bf16(wt_ref[...])
    logits = (lax.dot_general(w_hi, h_hi, nt_dims, preferred_element_type=F32)
              + (lax.dot_general(w_hi, h_lo, nt_dims, preferred_element_type=F32)
                 + lax.dot_general(w_lo, h_hi, nt_dims, preferred_element_type=F32))
              + b_ref[...])
    expert = lax.broadcasted_iota(jnp.int32, (N_EXPERTS, tm), 0).astype(F32)
    row = lax.broadcasted_iota(jnp.int32, (SUBLANES, tm), 0)

    work = logits
    vals, hots = [], []
    ids = jnp.zeros((SUBLANES, tm), F32)
    for k in range(TOP_K):
        v = jnp.max(work, axis=0, keepdims=True)
        idx = jnp.min(jnp.where(work == v, expert, float(N_EXPERTS)), axis=0, keepdims=True)
        hot = expert == idx
        work = jnp.where(hot, -jnp.inf, work)
        ids = jnp.where(row == k, idx, ids)
        vals.append(v)
        hots.append(hot)

    es = [jnp.exp(v - vals[0]) for v in vals]
    esum = es[0] + es[1] + es[2] + es[3]
    gates = jnp.zeros((SUBLANES, tm), F32)
    for k in range(TOP_K):
        gates = jnp.where(row == k, es[k] / esum, gates)

    sel = (hots[0] | hots[1] | hots[2] | hots[3])
    before = jnp.dot(sel.astype(BF16), tri_ref[...], preferred_element_type=F32) + carry_ref[:, 0:1]
    rank = jnp.zeros((SUBLANES, tm), F32)
    for k in range(TOP_K):
        rk = jnp.sum(jnp.where(hots[k], before, 0.0), axis=0, keepdims=True)
        rank = jnp.where(row == k, rk, rank)
    carry_ref[...] = carry_ref[...] + jnp.sum(sel.astype(F32), axis=1, keepdims=True)

    gates_ref[...] = gates
    ids_ref[...] = ids.astype(jnp.int32)
    rank_ref[...] = rank.astype(jnp.int32)
    cnt_ref[...] = carry_ref[...].astype(jnp.int32)


def router(h2, w_router, b_router, tm=512):
    T = h2.shape[0]
    tm = min(tm, T)
    tile = pl.BlockSpec((SUBLANES, tm), lambda i: (0, i))
    return pl.pallas_call(
        functools.partial(_router_body, tm=tm),
        grid=(T // tm,),
        in_specs=[pl.BlockSpec((tm, D_MODEL), lambda i: (i, 0)),
                  pl.BlockSpec((N_EXPERTS, D_MODEL), lambda i: (0, 0)),
                  pl.BlockSpec((N_EXPERTS, 1), lambda i: (0, 0))],
        out_specs=[tile, tile, tile, pl.BlockSpec((N_EXPERTS, LANES), lambda i: (0, 0))],
        out_shape=[jax.ShapeDtypeStruct((SUBLANES, T), F32), jax.ShapeDtypeStruct((SUBLANES, T), jnp.int32),
                   jax.ShapeDtypeStruct((SUBLANES, T), jnp.int32), jax.ShapeDtypeStruct((N_EXPERTS, LANES), jnp.int32)],
        scratch_shapes=[pltpu.VMEM((N_EXPERTS, LANES), F32), pltpu.VMEM((tm, tm), BF16)],
        compiler_params=_params(("arbitrary",)),
        name="router",
    )(h2, w_router.astype(F32).T, b_router.astype(F32).reshape(N_EXPERTS, 1))


def _token_copy(src, dst, s_tok, d_tok, sem, n=1):
    rows = n * SUBLANES
    return pltpu.make_async_copy(src.at[pl.ds(pl.multiple_of(s_tok * SUBLANES, SUBLANES), rows), :],
                                 dst.at[pl.ds(pl.multiple_of(d_tok * SUBLANES, SUBLANES), rows), :], sem)


def _dispatch_body(zstart_ref, zcount_ref, n_used_ref, pos_ref, h_ref, xs_ref, zeros_ref, sem, zsem, *, tm, nb):
    half = EXPERT_BLOCK // 2
    bits = [1 << s for s in reversed(range(half.bit_length()))]

    def zero_fill(e, wait):
        start, count = zstart_ref[e], zcount_ref[e]
        for bit in bits:
            @pl.when((count & bit) != 0)
            def _():
                cp = _token_copy(zeros_ref, xs_ref, 0, 0 if wait else start + (count & ~(2 * bit - 1)), zsem, bit)
                cp.wait() if wait else cp.start()

    def zero_tail(blk, wait):
        for part in range(2):
            cp = _token_copy(zeros_ref, xs_ref, 0, 0 if wait else blk * EXPERT_BLOCK + part * half, zsem, half)
            cp.wait() if wait else cp.start()

    @pl.when(pl.program_id(0) == 0)
    def _():
        zeros_ref[...] = jnp.zeros_like(zeros_ref)
        lax.fori_loop(0, N_EXPERTS, lambda e, c: (zero_fill(e, False), c)[1], 0)
        lax.fori_loop(n_used_ref[0], nb, lambda blk, c: (zero_tail(blk, False), c)[1], 0)

    def issue(t, c):
        for k in range(TOP_K):
            _token_copy(h_ref, xs_ref, t, pos_ref[k * tm + t], sem).start(priority=k % 2)
        return c

    lax.fori_loop(0, tm, issue, 0)
    for k in range(TOP_K):
        _token_copy(h_ref, xs_ref, 0, 0, sem, tm).wait()

    @pl.when(pl.program_id(0) == 0)
    def _():
        lax.fori_loop(0, N_EXPERTS, lambda e, c: (zero_fill(e, True), c)[1], 0)
        lax.fori_loop(n_used_ref[0], nb, lambda blk, c: (zero_tail(blk, True), c)[1], 0)


def _tile_major(pos_t, tm):
    T = pos_t.shape[1]
    return pos_t.reshape(TOP_K, T // tm, tm).transpose(1, 0, 2).reshape(-1)


def dispatch(ht, pos_t, zstart, zcount, n_used, n_rows, tm=512):
    T = ht.shape[0] // SUBLANES
    tm = min(tm, T)
    grid_spec = pltpu.PrefetchScalarGridSpec(
        num_scalar_prefetch=3,
        grid=(T // tm,),
        in_specs=[pl.BlockSpec((TOP_K * tm,), lambda i, zs, zc, nu: (i,), memory_space=pltpu.SMEM),
                  pl.BlockSpec((tm * SUBLANES, LANES), lambda i, zs, zc, nu: (i, 0))],
        out_specs=pl.BlockSpec(memory_space=pl.ANY),
        scratch_shapes=[pltpu.VMEM((EXPERT_BLOCK // 2 * SUBLANES, LANES), F32), pltpu.SemaphoreType.DMA,
                        pltpu.SemaphoreType.DMA],
    )
    return pl.pallas_call(
        functools.partial(_dispatch_body, tm=tm, nb=n_rows // EXPERT_BLOCK),
        grid_spec=grid_spec,
        out_shape=jax.ShapeDtypeStruct((n_rows * SUBLANES, LANES), F32),
        compiler_params=_params(("arbitrary",)),
        name="dispatch",
    )(zstart, zcount, n_used, _tile_major(pos_t, tm), ht)


def _combine_body(pos_ref, pos_next_ref, gates_ref, h_ref, g2_ref, b2_ref, ys_ref, o_ref, buf, sems, *, tm):
    i = pl.program_id(0)

    def issue(p_ref, s):
        def body(t, c):
            for k in range(TOP_K):
                _token_copy(ys_ref, buf.at[s, k], p_ref[k * tm + t], t, sems.at[s]).start(priority=k % 2)
            return c

        lax.fori_loop(0, tm, body, 0)

    @pl.when(i == 0)
    def _():
        issue(pos_ref, 0)

    def step(slot):
        @pl.when(i + 1 < pl.num_programs(0))
        def _():
            issue(pos_next_ref, 1 - slot)

        for k in range(TOP_K):
            _token_copy(ys_ref, buf.at[slot, k], 0, 0, sems.at[slot], tm).wait()

        gates = gates_ref[...].T
        ffn = gates[:, 0:1] * _load_token_tiles(buf.at[slot, 0], tm)
        for k in range(1, TOP_K):
            ffn = ffn + gates[:, k:k + 1] * _load_token_tiles(buf.at[slot, k], tm)
        o_ref[...] = _layer_norm(DEEPNORM_ALPHA * h_ref[...] + ffn, g2_ref[...], b2_ref[...])

    for slot in range(2):
        pl.when(i % 2 == slot)(functools.partial(step, slot))


def combine(ys, pos_t, gates_t, h2, g2, b2, tm=256):
    T = h2.shape[0]
    tm = min(tm, T)
    last = T // tm - 1
    pos_flat = _tile_major(pos_t, tm)
    return pl.pallas_call(
        functools.partial(_combine_body, tm=tm),
        grid=(T // tm,),
        in_specs=[pl.BlockSpec((TOP_K * tm,), lambda i: (i,), memory_space=pltpu.SMEM),
                  pl.BlockSpec((TOP_K * tm,), lambda i: (jnp.minimum(i + 1, last),), memory_space=pltpu.SMEM),
                  pl.BlockSpec((SUBLANES, tm), lambda i: (0, i)),
                  pl.BlockSpec((tm, D_MODEL), lambda i: (i, 0)),
                  pl.BlockSpec((1, D_MODEL), lambda i: (0, 0)),
                  pl.BlockSpec((1, D_MODEL), lambda i: (0, 0)),
                  pl.BlockSpec(memory_space=pl.ANY)],
        out_specs=pl.BlockSpec((tm, D_MODEL), lambda i: (i, 0)),
        out_shape=jax.ShapeDtypeStruct((T, D_MODEL), F32),
        scratch_shapes=[pltpu.VMEM((2, TOP_K, tm * SUBLANES, LANES), F32), pltpu.SemaphoreType.DMA((2,))],
        compiler_params=_params(("arbitrary",)),
        name="combine",
    )(pos_flat, pos_flat, gates_t, h2, g2, b2, ys)


def _experts_body(blk_e, blk_src, n_used, next_e, wslot, xs_ref, wgu_hbm, bgu_ref, wdn_hbm, bdn_ref, ys_ref,
                  wgu_f, wdn_f, wgu_b, wdn_b, sems):
    del blk_src
    i = pl.program_id(0)
    used = i < n_used[0]
    e = blk_e[i]

    def weight_copies(expert, slot):
        return (pltpu.make_async_copy(wgu_hbm.at[expert], wgu_f.at[slot], sems.at[slot]),
                pltpu.make_async_copy(wdn_hbm.at[expert], wdn_f.at[slot], sems.at[slot]))

    @pl.when(i == 0)
    def _():
        for cp in weight_copies(e, 0):
            cp.start()

    @pl.when((i == 0) | (e != blk_e[jnp.maximum(i - 1, 0)]))
    def _():
        nxt = next_e[e]
        for slot in range(2):
            @pl.when(wslot[e] == slot)
            def _():
                for cp in weight_copies(e, slot):
                    cp.wait()
                wgu_b[...] = wgu_f[slot].astype(BF16)
                wdn_b[...] = wdn_f[slot].astype(BF16)

                @pl.when(nxt < N_EXPERTS)
                def _():
                    for cp in weight_copies(nxt, 1 - slot):
                        cp.start()

    @pl.when(used)
    def _():
        x = _load_token_tiles(xs_ref, EXPERT_BLOCK).astype(BF16)
        hgu = jnp.dot(x, wgu_b[...], preferred_element_type=F32) + bgu_ref[...]
        gate = jnp.minimum(hgu[:, :D_FF], SWIGLU_LIMIT)
        up = jnp.clip(hgu[:, D_FF:], -SWIGLU_LIMIT, SWIGLU_LIMIT)
        act = (up + 1.0) * gate * _sigmoid(SWIGLU_ALPHA * gate)
        y = jnp.dot(act.astype(BF16), wdn_b[...], preferred_element_type=F32) + bdn_ref[...]
        _store_token_tiles(ys_ref, y, EXPERT_BLOCK)

    @pl.when(jnp.logical_not(used))
    def _():
        ys_ref[...] = jnp.zeros_like(ys_ref)


def experts(xs, blk_e, blk_src, n_used, next_e, wslot, wgu, bgu, wdn, bdn):
    n_rows = xs.shape[0] // SUBLANES
    nb = n_rows // EXPERT_BLOCK
    rows = EXPERT_BLOCK * SUBLANES
    grid_spec = pltpu.PrefetchScalarGridSpec(
        num_scalar_prefetch=5,
        grid=(nb,),
        in_specs=[pl.BlockSpec((rows, LANES), lambda i, e, s, n, ne, ws: (s[i], 0)),
                  pl.BlockSpec(memory_space=pl.ANY),
                  pl.BlockSpec((None, 1, 2 * D_FF), lambda i, e, s, n, ne, ws: (e[i], 0, 0)),
                  pl.BlockSpec(memory_space=pl.ANY),
                  pl.BlockSpec((None, 1, D_MODEL), lambda i, e, s, n, ne, ws: (e[i], 0, 0))],
        out_specs=pl.BlockSpec((rows, LANES), lambda i, e, s, n, ne, ws: (i, 0)),
        scratch_shapes=[pltpu.VMEM((2, D_MODEL, 2 * D_FF), F32), pltpu.VMEM((2, D_FF, D_MODEL), F32),
                        pltpu.VMEM((D_MODEL, 2 * D_FF), BF16), pltpu.VMEM((D_FF, D_MODEL), BF16),
                        pltpu.SemaphoreType.DMA((2,))],
    )
    return pl.pallas_call(
        _experts_body,
        grid_spec=grid_spec,
        out_shape=jax.ShapeDtypeStruct((n_rows * SUBLANES, LANES), F32),
        compiler_params=_params(("arbitrary",)),
        name="experts",
    )(blk_e, blk_src, n_used, next_e, wslot, xs, wgu, bgu, wdn, bdn)


def routing_layout(ids, rank, counts, n_blocks):
    padded = (counts + EXPERT_BLOCK - 1) // EXPERT_BLOCK * EXPERT_BLOCK
    pad_ends = jnp.cumsum(padded)
    pad_starts = pad_ends - padded
    start_of = jnp.zeros_like(ids)
    for e in range(N_EXPERTS):
        start_of = jnp.where(ids == e, pad_starts[e], start_of)
    pos_t = (start_of + rank).astype(jnp.int32)
    n_used = jnp.maximum(pad_ends[-1] // EXPERT_BLOCK, 1).astype(jnp.int32)
    blk = jnp.minimum(jnp.arange(n_blocks, dtype=jnp.int32), n_used - 1)
    blk_e = jnp.sum(pad_ends[None, :] <= (blk * EXPERT_BLOCK)[:, None], axis=1)
    blk_e = jnp.minimum(blk_e, N_EXPERTS - 1).astype(jnp.int32)
    zstart = (pad_starts + counts).astype(jnp.int32)
    zcount = (padded - counts).astype(jnp.int32)
    nonempty = counts > 0
    expert_ids = jnp.arange(N_EXPERTS, dtype=jnp.int32)
    later = jnp.where(nonempty[None, :] & (expert_ids[None, :] > expert_ids[:, None]), expert_ids[None, :], N_EXPERTS)
    next_e = jnp.min(later, axis=1).astype(jnp.int32)
    wslot = ((jnp.cumsum(nonempty) - 1) % 2).astype(jnp.int32)
    return pos_t, blk_e, blk, n_used.reshape(1), zstart, zcount, next_e, wslot


DILATE_WIDTH = 4 * LANES


def _dilate_body(x_ref, *refs, n):
    o_refs, slab = refs[:-1], refs[-1]
    for c0 in range(0, DILATE_WIDTH, LANES):
        slab[...] = x_ref[:, c0:c0 + LANES]
        for (_, dil), o_ref in zip(ATT_GROUPS, o_refs):
            for r in range(dil):
                o_ref[r, :, c0:c0 + LANES] = slab[pl.ds(r, n // dil, stride=dil), :].astype(BF16)


def dilate_cast(x3, n=2048):
    B, S, D = x3.shape
    n = min(n, S)
    dils = [dil for _, dil in ATT_GROUPS]
    return pl.pallas_call(
        functools.partial(_dilate_body, n=n),
        grid=(B, S // n, D // DILATE_WIDTH),
        in_specs=[pl.BlockSpec((None, n, DILATE_WIDTH), lambda b, i, c: (b, i, c))],
        out_specs=[pl.BlockSpec((None, dil, n // dil, DILATE_WIDTH), lambda b, i, c: (b, 0, i, c)) for dil in dils],
        out_shape=[jax.ShapeDtypeStruct((B, dil, S // dil, D), BF16) for dil in dils],
        scratch_shapes=[pltpu.VMEM((n, LANES), F32)],
        compiler_params=_params(("arbitrary", "arbitrary", "arbitrary")),
        name="dilate_cast",
    )(x3)


def kernel(x, w_in, rel_bias, w_dw, b_dw, conv_ln_g, conv_ln_b, w_o_attn, w_o_conv, w_out, ln1_g, ln1_b,
           w_router, b_router, w_gate_up, b_gate_up, w_down, b_down, ln2_g, ln2_b):
    B, S, D = x.shape
    T = B * S
    h = x
    bm = bias_tables(rel_bias)
    q_off, k_off, v_off, rest = 0, QKV_WIDTH, 2 * QKV_WIDTH, 3 * QKV_WIDTH
    for l in range(DEPTH):
        wb = w_in[l].astype(BF16)

        def group_cols(gi):
            return [wb[:, off + gi * GROUP_WIDTH:off + (gi + 1) * GROUP_WIDTH] for off in (q_off, k_off, v_off)]

        x_dil = dilate_cast(h)
        xb = x_dil[0].reshape(T, D)
        w_main = jnp.concatenate(group_cols(0) + [wb[:, rest:]], axis=1)
        proj = in_proj(xb, w_main, w_main.shape[1] // 2, "in_proj_main").reshape(B, S, w_main.shape[1])
        o_list, st_list = [], []
        for gi in range(N_GROUPS):
            dil = ATT_GROUPS[gi][1]
            if dil == 1:
                qkv = proj.reshape(B, 1, S, proj.shape[-1])
            else:
                w_g = jnp.concatenate(group_cols(gi), axis=1)
                qkv = in_proj(x_dil[gi].reshape(T, D), w_g, w_g.shape[1], f"in_proj_g{gi}")
                qkv = qkv.reshape(B, dil, S // dil, w_g.shape[1])
            o_g, st_g = attention_group(qkv, bm[gi], gi)
            o_list.append(o_g)
            st_list.append(st_g)
        h1, h1_tiles = mixer_out(o_list, st_list, proj, h,
                       w_dw[l].reshape(CONV_WIDTH, CONV_CH), b_dw[l].reshape(1, CONV_CH),
                       conv_ln_g[l].reshape(1, CONV_CH), conv_ln_b[l].reshape(1, CONV_CH),
                       w_o_attn[l].astype(BF16), w_o_conv[l].astype(BF16), w_out[l].astype(BF16),
                       ln1_g[l].reshape(1, D), ln1_b[l].reshape(1, D))
        h2 = h1.reshape(T, D)
        gates_t, ids_t, rank_t, counts = router(h2, w_router[l], b_router[l])
        n_rows = T * TOP_K + N_EXPERTS * EXPERT_BLOCK
        pos_t, blk_e, blk_src, n_used, zstart, zcount, next_e, wslot = routing_layout(
            ids_t[:TOP_K], rank_t[:TOP_K], counts[:, 0], n_rows // EXPERT_BLOCK)
        xs = dispatch(h1_tiles, pos_t, zstart, zcount, n_used, n_rows)
        ys = experts(xs, blk_e, blk_src, n_used, next_e, wslot,
                     w_gate_up[l], b_gate_up[l].reshape(N_EXPERTS, 1, 2 * D_FF),
                     w_down[l], b_down[l].reshape(N_EXPERTS, 1, D))
        out = combine(ys, pos_t, gates_t, h2, ln2_g[l].reshape(1, D), ln2_b[l].reshape(1, D))
        h = out.reshape(B, S, D)
    return h
```

```python
import functools
import math

import jax
import jax.numpy as jnp
from jax import lax
from jax.experimental import pallas as pl
from jax.experimental.pallas import tpu as pltpu

D_MODEL = 1024
ATT_GROUPS = ((128, 1), (512, 4), (2048, 16))
N_GROUPS = len(ATT_GROUPS)
HEADS_PER_GROUP = 8
HEAD_DIM = 64
GROUP_WIDTH = HEADS_PER_GROUP * HEAD_DIM
QKV_WIDTH = N_GROUPS * GROUP_WIDTH
ATT_BLOCK = 128
N_BUCKETS = 32
MAX_DISTANCE = 2048
CONV_CH = 768
CONV_WIDTH = 31
CONV_HALO = 32
N_EXPERTS = 32
TOP_K = 4
D_FF = 1024
SWIGLU_LIMIT = 7.0
SWIGLU_ALPHA = 1.702
IN_WIDTH = 3 * QKV_WIDTH + 2 * CONV_CH + 2 * D_MODEL
LN_EPS = 1e-5
NEG_INF = -1e30
DEPTH = 1
DEEPNORM_ALPHA = (2 * DEPTH) ** 0.25

LANES = 128
SUBLANES = 8
EXPERT_BLOCK = 512
VMEM_LIMIT = 56 * 1024 * 1024

F32 = jnp.float32
BF16 = jnp.bfloat16


def _params(sem, vmem=VMEM_LIMIT):
    return pltpu.CompilerParams(dimension_semantics=sem, vmem_limit_bytes=vmem)


def _sigmoid(x):
    return 0.5 * jnp.tanh(0.5 * x) + 0.5


def _layer_norm(x, g, b):
    mu = jnp.mean(x, axis=-1, keepdims=True)
    xc = x - mu
    var = jnp.mean(xc * xc, axis=-1, keepdims=True)
    return xc * lax.rsqrt(var + LN_EPS) * g + b


def _in_proj_body(x_ref, w_ref, o_ref):
    o_ref[...] = jnp.dot(x_ref[...], w_ref[...], preferred_element_type=F32).astype(o_ref.dtype)


def in_proj(xb, w_b, tn, name, tm=1024):
    T = xb.shape[0]
    N = w_b.shape[1]
    tm = min(tm, T)
    return pl.pallas_call(
        _in_proj_body,
        grid=(N // tn, T // tm),
        in_specs=[pl.BlockSpec((tm, D_MODEL), lambda n, m: (m, 0)),
                  pl.BlockSpec((D_MODEL, tn), lambda n, m: (0, n))],
        out_specs=pl.BlockSpec((tm, tn), lambda n, m: (m, n)),
        out_shape=jax.ShapeDtypeStruct((T, N), BF16),
        compiler_params=_params(("arbitrary", "arbitrary")),
        name=name,
    )(xb, w_b)


def _t5_bucket(dist):
    max_exact = N_BUCKETS // 2
    log_ratio = jnp.log(jnp.maximum(dist, max_exact).astype(F32) / max_exact) / math.log(MAX_DISTANCE / max_exact)
    large = jnp.minimum(max_exact + (log_ratio * (N_BUCKETS - max_exact)).astype(jnp.int32), N_BUCKETS - 1)
    return jnp.where(dist < max_exact, dist, large)


def _bias_body(tbl_ref, bucket_ref, band_ref, o_ref):
    g = pl.program_id(0)
    h = pl.program_id(1)
    col = g * HEADS_PER_GROUP + h
    bucket = bucket_ref[...]
    acc = jnp.zeros(bucket.shape, F32)
    for k in range(N_BUCKETS):
        acc = jnp.where(bucket == k, tbl_ref[k, col], acc)
    band = band_ref[...] != 0
    kj = lax.broadcasted_iota(jnp.int32, bucket.shape, 1)
    o_ref[0] = jnp.where(band, acc, NEG_INF)
    o_ref[1] = jnp.where(band & (kj >= ATT_BLOCK), acc, NEG_INF)


def bias_tables(rel_bias):
    qi = jnp.arange(ATT_BLOCK)[:, None]
    kj = jnp.arange(2 * ATT_BLOCK)[None, :]
    dist = qi - kj + ATT_BLOCK
    buckets, bands = [], []
    for window, dil in ATT_GROUPS:
        bands.append(((dist >= 0) & (dist <= window // dil)).astype(jnp.int32))
        buckets.append(_t5_bucket(jnp.maximum(dist, 0) * dil).astype(jnp.int32))
    buckets = jnp.stack(buckets)
    bands = jnp.stack(bands)
    blk = (None, ATT_BLOCK, 2 * ATT_BLOCK)
    return pl.pallas_call(
        _bias_body,
        grid=(N_GROUPS, HEADS_PER_GROUP),
        in_specs=[pl.BlockSpec(memory_space=pltpu.SMEM),
                  pl.BlockSpec(blk, lambda g, h: (g, 0, 0)),
                  pl.BlockSpec(blk, lambda g, h: (g, 0, 0))],
        out_specs=pl.BlockSpec((None, None, 2, ATT_BLOCK, 2 * ATT_BLOCK), lambda g, h: (g, h, 0, 0, 0)),
        out_shape=jax.ShapeDtypeStruct((N_GROUPS, HEADS_PER_GROUP, 2, ATT_BLOCK, 2 * ATT_BLOCK), F32),
        compiler_params=_params(("arbitrary", "arbitrary")),
        name="bias_tables",
    )(rel_bias.astype(F32), buckets, bands)


def _attn_body(q_ref, kc_ref, kp_ref, vc_ref, vp_ref, bm_ref, o_ref, st_ref, k_all, v_all, s_buf, *, tq):
    i = pl.program_id(2)
    nsub = tq // ATT_BLOCK
    k_all[0:ATT_BLOCK] = kp_ref[...]
    k_all[ATT_BLOCK:] = kc_ref[...]
    v_all[0:ATT_BLOCK] = vp_ref[...]
    v_all[ATT_BLOCK:] = vc_ref[...]
    lo = lax.broadcasted_iota(jnp.int32, (ATT_BLOCK, LANES), 1) < HEAD_DIM
    nt_dims = (((1,), (1,)), ((), ()))
    first = jnp.where(i == 0, 1, 0)

    def scores(s, slot):
        r0 = s * ATT_BLOCK
        for j in range(HEADS_PER_GROUP // 2):
            cs = slice(j * LANES, (j + 1) * LANES)
            qp = q_ref[r0:r0 + ATT_BLOCK, cs] * jnp.asarray(HEAD_DIM ** -0.5, BF16)
            kp = k_all[r0:r0 + 2 * ATT_BLOCK, cs]
            for hh in range(2):
                h = 2 * j + hh
                qh = jnp.where(lo if hh == 0 else ~lo, qp, jnp.zeros_like(qp))
                bias = bm_ref[h, first] if s == 0 else bm_ref[h, 0]
                s_buf[slot, h] = lax.dot_general(qh, kp, nt_dims, preferred_element_type=F32) + bias

    def softmax_pv(s, slot):
        rows = slice(s * ATT_BLOCK, (s + 1) * ATT_BLOCK)
        st_ref[rows, :] = jnp.zeros((ATT_BLOCK, LANES), F32)
        for j in range(HEADS_PER_GROUP // 2):
            vp = v_all[s * ATT_BLOCK:(s + 2) * ATT_BLOCK, j * LANES:(j + 1) * LANES]
            for hh in range(2):
                h = 2 * j + hh
                sc = s_buf[slot, h]
                m = jnp.max(sc, axis=-1, keepdims=True)
                p = jnp.exp(sc - m)
                den = jnp.sum(p, axis=-1, keepdims=True)
                pv = jnp.dot(p.astype(BF16), vp, preferred_element_type=F32) * (1.0 / den)
                c0 = h * HEAD_DIM
                o_ref[rows, c0:c0 + HEAD_DIM] = pv[:, hh * HEAD_DIM:(hh + 1) * HEAD_DIM]
                st_ref[rows, h:h + 1] = m
                st_ref[rows, HEADS_PER_GROUP + h:HEADS_PER_GROUP + h + 1] = den

    scores(0, 0)
    for s in range(nsub):
        softmax_pv(s, s % 2)
        if s + 1 < nsub:
            scores(s + 1, (s + 1) % 2)


def attention_group(qkv, bm_g, gi, tq=512):
    B, dil, L, _ = qkv.shape
    tq = min(tq, L)
    sub = tq // ATT_BLOCK

    def cur(col):
        return pl.BlockSpec((None, None, tq, GROUP_WIDTH), lambda b, r, i: (b, r, i, col))

    def prev(col):
        return pl.BlockSpec((None, None, ATT_BLOCK, GROUP_WIDTH),
                            lambda b, r, i: (b, r, jnp.maximum(i * sub - 1, 0), col))

    return pl.pallas_call(
        functools.partial(_attn_body, tq=tq),
        grid=(B, dil, L // tq),
        in_specs=[cur(0), cur(1), prev(1), cur(2), prev(2),
                  pl.BlockSpec((HEADS_PER_GROUP, 2, ATT_BLOCK, 2 * ATT_BLOCK), lambda b, r, i: (0, 0, 0, 0))],
        out_specs=[pl.BlockSpec((None, None, tq, GROUP_WIDTH), lambda b, r, i: (b, r, i, 0)),
                   pl.BlockSpec((None, None, tq, LANES), lambda b, r, i: (b, r, i, 0))],
        out_shape=[jax.ShapeDtypeStruct((B, dil, L, GROUP_WIDTH), F32),
                   jax.ShapeDtypeStruct((B, dil, L, LANES), F32)],
        scratch_shapes=[pltpu.VMEM((ATT_BLOCK + tq, GROUP_WIDTH), BF16),
                        pltpu.VMEM((ATT_BLOCK + tq, GROUP_WIDTH), BF16),
                        pltpu.VMEM((2, HEADS_PER_GROUP, ATT_BLOCK, 2 * ATT_BLOCK), F32)],
        compiler_params=_params(("arbitrary", "arbitrary", "arbitrary")),
        name=f"attention_g{gi}",
    )(qkv, qkv, qkv, qkv, qkv, bm_g)


def _split_bf16(x):
    hi = x.astype(BF16)
    lo = (x - hi.astype(F32)).astype(BF16)
    return hi, lo


def _load_token_tiles(ref, n):
    return jnp.concatenate([ref[pl.ds(c, n, stride=SUBLANES), :] for c in range(D_MODEL // LANES)], axis=1)


def _store_token_tiles(ref, x, n):
    for c in range(D_MODEL // LANES):
        ref[pl.ds(c, n, stride=SUBLANES), :] = x[:, c * LANES:(c + 1) * LANES]


def _to_token_order(blk_ref, tok_ref, dil, tm):
    n = tm // dil
    for r in range(dil):
        for c in range(tok_ref.shape[0]):
            tok_ref[c, pl.ds(r, n, stride=dil), :] = blk_ref[r, :, c * LANES:(c + 1) * LANES]


def _mixer_out_body(o0, o1, o2, s0, s1, s2, uv_ref, ug_ref, uvh_ref, ugh_ref, ga_ref, gc_ref, x_ref,
                    wdw_ref, bdw_ref, cg_ref, cb_ref, woa_ref, woc_ref, wout_ref, g1_ref, b1_ref,
                    h_ref, ht_ref, glu_ref, dw_ref, shift_ref, tok_o1, tok_s1, tok_o2, tok_s2, *, tm, chunk):
    i = pl.program_id(1)

    ncol = GROUP_WIDTH // LANES
    _to_token_order(o1, tok_o1, ATT_GROUPS[1][1], tm)
    _to_token_order(s1, tok_s1, ATT_GROUPS[1][1], tm)
    _to_token_order(o2, tok_o2, ATT_GROUPS[2][1], tm)
    _to_token_order(s2, tok_s2, ATT_GROUPS[2][1], tm)
    outs = [o0[0],
            jnp.concatenate([tok_o1[c] for c in range(ncol)], axis=1),
            jnp.concatenate([tok_o2[c] for c in range(ncol)], axis=1)]
    sts = [s0[0], tok_s1[0], tok_s2[0]]

    mx = jnp.maximum(jnp.maximum(sts[0], sts[1]), sts[2])
    wts = [pltpu.roll(st, LANES - HEADS_PER_GROUP, axis=1) * jnp.exp(st - mx) for st in sts]
    wsum = wts[0] + wts[1] + wts[2]
    row = lax.broadcasted_iota(jnp.int32, (LANES, GROUP_WIDTH), 0)
    colh = lax.broadcasted_iota(jnp.int32, (LANES, GROUP_WIDTH), 1) // HEAD_DIM
    expand = (row == colh).astype(BF16)
    attn = jnp.zeros((tm, GROUP_WIDTH), F32)
    head_lane = lax.broadcasted_iota(jnp.int32, (tm, LANES), 1) < HEADS_PER_GROUP
    for wt, o in zip(wts, outs):
        c_hi, c_lo = _split_bf16(jnp.where(head_lane, wt / wsum, 0.0))
        c = (jnp.dot(c_hi, expand, preferred_element_type=F32)
             + jnp.dot(c_lo, expand, preferred_element_type=F32))
        attn = attn + c * o
    a_out = jnp.dot(attn.astype(BF16), woa_ref[...], preferred_element_type=F32)

    gh = uvh_ref[...].astype(F32) * _sigmoid(ugh_ref[...].astype(F32))
    glu_ref[0:CONV_HALO] = jnp.where(i == 0, 0.0, gh)
    glu_ref[CONV_HALO:] = uv_ref[...].astype(F32) * _sigmoid(ug_ref[...].astype(F32))
    first_tap = CONV_HALO - (CONV_WIDTH - 1)
    for b in range(1, SUBLANES):
        shift_ref[b - 1] = glu_ref[b:b + shift_ref.shape[1], :]

    for c0 in range(0, CONV_CH, LANES):
        cs = slice(c0, c0 + LANES)
        bias = jnp.broadcast_to(bdw_ref[:, cs], (chunk, LANES))
        for r0 in range(0, tm, chunk):
            acc = bias
            for j in range(CONV_WIDTH):
                a, b = divmod(first_tap + j, SUBLANES)
                lo_row = r0 + a * SUBLANES
                rows = glu_ref[lo_row:lo_row + chunk, cs] if b == 0 else shift_ref[b - 1, lo_row:lo_row + chunk, cs]
                acc = acc + wdw_ref[j:j + 1, cs] * rows
            dw_ref[r0:r0 + chunk, cs] = acc
    cn = _layer_norm(dw_ref[...], cg_ref[...], cb_ref[...])
    conv = cn * _sigmoid(cn)
    c_out = jnp.dot(conv.astype(BF16), woc_ref[...], preferred_element_type=F32)

    merged = (_sigmoid(ga_ref[...].astype(F32)) * a_out + _sigmoid(gc_ref[...].astype(F32)) * c_out)
    mix = jnp.dot(merged.astype(BF16), wout_ref[...], preferred_element_type=F32)
    h = _layer_norm(DEEPNORM_ALPHA * x_ref[...] + mix, g1_ref[...], b1_ref[...])
    h_ref[...] = h
    _store_token_tiles(ht_ref, h, tm)


def mixer_out(o_list, st_list, proj3, x3, w_dw, b_dw, cg, cb, woa_b, woc_b, wout_b, g1, b1, tm=512, chunk=64):
    B, S, _ = x3.shape
    tm = min(tm, S)
    halo_blocks = tm // CONV_HALO
    uv_col = QKV_WIDTH // CONV_CH
    ga_col = (QKV_WIDTH + 2 * CONV_CH) // D_MODEL
    ncol = GROUP_WIDTH // LANES

    def tile(width, col=0):
        return pl.BlockSpec((None, tm, width), lambda b, i: (b, i, col))

    def dilated(gi, width):
        dil = ATT_GROUPS[gi][1]
        return pl.BlockSpec((None, dil, tm // dil, width), lambda b, i: (b, 0, i, 0))

    def halo(col):
        return pl.BlockSpec((None, CONV_HALO, CONV_CH), lambda b, i: (b, jnp.maximum(i * halo_blocks - 1, 0), col))

    def whole(shape):
        return pl.BlockSpec(shape, lambda b, i: (0,) * len(shape))

    return pl.pallas_call(
        functools.partial(_mixer_out_body, tm=tm, chunk=chunk),
        grid=(B, S // tm),
        in_specs=[dilated(g, GROUP_WIDTH) for g in range(N_GROUPS)] + [dilated(g, LANES) for g in range(N_GROUPS)]
        + [tile(CONV_CH, uv_col), tile(CONV_CH, uv_col + 1), halo(uv_col), halo(uv_col + 1),
           tile(D_MODEL, ga_col), tile(D_MODEL, ga_col + 1), tile(D_MODEL),
           whole((CONV_WIDTH, CONV_CH)), whole((1, CONV_CH)), whole((1, CONV_CH)), whole((1, CONV_CH)),
           whole((GROUP_WIDTH, D_MODEL)), whole((CONV_CH, D_MODEL)), whole((D_MODEL, D_MODEL)),
           whole((1, D_MODEL)), whole((1, D_MODEL))],
        out_specs=[tile(D_MODEL), pl.BlockSpec((tm * SUBLANES, LANES), lambda b, i: (b * (S // tm) + i, 0))],
        out_shape=[jax.ShapeDtypeStruct((B, S, D_MODEL), F32),
                   jax.ShapeDtypeStruct((B * S * SUBLANES, LANES), F32)],
        scratch_shapes=[pltpu.VMEM((CONV_HALO + tm, CONV_CH), F32), pltpu.VMEM((tm, CONV_CH), F32),
                        pltpu.VMEM((SUBLANES - 1, tm + CONV_HALO - SUBLANES, CONV_CH), F32),
                        pltpu.VMEM((ncol, tm, LANES), F32), pltpu.VMEM((1, tm, LANES), F32),
                        pltpu.VMEM((ncol, tm, LANES), F32), pltpu.VMEM((1, tm, LANES), F32)],
        compiler_params=_params(("arbitrary", "arbitrary")),
        name="mixer_out",
    )(*o_list, *st_list, proj3, proj3, proj3, proj3, proj3, proj3, x3,
      w_dw, b_dw, cg, cb, woa_b, woc_b, wout_b, g1, b1)


def _router_body(h_ref, wt_ref, b_ref, gates_ref, ids_ref, rank_ref, cnt_ref, carry_ref, tri_ref, *, tm):
    step = pl.program_id(0)

    @pl.when(step == 0)
    def _():
        carry_ref[...] = jnp.zeros_like(carry_ref)
        r_i = lax.broadcasted_iota(jnp.int32, (tm, tm), 0)
        c_i = lax.broadcasted_iota(jnp.int32, (tm, tm), 1)
        tri_ref[...] = (r_i < c_i).astype(BF16)

    nt_dims = (((1,), (1,)), ((), ()))
    h_hi, h_lo = _split_bf16(h_ref[...])
    w_hi, w_lo = _split_bf16(wt_ref[...])
    logits = (lax.dot_general(w_hi, h_hi, nt_dims, preferred_element_type=F32)
              + (lax.dot_general(w_hi, h_lo, nt_dims, preferred_element_type=F32)
                 + lax.dot_general(w_lo, h_hi, nt_dims, preferred_element_type=F32))
              + b_ref[...])
    expert = lax.broadcasted_iota(jnp.int32, (N_EXPERTS, tm), 0).astype(F32)
    row = lax.broadcasted_iota(jnp.int32, (SUBLANES, tm), 0)

    work = logits
    vals, hots = [], []
    ids = jnp.zeros((SUBLANES, tm), F32)
    for k in range(TOP_K):
        v = jnp.max(work, axis=0, keepdims=True)
        idx = jnp.min(jnp.where(work == v, expert, float(N_EXPERTS)), axis=0, keepdims=True)
        hot = expert == idx
        work = jnp.where(hot, -jnp.inf, work)
        ids = jnp.where(row == k, idx, ids)
        vals.append(v)
        hots.append(hot)

    es = [jnp.exp(v - vals[0]) for v in vals]
    esum = es[0] + es[1] + es[2] + es[3]
    gates = jnp.zeros((SUBLANES, tm), F32)
    for k in range(TOP_K):
        gates = jnp.where(row == k, es[k] / esum, gates)

    sel = (hots[0] | hots[1] | hots[2] | hots[3])
    before = jnp.dot(sel.astype(BF16), tri_ref[...], preferred_element_type=F32) + carry_ref[:, 0:1]
    rank = jnp.zeros((SUBLANES, tm), F32)
    for k in range(TOP_K):
        rk = jnp.sum(jnp.where(hots[k], before, 0.0), axis=0, keepdims=True)
        rank = jnp.where(row == k, rk, rank)
    carry_ref[...] = carry_ref[...] + jnp.sum(sel.astype(F32), axis=1, keepdims=True)

    gates_ref[...] = gates
    ids_ref[...] = ids.astype(jnp.int32)
    rank_ref[...] = rank.astype(jnp.int32)
    cnt_ref[...] = carry_ref[...].astype(jnp.int32)


def router(h2, w_router, b_router, tm=512):
    T = h2.shape[0]
    tm = min(tm, T)
    tile = pl.BlockSpec((SUBLANES, tm), lambda i: (0, i))
    return pl.pallas_call(
        functools.partial(_router_body, tm=tm),
        grid=(T // tm,),
        in_specs=[pl.BlockSpec((tm, D_MODEL), lambda i: (i, 0)),
                  pl.BlockSpec((N_EXPERTS, D_MODEL), lambda i: (0, 0)),
                  pl.BlockSpec((N_EXPERTS, 1), lambda i: (0, 0))],
        out_specs=[tile, tile, tile, pl.BlockSpec((N_EXPERTS, LANES), lambda i: (0, 0))],
        out_shape=[jax.ShapeDtypeStruct((SUBLANES, T), F32), jax.ShapeDtypeStruct((SUBLANES, T), jnp.int32),
                   jax.ShapeDtypeStruct((SUBLANES, T), jnp.int32), jax.ShapeDtypeStruct((N_EXPERTS, LANES), jnp.int32)],
        scratch_shapes=[pltpu.VMEM((N_EXPERTS, LANES), F32), pltpu.VMEM((tm, tm), BF16)],
        compiler_params=_params(("arbitrary",)),
        name="router",
    )(h2, w_router.astype(F32).T, b_router.astype(F32).reshape(N_EXPERTS, 1))


def _token_copy(src, dst, s_tok, d_tok, sem, n=1):
    rows = n * SUBLANES
    return pltpu.make_async_copy(src.at[pl.ds(pl.multiple_of(s_tok * SUBLANES, SUBLANES), rows), :],
                                 dst.at[pl.ds(pl.multiple_of(d_tok * SUBLANES, SUBLANES), rows), :], sem)


def _dispatch_body(zstart_ref, zcount_ref, n_used_ref, pos_ref, h_ref, xs_ref, zeros_ref, sem, zsem, *, tm, nb):
    half = EXPERT_BLOCK // 2
    bits = [1 << s for s in reversed(range(half.bit_length()))]

    def zero_fill(e, wait):
        start, count = zstart_ref[e], zcount_ref[e]
        for bit in bits:
            @pl.when((count & bit) != 0)
            def _():
                cp = _token_copy(zeros_ref, xs_ref, 0, 0 if wait else start + (count & ~(2 * bit - 1)), zsem, bit)
                cp.wait() if wait else cp.start()

    def zero_tail(blk, wait):
        for part in range(2):
            cp = _token_copy(zeros_ref, xs_ref, 0, 0 if wait else blk * EXPERT_BLOCK + part * half, zsem, half)
            cp.wait() if wait else cp.start()

    @pl.when(pl.program_id(0) == 0)
    def _():
        zeros_ref[...] = jnp.zeros_like(zeros_ref)
        lax.fori_loop(0, N_EXPERTS, lambda e, c: (zero_fill(e, False), c)[1], 0)
        lax.fori_loop(n_used_ref[0], nb, lambda blk, c: (zero_tail(blk, False), c)[1], 0)

    base = pl.program_id(0) * tm

    def issue(t, c):
        for k in range(TOP_K):
            _token_copy(h_ref, xs_ref, base + t, pos_ref[k * tm + t], sem).start(priority=k % 2)
        return c

    lax.fori_loop(0, tm, issue, 0)

    def retire():
        for k in range(TOP_K):
            _token_copy(h_ref, xs_ref, 0, 0, sem, tm).wait()

    pl.when(pl.program_id(0) > 0)(retire)
    pl.when(pl.program_id(0) == pl.num_programs(0) - 1)(retire)

    @pl.when(pl.program_id(0) == 0)
    def _():
        lax.fori_loop(0, N_EXPERTS, lambda e, c: (zero_fill(e, True), c)[1], 0)
        lax.fori_loop(n_used_ref[0], nb, lambda blk, c: (zero_tail(blk, True), c)[1], 0)


def _tile_major(pos_t, tm):
    T = pos_t.shape[1]
    return pos_t.reshape(TOP_K, T // tm, tm).transpose(1, 0, 2).reshape(-1)


def dispatch(ht, pos_t, zstart, zcount, n_used, n_rows, tm=512):
    T = ht.shape[0] // SUBLANES
    tm = min(tm, T)
    grid_spec = pltpu.PrefetchScalarGridSpec(
        num_scalar_prefetch=3,
        grid=(T // tm,),
        in_specs=[pl.BlockSpec((TOP_K * tm,), lambda i, zs, zc, nu: (i,), memory_space=pltpu.SMEM),
                  pl.BlockSpec(memory_space=pl.ANY)],
        out_specs=pl.BlockSpec(memory_space=pl.ANY),
        scratch_shapes=[pltpu.VMEM((EXPERT_BLOCK // 2 * SUBLANES, LANES), F32), pltpu.SemaphoreType.DMA,
                        pltpu.SemaphoreType.DMA],
    )
    return pl.pallas_call(
        functools.partial(_dispatch_body, tm=tm, nb=n_rows // EXPERT_BLOCK),
        grid_spec=grid_spec,
        out_shape=jax.ShapeDtypeStruct((n_rows * SUBLANES, LANES), F32),
        compiler_params=_params(("arbitrary",)),
        name="dispatch",
    )(zstart, zcount, n_used, _tile_major(pos_t, tm), ht)


def _combine_body(pos_ref, pos_next_ref, gates_ref, h_ref, g2_ref, b2_ref, ys_ref, o_ref, buf, sems, *, tm):
    i = pl.program_id(0)

    def issue(p_ref, s):
        def body(t, c):
            for k in range(TOP_K):
                _token_copy(ys_ref, buf.at[s, k], p_ref[k * tm + t], t, sems.at[s]).start(priority=k % 2)
            return c

        lax.fori_loop(0, tm, body, 0)

    @pl.when(i == 0)
    def _():
        issue(pos_ref, 0)

    def step(slot):
        @pl.when(i + 1 < pl.num_programs(0))
        def _():
            issue(pos_next_ref, 1 - slot)

        for k in range(TOP_K):
            _token_copy(ys_ref, buf.at[slot, k], 0, 0, sems.at[slot], tm).wait()

        gates = gates_ref[...].T
        ffn = gates[:, 0:1] * _load_token_tiles(buf.at[slot, 0], tm)
        for k in range(1, TOP_K):
            ffn = ffn + gates[:, k:k + 1] * _load_token_tiles(buf.at[slot, k], tm)
        o_ref[...] = _layer_norm(DEEPNORM_ALPHA * h_ref[...] + ffn, g2_ref[...], b2_ref[...])

    for slot in range(2):
        pl.when(i % 2 == slot)(functools.partial(step, slot))


def combine(ys, pos_t, gates_t, h2, g2, b2, tm=256):
    T = h2.shape[0]
    tm = min(tm, T)
    last = T // tm - 1
    pos_flat = _tile_major(pos_t, tm)
    return pl.pallas_call(
        functools.partial(_combine_body, tm=tm),
        grid=(T // tm,),
        in_specs=[pl.BlockSpec((TOP_K * tm,), lambda i: (i,), memory_space=pltpu.SMEM),
                  pl.BlockSpec((TOP_K * tm,), lambda i: (jnp.minimum(i + 1, last),), memory_space=pltpu.SMEM),
                  pl.BlockSpec((SUBLANES, tm), lambda i: (0, i)),
                  pl.BlockSpec((tm, D_MODEL), lambda i: (i, 0)),
                  pl.BlockSpec((1, D_MODEL), lambda i: (0, 0)),
                  pl.BlockSpec((1, D_MODEL), lambda i: (0, 0)),
                  pl.BlockSpec(memory_space=pl.ANY)],
        out_specs=pl.BlockSpec((tm, D_MODEL), lambda i: (i, 0)),
        out_shape=jax.ShapeDtypeStruct((T, D_MODEL), F32),
        scratch_shapes=[pltpu.VMEM((2, TOP_K, tm * SUBLANES, LANES), F32), pltpu.SemaphoreType.DMA((2,))],
        compiler_params=_params(("arbitrary",)),
        name="combine",
    )(pos_flat, pos_flat, gates_t, h2, g2, b2, ys)


def _experts_body(blk_e, blk_src, n_used, next_e, wslot, xs_ref, wgu_hbm, bgu_ref, wdn_hbm, bdn_ref, ys_ref,
                  wgu_f, wdn_f, wgu_b, wdn_b, sems):
    del blk_src
    i = pl.program_id(0)
    used = i < n_used[0]
    e = blk_e[i]

    def weight_copies(expert, slot):
        return (pltpu.make_async_copy(wgu_hbm.at[expert], wgu_f.at[slot], sems.at[slot]),
                pltpu.make_async_copy(wdn_hbm.at[expert], wdn_f.at[slot], sems.at[slot]))

    @pl.when(i == 0)
    def _():
        for cp in weight_copies(e, 0):
            cp.start()

    @pl.when((i == 0) | (e != blk_e[jnp.maximum(i - 1, 0)]))
    def _():
        nxt = next_e[e]
        for slot in range(2):
            @pl.when(wslot[e] == slot)
            def _():
                for cp in weight_copies(e, slot):
                    cp.wait()
                wgu_b[...] = wgu_f[slot].astype(BF16)
                wdn_b[...] = wdn_f[slot].astype(BF16)

                @pl.when(nxt < N_EXPERTS)
                def _():
                    for cp in weight_copies(nxt, 1 - slot):
                        cp.start()

    @pl.when(used)
    def _():
        x = _load_token_tiles(xs_ref, EXPERT_BLOCK).astype(BF16)
        hgu = jnp.dot(x, wgu_b[...], preferred_element_type=F32) + bgu_ref[...]
        gate = jnp.minimum(hgu[:, :D_FF], SWIGLU_LIMIT)
        up = jnp.clip(hgu[:, D_FF:], -SWIGLU_LIMIT, SWIGLU_LIMIT)
        act = (up + 1.0) * gate * _sigmoid(SWIGLU_ALPHA * gate)
        y = jnp.dot(act.astype(BF16), wdn_b[...], preferred_element_type=F32) + bdn_ref[...]
        _store_token_tiles(ys_ref, y, EXPERT_BLOCK)

    @pl.when(jnp.logical_not(used))
    def _():
        ys_ref[...] = jnp.zeros_like(ys_ref)


def experts(xs, blk_e, blk_src, n_used, next_e, wslot, wgu, bgu, wdn, bdn):
    n_rows = xs.shape[0] // SUBLANES
    nb = n_rows // EXPERT_BLOCK
    rows = EXPERT_BLOCK * SUBLANES
    grid_spec = pltpu.PrefetchScalarGridSpec(
        num_scalar_prefetch=5,
        grid=(nb,),
        in_specs=[pl.BlockSpec((rows, LANES), lambda i, e, s, n, ne, ws: (s[i], 0)),
                  pl.BlockSpec(memory_space=pl.ANY),
                  pl.BlockSpec((None, 1, 2 * D_FF), lambda i, e, s, n, ne, ws: (e[i], 0, 0)),
                  pl.BlockSpec(memory_space=pl.ANY),
                  pl.BlockSpec((None, 1, D_MODEL), lambda i, e, s, n, ne, ws: (e[i], 0, 0))],
        out_specs=pl.BlockSpec((rows, LANES), lambda i, e, s, n, ne, ws: (i, 0)),
        scratch_shapes=[pltpu.VMEM((2, D_MODEL, 2 * D_FF), F32), pltpu.VMEM((2, D_FF, D_MODEL), F32),
                        pltpu.VMEM((D_MODEL, 2 * D_FF), BF16), pltpu.VMEM((D_FF, D_MODEL), BF16),
                        pltpu.SemaphoreType.DMA((2,))],
    )
    return pl.pallas_call(
        _experts_body,
        grid_spec=grid_spec,
        out_shape=jax.ShapeDtypeStruct((n_rows * SUBLANES, LANES), F32),
        compiler_params=_params(("arbitrary",)),
        name="experts",
    )(blk_e, blk_src, n_used, next_e, wslot, xs, wgu, bgu, wdn, bdn)


def routing_layout(ids, rank, counts, n_blocks):
    padded = (counts + EXPERT_BLOCK - 1) // EXPERT_BLOCK * EXPERT_BLOCK
    pad_ends = jnp.cumsum(padded)
    pad_starts = pad_ends - padded
    start_of = jnp.zeros_like(ids)
    for e in range(N_EXPERTS):
        start_of = jnp.where(ids == e, pad_starts[e], start_of)
    pos_t = (start_of + rank).astype(jnp.int32)
    n_used = jnp.maximum(pad_ends[-1] // EXPERT_BLOCK, 1).astype(jnp.int32)
    blk = jnp.minimum(jnp.arange(n_blocks, dtype=jnp.int32), n_used - 1)
    blk_e = jnp.sum(pad_ends[None, :] <= (blk * EXPERT_BLOCK)[:, None], axis=1)
    blk_e = jnp.minimum(blk_e, N_EXPERTS - 1).astype(jnp.int32)
    zstart = (pad_starts + counts).astype(jnp.int32)
    zcount = (padded - counts).astype(jnp.int32)
    nonempty = counts > 0
    expert_ids = jnp.arange(N_EXPERTS, dtype=jnp.int32)
    later = jnp.where(nonempty[None, :] & (expert_ids[None, :] > expert_ids[:, None]), expert_ids[None, :], N_EXPERTS)
    next_e = jnp.min(later, axis=1).astype(jnp.int32)
    wslot = ((jnp.cumsum(nonempty) - 1) % 2).astype(jnp.int32)
    return pos_t, blk_e, blk, n_used.reshape(1), zstart, zcount, next_e, wslot


DILATE_WIDTH = 4 * LANES


def _dilate_body(x_ref, *refs, n):
    o_refs, slab = refs[:-1], refs[-1]
    for c0 in range(0, DILATE_WIDTH, LANES):
        slab[...] = x_ref[:, c0:c0 + LANES]
        for (_, dil), o_ref in zip(ATT_GROUPS, o_refs):
            for r in range(dil):
                o_ref[r, :, c0:c0 + LANES] = slab[pl.ds(r, n // dil, stride=dil), :].astype(BF16)


def dilate_cast(x3, n=2048):
    B, S, D = x3.shape
    n = min(n, S)
    dils = [dil for _, dil in ATT_GROUPS]
    return pl.pallas_call(
        functools.partial(_dilate_body, n=n),
        grid=(B, S // n, D // DILATE_WIDTH),
        in_specs=[pl.BlockSpec((None, n, DILATE_WIDTH), lambda b, i, c: (b, i, c))],
        out_specs=[pl.BlockSpec((None, dil, n // dil, DILATE_WIDTH), lambda b, i, c: (b, 0, i, c)) for dil in dils],
        out_shape=[jax.ShapeDtypeStruct((B, dil, S // dil, D), BF16) for dil in dils],
        scratch_shapes=[pltpu.VMEM((n, LANES), F32)],
        compiler_params=_params(("arbitrary", "arbitrary", "arbitrary")),
        name="dilate_cast",
    )(x3)


def kernel(x, w_in, rel_bias, w_dw, b_dw, conv_ln_g, conv_ln_b, w_o_attn, w_o_conv, w_out, ln1_g, ln1_b,
           w_router, b_router, w_gate_up, b_gate_up, w_down, b_down, ln2_g, ln2_b):
    B, S, D = x.shape
    T = B * S
    h = x
    bm = bias_tables(rel_bias)
    q_off, k_off, v_off, rest = 0, QKV_WIDTH, 2 * QKV_WIDTH, 3 * QKV_WIDTH
    for l in range(DEPTH):
        wb = w_in[l].astype(BF16)

        def group_cols(gi):
            return [wb[:, off + gi * GROUP_WIDTH:off + (gi + 1) * GROUP_WIDTH] for off in (q_off, k_off, v_off)]

        x_dil = dilate_cast(h)
        xb = x_dil[0].reshape(T, D)
        w_main = jnp.concatenate(group_cols(0) + [wb[:, rest:]], axis=1)
        proj = in_proj(xb, w_main, w_main.shape[1] // 2, "in_proj_main").reshape(B, S, w_main.shape[1])
        o_list, st_list = [], []
        for gi in range(N_GROUPS):
            dil = ATT_GROUPS[gi][1]
            if dil == 1:
                qkv = proj.reshape(B, 1, S, proj.shape[-1])
            else:
                w_g = jnp.concatenate(group_cols(gi), axis=1)
                qkv = in_proj(x_dil[gi].reshape(T, D), w_g, w_g.shape[1], f"in_proj_g{gi}")
                qkv = qkv.reshape(B, dil, S // dil, w_g.shape[1])
            o_g, st_g = attention_group(qkv, bm[gi], gi)
            o_list.append(o_g)
            st_list.append(st_g)
        h1, h1_tiles = mixer_out(o_list, st_list, proj, h,
                       w_dw[l].reshape(CONV_WIDTH, CONV_CH), b_dw[l].reshape(1, CONV_CH),
                       conv_ln_g[l].reshape(1, CONV_CH), conv_ln_b[l].reshape(1, CONV_CH),
                       w_o_attn[l].astype(BF16), w_o_conv[l].astype(BF16), w_out[l].astype(BF16),
                       ln1_g[l].reshape(1, D), ln1_b[l].reshape(1, D))
        h2 = h1.reshape(T, D)
        gates_t, ids_t, rank_t, counts = router(h2, w_router[l], b_router[l])
        n_rows = T * TOP_K + N_EXPERTS * EXPERT_BLOCK
        pos_t, blk_e, blk_src, n_used, zstart, zcount, next_e, wslot = routing_layout(
            ids_t[:TOP_K], rank_t[:TOP_K], counts[:, 0], n_rows // EXPERT_BLOCK)
        xs = dispatch(h1_tiles, pos_t, zstart, zcount, n_used, n_rows)
        ys = experts(xs, blk_e, blk_src, n_used, next_e, wslot,
                     w_gate_up[l], b_gate_up[l].reshape(N_EXPERTS, 1, 2 * D_FF),
                     w_down[l], b_down[l].reshape(N_EXPERTS, 1, D))
        out = combine(ys, pos_t, gates_t, h2, ln2_g[l].reshape(1, D), ln2_b[l].reshape(1, D))
        h = out.reshape(B, S, D)
    return h
```

```python
import functools
import math

import jax
import jax.numpy as jnp
from jax import lax
from jax.experimental import pallas as pl
from jax.experimental.pallas import tpu as pltpu

D_MODEL = 1024
ATT_GROUPS = ((128, 1), (512, 4), (2048, 16))
N_GROUPS = len(ATT_GROUPS)
HEADS_PER_GROUP = 8
HEAD_DIM = 64
GROUP_WIDTH = HEADS_PER_GROUP * HEAD_DIM
QKV_WIDTH = N_GROUPS * GROUP_WIDTH
ATT_BLOCK = 128
N_BUCKETS = 32
MAX_DISTANCE = 2048
CONV_CH = 768
CONV_WIDTH = 31
CONV_HALO = 32
N_EXPERTS = 32
TOP_K = 4
D_FF = 1024
SWIGLU_LIMIT = 7.0
SWIGLU_ALPHA = 1.702
IN_WIDTH = 3 * QKV_WIDTH + 2 * CONV_CH + 2 * D_MODEL
LN_EPS = 1e-5
NEG_INF = -1e30
DEPTH = 1
DEEPNORM_ALPHA = (2 * DEPTH) ** 0.25

LANES = 128
SUBLANES = 8
EXPERT_BLOCK = 512
VMEM_LIMIT = 56 * 1024 * 1024

F32 = jnp.float32
BF16 = jnp.bfloat16


def _params(sem, vmem=VMEM_LIMIT):
    return pltpu.CompilerParams(dimension_semantics=sem, vmem_limit_bytes=vmem)


def _sigmoid(x):
    return 0.5 * jnp.tanh(0.5 * x) + 0.5


def _layer_norm(x, g, b):
    mu = jnp.mean(x, axis=-1, keepdims=True)
    xc = x - mu
    var = jnp.mean(xc * xc, axis=-1, keepdims=True)
    return xc * lax.rsqrt(var + LN_EPS) * g + b


def _in_proj_body(x_ref, w_ref, o_ref):
    o_ref[...] = jnp.dot(x_ref[...], w_ref[...], preferred_element_type=F32).astype(o_ref.dtype)


def in_proj(xb, w_b, tn, name, tm=1024):
    T = xb.shape[0]
    N = w_b.shape[1]
    tm = min(tm, T)
    return pl.pallas_call(
        _in_proj_body,
        grid=(N // tn, T // tm),
        in_specs=[pl.BlockSpec((tm, D_MODEL), lambda n, m: (m, 0)),
                  pl.BlockSpec((D_MODEL, tn), lambda n, m: (0, n))],
        out_specs=pl.BlockSpec((tm, tn), lambda n, m: (m, n)),
        out_shape=jax.ShapeDtypeStruct((T, N), BF16),
        compiler_params=_params(("arbitrary", "arbitrary")),
        name=name,
    )(xb, w_b)


def _t5_bucket(dist):
    max_exact = N_BUCKETS // 2
    log_ratio = jnp.log(jnp.maximum(dist, max_exact).astype(F32) / max_exact) / math.log(MAX_DISTANCE / max_exact)
    large = jnp.minimum(max_exact + (log_ratio * (N_BUCKETS - max_exact)).astype(jnp.int32), N_BUCKETS - 1)
    return jnp.where(dist < max_exact, dist, large)


def _bias_body(tbl_ref, bucket_ref, band_ref, o_ref):
    g = pl.program_id(0)
    h = pl.program_id(1)
    col = g * HEADS_PER_GROUP + h
    bucket = bucket_ref[...]
    acc = jnp.zeros(bucket.shape, F32)
    for k in range(N_BUCKETS):
        acc = jnp.where(bucket == k, tbl_ref[k, col], acc)
    band = band_ref[...] != 0
    kj = lax.broadcasted_iota(jnp.int32, bucket.shape, 1)
    o_ref[0] = jnp.where(band, acc, NEG_INF)
    o_ref[1] = jnp.where(band & (kj >= ATT_BLOCK), acc, NEG_INF)


def bias_tables(rel_bias):
    qi = jnp.arange(ATT_BLOCK)[:, None]
    kj = jnp.arange(2 * ATT_BLOCK)[None, :]
    dist = qi - kj + ATT_BLOCK
    buckets, bands = [], []
    for window, dil in ATT_GROUPS:
        bands.append(((dist >= 0) & (dist <= window // dil)).astype(jnp.int32))
        buckets.append(_t5_bucket(jnp.maximum(dist, 0) * dil).astype(jnp.int32))
    buckets = jnp.stack(buckets)
    bands = jnp.stack(bands)
    blk = (None, ATT_BLOCK, 2 * ATT_BLOCK)
    return pl.pallas_call(
        _bias_body,
        grid=(N_GROUPS, HEADS_PER_GROUP),
        in_specs=[pl.BlockSpec(memory_space=pltpu.SMEM),
                  pl.BlockSpec(blk, lambda g, h: (g, 0, 0)),
                  pl.BlockSpec(blk, lambda g, h: (g, 0, 0))],
        out_specs=pl.BlockSpec((None, None, 2, ATT_BLOCK, 2 * ATT_BLOCK), lambda g, h: (g, h, 0, 0, 0)),
        out_shape=jax.ShapeDtypeStruct((N_GROUPS, HEADS_PER_GROUP, 2, ATT_BLOCK, 2 * ATT_BLOCK), F32),
        compiler_params=_params(("arbitrary", "arbitrary")),
        name="bias_tables",
    )(rel_bias.astype(F32), buckets, bands)


def _attn_body(q_ref, kc_ref, kp_ref, vc_ref, vp_ref, bm_ref, o_ref, st_ref, k_all, v_all, s_buf, *, tq):
    i = pl.program_id(2)
    nsub = tq // ATT_BLOCK
    k_all[0:ATT_BLOCK] = kp_ref[...]
    k_all[ATT_BLOCK:] = kc_ref[...]
    v_all[0:ATT_BLOCK] = vp_ref[...]
    v_all[ATT_BLOCK:] = vc_ref[...]
    lo = lax.broadcasted_iota(jnp.int32, (ATT_BLOCK, LANES), 1) < HEAD_DIM
    nt_dims = (((1,), (1,)), ((), ()))
    first = jnp.where(i == 0, 1, 0)

    def scores(s, slot):
        r0 = s * ATT_BLOCK
        for j in range(HEADS_PER_GROUP // 2):
            cs = slice(j * LANES, (j + 1) * LANES)
            qp = q_ref[r0:r0 + ATT_BLOCK, cs] * jnp.asarray(HEAD_DIM ** -0.5, BF16)
            kp = k_all[r0:r0 + 2 * ATT_BLOCK, cs]
            for hh in range(2):
                h = 2 * j + hh
                qh = jnp.where(lo if hh == 0 else ~lo, qp, jnp.zeros_like(qp))
                bias = bm_ref[h, first] if s == 0 else bm_ref[h, 0]
                s_buf[slot, h] = lax.dot_general(qh, kp, nt_dims, preferred_element_type=F32) + bias

    def softmax_pv(s, slot):
        rows = slice(s * ATT_BLOCK, (s + 1) * ATT_BLOCK)
        st_ref[rows, :] = jnp.zeros((ATT_BLOCK, LANES), F32)
        for j in range(HEADS_PER_GROUP // 2):
            vp = v_all[s * ATT_BLOCK:(s + 2) * ATT_BLOCK, j * LANES:(j + 1) * LANES]
            for hh in range(2):
                h = 2 * j + hh
                sc = s_buf[slot, h]
                m = jnp.max(sc, axis=-1, keepdims=True)
                p = jnp.exp(sc - m)
                den = jnp.sum(p, axis=-1, keepdims=True)
                pv = jnp.dot(p.astype(BF16), vp, preferred_element_type=F32) * (1.0 / den)
                c0 = h * HEAD_DIM
                o_ref[rows, c0:c0 + HEAD_DIM] = pv[:, hh * HEAD_DIM:(hh + 1) * HEAD_DIM]
                st_ref[rows, h:h + 1] = m
                st_ref[rows, HEADS_PER_GROUP + h:HEADS_PER_GROUP + h + 1] = den

    scores(0, 0)
    for s in range(nsub):
        softmax_pv(s, s % 2)
        if s + 1 < nsub:
            scores(s + 1, (s + 1) % 2)


def attention_group(qkv, bm_g, gi, tq=512):
    B, dil, L, _ = qkv.shape
    tq = min(tq, L)
    sub = tq // ATT_BLOCK

    def cur(col):
        return pl.BlockSpec((None, None, tq, GROUP_WIDTH), lambda b, r, i: (b, r, i, col))

    def prev(col):
        return pl.BlockSpec((None, None, ATT_BLOCK, GROUP_WIDTH),
                            lambda b, r, i: (b, r, jnp.maximum(i * sub - 1, 0), col))

    return pl.pallas_call(
        functools.partial(_attn_body, tq=tq),
        grid=(B, dil, L // tq),
        in_specs=[cur(0), cur(1), prev(1), cur(2), prev(2),
                  pl.BlockSpec((HEADS_PER_GROUP, 2, ATT_BLOCK, 2 * ATT_BLOCK), lambda b, r, i: (0, 0, 0, 0))],
        out_specs=[pl.BlockSpec((None, None, tq, GROUP_WIDTH), lambda b, r, i: (b, r, i, 0)),
                   pl.BlockSpec((None, None, tq, LANES), lambda b, r, i: (b, r, i, 0))],
        out_shape=[jax.ShapeDtypeStruct((B, dil, L, GROUP_WIDTH), F32),
                   jax.ShapeDtypeStruct((B, dil, L, LANES), F32)],
        scratch_shapes=[pltpu.VMEM((ATT_BLOCK + tq, GROUP_WIDTH), BF16),
                        pltpu.VMEM((ATT_BLOCK + tq, GROUP_WIDTH), BF16),
                        pltpu.VMEM((2, HEADS_PER_GROUP, ATT_BLOCK, 2 * ATT_BLOCK), F32)],
        compiler_params=_params(("arbitrary", "arbitrary", "arbitrary")),
        name=f"attention_g{gi}",
    )(qkv, qkv, qkv, qkv, qkv, bm_g)


def _split_bf16(x):
    hi = x.astype(BF16)
    lo = (x - hi.astype(F32)).astype(BF16)
    return hi, lo


def _load_token_tiles(ref, n):
    return jnp.concatenate([ref[pl.ds(c, n, stride=SUBLANES), :] for c in range(D_MODEL // LANES)], axis=1)


def _store_token_tiles(ref, x, n):
    for c in range(D_MODEL // LANES):
        ref[pl.ds(c, n, stride=SUBLANES), :] = x[:, c * LANES:(c + 1) * LANES]


def _to_token_order(blk_ref, tok_ref, dil, tm):
    n = tm // dil
    for r in range(dil):
        for c in range(tok_ref.shape[0]):
            tok_ref[c, pl.ds(r, n, stride=dil), :] = blk_ref[r, :, c * LANES:(c + 1) * LANES]


def _mixer_out_body(o0, o1, o2, s0, s1, s2, uv_ref, ug_ref, uvh_ref, ugh_ref, ga_ref, gc_ref, x_ref,
                    wdw_ref, bdw_ref, cg_ref, cb_ref, woa_ref, woc_ref, wout_ref, g1_ref, b1_ref,
                    h_ref, ht_ref, glu_ref, dw_ref, shift_ref, tok_o1, tok_s1, tok_o2, tok_s2, *, tm, chunk):
    i = pl.program_id(1)

    ncol = GROUP_WIDTH // LANES
    _to_token_order(o1, tok_o1, ATT_GROUPS[1][1], tm)
    _to_token_order(s1, tok_s1, ATT_GROUPS[1][1], tm)
    _to_token_order(o2, tok_o2, ATT_GROUPS[2][1], tm)
    _to_token_order(s2, tok_s2, ATT_GROUPS[2][1], tm)
    outs = [o0[0],
            jnp.concatenate([tok_o1[c] for c in range(ncol)], axis=1),
            jnp.concatenate([tok_o2[c] for c in range(ncol)], axis=1)]
    sts = [s0[0], tok_s1[0], tok_s2[0]]

    mx = jnp.maximum(jnp.maximum(sts[0], sts[1]), sts[2])
    wts = [pltpu.roll(st, LANES - HEADS_PER_GROUP, axis=1) * jnp.exp(st - mx) for st in sts]
    wsum = wts[0] + wts[1] + wts[2]
    row = lax.broadcasted_iota(jnp.int32, (LANES, GROUP_WIDTH), 0)
    colh = lax.broadcasted_iota(jnp.int32, (LANES, GROUP_WIDTH), 1) // HEAD_DIM
    expand = (row == colh).astype(BF16)
    attn = jnp.zeros((tm, GROUP_WIDTH), F32)
    head_lane = lax.broadcasted_iota(jnp.int32, (tm, LANES), 1) < HEADS_PER_GROUP
    for wt, o in zip(wts, outs):
        c_hi, c_lo = _split_bf16(jnp.where(head_lane, wt / wsum, 0.0))
        c = (jnp.dot(c_hi, expand, preferred_element_type=F32)
             + jnp.dot(c_lo, expand, preferred_element_type=F32))
        attn = attn + c * o
    a_out = jnp.dot(attn.astype(BF16), woa_ref[...], preferred_element_type=F32)

    gh = uvh_ref[...].astype(F32) * _sigmoid(ugh_ref[...].astype(F32))
    glu_ref[0:CONV_HALO] = jnp.where(i == 0, 0.0, gh)
    glu_ref[CONV_HALO:] = uv_ref[...].astype(F32) * _sigmoid(ug_ref[...].astype(F32))
    first_tap = CONV_HALO - (CONV_WIDTH - 1)
    for b in range(1, SUBLANES):
        shift_ref[b - 1] = glu_ref[b:b + shift_ref.shape[1], :]

    for c0 in range(0, CONV_CH, LANES):
        cs = slice(c0, c0 + LANES)
        bias = jnp.broadcast_to(bdw_ref[:, cs], (chunk, LANES))
        for r0 in range(0, tm, chunk):
            acc = bias
            for j in range(CONV_WIDTH):
                a, b = divmod(first_tap + j, SUBLANES)
                lo_row = r0 + a * SUBLANES
                rows = glu_ref[lo_row:lo_row + chunk, cs] if b == 0 else shift_ref[b - 1, lo_row:lo_row + chunk, cs]
                acc = acc + wdw_ref[j:j + 1, cs] * rows
            dw_ref[r0:r0 + chunk, cs] = acc
    cn = _layer_norm(dw_ref[...], cg_ref[...], cb_ref[...])
    conv = cn * _sigmoid(cn)
    c_out = jnp.dot(conv.astype(BF16), woc_ref[...], preferred_element_type=F32)

    merged = (_sigmoid(ga_ref[...].astype(F32)) * a_out + _sigmoid(gc_ref[...].astype(F32)) * c_out)
    mix = jnp.dot(merged.astype(BF16), wout_ref[...], preferred_element_type=F32)
    h = _layer_norm(DEEPNORM_ALPHA * x_ref[...] + mix, g1_ref[...], b1_ref[...])
    h_ref[...] = h
    _store_token_tiles(ht_ref, h, tm)


def mixer_out(o_list, st_list, proj3, x3, w_dw, b_dw, cg, cb, woa_b, woc_b, wout_b, g1, b1, tm=512, chunk=64):
    B, S, _ = x3.shape
    tm = min(tm, S)
    halo_blocks = tm // CONV_HALO
    uv_col = QKV_WIDTH // CONV_CH
    ga_col = (QKV_WIDTH + 2 * CONV_CH) // D_MODEL
    ncol = GROUP_WIDTH // LANES

    def tile(width, col=0):
        return pl.BlockSpec((None, tm, width), lambda b, i: (b, i, col))

    def dilated(gi, width):
        dil = ATT_GROUPS[gi][1]
        return pl.BlockSpec((None, dil, tm // dil, width), lambda b, i: (b, 0, i, 0))

    def halo(col):
        return pl.BlockSpec((None, CONV_HALO, CONV_CH), lambda b, i: (b, jnp.maximum(i * halo_blocks - 1, 0), col))

    def whole(shape):
        return pl.BlockSpec(shape, lambda b, i: (0,) * len(shape))

    return pl.pallas_call(
        functools.partial(_mixer_out_body, tm=tm, chunk=chunk),
        grid=(B, S // tm),
        in_specs=[dilated(g, GROUP_WIDTH) for g in range(N_GROUPS)] + [dilated(g, LANES) for g in range(N_GROUPS)]
        + [tile(CONV_CH, uv_col), tile(CONV_CH, uv_col + 1), halo(uv_col), halo(uv_col + 1),
           tile(D_MODEL, ga_col), tile(D_MODEL, ga_col + 1), tile(D_MODEL),
           whole((CONV_WIDTH, CONV_CH)), whole((1, CONV_CH)), whole((1, CONV_CH)), whole((1, CONV_CH)),
           whole((GROUP_WIDTH, D_MODEL)), whole((CONV_CH, D_MODEL)), whole((D_MODEL, D_MODEL)),
           whole((1, D_MODEL)), whole((1, D_MODEL))],
        out_specs=[tile(D_MODEL), pl.BlockSpec((tm * SUBLANES, LANES), lambda b, i: (b * (S // tm) + i, 0))],
        out_shape=[jax.ShapeDtypeStruct((B, S, D_MODEL), F32),
                   jax.ShapeDtypeStruct((B * S * SUBLANES, LANES), F32)],
        scratch_shapes=[pltpu.VMEM((CONV_HALO + tm, CONV_CH), F32), pltpu.VMEM((tm, CONV_CH), F32),
                        pltpu.VMEM((SUBLANES - 1, tm + CONV_HALO - SUBLANES, CONV_CH), F32),
                        pltpu.VMEM((ncol, tm, LANES), F32), pltpu.VMEM((1, tm, LANES), F32),
                        pltpu.VMEM((ncol, tm, LANES), F32), pltpu.VMEM((1, tm, LANES), F32)],
        compiler_params=_params(("arbitrary", "arbitrary")),
        name="mixer_out",
    )(*o_list, *st_list, proj3, proj3, proj3, proj3, proj3, proj3, x3,
      w_dw, b_dw, cg, cb, woa_b, woc_b, wout_b, g1, b1)


def _router_body(h_ref, wt_ref, b_ref, gates_ref, ids_ref, rank_ref, cnt_ref, carry_ref, tri_ref, *, tm):
    step = pl.program_id(0)

    @pl.when(step == 0)
    def _():
        carry_ref[...] = jnp.zeros_like(carry_ref)
        r_i = lax.broadcasted_iota(jnp.int32, (tm, tm), 0)
        c_i = lax.broadcasted_iota(jnp.int32, (tm, tm), 1)
        tri_ref[...] = (r_i < c_i).astype(BF16)

    nt_dims = (((1,), (1,)), ((), ()))
    h_hi, h_lo = _split_bf16(h_ref[...])
    w_hi, w_lo = _split_bf16(wt_ref[...])
    logits = (lax.dot_general(w_hi, h_hi, nt_dims, preferred_element_type=F32)
              + (lax.dot_general(w_hi, h_lo, nt_dims, preferred_element_type=F32)
                 + lax.dot_general(w_lo, h_hi, nt_dims, preferred_element_type=F32))
              + b_ref[...])
    expert = lax.broadcasted_iota(jnp.int32, (N_EXPERTS, tm), 0).astype(F32)
    row = lax.broadcasted_iota(jnp.int32, (SUBLANES, tm), 0)

    work = logits
    vals, hots = [], []
    ids = jnp.zeros((SUBLANES, tm), F32)
    for k in range(TOP_K):
        v = jnp.max(work, axis=0, keepdims=True)
        idx = jnp.min(jnp.where(work == v, expert, float(N_EXPERTS)), axis=0, keepdims=True)
        hot = expert == idx
        work = jnp.where(hot, -jnp.inf, work)
        ids = jnp.where(row == k, idx, ids)
        vals.append(v)
        hots.append(hot)

    es = [jnp.exp(v - vals[0]) for v in vals]
    esum = es[0] + es[1] + es[2] + es[3]
    gates = jnp.zeros((SUBLANES, tm), F32)
    for k in range(TOP_K):
        gates = jnp.where(row == k, es[k] / esum, gates)

    sel = (hots[0] | hots[1] | hots[2] | hots[3])
    before = jnp.dot(sel.astype(BF16), tri_ref[...], preferred_element_type=F32) + carry_ref[:, 0:1]
    rank = jnp.zeros((SUBLANES, tm), F32)
    for k in range(TOP_K):
        rk = jnp.sum(jnp.where(hots[k], before, 0.0), axis=0, keepdims=True)
        rank = jnp.where(row == k, rk, rank)
    carry_ref[...] = carry_ref[...] + jnp.sum(sel.astype(F32), axis=1, keepdims=True)

    gates_ref[...] = gates
    ids_ref[...] = ids.astype(jnp.int32)
    rank_ref[...] = rank.astype(jnp.int32)
    cnt_ref[...] = carry_ref[...].astype(jnp.int32)


def router(h2, w_router, b_router, tm=512):
    T = h2.shape[0]
    tm = min(tm, T)
    tile = pl.BlockSpec((SUBLANES, tm), lambda i: (0, i))
    return pl.pallas_call(
        functools.partial(_router_body, tm=tm),
        grid=(T // tm,),
        in_specs=[pl.BlockSpec((tm, D_MODEL), lambda i: (i, 0)),
                  pl.BlockSpec((N_EXPERTS, D_MODEL), lambda i: (0, 0)),
                  pl.BlockSpec((N_EXPERTS, 1), lambda i: (0, 0))],
        out_specs=[tile, tile, tile, pl.BlockSpec((N_EXPERTS, LANES), lambda i: (0, 0))],
        out_shape=[jax.ShapeDtypeStruct((SUBLANES, T), F32), jax.ShapeDtypeStruct((SUBLANES, T), jnp.int32),
                   jax.ShapeDtypeStruct((SUBLANES, T), jnp.int32), jax.ShapeDtypeStruct((N_EXPERTS, LANES), jnp.int32)],
        scratch_shapes=[pltpu.VMEM((N_EXPERTS, LANES), F32), pltpu.VMEM((tm, tm), BF16)],
        compiler_params=_params(("arbitrary",)),
        name="router",
    )(h2, w_router.astype(F32).T, b_router.astype(F32).reshape(N_EXPERTS, 1))


def _token_copy(src, dst, s_tok, d_tok, sem, n=1):
    rows = n * SUBLANES
    return pltpu.make_async_copy(src.at[pl.ds(pl.multiple_of(s_tok * SUBLANES, SUBLANES), rows), :],
                                 dst.at[pl.ds(pl.multiple_of(d_tok * SUBLANES, SUBLANES), rows), :], sem)


def _dispatch_body(zstart_ref, zcount_ref, n_used_ref, pos_ref, h_ref, xs_ref, zeros_ref, ring, sems, zsem, *, tm, nb):
    half = EXPERT_BLOCK // 2
    bits = [1 << s for s in reversed(range(half.bit_length()))]

    def zero_fill(e, wait):
        start, count = zstart_ref[e], zcount_ref[e]
        for bit in bits:
            @pl.when((count & bit) != 0)
            def _():
                cp = _token_copy(zeros_ref, xs_ref, 0, 0 if wait else start + (count & ~(2 * bit - 1)), zsem, bit)
                cp.wait() if wait else cp.start()

    def zero_tail(blk, wait):
        for part in range(2):
            cp = _token_copy(zeros_ref, xs_ref, 0, 0 if wait else blk * EXPERT_BLOCK + part * half, zsem, half)
            cp.wait() if wait else cp.start()

    @pl.when(pl.program_id(0) == 0)
    def _():
        zeros_ref[...] = jnp.zeros_like(zeros_ref)
        lax.fori_loop(0, N_EXPERTS, lambda e, c: (zero_fill(e, False), c)[1], 0)
        lax.fori_loop(n_used_ref[0], nb, lambda blk, c: (zero_tail(blk, False), c)[1], 0)

    i = pl.program_id(0)
    last = pl.num_programs(0) - 1

    def retire(slot):
        for k in range(TOP_K):
            _token_copy(ring.at[slot], xs_ref, 0, 0, sems.at[slot], tm).wait()

    def step(slot):
        pl.when(i >= 2)(functools.partial(retire, slot))
        ring[slot] = h_ref[...]

        def issue(t, c):
            for k in range(TOP_K):
                _token_copy(ring.at[slot], xs_ref, t, pos_ref[k * tm + t], sems.at[slot]).start(priority=k % 2)
            return c

        lax.fori_loop(0, tm, issue, 0)

        @pl.when(i == last)
        def _():
            retire(slot)
            pl.when(i >= 1)(functools.partial(retire, 1 - slot))

    for slot in range(2):
        pl.when(i % 2 == slot)(functools.partial(step, slot))

    @pl.when(pl.program_id(0) == 0)
    def _():
        lax.fori_loop(0, N_EXPERTS, lambda e, c: (zero_fill(e, True), c)[1], 0)
        lax.fori_loop(n_used_ref[0], nb, lambda blk, c: (zero_tail(blk, True), c)[1], 0)


def _tile_major(pos_t, tm):
    T = pos_t.shape[1]
    return pos_t.reshape(TOP_K, T // tm, tm).transpose(1, 0, 2).reshape(-1)


def dispatch(ht, pos_t, zstart, zcount, n_used, n_rows, tm=512):
    T = ht.shape[0] // SUBLANES
    tm = min(tm, T)
    grid_spec = pltpu.PrefetchScalarGridSpec(
        num_scalar_prefetch=3,
        grid=(T // tm,),
        in_specs=[pl.BlockSpec((TOP_K * tm,), lambda i, zs, zc, nu: (i,), memory_space=pltpu.SMEM),
                  pl.BlockSpec((tm * SUBLANES, LANES), lambda i, zs, zc, nu: (i, 0))],
        out_specs=pl.BlockSpec(memory_space=pl.ANY),
        scratch_shapes=[pltpu.VMEM((EXPERT_BLOCK // 2 * SUBLANES, LANES), F32),
                        pltpu.VMEM((2, tm * SUBLANES, LANES), F32), pltpu.SemaphoreType.DMA((2,)),
                        pltpu.SemaphoreType.DMA],
    )
    return pl.pallas_call(
        functools.partial(_dispatch_body, tm=tm, nb=n_rows // EXPERT_BLOCK),
        grid_spec=grid_spec,
        out_shape=jax.ShapeDtypeStruct((n_rows * SUBLANES, LANES), F32),
        compiler_params=_params(("arbitrary",)),
        name="dispatch",
    )(zstart, zcount, n_used, _tile_major(pos_t, tm), ht)


def _combine_body(pos_ref, pos_next_ref, gates_ref, h_ref, g2_ref, b2_ref, ys_ref, o_ref, buf, sems, *, tm):
    i = pl.program_id(0)

    def issue(p_ref, s):
        def body(t, c):
            for k in range(TOP_K):
                _token_copy(ys_ref, buf.at[s, k], p_ref[k * tm + t], t, sems.at[s]).start(priority=k % 2)
            return c

        lax.fori_loop(0, tm, body, 0)

    @pl.when(i == 0)
    def _():
        issue(pos_ref, 0)

    def step(slot):
        @pl.when(i + 1 < pl.num_programs(0))
        def _():
            issue(pos_next_ref, 1 - slot)

        for k in range(TOP_K):
            _token_copy(ys_ref, buf.at[slot, k], 0, 0, sems.at[slot], tm).wait()

        gates = gates_ref[...].T
        ffn = gates[:, 0:1] * _load_token_tiles(buf.at[slot, 0], tm)
        for k in range(1, TOP_K):
            ffn = ffn + gates[:, k:k + 1] * _load_token_tiles(buf.at[slot, k], tm)
        o_ref[...] = _layer_norm(DEEPNORM_ALPHA * h_ref[...] + ffn, g2_ref[...], b2_ref[...])

    for slot in range(2):
        pl.when(i % 2 == slot)(functools.partial(step, slot))


def combine(ys, pos_t, gates_t, h2, g2, b2, tm=256):
    T = h2.shape[0]
    tm = min(tm, T)
    last = T // tm - 1
    pos_flat = _tile_major(pos_t, tm)
    return pl.pallas_call(
        functools.partial(_combine_body, tm=tm),
        grid=(T // tm,),
        in_specs=[pl.BlockSpec((TOP_K * tm,), lambda i: (i,), memory_space=pltpu.SMEM),
                  pl.BlockSpec((TOP_K * tm,), lambda i: (jnp.minimum(i + 1, last),), memory_space=pltpu.SMEM),
                  pl.BlockSpec((SUBLANES, tm), lambda i: (0, i)),
                  pl.BlockSpec((tm, D_MODEL), lambda i: (i, 0)),
                  pl.BlockSpec((1, D_MODEL), lambda i: (0, 0)),
                  pl.BlockSpec((1, D_MODEL), lambda i: (0, 0)),
                  pl.BlockSpec(memory_space=pl.ANY)],
        out_specs=pl.BlockSpec((tm, D_MODEL), lambda i: (i, 0)),
        out_shape=jax.ShapeDtypeStruct((T, D_MODEL), F32),
        scratch_shapes=[pltpu.VMEM((2, TOP_K, tm * SUBLANES, LANES), F32), pltpu.SemaphoreType.DMA((2,))],
        compiler_params=_params(("arbitrary",)),
        name="combine",
    )(pos_flat, pos_flat, gates_t, h2, g2, b2, ys)


def _experts_body(blk_e, blk_src, n_used, next_e, wslot, xs_ref, wgu_hbm, bgu_ref, wdn_hbm, bdn_ref, ys_ref,
                  wgu_f, wdn_f, wgu_b, wdn_b, sems):
    del blk_src
    i = pl.program_id(0)
    used = i < n_used[0]
    e = blk_e[i]

    def weight_copies(expert, slot):
        return (pltpu.make_async_copy(wgu_hbm.at[expert], wgu_f.at[slot], sems.at[slot]),
                pltpu.make_async_copy(wdn_hbm.at[expert], wdn_f.at[slot], sems.at[slot]))

    @pl.when(i == 0)
    def _():
        for cp in weight_copies(e, 0):
            cp.start()

    @pl.when((i == 0) | (e != blk_e[jnp.maximum(i - 1, 0)]))
    def _():
        nxt = next_e[e]
        for slot in range(2):
            @pl.when(wslot[e] == slot)
            def _():
                for cp in weight_copies(e, slot):
                    cp.wait()
                wgu_b[...] = wgu_f[slot].astype(BF16)
                wdn_b[...] = wdn_f[slot].astype(BF16)

                @pl.when(nxt < N_EXPERTS)
                def _():
                    for cp in weight_copies(nxt, 1 - slot):
                        cp.start()

    @pl.when(used)
    def _():
        x = _load_token_tiles(xs_ref, EXPERT_BLOCK).astype(BF16)
        hgu = jnp.dot(x, wgu_b[...], preferred_element_type=F32) + bgu_ref[...]
        gate = jnp.minimum(hgu[:, :D_FF], SWIGLU_LIMIT)
        up = jnp.clip(hgu[:, D_FF:], -SWIGLU_LIMIT, SWIGLU_LIMIT)
        act = (up + 1.0) * gate * _sigmoid(SWIGLU_ALPHA * gate)
        y = jnp.dot(act.astype(BF16), wdn_b[...], preferred_element_type=F32) + bdn_ref[...]
        _store_token_tiles(ys_ref, y, EXPERT_BLOCK)

    @pl.when(jnp.logical_not(used))
    def _():
        ys_ref[...] = jnp.zeros_like(ys_ref)


def experts(xs, blk_e, blk_src, n_used, next_e, wslot, wgu, bgu, wdn, bdn):
    n_rows = xs.shape[0] // SUBLANES
    nb = n_rows // EXPERT_BLOCK
    rows = EXPERT_BLOCK * SUBLANES
    grid_spec = pltpu.PrefetchScalarGridSpec(
        num_scalar_prefetch=5,
        grid=(nb,),
        in_specs=[pl.BlockSpec((rows, LANES), lambda i, e, s, n, ne, ws: (s[i], 0)),
                  pl.BlockSpec(memory_space=pl.ANY),
                  pl.BlockSpec((None, 1, 2 * D_FF), lambda i, e, s, n, ne, ws: (e[i], 0, 0)),
                  pl.BlockSpec(memory_space=pl.ANY),
                  pl.BlockSpec((None, 1, D_MODEL), lambda i, e, s, n, ne, ws: (e[i], 0, 0))],
        out_specs=pl.BlockSpec((rows, LANES), lambda i, e, s, n, ne, ws: (i, 0)),
        scratch_shapes=[pltpu.VMEM((2, D_MODEL, 2 * D_FF), F32), pltpu.VMEM((2, D_FF, D_MODEL), F32),
                        pltpu.VMEM((D_MODEL, 2 * D_FF), BF16), pltpu.VMEM((D_FF, D_MODEL), BF16),
                        pltpu.SemaphoreType.DMA((2,))],
    )
    return pl.pallas_call(
        _experts_body,
        grid_spec=grid_spec,
        out_shape=jax.ShapeDtypeStruct((n_rows * SUBLANES, LANES), F32),
        compiler_params=_params(("arbitrary",)),
        name="experts",
    )(blk_e, blk_src, n_used, next_e, wslot, xs, wgu, bgu, wdn, bdn)


def routing_layout(ids, rank, counts, n_blocks):
    padded = (counts + EXPERT_BLOCK - 1) // EXPERT_BLOCK * EXPERT_BLOCK
    pad_ends = jnp.cumsum(padded)
    pad_starts = pad_ends - padded
    start_of = jnp.zeros_like(ids)
    for e in range(N_EXPERTS):
        start_of = jnp.where(ids == e, pad_starts[e], start_of)
    pos_t = (start_of + rank).astype(jnp.int32)
    n_used = jnp.maximum(pad_ends[-1] // EXPERT_BLOCK, 1).astype(jnp.int32)
    blk = jnp.minimum(jnp.arange(n_blocks, dtype=jnp.int32), n_used - 1)
    blk_e = jnp.sum(pad_ends[None, :] <= (blk * EXPERT_BLOCK)[:, None], axis=1)
    blk_e = jnp.minimum(blk_e, N_EXPERTS - 1).astype(jnp.int32)
    zstart = (pad_starts + counts).astype(jnp.int32)
    zcount = (padded - counts).astype(jnp.int32)
    nonempty = counts > 0
    expert_ids = jnp.arange(N_EXPERTS, dtype=jnp.int32)
    later = jnp.where(nonempty[None, :] & (expert_ids[None, :] > expert_ids[:, None]), expert_ids[None, :], N_EXPERTS)
    next_e = jnp.min(later, axis=1).astype(jnp.int32)
    wslot = ((jnp.cumsum(nonempty) - 1) % 2).astype(jnp.int32)
    return pos_t, blk_e, blk, n_used.reshape(1), zstart, zcount, next_e, wslot


DILATE_WIDTH = 4 * LANES


def _dilate_body(x_ref, *refs, n):
    o_refs, slab = refs[:-1], refs[-1]
    for c0 in range(0, DILATE_WIDTH, LANES):
        slab[...] = x_ref[:, c0:c0 + LANES]
        for (_, dil), o_ref in zip(ATT_GROUPS, o_refs):
            for r in range(dil):
                o_ref[r, :, c0:c0 + LANES] = slab[pl.ds(r, n // dil, stride=dil), :].astype(BF16)


def dilate_cast(x3, n=2048):
    B, S, D = x3.shape
    n = min(n, S)
    dils = [dil for _, dil in ATT_GROUPS]
    return pl.pallas_call(
        functools.partial(_dilate_body, n=n),
        grid=(B, S // n, D // DILATE_WIDTH),
        in_specs=[pl.BlockSpec((None, n, DILATE_WIDTH), lambda b, i, c: (b, i, c))],
        out_specs=[pl.BlockSpec((None, dil, n // dil, DILATE_WIDTH), lambda b, i, c: (b, 0, i, c)) for dil in dils],
        out_shape=[jax.ShapeDtypeStruct((B, dil, S // dil, D), BF16) for dil in dils],
        scratch_shapes=[pltpu.VMEM((n, LANES), F32)],
        compiler_params=_params(("arbitrary", "arbitrary", "arbitrary")),
        name="dilate_cast",
    )(x3)


def kernel(x, w_in, rel_bias, w_dw, b_dw, conv_ln_g, conv_ln_b, w_o_attn, w_o_conv, w_out, ln1_g, ln1_b,
           w_router, b_router, w_gate_up, b_gate_up, w_down, b_down, ln2_g, ln2_b):
    B, S, D = x.shape
    T = B * S
    h = x
    bm = bias_tables(rel_bias)
    q_off, k_off, v_off, rest = 0, QKV_WIDTH, 2 * QKV_WIDTH, 3 * QKV_WIDTH
    for l in range(DEPTH):
        wb = w_in[l].astype(BF16)

        def group_cols(gi):
            return [wb[:, off + gi * GROUP_WIDTH:off + (gi + 1) * GROUP_WIDTH] for off in (q_off, k_off, v_off)]

        x_dil = dilate_cast(h)
        xb = x_dil[0].reshape(T, D)
        w_main = jnp.concatenate(group_cols(0) + [wb[:, rest:]], axis=1)
        proj = in_proj(xb, w_main, w_main.shape[1] // 2, "in_proj_main").reshape(B, S, w_main.shape[1])
        o_list, st_list = [], []
        for gi in range(N_GROUPS):
            dil = ATT_GROUPS[gi][1]
            if dil == 1:
                qkv = proj.reshape(B, 1, S, proj.shape[-1])
            else:
                w_g = jnp.concatenate(group_cols(gi), axis=1)
                qkv = in_proj(x_dil[gi].reshape(T, D), w_g, w_g.shape[1], f"in_proj_g{gi}")
                qkv = qkv.reshape(B, dil, S // dil, w_g.shape[1])
            o_g, st_g = attention_group(qkv, bm[gi], gi)
            o_list.append(o_g)
            st_list.append(st_g)
        h1, h1_tiles = mixer_out(o_list, st_list, proj, h,
                       w_dw[l].reshape(CONV_WIDTH, CONV_CH), b_dw[l].reshape(1, CONV_CH),
                       conv_ln_g[l].reshape(1, CONV_CH), conv_ln_b[l].reshape(1, CONV_CH),
                       w_o_attn[l].astype(BF16), w_o_conv[l].astype(BF16), w_out[l].astype(BF16),
                       ln1_g[l].reshape(1, D), ln1_b[l].reshape(1, D))
        h2 = h1.reshape(T, D)
        gates_t, ids_t, rank_t, counts = router(h2, w_router[l], b_router[l])
        n_rows = T * TOP_K + N_EXPERTS * EXPERT_BLOCK
        pos_t, blk_e, blk_src, n_used, zstart, zcount, next_e, wslot = routing_layout(
            ids_t[:TOP_K], rank_t[:TOP_K], counts[:, 0], n_rows // EXPERT_BLOCK)
        xs = dispatch(h1_tiles, pos_t, zstart, zcount, n_used, n_rows)
        ys = experts(xs, blk_e, blk_src, n_used, next_e, wslot,
                     w_gate_up[l], b_gate_up[l].reshape(N_EXPERTS, 1, 2 * D_FF),
                     w_down[l], b_down[l].reshape(N_EXPERTS, 1, D))
        out = combine(ys, pos_t, gates_t, h2, ln2_g[l].reshape(1, D), ln2_b[l].reshape(1, D))
        h = out.reshape(B, S, D)
    return h
```

```python
import functools
import math

import jax
import jax.numpy as jnp
from jax import lax
from jax.experimental import pallas as pl
from jax.experimental.pallas import tpu as pltpu

D_MODEL = 1024
ATT_GROUPS = ((128, 1), (512, 4), (2048, 16))
N_GROUPS = len(ATT_GROUPS)
HEADS_PER_GROUP = 8
HEAD_DIM = 64
GROUP_WIDTH = HEADS_PER_GROUP * HEAD_DIM
QKV_WIDTH = N_GROUPS * GROUP_WIDTH
ATT_BLOCK = 128
N_BUCKETS = 32
MAX_DISTANCE = 2048
CONV_CH = 768
CONV_WIDTH = 31
CONV_HALO = 32
N_EXPERTS = 32
TOP_K = 4
D_FF = 1024
SWIGLU_LIMIT = 7.0
SWIGLU_ALPHA = 1.702
IN_WIDTH = 3 * QKV_WIDTH + 2 * CONV_CH + 2 * D_MODEL
LN_EPS = 1e-5
NEG_INF = -1e30
DEPTH = 1
DEEPNORM_ALPHA = (2 * DEPTH) ** 0.25

LANES = 128
SUBLANES = 8
EXPERT_BLOCK = 512
VMEM_LIMIT = 56 * 1024 * 1024

F32 = jnp.float32
BF16 = jnp.bfloat16


def _params(sem, vmem=VMEM_LIMIT):
    return pltpu.CompilerParams(dimension_semantics=sem, vmem_limit_bytes=vmem)


def _sigmoid(x):
    return 0.5 * jnp.tanh(0.5 * x) + 0.5


def _layer_norm(x, g, b):
    mu = jnp.mean(x, axis=-1, keepdims=True)
    xc = x - mu
    var = jnp.mean(xc * xc, axis=-1, keepdims=True)
    return xc * lax.rsqrt(var + LN_EPS) * g + b


def _in_proj_body(x_ref, w_ref, o_ref):
    o_ref[...] = jnp.dot(x_ref[...], w_ref[...], preferred_element_type=F32).astype(o_ref.dtype)


def in_proj(xb, w_b, tn, name, tm=1024):
    T = xb.shape[0]
    N = w_b.shape[1]
    tm = min(tm, T)
    return pl.pallas_call(
        _in_proj_body,
        grid=(N // tn, T // tm),
        in_specs=[pl.BlockSpec((tm, D_MODEL), lambda n, m: (m, 0)),
                  pl.BlockSpec((D_MODEL, tn), lambda n, m: (0, n))],
        out_specs=pl.BlockSpec((tm, tn), lambda n, m: (m, n)),
        out_shape=jax.ShapeDtypeStruct((T, N), BF16),
        compiler_params=_params(("arbitrary", "arbitrary")),
        name=name,
    )(xb, w_b)


def _t5_bucket(dist):
    max_exact = N_BUCKETS // 2
    log_ratio = jnp.log(jnp.maximum(dist, max_exact).astype(F32) / max_exact) / math.log(MAX_DISTANCE / max_exact)
    large = jnp.minimum(max_exact + (log_ratio * (N_BUCKETS - max_exact)).astype(jnp.int32), N_BUCKETS - 1)
    return jnp.where(dist < max_exact, dist, large)


def _bias_body(tbl_ref, bucket_ref, band_ref, o_ref):
    g = pl.program_id(0)
    h = pl.program_id(1)
    col = g * HEADS_PER_GROUP + h
    bucket = bucket_ref[...]
    acc = jnp.zeros(bucket.shape, F32)
    for k in range(N_BUCKETS):
        acc = jnp.where(bucket == k, tbl_ref[k, col], acc)
    band = band_ref[...] != 0
    kj = lax.broadcasted_iota(jnp.int32, bucket.shape, 1)
    o_ref[0] = jnp.where(band, acc, NEG_INF)
    o_ref[1] = jnp.where(band & (kj >= ATT_BLOCK), acc, NEG_INF)


def bias_tables(rel_bias):
    qi = jnp.arange(ATT_BLOCK)[:, None]
    kj = jnp.arange(2 * ATT_BLOCK)[None, :]
    dist = qi - kj + ATT_BLOCK
    buckets, bands = [], []
    for window, dil in ATT_GROUPS:
        bands.append(((dist >= 0) & (dist <= window // dil)).astype(jnp.int32))
        buckets.append(_t5_bucket(jnp.maximum(dist, 0) * dil).astype(jnp.int32))
    buckets = jnp.stack(buckets)
    bands = jnp.stack(bands)
    blk = (None, ATT_BLOCK, 2 * ATT_BLOCK)
    return pl.pallas_call(
        _bias_body,
        grid=(N_GROUPS, HEADS_PER_GROUP),
        in_specs=[pl.BlockSpec(memory_space=pltpu.SMEM),
                  pl.BlockSpec(blk, lambda g, h: (g, 0, 0)),
                  pl.BlockSpec(blk, lambda g, h: (g, 0, 0))],
        out_specs=pl.BlockSpec((None, None, 2, ATT_BLOCK, 2 * ATT_BLOCK), lambda g, h: (g, h, 0, 0, 0)),
        out_shape=jax.ShapeDtypeStruct((N_GROUPS, HEADS_PER_GROUP, 2, ATT_BLOCK, 2 * ATT_BLOCK), F32),
        compiler_params=_params(("arbitrary", "arbitrary")),
        name="bias_tables",
    )(rel_bias.astype(F32), buckets, bands)


def _attn_body(q_ref, kc_ref, kp_ref, vc_ref, vp_ref, bm_ref, o_ref, st_ref, k_all, v_all, s_buf, *, tq):
    i = pl.program_id(2)
    nsub = tq // ATT_BLOCK
    k_all[0:ATT_BLOCK] = kp_ref[...]
    k_all[ATT_BLOCK:] = kc_ref[...]
    v_all[0:ATT_BLOCK] = vp_ref[...]
    v_all[ATT_BLOCK:] = vc_ref[...]
    lo = lax.broadcasted_iota(jnp.int32, (ATT_BLOCK, LANES), 1) < HEAD_DIM
    nt_dims = (((1,), (1,)), ((), ()))
    first = jnp.where(i == 0, 1, 0)

    def scores(s, slot):
        r0 = s * ATT_BLOCK
        for j in range(HEADS_PER_GROUP // 2):
            cs = slice(j * LANES, (j + 1) * LANES)
            qp = q_ref[r0:r0 + ATT_BLOCK, cs] * jnp.asarray(HEAD_DIM ** -0.5, BF16)
            kp = k_all[r0:r0 + 2 * ATT_BLOCK, cs]
            for hh in range(2):
                h = 2 * j + hh
                qh = jnp.where(lo if hh == 0 else ~lo, qp, jnp.zeros_like(qp))
                bias = bm_ref[h, first] if s == 0 else bm_ref[h, 0]
                s_buf[slot, h] = lax.dot_general(qh, kp, nt_dims, preferred_element_type=F32) + bias

    def softmax_pv(s, slot):
        rows = slice(s * ATT_BLOCK, (s + 1) * ATT_BLOCK)
        st_ref[rows, :] = jnp.zeros((ATT_BLOCK, LANES), F32)
        for j in range(HEADS_PER_GROUP // 2):
            vp = v_all[s * ATT_BLOCK:(s + 2) * ATT_BLOCK, j * LANES:(j + 1) * LANES]
            for hh in range(2):
                h = 2 * j + hh
                sc = s_buf[slot, h]
                m = jnp.max(sc, axis=-1, keepdims=True)
                p = jnp.exp(sc - m)
                den = jnp.sum(p, axis=-1, keepdims=True)
                pv = jnp.dot(p.astype(BF16), vp, preferred_element_type=F32) * (1.0 / den)
                c0 = h * HEAD_DIM
                o_ref[rows, c0:c0 + HEAD_DIM] = pv[:, hh * HEAD_DIM:(hh + 1) * HEAD_DIM]
                st_ref[rows, h:h + 1] = m
                st_ref[rows, HEADS_PER_GROUP + h:HEADS_PER_GROUP + h + 1] = den

    scores(0, 0)
    for s in range(nsub):
        softmax_pv(s, s % 2)
        if s + 1 < nsub:
            scores(s + 1, (s + 1) % 2)


def attention_group(qkv, bm_g, gi, tq=512):
    B, dil, L, _ = qkv.shape
    tq = min(tq, L)
    sub = tq // ATT_BLOCK

    def cur(col):
        return pl.BlockSpec((None, None, tq, GROUP_WIDTH), lambda b, r, i: (b, r, i, col))

    def prev(col):
        return pl.BlockSpec((None, None, ATT_BLOCK, GROUP_WIDTH),
                            lambda b, r, i: (b, r, jnp.maximum(i * sub - 1, 0), col))

    return pl.pallas_call(
        functools.partial(_attn_body, tq=tq),
        grid=(B, dil, L // tq),
        in_specs=[cur(0), cur(1), prev(1), cur(2), prev(2),
                  pl.BlockSpec((HEADS_PER_GROUP, 2, ATT_BLOCK, 2 * ATT_BLOCK), lambda b, r, i: (0, 0, 0, 0))],
        out_specs=[pl.BlockSpec((None, None, tq, GROUP_WIDTH), lambda b, r, i: (b, r, i, 0)),
                   pl.BlockSpec((None, None, tq, LANES), lambda b, r, i: (b, r, i, 0))],
        out_shape=[jax.ShapeDtypeStruct((B, dil, L, GROUP_WIDTH), F32),
                   jax.ShapeDtypeStruct((B, dil, L, LANES), F32)],
        scratch_shapes=[pltpu.VMEM((ATT_BLOCK + tq, GROUP_WIDTH), BF16),
                        pltpu.VMEM((ATT_BLOCK + tq, GROUP_WIDTH), BF16),
                        pltpu.VMEM((2, HEADS_PER_GROUP, ATT_BLOCK, 2 * ATT_BLOCK), F32)],
        compiler_params=_params(("arbitrary", "arbitrary", "arbitrary")),
        name=f"attention_g{gi}",
    )(qkv, qkv, qkv, qkv, qkv, bm_g)


def _split_bf16(x):
    hi = x.astype(BF16)
    lo = (x - hi.astype(F32)).astype(BF16)
    return hi, lo


def _load_token_tiles(ref, n):
    return jnp.concatenate([ref[pl.ds(c, n, stride=SUBLANES), :] for c in range(D_MODEL // LANES)], axis=1)


def _store_token_tiles(ref, x, n):
    for c in range(D_MODEL // LANES):
        ref[pl.ds(c, n, stride=SUBLANES), :] = x[:, c * LANES:(c + 1) * LANES]


def _to_token_order(blk_ref, tok_ref, dil, tm):
    n = tm // dil
    for r in range(dil):
        for c in range(tok_ref.shape[0]):
            tok_ref[c, pl.ds(r, n, stride=dil), :] = blk_ref[r, :, c * LANES:(c + 1) * LANES]


def _mixer_out_body(o0, o1, o2, s0, s1, s2, uv_ref, ug_ref, uvh_ref, ugh_ref, ga_ref, gc_ref, x_ref,
                    wdw_ref, bdw_ref, cg_ref, cb_ref, woa_ref, woc_ref, wout_ref, g1_ref, b1_ref,
                    h_ref, ht_ref, glu_ref, dw_ref, shift_ref, tok_o1, tok_s1, tok_o2, tok_s2, *, tm, chunk):
    i = pl.program_id(1)

    ncol = GROUP_WIDTH // LANES
    _to_token_order(o1, tok_o1, ATT_GROUPS[1][1], tm)
    _to_token_order(s1, tok_s1, ATT_GROUPS[1][1], tm)
    _to_token_order(o2, tok_o2, ATT_GROUPS[2][1], tm)
    _to_token_order(s2, tok_s2, ATT_GROUPS[2][1], tm)
    outs = [o0[0],
            jnp.concatenate([tok_o1[c] for c in range(ncol)], axis=1),
            jnp.concatenate([tok_o2[c] for c in range(ncol)], axis=1)]
    sts = [s0[0], tok_s1[0], tok_s2[0]]

    mx = jnp.maximum(jnp.maximum(sts[0], sts[1]), sts[2])
    wts = [pltpu.roll(st, LANES - HEADS_PER_GROUP, axis=1) * jnp.exp(st - mx) for st in sts]
    wsum = wts[0] + wts[1] + wts[2]
    row = lax.broadcasted_iota(jnp.int32, (LANES, GROUP_WIDTH), 0)
    colh = lax.broadcasted_iota(jnp.int32, (LANES, GROUP_WIDTH), 1) // HEAD_DIM
    expand = (row == colh).astype(BF16)
    attn = jnp.zeros((tm, GROUP_WIDTH), F32)
    head_lane = lax.broadcasted_iota(jnp.int32, (tm, LANES), 1) < HEADS_PER_GROUP
    for wt, o in zip(wts, outs):
        c_hi, c_lo = _split_bf16(jnp.where(head_lane, wt / wsum, 0.0))
        c = (jnp.dot(c_hi, expand, preferred_element_type=F32)
             + jnp.dot(c_lo, expand, preferred_element_type=F32))
        attn = attn + c * o
    a_out = jnp.dot(attn.astype(BF16), woa_ref[...], preferred_element_type=F32)

    gh = uvh_ref[...].astype(F32) * _sigmoid(ugh_ref[...].astype(F32))
    glu_ref[0:CONV_HALO] = jnp.where(i == 0, 0.0, gh)
    glu_ref[CONV_HALO:] = uv_ref[...].astype(F32) * _sigmoid(ug_ref[...].astype(F32))
    first_tap = CONV_HALO - (CONV_WIDTH - 1)
    for b in range(1, SUBLANES):
        shift_ref[b - 1] = glu_ref[b:b + shift_ref.shape[1], :]

    for c0 in range(0, CONV_CH, LANES):
        cs = slice(c0, c0 + LANES)
        bias = jnp.broadcast_to(bdw_ref[:, cs], (chunk, LANES))
        for r0 in range(0, tm, chunk):
            acc = bias
            for j in range(CONV_WIDTH):
                a, b = divmod(first_tap + j, SUBLANES)
                lo_row = r0 + a * SUBLANES
                rows = glu_ref[lo_row:lo_row + chunk, cs] if b == 0 else shift_ref[b - 1, lo_row:lo_row + chunk, cs]
                acc = acc + wdw_ref[j:j + 1, cs] * rows
            dw_ref[r0:r0 + chunk, cs] = acc
    cn = _layer_norm(dw_ref[...], cg_ref[...], cb_ref[...])
    conv = cn * _sigmoid(cn)
    c_out = jnp.dot(conv.astype(BF16), woc_ref[...], preferred_element_type=F32)

    merged = (_sigmoid(ga_ref[...].astype(F32)) * a_out + _sigmoid(gc_ref[...].astype(F32)) * c_out)
    mix = jnp.dot(merged.astype(BF16), wout_ref[...], preferred_element_type=F32)
    h = _layer_norm(DEEPNORM_ALPHA * x_ref[...] + mix, g1_ref[...], b1_ref[...])
    h_ref[...] = h
    _store_token_tiles(ht_ref, h, tm)


def mixer_out(o_list, st_list, proj3, x3, w_dw, b_dw, cg, cb, woa_b, woc_b, wout_b, g1, b1, tm=512, chunk=64):
    B, S, _ = x3.shape
    tm = min(tm, S)
    halo_blocks = tm // CONV_HALO
    uv_col = QKV_WIDTH // CONV_CH
    ga_col = (QKV_WIDTH + 2 * CONV_CH) // D_MODEL
    ncol = GROUP_WIDTH // LANES

    def tile(width, col=0):
        return pl.BlockSpec((None, tm, width), lambda b, i: (b, i, col))

    def dilated(gi, width):
        dil = ATT_GROUPS[gi][1]
        return pl.BlockSpec((None, dil, tm // dil, width), lambda b, i: (b, 0, i, 0))

    def halo(col):
        return pl.BlockSpec((None, CONV_HALO, CONV_CH), lambda b, i: (b, jnp.maximum(i * halo_blocks - 1, 0), col))

    def whole(shape):
        return pl.BlockSpec(shape, lambda b, i: (0,) * len(shape))

    return pl.pallas_call(
        functools.partial(_mixer_out_body, tm=tm, chunk=chunk),
        grid=(B, S // tm),
        in_specs=[dilated(g, GROUP_WIDTH) for g in range(N_GROUPS)] + [dilated(g, LANES) for g in range(N_GROUPS)]
        + [tile(CONV_CH, uv_col), tile(CONV_CH, uv_col + 1), halo(uv_col), halo(uv_col + 1),
           tile(D_MODEL, ga_col), tile(D_MODEL, ga_col + 1), tile(D_MODEL),
           whole((CONV_WIDTH, CONV_CH)), whole((1, CONV_CH)), whole((1, CONV_CH)), whole((1, CONV_CH)),
           whole((GROUP_WIDTH, D_MODEL)), whole((CONV_CH, D_MODEL)), whole((D_MODEL, D_MODEL)),
           whole((1, D_MODEL)), whole((1, D_MODEL))],
        out_specs=[tile(D_MODEL), pl.BlockSpec((tm * SUBLANES, LANES), lambda b, i: (b * (S // tm) + i, 0))],
        out_shape=[jax.ShapeDtypeStruct((B, S, D_MODEL), F32),
                   jax.ShapeDtypeStruct((B * S * SUBLANES, LANES), F32)],
        scratch_shapes=[pltpu.VMEM((CONV_HALO + tm, CONV_CH), F32), pltpu.VMEM((tm, CONV_CH), F32),
                        pltpu.VMEM((SUBLANES - 1, tm + CONV_HALO - SUBLANES, CONV_CH), F32),
                        pltpu.VMEM((ncol, tm, LANES), F32), pltpu.VMEM((1, tm, LANES), F32),
                        pltpu.VMEM((ncol, tm, LANES), F32), pltpu.VMEM((1, tm, LANES), F32)],
        compiler_params=_params(("arbitrary", "arbitrary")),
        name="mixer_out",
    )(*o_list, *st_list, proj3, proj3, proj3, proj3, proj3, proj3, x3,
      w_dw, b_dw, cg, cb, woa_b, woc_b, wout_b, g1, b1)


def _router_body(h_ref, wt_ref, b_ref, gates_ref, ids_ref, rank_ref, cnt_ref, carry_ref, tri_ref, *, tm):
    step = pl.program_id(0)

    @pl.when(step == 0)
    def _():
        carry_ref[...] = jnp.zeros_like(carry_ref)
        r_i = lax.broadcasted_iota(jnp.int32, (tm, tm), 0)
        c_i = lax.broadcasted_iota(jnp.int32, (tm, tm), 1)
        tri_ref[...] = (r_i < c_i).astype(BF16)

    nt_dims = (((1,), (1,)), ((), ()))
    h_hi, h_lo = _split_bf16(h_ref[...])
    w_hi, w_lo = _split_bf16(wt_ref[...])
    logits = (lax.dot_general(w_hi, h_hi, nt_dims, preferred_element_type=F32)
              + (lax.dot_general(w_hi, h_lo, nt_dims, preferred_element_type=F32)
                 + lax.dot_general(w_lo, h_hi, nt_dims, preferred_element_type=F32))
              + b_ref[...])
    expert = lax.broadcasted_iota(jnp.int32, (N_EXPERTS, tm), 0).astype(F32)
    row = lax.broadcasted_iota(jnp.int32, (SUBLANES, tm), 0)

    work = logits
    vals, hots = [], []
    ids = jnp.zeros((SUBLANES, tm), F32)
    for k in range(TOP_K):
        v = jnp.max(work, axis=0, keepdims=True)
        idx = jnp.min(jnp.where(work == v, expert, float(N_EXPERTS)), axis=0, keepdims=True)
        hot = expert == idx
        work = jnp.where(hot, -jnp.inf, work)
        ids = jnp.where(row == k, idx, ids)
        vals.append(v)
        hots.append(hot)

    es = [jnp.exp(v - vals[0]) for v in vals]
    esum = es[0] + es[1] + es[2] + es[3]
    gates = jnp.zeros((SUBLANES, tm), F32)
    for k in range(TOP_K):
        gates = jnp.where(row == k, es[k] / esum, gates)

    sel = (hots[0] | hots[1] | hots[2] | hots[3])
    before = jnp.dot(sel.astype(BF16), tri_ref[...], preferred_element_type=F32) + carry_ref[:, 0:1]
    rank = jnp.zeros((SUBLANES, tm), F32)
    for k in range(TOP_K):
        rk = jnp.sum(jnp.where(hots[k], before, 0.0), axis=0, keepdims=True)
        rank = jnp.where(row == k, rk, rank)
    carry_ref[...] = carry_ref[...] + jnp.sum(sel.astype(F32), axis=1, keepdims=True)

    gates_ref[...] = gates
    ids_ref[...] = ids.astype(jnp.int32)
    rank_ref[...] = rank.astype(jnp.int32)
    cnt_ref[...] = carry_ref[...].astype(jnp.int32)


def router(h2, w_router, b_router, tm=512):
    T = h2.shape[0]
    tm = min(tm, T)
    tile = pl.BlockSpec((SUBLANES, tm), lambda i: (0, i))
    return pl.pallas_call(
        functools.partial(_router_body, tm=tm),
        grid=(T // tm,),
        in_specs=[pl.BlockSpec((tm, D_MODEL), lambda i: (i, 0)),
                  pl.BlockSpec((N_EXPERTS, D_MODEL), lambda i: (0, 0)),
                  pl.BlockSpec((N_EXPERTS, 1), lambda i: (0, 0))],
        out_specs=[tile, tile, tile, pl.BlockSpec((N_EXPERTS, LANES), lambda i: (0, 0))],
        out_shape=[jax.ShapeDtypeStruct((SUBLANES, T), F32), jax.ShapeDtypeStruct((SUBLANES, T), jnp.int32),
                   jax.ShapeDtypeStruct((SUBLANES, T), jnp.int32), jax.ShapeDtypeStruct((N_EXPERTS, LANES), jnp.int32)],
        scratch_shapes=[pltpu.VMEM((N_EXPERTS, LANES), F32), pltpu.VMEM((tm, tm), BF16)],
        compiler_params=_params(("arbitrary",)),
        name="router",
    )(h2, w_router.astype(F32).T, b_router.astype(F32).reshape(N_EXPERTS, 1))


def _token_copy(src, dst, s_tok, d_tok, sem, n=1):
    rows = n * SUBLANES
    return pltpu.make_async_copy(src.at[pl.ds(pl.multiple_of(s_tok * SUBLANES, SUBLANES), rows), :],
                                 dst.at[pl.ds(pl.multiple_of(d_tok * SUBLANES, SUBLANES), rows), :], sem)


def _dispatch_body(zstart_ref, zcount_ref, n_used_ref, pos_ref, h_ref, xs_ref, zeros_ref, ring, sems, zsem, *, tm, nb):
    half = EXPERT_BLOCK // 2
    bits = [1 << s for s in reversed(range(half.bit_length()))]

    def zero_fill(e, wait):
        start, count = zstart_ref[e], zcount_ref[e]
        for bit in bits:
            @pl.when((count & bit) != 0)
            def _():
                cp = _token_copy(zeros_ref, xs_ref, 0, 0 if wait else start + (count & ~(2 * bit - 1)), zsem, bit)
                cp.wait() if wait else cp.start()

    def zero_tail(blk, wait):
        for part in range(2):
            cp = _token_copy(zeros_ref, xs_ref, 0, 0 if wait else blk * EXPERT_BLOCK + part * half, zsem, half)
            cp.wait() if wait else cp.start()

    @pl.when(pl.program_id(0) == 0)
    def _():
        zeros_ref[...] = jnp.zeros_like(zeros_ref)
        lax.fori_loop(0, N_EXPERTS, lambda e, c: (zero_fill(e, False), c)[1], 0)
        lax.fori_loop(n_used_ref[0], nb, lambda blk, c: (zero_tail(blk, False), c)[1], 0)

    i = pl.program_id(0)
    last = pl.num_programs(0) - 1

    def retire(slot):
        for k in range(TOP_K):
            _token_copy(ring.at[slot], xs_ref, 0, 0, sems.at[slot], tm).wait()

    def step(slot):
        pl.when(i >= 2)(functools.partial(retire, slot))
        ring[slot] = h_ref[...]

        def issue(t, c):
            for k in range(TOP_K):
                _token_copy(ring.at[slot], xs_ref, t, pos_ref[k * tm + t], sems.at[slot]).start(priority=k % 2)
            return c

        lax.fori_loop(0, tm, issue, 0)

        @pl.when(i == last)
        def _():
            retire(slot)
            pl.when(i >= 1)(functools.partial(retire, 1 - slot))

    for slot in range(2):
        pl.when(i % 2 == slot)(functools.partial(step, slot))

    @pl.when(pl.program_id(0) == 0)
    def _():
        lax.fori_loop(0, N_EXPERTS, lambda e, c: (zero_fill(e, True), c)[1], 0)
        lax.fori_loop(n_used_ref[0], nb, lambda blk, c: (zero_tail(blk, True), c)[1], 0)


def _tile_major(pos_t, tm):
    T = pos_t.shape[1]
    return pos_t[:TOP_K].reshape(TOP_K, T // tm, tm).transpose(1, 0, 2).reshape(-1)


def dispatch(ht, pos_t, zstart, zcount, n_used, n_rows, tm=512):
    T = ht.shape[0] // SUBLANES
    tm = min(tm, T)
    grid_spec = pltpu.PrefetchScalarGridSpec(
        num_scalar_prefetch=3,
        grid=(T // tm,),
        in_specs=[pl.BlockSpec((TOP_K * tm,), lambda i, zs, zc, nu: (i,), memory_space=pltpu.SMEM),
                  pl.BlockSpec((tm * SUBLANES, LANES), lambda i, zs, zc, nu: (i, 0))],
        out_specs=pl.BlockSpec(memory_space=pl.ANY),
        scratch_shapes=[pltpu.VMEM((EXPERT_BLOCK // 2 * SUBLANES, LANES), F32),
                        pltpu.VMEM((2, tm * SUBLANES, LANES), F32), pltpu.SemaphoreType.DMA((2,)),
                        pltpu.SemaphoreType.DMA],
    )
    return pl.pallas_call(
        functools.partial(_dispatch_body, tm=tm, nb=n_rows // EXPERT_BLOCK),
        grid_spec=grid_spec,
        out_shape=jax.ShapeDtypeStruct((n_rows * SUBLANES, LANES), F32),
        compiler_params=_params(("arbitrary",)),
        name="dispatch",
    )(zstart, zcount, n_used, _tile_major(pos_t, tm), ht)


def _combine_body(pos_ref, pos_next_ref, gates_ref, h_ref, g2_ref, b2_ref, ys_ref, o_ref, buf, sems, *, tm):
    i = pl.program_id(0)

    def issue(p_ref, s):
        def body(t, c):
            for k in range(TOP_K):
                _token_copy(ys_ref, buf.at[s, k], p_ref[k * tm + t], t, sems.at[s]).start(priority=k % 2)
            return c

        lax.fori_loop(0, tm, body, 0)

    @pl.when(i == 0)
    def _():
        issue(pos_ref, 0)

    def step(slot):
        @pl.when(i + 1 < pl.num_programs(0))
        def _():
            issue(pos_next_ref, 1 - slot)

        for k in range(TOP_K):
            _token_copy(ys_ref, buf.at[slot, k], 0, 0, sems.at[slot], tm).wait()

        gates = gates_ref[...].T
        ffn = gates[:, 0:1] * _load_token_tiles(buf.at[slot, 0], tm)
        for k in range(1, TOP_K):
            ffn = ffn + gates[:, k:k + 1] * _load_token_tiles(buf.at[slot, k], tm)
        o_ref[...] = _layer_norm(DEEPNORM_ALPHA * h_ref[...] + ffn, g2_ref[...], b2_ref[...])

    for slot in range(2):
        pl.when(i % 2 == slot)(functools.partial(step, slot))


def combine(ys, pos_t, gates_t, h2, g2, b2, tm=256):
    T = h2.shape[0]
    tm = min(tm, T)
    last = T // tm - 1
    pos_flat = _tile_major(pos_t, tm)
    return pl.pallas_call(
        functools.partial(_combine_body, tm=tm),
        grid=(T // tm,),
        in_specs=[pl.BlockSpec((TOP_K * tm,), lambda i: (i,), memory_space=pltpu.SMEM),
                  pl.BlockSpec((TOP_K * tm,), lambda i: (jnp.minimum(i + 1, last),), memory_space=pltpu.SMEM),
                  pl.BlockSpec((SUBLANES, tm), lambda i: (0, i)),
                  pl.BlockSpec((tm, D_MODEL), lambda i: (i, 0)),
                  pl.BlockSpec((1, D_MODEL), lambda i: (0, 0)),
                  pl.BlockSpec((1, D_MODEL), lambda i: (0, 0)),
                  pl.BlockSpec(memory_space=pl.ANY)],
        out_specs=pl.BlockSpec((tm, D_MODEL), lambda i: (i, 0)),
        out_shape=jax.ShapeDtypeStruct((T, D_MODEL), F32),
        scratch_shapes=[pltpu.VMEM((2, TOP_K, tm * SUBLANES, LANES), F32), pltpu.SemaphoreType.DMA((2,))],
        compiler_params=_params(("arbitrary",)),
        name="combine",
    )(pos_flat, pos_flat, gates_t, h2, g2, b2, ys)


def _experts_body(blk_e, blk_src, n_used, next_e, wslot, xs_ref, wgu_hbm, bgu_ref, wdn_hbm, bdn_ref, ys_ref,
                  wgu_f, wdn_f, wgu_b, wdn_b, sems):
    del blk_src
    i = pl.program_id(0)
    used = i < n_used[0]
    e = blk_e[i]

    def weight_copies(expert, slot):
        return (pltpu.make_async_copy(wgu_hbm.at[expert], wgu_f.at[slot], sems.at[slot]),
                pltpu.make_async_copy(wdn_hbm.at[expert], wdn_f.at[slot], sems.at[slot]))

    @pl.when(i == 0)
    def _():
        for cp in weight_copies(e, 0):
            cp.start()

    @pl.when((i == 0) | (e != blk_e[jnp.maximum(i - 1, 0)]))
    def _():
        nxt = next_e[e]
        for slot in range(2):
            @pl.when(wslot[e] == slot)
            def _():
                for cp in weight_copies(e, slot):
                    cp.wait()
                wgu_b[...] = wgu_f[slot].astype(BF16)
                wdn_b[...] = wdn_f[slot].astype(BF16)

                @pl.when(nxt < N_EXPERTS)
                def _():
                    for cp in weight_copies(nxt, 1 - slot):
                        cp.start()

    @pl.when(used)
    def _():
        x = _load_token_tiles(xs_ref, EXPERT_BLOCK).astype(BF16)
        hgu = jnp.dot(x, wgu_b[...], preferred_element_type=F32) + bgu_ref[...]
        gate = jnp.minimum(hgu[:, :D_FF], SWIGLU_LIMIT)
        up = jnp.clip(hgu[:, D_FF:], -SWIGLU_LIMIT, SWIGLU_LIMIT)
        act = (up + 1.0) * gate * _sigmoid(SWIGLU_ALPHA * gate)
        y = jnp.dot(act.astype(BF16), wdn_b[...], preferred_element_type=F32) + bdn_ref[...]
        _store_token_tiles(ys_ref, y, EXPERT_BLOCK)

    @pl.when(jnp.logical_not(used))
    def _():
        ys_ref[...] = jnp.zeros_like(ys_ref)


def experts(xs, blk_e, blk_src, n_used, next_e, wslot, wgu, bgu, wdn, bdn):
    n_rows = xs.shape[0] // SUBLANES
    nb = n_rows // EXPERT_BLOCK
    rows = EXPERT_BLOCK * SUBLANES
    grid_spec = pltpu.PrefetchScalarGridSpec(
        num_scalar_prefetch=5,
        grid=(nb,),
        in_specs=[pl.BlockSpec((rows, LANES), lambda i, e, s, n, ne, ws: (s[i], 0)),
                  pl.BlockSpec(memory_space=pl.ANY),
                  pl.BlockSpec((None, 1, 2 * D_FF), lambda i, e, s, n, ne, ws: (e[i], 0, 0)),
                  pl.BlockSpec(memory_space=pl.ANY),
                  pl.BlockSpec((None, 1, D_MODEL), lambda i, e, s, n, ne, ws: (e[i], 0, 0))],
        out_specs=pl.BlockSpec((rows, LANES), lambda i, e, s, n, ne, ws: (i, 0)),
        scratch_shapes=[pltpu.VMEM((2, D_MODEL, 2 * D_FF), F32), pltpu.VMEM((2, D_FF, D_MODEL), F32),
                        pltpu.VMEM((D_MODEL, 2 * D_FF), BF16), pltpu.VMEM((D_FF, D_MODEL), BF16),
                        pltpu.SemaphoreType.DMA((2,))],
    )
    return pl.pallas_call(
        _experts_body,
        grid_spec=grid_spec,
        out_shape=jax.ShapeDtypeStruct((n_rows * SUBLANES, LANES), F32),
        compiler_params=_params(("arbitrary",)),
        name="experts",
    )(blk_e, blk_src, n_used, next_e, wslot, xs, wgu, bgu, wdn, bdn)


def routing_layout(ids, rank, counts, n_blocks):
    padded = (counts + EXPERT_BLOCK - 1) // EXPERT_BLOCK * EXPERT_BLOCK
    pad_ends = jnp.cumsum(padded)
    pad_starts = pad_ends - padded
    start_of = jnp.zeros_like(ids)
    for e in range(N_EXPERTS):
        start_of = jnp.where(ids == e, pad_starts[e], start_of)
    pos_t = (start_of + rank).astype(jnp.int32)
    n_used = jnp.maximum(pad_ends[-1] // EXPERT_BLOCK, 1).astype(jnp.int32)
    blk = jnp.minimum(jnp.arange(n_blocks, dtype=jnp.int32), n_used - 1)
    blk_e = jnp.sum(pad_ends[None, :] <= (blk * EXPERT_BLOCK)[:, None], axis=1)
    blk_e = jnp.minimum(blk_e, N_EXPERTS - 1).astype(jnp.int32)
    zstart = (pad_starts + counts).astype(jnp.int32)
    zcount = (padded - counts).astype(jnp.int32)
    nonempty = counts > 0
    expert_ids = jnp.arange(N_EXPERTS, dtype=jnp.int32)
    later = jnp.where(nonempty[None, :] & (expert_ids[None, :] > expert_ids[:, None]), expert_ids[None, :], N_EXPERTS)
    next_e = jnp.min(later, axis=1).astype(jnp.int32)
    wslot = ((jnp.cumsum(nonempty) - 1) % 2).astype(jnp.int32)
    return pos_t, blk_e, blk, n_used.reshape(1), zstart, zcount, next_e, wslot


DILATE_WIDTH = 4 * LANES


def _dilate_body(x_ref, *refs, n):
    o_refs, slab = refs[:-1], refs[-1]
    for c0 in range(0, DILATE_WIDTH, LANES):
        slab[...] = x_ref[:, c0:c0 + LANES]
        for (_, dil), o_ref in zip(ATT_GROUPS, o_refs):
            for r in range(dil):
                o_ref[r, :, c0:c0 + LANES] = slab[pl.ds(r, n // dil, stride=dil), :].astype(BF16)


def dilate_cast(x3, n=2048):
    B, S, D = x3.shape
    n = min(n, S)
    dils = [dil for _, dil in ATT_GROUPS]
    return pl.pallas_call(
        functools.partial(_dilate_body, n=n),
        grid=(B, S // n, D // DILATE_WIDTH),
        in_specs=[pl.BlockSpec((None, n, DILATE_WIDTH), lambda b, i, c: (b, i, c))],
        out_specs=[pl.BlockSpec((None, dil, n // dil, DILATE_WIDTH), lambda b, i, c: (b, 0, i, c)) for dil in dils],
        out_shape=[jax.ShapeDtypeStruct((B, dil, S // dil, D), BF16) for dil in dils],
        scratch_shapes=[pltpu.VMEM((n, LANES), F32)],
        compiler_params=_params(("arbitrary", "arbitrary", "arbitrary")),
        name="dilate_cast",
    )(x3)


def kernel(x, w_in, rel_bias, w_dw, b_dw, conv_ln_g, conv_ln_b, w_o_attn, w_o_conv, w_out, ln1_g, ln1_b,
           w_router, b_router, w_gate_up, b_gate_up, w_down, b_down, ln2_g, ln2_b):
    B, S, D = x.shape
    T = B * S
    h = x
    bm = bias_tables(rel_bias)
    q_off, k_off, v_off, rest = 0, QKV_WIDTH, 2 * QKV_WIDTH, 3 * QKV_WIDTH
    for l in range(DEPTH):
        wb = w_in[l].astype(BF16)

        def group_cols(gi):
            return [wb[:, off + gi * GROUP_WIDTH:off + (gi + 1) * GROUP_WIDTH] for off in (q_off, k_off, v_off)]

        x_dil = dilate_cast(h)
        xb = x_dil[0].reshape(T, D)
        w_main = jnp.concatenate(group_cols(0) + [wb[:, rest:]], axis=1)
        proj = in_proj(xb, w_main, w_main.shape[1] // 2, "in_proj_main").reshape(B, S, w_main.shape[1])
        o_list, st_list = [], []
        for gi in range(N_GROUPS):
            dil = ATT_GROUPS[gi][1]
            if dil == 1:
                qkv = proj.reshape(B, 1, S, proj.shape[-1])
            else:
                w_g = jnp.concatenate(group_cols(gi), axis=1)
                qkv = in_proj(x_dil[gi].reshape(T, D), w_g, w_g.shape[1], f"in_proj_g{gi}")
                qkv = qkv.reshape(B, dil, S // dil, w_g.shape[1])
            o_g, st_g = attention_group(qkv, bm[gi], gi)
            o_list.append(o_g)
            st_list.append(st_g)
        h1, h1_tiles = mixer_out(o_list, st_list, proj, h,
                       w_dw[l].reshape(CONV_WIDTH, CONV_CH), b_dw[l].reshape(1, CONV_CH),
                       conv_ln_g[l].reshape(1, CONV_CH), conv_ln_b[l].reshape(1, CONV_CH),
                       w_o_attn[l].astype(BF16), w_o_conv[l].astype(BF16), w_out[l].astype(BF16),
                       ln1_g[l].reshape(1, D), ln1_b[l].reshape(1, D))
        h2 = h1.reshape(T, D)
        gates_t, ids_t, rank_t, counts = router(h2, w_router[l], b_router[l])
        n_rows = T * TOP_K + N_EXPERTS * EXPERT_BLOCK
        pos_t, blk_e, blk_src, n_used, zstart, zcount, next_e, wslot = routing_layout(
            ids_t, rank_t, counts[:, 0], n_rows // EXPERT_BLOCK)
        xs = dispatch(h1_tiles, pos_t, zstart, zcount, n_used, n_rows)
        ys = experts(xs, blk_e, blk_src, n_used, next_e, wslot,
                     w_gate_up[l], b_gate_up[l].reshape(N_EXPERTS, 1, 2 * D_FF),
                     w_down[l], b_down[l].reshape(N_EXPERTS, 1, D))
        out = combine(ys, pos_t, gates_t, h2, ln2_g[l].reshape(1, D), ln2_b[l].reshape(1, D))
        h = out.reshape(B, S, D)
    return h
```

```python
import functools
import math

import jax
import jax.numpy as jnp
from jax import lax
from jax.experimental import pallas as pl
from jax.experimental.pallas import tpu as pltpu

D_MODEL = 1024
ATT_GROUPS = ((128, 1), (512, 4), (2048, 16))
N_GROUPS = len(ATT_GROUPS)
HEADS_PER_GROUP = 8
HEAD_DIM = 64
GROUP_WIDTH = HEADS_PER_GROUP * HEAD_DIM
QKV_WIDTH = N_GROUPS * GROUP_WIDTH
ATT_BLOCK = 128
N_BUCKETS = 32
MAX_DISTANCE = 2048
CONV_CH = 768
CONV_WIDTH = 31
CONV_HALO = 32
N_EXPERTS = 32
TOP_K = 4
D_FF = 1024
SWIGLU_LIMIT = 7.0
SWIGLU_ALPHA = 1.702
IN_WIDTH = 3 * QKV_WIDTH + 2 * CONV_CH + 2 * D_MODEL
LN_EPS = 1e-5
NEG_INF = -1e30
DEPTH = 1
DEEPNORM_ALPHA = (2 * DEPTH) ** 0.25

LANES = 128
SUBLANES = 8
EXPERT_BLOCK = 512
VMEM_LIMIT = 56 * 1024 * 1024

F32 = jnp.float32
BF16 = jnp.bfloat16


def _params(sem, vmem=VMEM_LIMIT):
    return pltpu.CompilerParams(dimension_semantics=sem, vmem_limit_bytes=vmem)


def _sigmoid(x):
    return 0.5 * jnp.tanh(0.5 * x) + 0.5


def _layer_norm(x, g, b):
    mu = jnp.mean(x, axis=-1, keepdims=True)
    xc = x - mu
    var = jnp.mean(xc * xc, axis=-1, keepdims=True)
    return xc * lax.rsqrt(var + LN_EPS) * g + b


def _in_proj_body(x_ref, w_ref, o_ref):
    o_ref[...] = jnp.dot(x_ref[...], w_ref[...], preferred_element_type=F32).astype(o_ref.dtype)


def in_proj(xb, w_b, tn, name, tm=1024):
    T = xb.shape[0]
    N = w_b.shape[1]
    tm = min(tm, T)
    return pl.pallas_call(
        _in_proj_body,
        grid=(N // tn, T // tm),
        in_specs=[pl.BlockSpec((tm, D_MODEL), lambda n, m: (m, 0)),
                  pl.BlockSpec((D_MODEL, tn), lambda n, m: (0, n))],
        out_specs=pl.BlockSpec((tm, tn), lambda n, m: (m, n)),
        out_shape=jax.ShapeDtypeStruct((T, N), BF16),
        compiler_params=_params(("arbitrary", "arbitrary")),
        name=name,
    )(xb, w_b)


def _in_proj_dilate_body(*refs, tm, nslab):
    x_slabs, w_ref, o_ref, d_refs = refs[:nslab], refs[nslab], refs[nslab + 1], refs[nslab + 2:]
    cols = []
    for c, x_ref in enumerate(x_slabs):
        cs = slice(c * LANES, (c + 1) * LANES)
        cols.append(x_ref[...].astype(BF16))
        for (_, dil), d_ref in zip(ATT_GROUPS[1:], d_refs):
            for r in range(dil):
                d_ref[r, :, cs] = x_ref[pl.ds(r, tm // dil, stride=dil), :].astype(BF16)
    xb = jnp.concatenate(cols, axis=1)
    o_ref[...] = jnp.dot(xb, w_ref[...], preferred_element_type=F32).astype(o_ref.dtype)


def in_proj_dilate(x3, w_b, name, tm=512):
    B, S, D = x3.shape
    N = w_b.shape[1]
    tm = min(tm, S)
    nslab = D // LANES
    dils = [dil for _, dil in ATT_GROUPS[1:]]
    per_seq = S // tm
    return pl.pallas_call(
        functools.partial(_in_proj_dilate_body, tm=tm, nslab=nslab),
        grid=(B, per_seq),
        in_specs=[pl.BlockSpec((None, tm, LANES), functools.partial(lambda c, b, i: (b, i, c), c))
                  for c in range(nslab)]
        + [pl.BlockSpec((D, N), lambda b, i: (0, 0))],
        out_specs=[pl.BlockSpec((None, tm, N), lambda b, i: (b, i, 0))]
        + [pl.BlockSpec((None, dil, tm // dil, D), lambda b, i: (b, 0, i, 0)) for dil in dils],
        out_shape=[jax.ShapeDtypeStruct((B, S, N), BF16)]
        + [jax.ShapeDtypeStruct((B, dil, S // dil, D), BF16) for dil in dils],
        compiler_params=_params(("arbitrary", "arbitrary")),
        name=name,
    )(*([x3] * nslab), w_b)


def _t5_bucket(dist):
    max_exact = N_BUCKETS // 2
    log_ratio = jnp.log(jnp.maximum(dist, max_exact).astype(F32) / max_exact) / math.log(MAX_DISTANCE / max_exact)
    large = jnp.minimum(max_exact + (log_ratio * (N_BUCKETS - max_exact)).astype(jnp.int32), N_BUCKETS - 1)
    return jnp.where(dist < max_exact, dist, large)


def _bias_body(tbl_ref, bucket_ref, band_ref, o_ref):
    g = pl.program_id(0)
    h = pl.program_id(1)
    col = g * HEADS_PER_GROUP + h
    bucket = bucket_ref[...]
    acc = jnp.zeros(bucket.shape, F32)
    for k in range(N_BUCKETS):
        acc = jnp.where(bucket == k, tbl_ref[k, col], acc)
    band = band_ref[...] != 0
    kj = lax.broadcasted_iota(jnp.int32, bucket.shape, 1)
    o_ref[0] = jnp.where(band, acc, NEG_INF)
    o_ref[1] = jnp.where(band & (kj >= ATT_BLOCK), acc, NEG_INF)


def bias_tables(rel_bias):
    qi = jnp.arange(ATT_BLOCK)[:, None]
    kj = jnp.arange(2 * ATT_BLOCK)[None, :]
    dist = qi - kj + ATT_BLOCK
    buckets, bands = [], []
    for window, dil in ATT_GROUPS:
        bands.append(((dist >= 0) & (dist <= window // dil)).astype(jnp.int32))
        buckets.append(_t5_bucket(jnp.maximum(dist, 0) * dil).astype(jnp.int32))
    buckets = jnp.stack(buckets)
    bands = jnp.stack(bands)
    blk = (None, ATT_BLOCK, 2 * ATT_BLOCK)
    return pl.pallas_call(
        _bias_body,
        grid=(N_GROUPS, HEADS_PER_GROUP),
        in_specs=[pl.BlockSpec(memory_space=pltpu.SMEM),
                  pl.BlockSpec(blk, lambda g, h: (g, 0, 0)),
                  pl.BlockSpec(blk, lambda g, h: (g, 0, 0))],
        out_specs=pl.BlockSpec((None, None, 2, ATT_BLOCK, 2 * ATT_BLOCK), lambda g, h: (g, h, 0, 0, 0)),
        out_shape=jax.ShapeDtypeStruct((N_GROUPS, HEADS_PER_GROUP, 2, ATT_BLOCK, 2 * ATT_BLOCK), F32),
        compiler_params=_params(("arbitrary", "arbitrary")),
        name="bias_tables",
    )(rel_bias.astype(F32), buckets, bands)


def _attn_body(q_ref, kc_ref, kp_ref, vc_ref, vp_ref, bm_ref, o_ref, st_ref, k_all, v_all, s_buf, *, tq):
    i = pl.program_id(2)
    nsub = tq // ATT_BLOCK
    k_all[0:ATT_BLOCK] = kp_ref[...]
    k_all[ATT_BLOCK:] = kc_ref[...]
    v_all[0:ATT_BLOCK] = vp_ref[...]
    v_all[ATT_BLOCK:] = vc_ref[...]
    lo = lax.broadcasted_iota(jnp.int32, (ATT_BLOCK, LANES), 1) < HEAD_DIM
    nt_dims = (((1,), (1,)), ((), ()))
    first = jnp.where(i == 0, 1, 0)

    def scores(s, slot):
        r0 = s * ATT_BLOCK
        for j in range(HEADS_PER_GROUP // 2):
            cs = slice(j * LANES, (j + 1) * LANES)
            qp = q_ref[r0:r0 + ATT_BLOCK, cs] * jnp.asarray(HEAD_DIM ** -0.5, BF16)
            kp = k_all[r0:r0 + 2 * ATT_BLOCK, cs]
            for hh in range(2):
                h = 2 * j + hh
                qh = jnp.where(lo if hh == 0 else ~lo, qp, jnp.zeros_like(qp))
                bias = bm_ref[h, first] if s == 0 else bm_ref[h, 0]
                s_buf[slot, h] = lax.dot_general(qh, kp, nt_dims, preferred_element_type=F32) + bias

    def softmax_pv(s, slot):
        rows = slice(s * ATT_BLOCK, (s + 1) * ATT_BLOCK)
        st_ref[rows, :] = jnp.zeros((ATT_BLOCK, LANES), F32)
        for j in range(HEADS_PER_GROUP // 2):
            vp = v_all[s * ATT_BLOCK:(s + 2) * ATT_BLOCK, j * LANES:(j + 1) * LANES]
            for hh in range(2):
                h = 2 * j + hh
                sc = s_buf[slot, h]
                m = jnp.max(sc, axis=-1, keepdims=True)
                p = jnp.exp(sc - m)
                den = jnp.sum(p, axis=-1, keepdims=True)
                pv = jnp.dot(p.astype(BF16), vp, preferred_element_type=F32) * (1.0 / den)
                c0 = h * HEAD_DIM
                o_ref[rows, c0:c0 + HEAD_DIM] = pv[:, hh * HEAD_DIM:(hh + 1) * HEAD_DIM]
                st_ref[rows, h:h + 1] = m
                st_ref[rows, HEADS_PER_GROUP + h:HEADS_PER_GROUP + h + 1] = den

    scores(0, 0)
    for s in range(nsub):
        softmax_pv(s, s % 2)
        if s + 1 < nsub:
            scores(s + 1, (s + 1) % 2)


def attention_group(qkv, bm_g, gi, tq=512):
    B, dil, L, _ = qkv.shape
    tq = min(tq, L)
    sub = tq // ATT_BLOCK

    def cur(col):
        return pl.BlockSpec((None, None, tq, GROUP_WIDTH), lambda b, r, i: (b, r, i, col))

    def prev(col):
        return pl.BlockSpec((None, None, ATT_BLOCK, GROUP_WIDTH),
                            lambda b, r, i: (b, r, jnp.maximum(i * sub - 1, 0), col))

    return pl.pallas_call(
        functools.partial(_attn_body, tq=tq),
        grid=(B, dil, L // tq),
        in_specs=[cur(0), cur(1), prev(1), cur(2), prev(2),
                  pl.BlockSpec((HEADS_PER_GROUP, 2, ATT_BLOCK, 2 * ATT_BLOCK), lambda b, r, i: (0, 0, 0, 0))],
        out_specs=[pl.BlockSpec((None, None, tq, GROUP_WIDTH), lambda b, r, i: (b, r, i, 0)),
                   pl.BlockSpec((None, None, tq, LANES), lambda b, r, i: (b, r, i, 0))],
        out_shape=[jax.ShapeDtypeStruct((B, dil, L, GROUP_WIDTH), F32),
                   jax.ShapeDtypeStruct((B, dil, L, LANES), F32)],
        scratch_shapes=[pltpu.VMEM((ATT_BLOCK + tq, GROUP_WIDTH), BF16),
                        pltpu.VMEM((ATT_BLOCK + tq, GROUP_WIDTH), BF16),
                        pltpu.VMEM((2, HEADS_PER_GROUP, ATT_BLOCK, 2 * ATT_BLOCK), F32)],
        compiler_params=_params(("arbitrary", "arbitrary", "arbitrary")),
        name=f"attention_g{gi}",
    )(qkv, qkv, qkv, qkv, qkv, bm_g)


def _split_bf16(x):
    hi = x.astype(BF16)
    lo = (x - hi.astype(F32)).astype(BF16)
    return hi, lo


def _load_token_tiles(ref, n):
    return jnp.concatenate([ref[pl.ds(c, n, stride=SUBLANES), :] for c in range(D_MODEL // LANES)], axis=1)


def _store_token_tiles(ref, x, n):
    for c in range(D_MODEL // LANES):
        ref[pl.ds(c, n, stride=SUBLANES), :] = x[:, c * LANES:(c + 1) * LANES]


def _to_token_order(blk_ref, tok_ref, dil, tm):
    n = tm // dil
    for r in range(dil):
        for c in range(tok_ref.shape[0]):
            tok_ref[c, pl.ds(r, n, stride=dil), :] = blk_ref[r, :, c * LANES:(c + 1) * LANES]


def _mixer_out_body(o0, o1, o2, s0, s1, s2, uv_ref, ug_ref, uvh_ref, ugh_ref, ga_ref, gc_ref, x_ref,
                    wdw_ref, bdw_ref, cg_ref, cb_ref, woa_ref, woc_ref, wout_ref, g1_ref, b1_ref,
                    h_ref, ht_ref, glu_ref, dw_ref, shift_ref, tok_o1, tok_s1, tok_o2, tok_s2, *, tm, chunk):
    i = pl.program_id(1)

    ncol = GROUP_WIDTH // LANES
    _to_token_order(o1, tok_o1, ATT_GROUPS[1][1], tm)
    _to_token_order(s1, tok_s1, ATT_GROUPS[1][1], tm)
    _to_token_order(o2, tok_o2, ATT_GROUPS[2][1], tm)
    _to_token_order(s2, tok_s2, ATT_GROUPS[2][1], tm)
    outs = [o0[0],
            jnp.concatenate([tok_o1[c] for c in range(ncol)], axis=1),
            jnp.concatenate([tok_o2[c] for c in range(ncol)], axis=1)]
    sts = [s0[0], tok_s1[0], tok_s2[0]]

    mx = jnp.maximum(jnp.maximum(sts[0], sts[1]), sts[2])
    wts = [pltpu.roll(st, LANES - HEADS_PER_GROUP, axis=1) * jnp.exp(st - mx) for st in sts]
    wsum = wts[0] + wts[1] + wts[2]
    row = lax.broadcasted_iota(jnp.int32, (LANES, GROUP_WIDTH), 0)
    colh = lax.broadcasted_iota(jnp.int32, (LANES, GROUP_WIDTH), 1) // HEAD_DIM
    expand = (row == colh).astype(BF16)
    attn = jnp.zeros((tm, GROUP_WIDTH), F32)
    head_lane = lax.broadcasted_iota(jnp.int32, (tm, LANES), 1) < HEADS_PER_GROUP
    for wt, o in zip(wts, outs):
        c_hi, c_lo = _split_bf16(jnp.where(head_lane, wt / wsum, 0.0))
        c = (jnp.dot(c_hi, expand, preferred_element_type=F32)
             + jnp.dot(c_lo, expand, preferred_element_type=F32))
        attn = attn + c * o
    a_out = jnp.dot(attn.astype(BF16), woa_ref[...], preferred_element_type=F32)

    gh = uvh_ref[...].astype(F32) * _sigmoid(ugh_ref[...].astype(F32))
    glu_ref[0:CONV_HALO] = jnp.where(i == 0, 0.0, gh)
    glu_ref[CONV_HALO:] = uv_ref[...].astype(F32) * _sigmoid(ug_ref[...].astype(F32))
    first_tap = CONV_HALO - (CONV_WIDTH - 1)
    for b in range(1, SUBLANES):
        shift_ref[b - 1] = glu_ref[b:b + shift_ref.shape[1], :]

    for c0 in range(0, CONV_CH, LANES):
        cs = slice(c0, c0 + LANES)
        bias = jnp.broadcast_to(bdw_ref[:, cs], (chunk, LANES))
        for r0 in range(0, tm, chunk):
            acc = bias
            for j in range(CONV_WIDTH):
                a, b = divmod(first_tap + j, SUBLANES)
                lo_row = r0 + a * SUBLANES
                rows = glu_ref[lo_row:lo_row + chunk, cs] if b == 0 else shift_ref[b - 1, lo_row:lo_row + chunk, cs]
                acc = acc + wdw_ref[j:j + 1, cs] * rows
            dw_ref[r0:r0 + chunk, cs] = acc
    cn = _layer_norm(dw_ref[...], cg_ref[...], cb_ref[...])
    conv = cn * _sigmoid(cn)
    c_out = jnp.dot(conv.astype(BF16), woc_ref[...], preferred_element_type=F32)

    merged = (_sigmoid(ga_ref[...].astype(F32)) * a_out + _sigmoid(gc_ref[...].astype(F32)) * c_out)
    mix = jnp.dot(merged.astype(BF16), wout_ref[...], preferred_element_type=F32)
    h = _layer_norm(DEEPNORM_ALPHA * x_ref[...] + mix, g1_ref[...], b1_ref[...])
    h_ref[...] = h
    _store_token_tiles(ht_ref, h, tm)


def mixer_out(o_list, st_list, proj3, x3, w_dw, b_dw, cg, cb, woa_b, woc_b, wout_b, g1, b1, tm=512, chunk=64):
    B, S, _ = x3.shape
    tm = min(tm, S)
    halo_blocks = tm // CONV_HALO
    uv_col = QKV_WIDTH // CONV_CH
    ga_col = (QKV_WIDTH + 2 * CONV_CH) // D_MODEL
    ncol = GROUP_WIDTH // LANES

    def tile(width, col=0):
        return pl.BlockSpec((None, tm, width), lambda b, i: (b, i, col))

    def dilated(gi, width):
        dil = ATT_GROUPS[gi][1]
        return pl.BlockSpec((None, dil, tm // dil, width), lambda b, i: (b, 0, i, 0))

    def halo(col):
        return pl.BlockSpec((None, CONV_HALO, CONV_CH), lambda b, i: (b, jnp.maximum(i * halo_blocks - 1, 0), col))

    def whole(shape):
        return pl.BlockSpec(shape, lambda b, i: (0,) * len(shape))

    return pl.pallas_call(
        functools.partial(_mixer_out_body, tm=tm, chunk=chunk),
        grid=(B, S // tm),
        in_specs=[dilated(g, GROUP_WIDTH) for g in range(N_GROUPS)] + [dilated(g, LANES) for g in range(N_GROUPS)]
        + [tile(CONV_CH, uv_col), tile(CONV_CH, uv_col + 1), halo(uv_col), halo(uv_col + 1),
           tile(D_MODEL, ga_col), tile(D_MODEL, ga_col + 1), tile(D_MODEL),
           whole((CONV_WIDTH, CONV_CH)), whole((1, CONV_CH)), whole((1, CONV_CH)), whole((1, CONV_CH)),
           whole((GROUP_WIDTH, D_MODEL)), whole((CONV_CH, D_MODEL)), whole((D_MODEL, D_MODEL)),
           whole((1, D_MODEL)), whole((1, D_MODEL))],
        out_specs=[tile(D_MODEL), pl.BlockSpec((tm * SUBLANES, LANES), lambda b, i: (b * (S // tm) + i, 0))],
        out_shape=[jax.ShapeDtypeStruct((B, S, D_MODEL), F32),
                   jax.ShapeDtypeStruct((B * S * SUBLANES, LANES), F32)],
        scratch_shapes=[pltpu.VMEM((CONV_HALO + tm, CONV_CH), F32), pltpu.VMEM((tm, CONV_CH), F32),
                        pltpu.VMEM((SUBLANES - 1, tm + CONV_HALO - SUBLANES, CONV_CH), F32),
                        pltpu.VMEM((ncol, tm, LANES), F32), pltpu.VMEM((1, tm, LANES), F32),
                        pltpu.VMEM((ncol, tm, LANES), F32), pltpu.VMEM((1, tm, LANES), F32)],
        compiler_params=_params(("arbitrary", "arbitrary")),
        name="mixer_out",
    )(*o_list, *st_list, proj3, proj3, proj3, proj3, proj3, proj3, x3,
      w_dw, b_dw, cg, cb, woa_b, woc_b, wout_b, g1, b1)


def _router_body(h_ref, wt_ref, b_ref, gates_ref, ids_ref, rank_ref, cnt_ref, carry_ref, tri_ref, *, tm):
    step = pl.program_id(0)

    @pl.when(step == 0)
    def _():
        carry_ref[...] = jnp.zeros_like(carry_ref)
        r_i = lax.broadcasted_iota(jnp.int32, (tm, tm), 0)
        c_i = lax.broadcasted_iota(jnp.int32, (tm, tm), 1)
        tri_ref[...] = (r_i < c_i).astype(BF16)

    nt_dims = (((1,), (1,)), ((), ()))
    h_hi, h_lo = _split_bf16(h_ref[...])
    w_hi, w_lo = _split_bf16(wt_ref[...])
    logits = (lax.dot_general(w_hi, h_hi, nt_dims, preferred_element_type=F32)
              + (lax.dot_general(w_hi, h_lo, nt_dims, preferred_element_type=F32)
                 + lax.dot_general(w_lo, h_hi, nt_dims, preferred_element_type=F32))
              + b_ref[...])
    expert = lax.broadcasted_iota(jnp.int32, (N_EXPERTS, tm), 0).astype(F32)
    row = lax.broadcasted_iota(jnp.int32, (SUBLANES, tm), 0)

    work = logits
    vals, hots = [], []
    ids = jnp.zeros((SUBLANES, tm), F32)
    for k in range(TOP_K):
        v = jnp.max(work, axis=0, keepdims=True)
        idx = jnp.min(jnp.where(work == v, expert, float(N_EXPERTS)), axis=0, keepdims=True)
        hot = expert == idx
        work = jnp.where(hot, -jnp.inf, work)
        ids = jnp.where(row == k, idx, ids)
        vals.append(v)
        hots.append(hot)

    es = [jnp.exp(v - vals[0]) for v in vals]
    esum = es[0] + es[1] + es[2] + es[3]
    gates = jnp.zeros((SUBLANES, tm), F32)
    for k in range(TOP_K):
        gates = jnp.where(row == k, es[k] / esum, gates)

    sel = (hots[0] | hots[1] | hots[2] | hots[3])
    before = jnp.dot(sel.astype(BF16), tri_ref[...], preferred_element_type=F32) + carry_ref[:, 0:1]
    rank = jnp.zeros((SUBLANES, tm), F32)
    for k in range(TOP_K):
        rk = jnp.sum(jnp.where(hots[k], before, 0.0), axis=0, keepdims=True)
        rank = jnp.where(row == k, rk, rank)
    carry_ref[...] = carry_ref[...] + jnp.sum(sel.astype(F32), axis=1, keepdims=True)

    gates_ref[...] = gates
    ids_ref[...] = ids.astype(jnp.int32)
    rank_ref[...] = rank.astype(jnp.int32)
    cnt_ref[...] = carry_ref[...].astype(jnp.int32)


def router(h2, w_router, b_router, tm=512):
    T = h2.shape[0]
    tm = min(tm, T)
    tile = pl.BlockSpec((SUBLANES, tm), lambda i: (0, i))
    return pl.pallas_call(
        functools.partial(_router_body, tm=tm),
        grid=(T // tm,),
        in_specs=[pl.BlockSpec((tm, D_MODEL), lambda i: (i, 0)),
                  pl.BlockSpec((N_EXPERTS, D_MODEL), lambda i: (0, 0)),
                  pl.BlockSpec((N_EXPERTS, 1), lambda i: (0, 0))],
        out_specs=[tile, tile, tile, pl.BlockSpec((N_EXPERTS, LANES), lambda i: (0, 0))],
        out_shape=[jax.ShapeDtypeStruct((SUBLANES, T), F32), jax.ShapeDtypeStruct((SUBLANES, T), jnp.int32),
                   jax.ShapeDtypeStruct((SUBLANES, T), jnp.int32), jax.ShapeDtypeStruct((N_EXPERTS, LANES), jnp.int32)],
        scratch_shapes=[pltpu.VMEM((N_EXPERTS, LANES), F32), pltpu.VMEM((tm, tm), BF16)],
        compiler_params=_params(("arbitrary",)),
        name="router",
    )(h2, w_router.astype(F32).T, b_router.astype(F32).reshape(N_EXPERTS, 1))


def _token_copy(src, dst, s_tok, d_tok, sem, n=1):
    rows = n * SUBLANES
    return pltpu.make_async_copy(src.at[pl.ds(pl.multiple_of(s_tok * SUBLANES, SUBLANES), rows), :],
                                 dst.at[pl.ds(pl.multiple_of(d_tok * SUBLANES, SUBLANES), rows), :], sem)


def _dispatch_body(zstart_ref, zcount_ref, n_used_ref, pos_ref, h_ref, xs_ref, zeros_ref, ring, sems, zsem, *, tm, nb):
    half = EXPERT_BLOCK // 2
    bits = [1 << s for s in reversed(range(half.bit_length()))]

    def zero_fill(e, wait):
        start, count = zstart_ref[e], zcount_ref[e]
        for bit in bits:
            @pl.when((count & bit) != 0)
            def _():
                cp = _token_copy(zeros_ref, xs_ref, 0, 0 if wait else start + (count & ~(2 * bit - 1)), zsem, bit)
                cp.wait() if wait else cp.start()

    def zero_tail(blk, wait):
        for part in range(2):
            cp = _token_copy(zeros_ref, xs_ref, 0, 0 if wait else blk * EXPERT_BLOCK + part * half, zsem, half)
            cp.wait() if wait else cp.start()

    @pl.when(pl.program_id(0) == 0)
    def _():
        zeros_ref[...] = jnp.zeros_like(zeros_ref)
        lax.fori_loop(0, N_EXPERTS, lambda e, c: (zero_fill(e, False), c)[1], 0)
        lax.fori_loop(n_used_ref[0], nb, lambda blk, c: (zero_tail(blk, False), c)[1], 0)

    i = pl.program_id(0)
    last = pl.num_programs(0) - 1

    def retire(slot):
        for k in range(TOP_K):
            _token_copy(ring.at[slot], xs_ref, 0, 0, sems.at[slot], tm).wait()

    def step(slot):
        pl.when(i >= 2)(functools.partial(retire, slot))
        ring[slot] = h_ref[...]

        def issue(t, c):
            for k in range(TOP_K):
                _token_copy(ring.at[slot], xs_ref, t, pos_ref[k * tm + t], sems.at[slot]).start(priority=k % 2)
            return c

        lax.fori_loop(0, tm, issue, 0)

        @pl.when(i == last)
        def _():
            retire(slot)
            pl.when(i >= 1)(functools.partial(retire, 1 - slot))

    for slot in range(2):
        pl.when(i % 2 == slot)(functools.partial(step, slot))

    @pl.when(pl.program_id(0) == 0)
    def _():
        lax.fori_loop(0, N_EXPERTS, lambda e, c: (zero_fill(e, True), c)[1], 0)
        lax.fori_loop(n_used_ref[0], nb, lambda blk, c: (zero_tail(blk, True), c)[1], 0)


def _tile_major(pos_t, tm):
    T = pos_t.shape[1]
    return pos_t[:TOP_K].reshape(TOP_K, T // tm, tm).transpose(1, 0, 2).reshape(-1)


def dispatch(ht, pos_t, zstart, zcount, n_used, n_rows, tm=512):
    T = ht.shape[0] // SUBLANES
    tm = min(tm, T)
    grid_spec = pltpu.PrefetchScalarGridSpec(
        num_scalar_prefetch=3,
        grid=(T // tm,),
        in_specs=[pl.BlockSpec((TOP_K * tm,), lambda i, zs, zc, nu: (i,), memory_space=pltpu.SMEM),
                  pl.BlockSpec((tm * SUBLANES, LANES), lambda i, zs, zc, nu: (i, 0))],
        out_specs=pl.BlockSpec(memory_space=pl.ANY),
        scratch_shapes=[pltpu.VMEM((EXPERT_BLOCK // 2 * SUBLANES, LANES), F32),
                        pltpu.VMEM((2, tm * SUBLANES, LANES), F32), pltpu.SemaphoreType.DMA((2,)),
                        pltpu.SemaphoreType.DMA],
    )
    return pl.pallas_call(
        functools.partial(_dispatch_body, tm=tm, nb=n_rows // EXPERT_BLOCK),
        grid_spec=grid_spec,
        out_shape=jax.ShapeDtypeStruct((n_rows * SUBLANES, LANES), F32),
        compiler_params=_params(("arbitrary",)),
        name="dispatch",
    )(zstart, zcount, n_used, _tile_major(pos_t, tm), ht)


def _combine_body(pos_ref, pos_next_ref, gates_ref, h_ref, g2_ref, b2_ref, ys_ref, o_ref, buf, sems, *, tm):
    i = pl.program_id(0)

    def issue(p_ref, s):
        def body(t, c):
            for k in range(TOP_K):
                _token_copy(ys_ref, buf.at[s, k], p_ref[k * tm + t], t, sems.at[s]).start(priority=k % 2)
            return c

        lax.fori_loop(0, tm, body, 0)

    @pl.when(i == 0)
    def _():
        issue(pos_ref, 0)

    def step(slot):
        @pl.when(i + 1 < pl.num_programs(0))
        def _():
            issue(pos_next_ref, 1 - slot)

        for k in range(TOP_K):
            _token_copy(ys_ref, buf.at[slot, k], 0, 0, sems.at[slot], tm).wait()

        gates = gates_ref[...].T
        ffn = gates[:, 0:1] * _load_token_tiles(buf.at[slot, 0], tm)
        for k in range(1, TOP_K):
            ffn = ffn + gates[:, k:k + 1] * _load_token_tiles(buf.at[slot, k], tm)
        o_ref[...] = _layer_norm(DEEPNORM_ALPHA * h_ref[...] + ffn, g2_ref[...], b2_ref[...])

    for slot in range(2):
        pl.when(i % 2 == slot)(functools.partial(step, slot))


def combine(ys, pos_t, gates_t, h2, g2, b2, tm=256):
    T = h2.shape[0]
    tm = min(tm, T)
    last = T // tm - 1
    pos_flat = _tile_major(pos_t, tm)
    return pl.pallas_call(
        functools.partial(_combine_body, tm=tm),
        grid=(T // tm,),
        in_specs=[pl.BlockSpec((TOP_K * tm,), lambda i: (i,), memory_space=pltpu.SMEM),
                  pl.BlockSpec((TOP_K * tm,), lambda i: (jnp.minimum(i + 1, last),), memory_space=pltpu.SMEM),
                  pl.BlockSpec((SUBLANES, tm), lambda i: (0, i)),
                  pl.BlockSpec((tm, D_MODEL), lambda i: (i, 0)),
                  pl.BlockSpec((1, D_MODEL), lambda i: (0, 0)),
                  pl.BlockSpec((1, D_MODEL), lambda i: (0, 0)),
                  pl.BlockSpec(memory_space=pl.ANY)],
        out_specs=pl.BlockSpec((tm, D_MODEL), lambda i: (i, 0)),
        out_shape=jax.ShapeDtypeStruct((T, D_MODEL), F32),
        scratch_shapes=[pltpu.VMEM((2, TOP_K, tm * SUBLANES, LANES), F32), pltpu.SemaphoreType.DMA((2,))],
        compiler_params=_params(("arbitrary",)),
        name="combine",
    )(pos_flat, pos_flat, gates_t, h2, g2, b2, ys)


def _experts_body(blk_e, blk_src, n_used, next_e, wslot, xs_ref, wgu_hbm, bgu_ref, wdn_hbm, bdn_ref, ys_ref,
                  wgu_f, wdn_f, wgu_b, wdn_b, sems):
    del blk_src
    i = pl.program_id(0)
    used = i < n_used[0]
    e = blk_e[i]

    def weight_copies(expert, slot):
        return (pltpu.make_async_copy(wgu_hbm.at[expert], wgu_f.at[slot], sems.at[slot]),
                pltpu.make_async_copy(wdn_hbm.at[expert], wdn_f.at[slot], sems.at[slot]))

    @pl.when(i == 0)
    def _():
        for cp in weight_copies(e, 0):
            cp.start()

    @pl.when((i == 0) | (e != blk_e[jnp.maximum(i - 1, 0)]))
    def _():
        nxt = next_e[e]
        for slot in range(2):
            @pl.when(wslot[e] == slot)
            def _():
                for cp in weight_copies(e, slot):
                    cp.wait()
                wgu_b[...] = wgu_f[slot].astype(BF16)
                wdn_b[...] = wdn_f[slot].astype(BF16)

                @pl.when(nxt < N_EXPERTS)
                def _():
                    for cp in weight_copies(nxt, 1 - slot):
                        cp.start()

    @pl.when(used)
    def _():
        x = _load_token_tiles(xs_ref, EXPERT_BLOCK).astype(BF16)
        hgu = jnp.dot(x, wgu_b[...], preferred_element_type=F32) + bgu_ref[...]
        gate = jnp.minimum(hgu[:, :D_FF], SWIGLU_LIMIT)
        up = jnp.clip(hgu[:, D_FF:], -SWIGLU_LIMIT, SWIGLU_LIMIT)
        act = (up + 1.0) * gate * _sigmoid(SWIGLU_ALPHA * gate)
        y = jnp.dot(act.astype(BF16), wdn_b[...], preferred_element_type=F32) + bdn_ref[...]
        _store_token_tiles(ys_ref, y, EXPERT_BLOCK)

    @pl.when(jnp.logical_not(used))
    def _():
        ys_ref[...] = jnp.zeros_like(ys_ref)


def experts(xs, blk_e, blk_src, n_used, next_e, wslot, wgu, bgu, wdn, bdn):
    n_rows = xs.shape[0] // SUBLANES
    nb = n_rows // EXPERT_BLOCK
    rows = EXPERT_BLOCK * SUBLANES
    grid_spec = pltpu.PrefetchScalarGridSpec(
        num_scalar_prefetch=5,
        grid=(nb,),
        in_specs=[pl.BlockSpec((rows, LANES), lambda i, e, s, n, ne, ws: (s[i], 0)),
                  pl.BlockSpec(memory_space=pl.ANY),
                  pl.BlockSpec((None, 1, 2 * D_FF), lambda i, e, s, n, ne, ws: (e[i], 0, 0)),
                  pl.BlockSpec(memory_space=pl.ANY),
                  pl.BlockSpec((None, 1, D_MODEL), lambda i, e, s, n, ne, ws: (e[i], 0, 0))],
        out_specs=pl.BlockSpec((rows, LANES), lambda i, e, s, n, ne, ws: (i, 0)),
        scratch_shapes=[pltpu.VMEM((2, D_MODEL, 2 * D_FF), F32), pltpu.VMEM((2, D_FF, D_MODEL), F32),
                        pltpu.VMEM((D_MODEL, 2 * D_FF), BF16), pltpu.VMEM((D_FF, D_MODEL), BF16),
                        pltpu.SemaphoreType.DMA((2,))],
    )
    return pl.pallas_call(
        _experts_body,
        grid_spec=grid_spec,
        out_shape=jax.ShapeDtypeStruct((n_rows * SUBLANES, LANES), F32),
        compiler_params=_params(("arbitrary",)),
        name="experts",
    )(blk_e, blk_src, n_used, next_e, wslot, xs, wgu, bgu, wdn, bdn)


def routing_layout(ids, rank, counts, n_blocks):
    padded = (counts + EXPERT_BLOCK - 1) // EXPERT_BLOCK * EXPERT_BLOCK
    pad_ends = jnp.cumsum(padded)
    pad_starts = pad_ends - padded
    start_of = jnp.zeros_like(ids)
    for e in range(N_EXPERTS):
        start_of = jnp.where(ids == e, pad_starts[e], start_of)
    pos_t = (start_of + rank).astype(jnp.int32)
    n_used = jnp.maximum(pad_ends[-1] // EXPERT_BLOCK, 1).astype(jnp.int32)
    blk = jnp.minimum(jnp.arange(n_blocks, dtype=jnp.int32), n_used - 1)
    blk_e = jnp.sum(pad_ends[None, :] <= (blk * EXPERT_BLOCK)[:, None], axis=1)
    blk_e = jnp.minimum(blk_e, N_EXPERTS - 1).astype(jnp.int32)
    zstart = (pad_starts + counts).astype(jnp.int32)
    zcount = (padded - counts).astype(jnp.int32)
    nonempty = counts > 0
    expert_ids = jnp.arange(N_EXPERTS, dtype=jnp.int32)
    later = jnp.where(nonempty[None, :] & (expert_ids[None, :] > expert_ids[:, None]), expert_ids[None, :], N_EXPERTS)
    next_e = jnp.min(later, axis=1).astype(jnp.int32)
    wslot = ((jnp.cumsum(nonempty) - 1) % 2).astype(jnp.int32)
    return pos_t, blk_e, blk, n_used.reshape(1), zstart, zcount, next_e, wslot


def kernel(x, w_in, rel_bias, w_dw, b_dw, conv_ln_g, conv_ln_b, w_o_attn, w_o_conv, w_out, ln1_g, ln1_b,
           w_router, b_router, w_gate_up, b_gate_up, w_down, b_down, ln2_g, ln2_b):
    B, S, D = x.shape
    T = B * S
    h = x
    bm = bias_tables(rel_bias)
    q_off, k_off, v_off, rest = 0, QKV_WIDTH, 2 * QKV_WIDTH, 3 * QKV_WIDTH
    for l in range(DEPTH):
        wb = w_in[l].astype(BF16)

        def group_cols(gi):
            return [wb[:, off + gi * GROUP_WIDTH:off + (gi + 1) * GROUP_WIDTH] for off in (q_off, k_off, v_off)]

        w_main = jnp.concatenate(group_cols(0) + [wb[:, rest:]], axis=1)
        proj, *x_dilated = in_proj_dilate(h, w_main, "in_proj_main")
        x_dil = [None] + x_dilated
        o_list, st_list = [], []
        for gi in range(N_GROUPS):
            dil = ATT_GROUPS[gi][1]
            if dil == 1:
                qkv = proj.reshape(B, 1, S, proj.shape[-1])
            else:
                w_g = jnp.concatenate(group_cols(gi), axis=1)
                qkv = in_proj(x_dil[gi].reshape(T, D), w_g, w_g.shape[1], f"in_proj_g{gi}")
                qkv = qkv.reshape(B, dil, S // dil, w_g.shape[1])
            o_g, st_g = attention_group(qkv, bm[gi], gi)
            o_list.append(o_g)
            st_list.append(st_g)
        h1, h1_tiles = mixer_out(o_list, st_list, proj, h,
                       w_dw[l].reshape(CONV_WIDTH, CONV_CH), b_dw[l].reshape(1, CONV_CH),
                       conv_ln_g[l].reshape(1, CONV_CH), conv_ln_b[l].reshape(1, CONV_CH),
                       w_o_attn[l].astype(BF16), w_o_conv[l].astype(BF16), w_out[l].astype(BF16),
                       ln1_g[l].reshape(1, D), ln1_b[l].reshape(1, D))
        h2 = h1.reshape(T, D)
        gates_t, ids_t, rank_t, counts = router(h2, w_router[l], b_router[l])
        n_rows = T * TOP_K + N_EXPERTS * EXPERT_BLOCK
        pos_t, blk_e, blk_src, n_used, zstart, zcount, next_e, wslot = routing_layout(
            ids_t, rank_t, counts[:, 0], n_rows // EXPERT_BLOCK)
        xs = dispatch(h1_tiles, pos_t, zstart, zcount, n_used, n_rows)
        ys = experts(xs, blk_e, blk_src, n_used, next_e, wslot,
                     w_gate_up[l], b_gate_up[l].reshape(N_EXPERTS, 1, 2 * D_FF),
                     w_down[l], b_down[l].reshape(N_EXPERTS, 1, D))
        out = combine(ys, pos_t, gates_t, h2, ln2_g[l].reshape(1, D), ln2_b[l].reshape(1, D))
        h = out.reshape(B, S, D)
    return h
```

```python
import functools
import math

import jax
import jax.numpy as jnp
from jax import lax
from jax.experimental import pallas as pl
from jax.experimental.pallas import tpu as pltpu

D_MODEL = 1024
ATT_GROUPS = ((128, 1), (512, 4), (2048, 16))
N_GROUPS = len(ATT_GROUPS)
HEADS_PER_GROUP = 8
HEAD_DIM = 64
GROUP_WIDTH = HEADS_PER_GROUP * HEAD_DIM
QKV_WIDTH = N_GROUPS * GROUP_WIDTH
ATT_BLOCK = 128
N_BUCKETS = 32
MAX_DISTANCE = 2048
CONV_CH = 768
CONV_WIDTH = 31
CONV_HALO = 32
N_EXPERTS = 32
TOP_K = 4
D_FF = 1024
SWIGLU_LIMIT = 7.0
SWIGLU_ALPHA = 1.702
IN_WIDTH = 3 * QKV_WIDTH + 2 * CONV_CH + 2 * D_MODEL
LN_EPS = 1e-5
NEG_INF = -1e30
DEPTH = 1
DEEPNORM_ALPHA = (2 * DEPTH) ** 0.25

LANES = 128
SUBLANES = 8
EXPERT_BLOCK = 512
VMEM_LIMIT = 56 * 1024 * 1024

F32 = jnp.float32
BF16 = jnp.bfloat16


def _params(sem, vmem=VMEM_LIMIT):
    return pltpu.CompilerParams(dimension_semantics=sem, vmem_limit_bytes=vmem)


def _sigmoid(x):
    return 0.5 * jnp.tanh(0.5 * x) + 0.5


def _layer_norm(x, g, b):
    mu = jnp.mean(x, axis=-1, keepdims=True)
    xc = x - mu
    var = jnp.mean(xc * xc, axis=-1, keepdims=True)
    return xc * lax.rsqrt(var + LN_EPS) * g + b


def _in_proj_body(x_ref, w_ref, o_ref):
    o_ref[...] = jnp.dot(x_ref[...], w_ref[...], preferred_element_type=F32).astype(o_ref.dtype)


def in_proj(xb, w_b, tn, name, tm=1024):
    T = xb.shape[0]
    N = w_b.shape[1]
    tm = min(tm, T)
    return pl.pallas_call(
        _in_proj_body,
        grid=(N // tn, T // tm),
        in_specs=[pl.BlockSpec((tm, D_MODEL), lambda n, m: (m, 0)),
                  pl.BlockSpec((D_MODEL, tn), lambda n, m: (0, n))],
        out_specs=pl.BlockSpec((tm, tn), lambda n, m: (m, n)),
        out_shape=jax.ShapeDtypeStruct((T, N), BF16),
        compiler_params=_params(("arbitrary", "arbitrary")),
        name=name,
    )(xb, w_b)


QKV_COLS = 3 * GROUP_WIDTH


def _in_proj_dilate_body(*refs, tm, nslab):
    x_slabs, w_ref = refs[:nslab], refs[nslab]
    qkv_ref, glu_ref, gate_ref = refs[nslab + 1:nslab + 4]
    d_refs = refs[nslab + 4:]
    cols = []
    for c, x_ref in enumerate(x_slabs):
        cs = slice(c * LANES, (c + 1) * LANES)
        cols.append(x_ref[...].astype(BF16))
        for (_, dil), d_ref in zip(ATT_GROUPS[1:], d_refs):
            for r in range(dil):
                d_ref[r, :, cs] = x_ref[pl.ds(r, tm // dil, stride=dil), :].astype(BF16)
    xb = jnp.concatenate(cols, axis=1)
    proj = jnp.dot(xb, w_ref[...], preferred_element_type=F32)
    qkv_ref[...] = proj[:, :QKV_COLS].astype(qkv_ref.dtype)
    u_val = proj[:, QKV_COLS:QKV_COLS + CONV_CH]
    u_gate = proj[:, QKV_COLS + CONV_CH:QKV_COLS + 2 * CONV_CH]
    glu_ref[...] = (u_val * _sigmoid(u_gate)).astype(glu_ref.dtype)
    gate_ref[...] = _sigmoid(proj[:, QKV_COLS + 2 * CONV_CH:]).astype(gate_ref.dtype)


def in_proj_dilate(x3, w_b, name, tm=512):
    B, S, D = x3.shape
    N = w_b.shape[1]
    tm = min(tm, S)
    nslab = D // LANES
    dils = [dil for _, dil in ATT_GROUPS[1:]]
    per_seq = S // tm
    widths = (QKV_COLS, CONV_CH, 2 * D_MODEL)
    assert N == QKV_COLS + 2 * CONV_CH + 2 * D_MODEL
    return pl.pallas_call(
        functools.partial(_in_proj_dilate_body, tm=tm, nslab=nslab),
        grid=(B, per_seq),
        in_specs=[pl.BlockSpec((None, tm, LANES), functools.partial(lambda c, b, i: (b, i, c), c))
                  for c in range(nslab)]
        + [pl.BlockSpec((D, N), lambda b, i: (0, 0))],
        out_specs=[pl.BlockSpec((None, tm, w), lambda b, i: (b, i, 0)) for w in widths]
        + [pl.BlockSpec((None, dil, tm // dil, D), lambda b, i: (b, 0, i, 0)) for dil in dils],
        out_shape=[jax.ShapeDtypeStruct((B, S, w), BF16) for w in widths]
        + [jax.ShapeDtypeStruct((B, dil, S // dil, D), BF16) for dil in dils],
        compiler_params=_params(("arbitrary", "arbitrary")),
        name=name,
    )(*([x3] * nslab), w_b)


def _t5_bucket(dist):
    max_exact = N_BUCKETS // 2
    log_ratio = jnp.log(jnp.maximum(dist, max_exact).astype(F32) / max_exact) / math.log(MAX_DISTANCE / max_exact)
    large = jnp.minimum(max_exact + (log_ratio * (N_BUCKETS - max_exact)).astype(jnp.int32), N_BUCKETS - 1)
    return jnp.where(dist < max_exact, dist, large)


def _bias_body(tbl_ref, bucket_ref, band_ref, o_ref):
    g = pl.program_id(0)
    h = pl.program_id(1)
    col = g * HEADS_PER_GROUP + h
    bucket = bucket_ref[...]
    acc = jnp.zeros(bucket.shape, F32)
    for k in range(N_BUCKETS):
        acc = jnp.where(bucket == k, tbl_ref[k, col], acc)
    band = band_ref[...] != 0
    kj = lax.broadcasted_iota(jnp.int32, bucket.shape, 1)
    o_ref[0] = jnp.where(band, acc, NEG_INF)
    o_ref[1] = jnp.where(band & (kj >= ATT_BLOCK), acc, NEG_INF)


def bias_tables(rel_bias):
    qi = jnp.arange(ATT_BLOCK)[:, None]
    kj = jnp.arange(2 * ATT_BLOCK)[None, :]
    dist = qi - kj + ATT_BLOCK
    buckets, bands = [], []
    for window, dil in ATT_GROUPS:
        bands.append(((dist >= 0) & (dist <= window // dil)).astype(jnp.int32))
        buckets.append(_t5_bucket(jnp.maximum(dist, 0) * dil).astype(jnp.int32))
    buckets = jnp.stack(buckets)
    bands = jnp.stack(bands)
    blk = (None, ATT_BLOCK, 2 * ATT_BLOCK)
    return pl.pallas_call(
        _bias_body,
        grid=(N_GROUPS, HEADS_PER_GROUP),
        in_specs=[pl.BlockSpec(memory_space=pltpu.SMEM),
                  pl.BlockSpec(blk, lambda g, h: (g, 0, 0)),
                  pl.BlockSpec(blk, lambda g, h: (g, 0, 0))],
        out_specs=pl.BlockSpec((None, None, 2, ATT_BLOCK, 2 * ATT_BLOCK), lambda g, h: (g, h, 0, 0, 0)),
        out_shape=jax.ShapeDtypeStruct((N_GROUPS, HEADS_PER_GROUP, 2, ATT_BLOCK, 2 * ATT_BLOCK), F32),
        compiler_params=_params(("arbitrary", "arbitrary")),
        name="bias_tables",
    )(rel_bias.astype(F32), buckets, bands)


def _attn_body(q_ref, kc_ref, kp_ref, vc_ref, vp_ref, bm_ref, o_ref, st_ref, k_all, v_all, s_buf, *, tq):
    i = pl.program_id(2)
    nsub = tq // ATT_BLOCK
    k_all[0:ATT_BLOCK] = kp_ref[...]
    k_all[ATT_BLOCK:] = kc_ref[...]
    v_all[0:ATT_BLOCK] = vp_ref[...]
    v_all[ATT_BLOCK:] = vc_ref[...]
    lo = lax.broadcasted_iota(jnp.int32, (ATT_BLOCK, LANES), 1) < HEAD_DIM
    nt_dims = (((1,), (1,)), ((), ()))
    first = jnp.where(i == 0, 1, 0)

    def scores(s, slot):
        r0 = s * ATT_BLOCK
        for j in range(HEADS_PER_GROUP // 2):
            cs = slice(j * LANES, (j + 1) * LANES)
            qp = q_ref[r0:r0 + ATT_BLOCK, cs] * jnp.asarray(HEAD_DIM ** -0.5, BF16)
            kp = k_all[r0:r0 + 2 * ATT_BLOCK, cs]
            for hh in range(2):
                h = 2 * j + hh
                qh = jnp.where(lo if hh == 0 else ~lo, qp, jnp.zeros_like(qp))
                bias = bm_ref[h, first] if s == 0 else bm_ref[h, 0]
                s_buf[slot, h] = lax.dot_general(qh, kp, nt_dims, preferred_element_type=F32) + bias

    def softmax_pv(s, slot):
        rows = slice(s * ATT_BLOCK, (s + 1) * ATT_BLOCK)
        st_ref[rows, :] = jnp.zeros((ATT_BLOCK, LANES), F32)
        for j in range(HEADS_PER_GROUP // 2):
            vp = v_all[s * ATT_BLOCK:(s + 2) * ATT_BLOCK, j * LANES:(j + 1) * LANES]
            for hh in range(2):
                h = 2 * j + hh
                sc = s_buf[slot, h]
                m = jnp.max(sc, axis=-1, keepdims=True)
                p = jnp.exp(sc - m)
                den = jnp.sum(p, axis=-1, keepdims=True)
                pv = jnp.dot(p.astype(BF16), vp, preferred_element_type=F32) * (1.0 / den)
                c0 = h * HEAD_DIM
                o_ref[rows, c0:c0 + HEAD_DIM] = pv[:, hh * HEAD_DIM:(hh + 1) * HEAD_DIM]
                st_ref[rows, h:h + 1] = m
                st_ref[rows, HEADS_PER_GROUP + h:HEADS_PER_GROUP + h + 1] = den

    scores(0, 0)
    for s in range(nsub):
        softmax_pv(s, s % 2)
        if s + 1 < nsub:
            scores(s + 1, (s + 1) % 2)


def attention_group(qkv, bm_g, gi, tq=512):
    B, dil, L, _ = qkv.shape
    tq = min(tq, L)
    sub = tq // ATT_BLOCK

    def cur(col):
        return pl.BlockSpec((None, None, tq, GROUP_WIDTH), lambda b, r, i: (b, r, i, col))

    def prev(col):
        return pl.BlockSpec((None, None, ATT_BLOCK, GROUP_WIDTH),
                            lambda b, r, i: (b, r, jnp.maximum(i * sub - 1, 0), col))

    return pl.pallas_call(
        functools.partial(_attn_body, tq=tq),
        grid=(B, dil, L // tq),
        in_specs=[cur(0), cur(1), prev(1), cur(2), prev(2),
                  pl.BlockSpec((HEADS_PER_GROUP, 2, ATT_BLOCK, 2 * ATT_BLOCK), lambda b, r, i: (0, 0, 0, 0))],
        out_specs=[pl.BlockSpec((None, None, tq, GROUP_WIDTH), lambda b, r, i: (b, r, i, 0)),
                   pl.BlockSpec((None, None, tq, LANES), lambda b, r, i: (b, r, i, 0))],
        out_shape=[jax.ShapeDtypeStruct((B, dil, L, GROUP_WIDTH), F32),
                   jax.ShapeDtypeStruct((B, dil, L, LANES), F32)],
        scratch_shapes=[pltpu.VMEM((ATT_BLOCK + tq, GROUP_WIDTH), BF16),
                        pltpu.VMEM((ATT_BLOCK + tq, GROUP_WIDTH), BF16),
                        pltpu.VMEM((2, HEADS_PER_GROUP, ATT_BLOCK, 2 * ATT_BLOCK), F32)],
        compiler_params=_params(("arbitrary", "arbitrary", "arbitrary")),
        name=f"attention_g{gi}",
    )(qkv, qkv, qkv, qkv, qkv, bm_g)


def _split_bf16(x):
    hi = x.astype(BF16)
    lo = (x - hi.astype(F32)).astype(BF16)
    return hi, lo


def _load_token_tiles(ref, n):
    return jnp.concatenate([ref[pl.ds(c, n, stride=SUBLANES), :] for c in range(D_MODEL // LANES)], axis=1)


def _store_token_tiles(ref, x, n):
    for c in range(D_MODEL // LANES):
        ref[pl.ds(c, n, stride=SUBLANES), :] = x[:, c * LANES:(c + 1) * LANES]


def _to_token_order(blk_ref, tok_ref, dil, tm):
    n = tm // dil
    for r in range(dil):
        for c in range(tok_ref.shape[0]):
            tok_ref[c, pl.ds(r, n, stride=dil), :] = blk_ref[r, :, c * LANES:(c + 1) * LANES]


def _mixer_out_body(o0, o1, o2, s0, s1, s2, glu_in, glu_halo, ga_ref, gc_ref, x_ref,
                    wdw_ref, bdw_ref, cg_ref, cb_ref, woa_ref, woc_ref, wout_ref, g1_ref, b1_ref,
                    h_ref, ht_ref, glu_ref, dw_ref, shift_ref, tok_o1, tok_s1, tok_o2, tok_s2, *, tm, chunk):
    i = pl.program_id(1)

    ncol = GROUP_WIDTH // LANES
    _to_token_order(o1, tok_o1, ATT_GROUPS[1][1], tm)
    _to_token_order(s1, tok_s1, ATT_GROUPS[1][1], tm)
    _to_token_order(o2, tok_o2, ATT_GROUPS[2][1], tm)
    _to_token_order(s2, tok_s2, ATT_GROUPS[2][1], tm)
    outs = [o0[0],
            jnp.concatenate([tok_o1[c] for c in range(ncol)], axis=1),
            jnp.concatenate([tok_o2[c] for c in range(ncol)], axis=1)]
    sts = [s0[0], tok_s1[0], tok_s2[0]]

    mx = jnp.maximum(jnp.maximum(sts[0], sts[1]), sts[2])
    wts = [pltpu.roll(st, LANES - HEADS_PER_GROUP, axis=1) * jnp.exp(st - mx) for st in sts]
    wsum = wts[0] + wts[1] + wts[2]
    row = lax.broadcasted_iota(jnp.int32, (LANES, GROUP_WIDTH), 0)
    colh = lax.broadcasted_iota(jnp.int32, (LANES, GROUP_WIDTH), 1) // HEAD_DIM
    expand = (row == colh).astype(BF16)
    attn = jnp.zeros((tm, GROUP_WIDTH), F32)
    head_lane = lax.broadcasted_iota(jnp.int32, (tm, LANES), 1) < HEADS_PER_GROUP
    for wt, o in zip(wts, outs):
        c_hi, c_lo = _split_bf16(jnp.where(head_lane, wt / wsum, 0.0))
        c = (jnp.dot(c_hi, expand, preferred_element_type=F32)
             + jnp.dot(c_lo, expand, preferred_element_type=F32))
        attn = attn + c * o
    a_out = jnp.dot(attn.astype(BF16), woa_ref[...], preferred_element_type=F32)

    glu_ref[0:CONV_HALO] = jnp.where(i == 0, 0.0, glu_halo[...].astype(F32))
    glu_ref[CONV_HALO:] = glu_in[...].astype(F32)
    first_tap = CONV_HALO - (CONV_WIDTH - 1)
    for b in range(1, SUBLANES):
        shift_ref[b - 1] = glu_ref[b:b + shift_ref.shape[1], :]

    for c0 in range(0, CONV_CH, LANES):
        cs = slice(c0, c0 + LANES)
        bias = jnp.broadcast_to(bdw_ref[:, cs], (chunk, LANES))
        for r0 in range(0, tm, chunk):
            acc = bias
            for j in range(CONV_WIDTH):
                a, b = divmod(first_tap + j, SUBLANES)
                lo_row = r0 + a * SUBLANES
                rows = glu_ref[lo_row:lo_row + chunk, cs] if b == 0 else shift_ref[b - 1, lo_row:lo_row + chunk, cs]
                acc = acc + wdw_ref[j:j + 1, cs] * rows
            dw_ref[r0:r0 + chunk, cs] = acc
    cn = _layer_norm(dw_ref[...], cg_ref[...], cb_ref[...])
    conv = cn * _sigmoid(cn)
    c_out = jnp.dot(conv.astype(BF16), woc_ref[...], preferred_element_type=F32)

    merged = ga_ref[...].astype(F32) * a_out + gc_ref[...].astype(F32) * c_out
    mix = jnp.dot(merged.astype(BF16), wout_ref[...], preferred_element_type=F32)
    h = _layer_norm(DEEPNORM_ALPHA * x_ref[...] + mix, g1_ref[...], b1_ref[...])
    h_ref[...] = h
    _store_token_tiles(ht_ref, h, tm)


def mixer_out(o_list, st_list, glu3, gates3, x3, w_dw, b_dw, cg, cb, woa_b, woc_b, wout_b, g1, b1, tm=512, chunk=64):
    B, S, _ = x3.shape
    tm = min(tm, S)
    halo_blocks = tm // CONV_HALO
    ncol = GROUP_WIDTH // LANES

    def tile(width, col=0):
        return pl.BlockSpec((None, tm, width), lambda b, i: (b, i, col))

    def dilated(gi, width):
        dil = ATT_GROUPS[gi][1]
        return pl.BlockSpec((None, dil, tm // dil, width), lambda b, i: (b, 0, i, 0))

    halo = pl.BlockSpec((None, CONV_HALO, CONV_CH), lambda b, i: (b, jnp.maximum(i * halo_blocks - 1, 0), 0))

    def whole(shape):
        return pl.BlockSpec(shape, lambda b, i: (0,) * len(shape))

    return pl.pallas_call(
        functools.partial(_mixer_out_body, tm=tm, chunk=chunk),
        grid=(B, S // tm),
        in_specs=[dilated(g, GROUP_WIDTH) for g in range(N_GROUPS)] + [dilated(g, LANES) for g in range(N_GROUPS)]
        + [tile(CONV_CH), halo, tile(D_MODEL, 0), tile(D_MODEL, 1), tile(D_MODEL),
           whole((CONV_WIDTH, CONV_CH)), whole((1, CONV_CH)), whole((1, CONV_CH)), whole((1, CONV_CH)),
           whole((GROUP_WIDTH, D_MODEL)), whole((CONV_CH, D_MODEL)), whole((D_MODEL, D_MODEL)),
           whole((1, D_MODEL)), whole((1, D_MODEL))],
        out_specs=[tile(D_MODEL), pl.BlockSpec((tm * SUBLANES, LANES), lambda b, i: (b * (S // tm) + i, 0))],
        out_shape=[jax.ShapeDtypeStruct((B, S, D_MODEL), F32),
                   jax.ShapeDtypeStruct((B * S * SUBLANES, LANES), F32)],
        scratch_shapes=[pltpu.VMEM((CONV_HALO + tm, CONV_CH), F32), pltpu.VMEM((tm, CONV_CH), F32),
                        pltpu.VMEM((SUBLANES - 1, tm + CONV_HALO - SUBLANES, CONV_CH), F32),
                        pltpu.VMEM((ncol, tm, LANES), F32), pltpu.VMEM((1, tm, LANES), F32),
                        pltpu.VMEM((ncol, tm, LANES), F32), pltpu.VMEM((1, tm, LANES), F32)],
        compiler_params=_params(("arbitrary", "arbitrary")),
        name="mixer_out",
    )(*o_list, *st_list, glu3, glu3, gates3, gates3, x3,
      w_dw, b_dw, cg, cb, woa_b, woc_b, wout_b, g1, b1)


def _router_body(h_ref, wt_ref, b_ref, gates_ref, ids_ref, rank_ref, cnt_ref, carry_ref, tri_ref, *, tm):
    step = pl.program_id(0)

    @pl.when(step == 0)
    def _():
        carry_ref[...] = jnp.zeros_like(carry_ref)
        r_i = lax.broadcasted_iota(jnp.int32, (tm, tm), 0)
        c_i = lax.broadcasted_iota(jnp.int32, (tm, tm), 1)
        tri_ref[...] = (r_i < c_i).astype(BF16)

    nt_dims = (((1,), (1,)), ((), ()))
    h_hi, h_lo = _split_bf16(h_ref[...])
    w_hi, w_lo = _split_bf16(wt_ref[...])
    logits = (lax.dot_general(w_hi, h_hi, nt_dims, preferred_element_type=F32)
              + (lax.dot_general(w_hi, h_lo, nt_dims, preferred_element_type=F32)
                 + lax.dot_general(w_lo, h_hi, nt_dims, preferred_element_type=F32))
              + b_ref[...])
    expert = lax.broadcasted_iota(jnp.int32, (N_EXPERTS, tm), 0).astype(F32)
    row = lax.broadcasted_iota(jnp.int32, (SUBLANES, tm), 0)

    work = logits
    vals, hots = [], []
    ids = jnp.zeros((SUBLANES, tm), F32)
    for k in range(TOP_K):
        v = jnp.max(work, axis=0, keepdims=True)
        idx = jnp.min(jnp.where(work == v, expert, float(N_EXPERTS)), axis=0, keepdims=True)
        hot = expert == idx
        work = jnp.where(hot, -jnp.inf, work)
        ids = jnp.where(row == k, idx, ids)
        vals.append(v)
        hots.append(hot)

    es = [jnp.exp(v - vals[0]) for v in vals]
    esum = es[0] + es[1] + es[2] + es[3]
    gates = jnp.zeros((SUBLANES, tm), F32)
    for k in range(TOP_K):
        gates = jnp.where(row == k, es[k] / esum, gates)

    sel = (hots[0] | hots[1] | hots[2] | hots[3])
    before = jnp.dot(sel.astype(BF16), tri_ref[...], preferred_element_type=F32) + carry_ref[:, 0:1]
    rank = jnp.zeros((SUBLANES, tm), F32)
    for k in range(TOP_K):
        rk = jnp.sum(jnp.where(hots[k], before, 0.0), axis=0, keepdims=True)
        rank = jnp.where(row == k, rk, rank)
    carry_ref[...] = carry_ref[...] + jnp.sum(sel.astype(F32), axis=1, keepdims=True)

    gates_ref[...] = gates
    ids_ref[...] = ids.astype(jnp.int32)
    rank_ref[...] = rank.astype(jnp.int32)
    cnt_ref[...] = carry_ref[...].astype(jnp.int32)


def router(h2, w_router, b_router, tm=512):
    T = h2.shape[0]
    tm = min(tm, T)
    tile = pl.BlockSpec((SUBLANES, tm), lambda i: (0, i))
    return pl.pallas_call(
        functools.partial(_router_body, tm=tm),
        grid=(T // tm,),
        in_specs=[pl.BlockSpec((tm, D_MODEL), lambda i: (i, 0)),
                  pl.BlockSpec((N_EXPERTS, D_MODEL), lambda i: (0, 0)),
                  pl.BlockSpec((N_EXPERTS, 1), lambda i: (0, 0))],
        out_specs=[tile, tile, tile, pl.BlockSpec((N_EXPERTS, LANES), lambda i: (0, 0))],
        out_shape=[jax.ShapeDtypeStruct((SUBLANES, T), F32), jax.ShapeDtypeStruct((SUBLANES, T), jnp.int32),
                   jax.ShapeDtypeStruct((SUBLANES, T), jnp.int32), jax.ShapeDtypeStruct((N_EXPERTS, LANES), jnp.int32)],
        scratch_shapes=[pltpu.VMEM((N_EXPERTS, LANES), F32), pltpu.VMEM((tm, tm), BF16)],
        compiler_params=_params(("arbitrary",)),
        name="router",
    )(h2, w_router.astype(F32).T, b_router.astype(F32).reshape(N_EXPERTS, 1))


def _token_copy(src, dst, s_tok, d_tok, sem, n=1):
    rows = n * SUBLANES
    return pltpu.make_async_copy(src.at[pl.ds(pl.multiple_of(s_tok * SUBLANES, SUBLANES), rows), :],
                                 dst.at[pl.ds(pl.multiple_of(d_tok * SUBLANES, SUBLANES), rows), :], sem)


def _dispatch_body(zstart_ref, zcount_ref, n_used_ref, pos_ref, h_ref, xs_ref, zeros_ref, ring, sems, zsem, *, tm, nb):
    half = EXPERT_BLOCK // 2
    bits = [1 << s for s in reversed(range(half.bit_length()))]

    def zero_fill(e, wait):
        start, count = zstart_ref[e], zcount_ref[e]
        for bit in bits:
            @pl.when((count & bit) != 0)
            def _():
                cp = _token_copy(zeros_ref, xs_ref, 0, 0 if wait else start + (count & ~(2 * bit - 1)), zsem, bit)
                cp.wait() if wait else cp.start()

    def zero_tail(blk, wait):
        for part in range(2):
            cp = _token_copy(zeros_ref, xs_ref, 0, 0 if wait else blk * EXPERT_BLOCK + part * half, zsem, half)
            cp.wait() if wait else cp.start()

    @pl.when(pl.program_id(0) == 0)
    def _():
        zeros_ref[...] = jnp.zeros_like(zeros_ref)
        lax.fori_loop(0, N_EXPERTS, lambda e, c: (zero_fill(e, False), c)[1], 0)
        lax.fori_loop(n_used_ref[0], nb, lambda blk, c: (zero_tail(blk, False), c)[1], 0)

    i = pl.program_id(0)
    last = pl.num_programs(0) - 1

    def retire(slot):
        for k in range(TOP_K):
            _token_copy(ring.at[slot], xs_ref, 0, 0, sems.at[slot], tm).wait()

    def step(slot):
        pl.when(i >= 2)(functools.partial(retire, slot))
        ring[slot] = h_ref[...]

        def issue(t, c):
            for k in range(TOP_K):
                _token_copy(ring.at[slot], xs_ref, t, pos_ref[k * tm + t], sems.at[slot]).start(priority=k % 2)
            return c

        lax.fori_loop(0, tm, issue, 0)

        @pl.when(i == last)
        def _():
            retire(slot)
            pl.when(i >= 1)(functools.partial(retire, 1 - slot))

    for slot in range(2):
        pl.when(i % 2 == slot)(functools.partial(step, slot))

    @pl.when(pl.program_id(0) == 0)
    def _():
        lax.fori_loop(0, N_EXPERTS, lambda e, c: (zero_fill(e, True), c)[1], 0)
        lax.fori_loop(n_used_ref[0], nb, lambda blk, c: (zero_tail(blk, True), c)[1], 0)


def _tile_major(pos_t, tm):
    T = pos_t.shape[1]
    return pos_t[:TOP_K].reshape(TOP_K, T // tm, tm).transpose(1, 0, 2).reshape(-1)


def dispatch(ht, pos_t, zstart, zcount, n_used, n_rows, tm=512):
    T = ht.shape[0] // SUBLANES
    tm = min(tm, T)
    grid_spec = pltpu.PrefetchScalarGridSpec(
        num_scalar_prefetch=3,
        grid=(T // tm,),
        in_specs=[pl.BlockSpec((TOP_K * tm,), lambda i, zs, zc, nu: (i,), memory_space=pltpu.SMEM),
                  pl.BlockSpec((tm * SUBLANES, LANES), lambda i, zs, zc, nu: (i, 0))],
        out_specs=pl.BlockSpec(memory_space=pl.ANY),
        scratch_shapes=[pltpu.VMEM((EXPERT_BLOCK // 2 * SUBLANES, LANES), F32),
                        pltpu.VMEM((2, tm * SUBLANES, LANES), F32), pltpu.SemaphoreType.DMA((2,)),
                        pltpu.SemaphoreType.DMA],
    )
    return pl.pallas_call(
        functools.partial(_dispatch_body, tm=tm, nb=n_rows // EXPERT_BLOCK),
        grid_spec=grid_spec,
        out_shape=jax.ShapeDtypeStruct((n_rows * SUBLANES, LANES), F32),
        compiler_params=_params(("arbitrary",)),
        name="dispatch",
    )(zstart, zcount, n_used, _tile_major(pos_t, tm), ht)


def _combine_body(pos_ref, pos_next_ref, gates_ref, h_ref, g2_ref, b2_ref, ys_ref, o_ref, buf, sems, *, tm):
    i = pl.program_id(0)

    def issue(p_ref, s):
        def body(t, c):
            for k in range(TOP_K):
                _token_copy(ys_ref, buf.at[s, k], p_ref[k * tm + t], t, sems.at[s]).start(priority=k % 2)
            return c

        lax.fori_loop(0, tm, body, 0)

    @pl.when(i == 0)
    def _():
        issue(pos_ref, 0)

    def step(slot):
        @pl.when(i + 1 < pl.num_programs(0))
        def _():
            issue(pos_next_ref, 1 - slot)

        for k in range(TOP_K):
            _token_copy(ys_ref, buf.at[slot, k], 0, 0, sems.at[slot], tm).wait()

        gates = gates_ref[...].T
        ffn = gates[:, 0:1] * _load_token_tiles(buf.at[slot, 0], tm)
        for k in range(1, TOP_K):
            ffn = ffn + gates[:, k:k + 1] * _load_token_tiles(buf.at[slot, k], tm)
        o_ref[...] = _layer_norm(DEEPNORM_ALPHA * h_ref[...] + ffn, g2_ref[...], b2_ref[...])

    for slot in range(2):
        pl.when(i % 2 == slot)(functools.partial(step, slot))


def combine(ys, pos_t, gates_t, h2, g2, b2, tm=256):
    T = h2.shape[0]
    tm = min(tm, T)
    last = T // tm - 1
    pos_flat = _tile_major(pos_t, tm)
    return pl.pallas_call(
        functools.partial(_combine_body, tm=tm),
        grid=(T // tm,),
        in_specs=[pl.BlockSpec((TOP_K * tm,), lambda i: (i,), memory_space=pltpu.SMEM),
                  pl.BlockSpec((TOP_K * tm,), lambda i: (jnp.minimum(i + 1, last),), memory_space=pltpu.SMEM),
                  pl.BlockSpec((SUBLANES, tm), lambda i: (0, i)),
                  pl.BlockSpec((tm, D_MODEL), lambda i: (i, 0)),
                  pl.BlockSpec((1, D_MODEL), lambda i: (0, 0)),
                  pl.BlockSpec((1, D_MODEL), lambda i: (0, 0)),
                  pl.BlockSpec(memory_space=pl.ANY)],
        out_specs=pl.BlockSpec((tm, D_MODEL), lambda i: (i, 0)),
        out_shape=jax.ShapeDtypeStruct((T, D_MODEL), F32),
        scratch_shapes=[pltpu.VMEM((2, TOP_K, tm * SUBLANES, LANES), F32), pltpu.SemaphoreType.DMA((2,))],
        compiler_params=_params(("arbitrary",)),
        name="combine",
    )(pos_flat, pos_flat, gates_t, h2, g2, b2, ys)


def _experts_body(blk_e, blk_src, n_used, next_e, wslot, xs_ref, wgu_hbm, bgu_ref, wdn_hbm, bdn_ref, ys_ref,
                  wgu_f, wdn_f, wgu_b, wdn_b, sems):
    del blk_src
    i = pl.program_id(0)
    used = i < n_used[0]
    e = blk_e[i]

    def weight_copies(expert, slot):
        return (pltpu.make_async_copy(wgu_hbm.at[expert], wgu_f.at[slot], sems.at[slot]),
                pltpu.make_async_copy(wdn_hbm.at[expert], wdn_f.at[slot], sems.at[slot]))

    @pl.when(i == 0)
    def _():
        for cp in weight_copies(e, 0):
            cp.start()

    @pl.when((i == 0) | (e != blk_e[jnp.maximum(i - 1, 0)]))
    def _():
        nxt = next_e[e]
        for slot in range(2):
            @pl.when(wslot[e] == slot)
            def _():
                for cp in weight_copies(e, slot):
                    cp.wait()
                wgu_b[...] = wgu_f[slot].astype(BF16)
                wdn_b[...] = wdn_f[slot].astype(BF16)

                @pl.when(nxt < N_EXPERTS)
                def _():
                    for cp in weight_copies(nxt, 1 - slot):
                        cp.start()

    @pl.when(used)
    def _():
        x = _load_token_tiles(xs_ref, EXPERT_BLOCK).astype(BF16)
        hgu = jnp.dot(x, wgu_b[...], preferred_element_type=F32) + bgu_ref[...]
        gate = jnp.minimum(hgu[:, :D_FF], SWIGLU_LIMIT)
        up = jnp.clip(hgu[:, D_FF:], -SWIGLU_LIMIT, SWIGLU_LIMIT)
        act = (up + 1.0) * gate * _sigmoid(SWIGLU_ALPHA * gate)
        y = jnp.dot(act.astype(BF16), wdn_b[...], preferred_element_type=F32) + bdn_ref[...]
        _store_token_tiles(ys_ref, y, EXPERT_BLOCK)

    @pl.when(jnp.logical_not(used))
    def _():
        ys_ref[...] = jnp.zeros_like(ys_ref)


def experts(xs, blk_e, blk_src, n_used, next_e, wslot, wgu, bgu, wdn, bdn):
    n_rows = xs.shape[0] // SUBLANES
    nb = n_rows // EXPERT_BLOCK
    rows = EXPERT_BLOCK * SUBLANES
    grid_spec = pltpu.PrefetchScalarGridSpec(
        num_scalar_prefetch=5,
        grid=(nb,),
        in_specs=[pl.BlockSpec((rows, LANES), lambda i, e, s, n, ne, ws: (s[i], 0)),
                  pl.BlockSpec(memory_space=pl.ANY),
                  pl.BlockSpec((None, 1, 2 * D_FF), lambda i, e, s, n, ne, ws: (e[i], 0, 0)),
                  pl.BlockSpec(memory_space=pl.ANY),
                  pl.BlockSpec((None, 1, D_MODEL), lambda i, e, s, n, ne, ws: (e[i], 0, 0))],
        out_specs=pl.BlockSpec((rows, LANES), lambda i, e, s, n, ne, ws: (i, 0)),
        scratch_shapes=[pltpu.VMEM((2, D_MODEL, 2 * D_FF), F32), pltpu.VMEM((2, D_FF, D_MODEL), F32),
                        pltpu.VMEM((D_MODEL, 2 * D_FF), BF16), pltpu.VMEM((D_FF, D_MODEL), BF16),
                        pltpu.SemaphoreType.DMA((2,))],
    )
    return pl.pallas_call(
        _experts_body,
        grid_spec=grid_spec,
        out_shape=jax.ShapeDtypeStruct((n_rows * SUBLANES, LANES), F32),
        compiler_params=_params(("arbitrary",)),
        name="experts",
    )(blk_e, blk_src, n_used, next_e, wslot, xs, wgu, bgu, wdn, bdn)


def routing_layout(ids, rank, counts, n_blocks):
    padded = (counts + EXPERT_BLOCK - 1) // EXPERT_BLOCK * EXPERT_BLOCK
    pad_ends = jnp.cumsum(padded)
    pad_starts = pad_ends - padded
    start_of = jnp.zeros_like(ids)
    for e in range(N_EXPERTS):
        start_of = jnp.where(ids == e, pad_starts[e], start_of)
    pos_t = (start_of + rank).astype(jnp.int32)
    n_used = jnp.maximum(pad_ends[-1] // EXPERT_BLOCK, 1).astype(jnp.int32)
    blk = jnp.minimum(jnp.arange(n_blocks, dtype=jnp.int32), n_used - 1)
    blk_e = jnp.sum(pad_ends[None, :] <= (blk * EXPERT_BLOCK)[:, None], axis=1)
    blk_e = jnp.minimum(blk_e, N_EXPERTS - 1).astype(jnp.int32)
    zstart = (pad_starts + counts).astype(jnp.int32)
    zcount = (padded - counts).astype(jnp.int32)
    nonempty = counts > 0
    expert_ids = jnp.arange(N_EXPERTS, dtype=jnp.int32)
    later = jnp.where(nonempty[None, :] & (expert_ids[None, :] > expert_ids[:, None]), expert_ids[None, :], N_EXPERTS)
    next_e = jnp.min(later, axis=1).astype(jnp.int32)
    wslot = ((jnp.cumsum(nonempty) - 1) % 2).astype(jnp.int32)
    return pos_t, blk_e, blk, n_used.reshape(1), zstart, zcount, next_e, wslot


def kernel(x, w_in, rel_bias, w_dw, b_dw, conv_ln_g, conv_ln_b, w_o_attn, w_o_conv, w_out, ln1_g, ln1_b,
           w_router, b_router, w_gate_up, b_gate_up, w_down, b_down, ln2_g, ln2_b):
    B, S, D = x.shape
    T = B * S
    h = x
    bm = bias_tables(rel_bias)
    q_off, k_off, v_off, rest = 0, QKV_WIDTH, 2 * QKV_WIDTH, 3 * QKV_WIDTH
    for l in range(DEPTH):
        wb = w_in[l].astype(BF16)

        def group_cols(gi):
            return [wb[:, off + gi * GROUP_WIDTH:off + (gi + 1) * GROUP_WIDTH] for off in (q_off, k_off, v_off)]

        w_main = jnp.concatenate(group_cols(0) + [wb[:, rest:]], axis=1)
        qkv0, glu, gates, *x_dilated = in_proj_dilate(h, w_main, "in_proj_main")
        x_dil = [None] + x_dilated
        o_list, st_list = [], []
        for gi in range(N_GROUPS):
            dil = ATT_GROUPS[gi][1]
            if dil == 1:
                qkv = qkv0.reshape(B, 1, S, qkv0.shape[-1])
            else:
                w_g = jnp.concatenate(group_cols(gi), axis=1)
                qkv = in_proj(x_dil[gi].reshape(T, D), w_g, w_g.shape[1], f"in_proj_g{gi}")
                qkv = qkv.reshape(B, dil, S // dil, w_g.shape[1])
            o_g, st_g = attention_group(qkv, bm[gi], gi)
            o_list.append(o_g)
            st_list.append(st_g)
        h1, h1_tiles = mixer_out(o_list, st_list, glu, gates, h,
                       w_dw[l].reshape(CONV_WIDTH, CONV_CH), b_dw[l].reshape(1, CONV_CH),
                       conv_ln_g[l].reshape(1, CONV_CH), conv_ln_b[l].reshape(1, CONV_CH),
                       w_o_attn[l].astype(BF16), w_o_conv[l].astype(BF16), w_out[l].astype(BF16),
                       ln1_g[l].reshape(1, D), ln1_b[l].reshape(1, D))
        h2 = h1.reshape(T, D)
        gates_t, ids_t, rank_t, counts = router(h2, w_router[l], b_router[l])
        n_rows = T * TOP_K + N_EXPERTS * EXPERT_BLOCK
        pos_t, blk_e, blk_src, n_used, zstart, zcount, next_e, wslot = routing_layout(
            ids_t, rank_t, counts[:, 0], n_rows // EXPERT_BLOCK)
        xs = dispatch(h1_tiles, pos_t, zstart, zcount, n_used, n_rows)
        ys = experts(xs, blk_e, blk_src, n_used, next_e, wslot,
                     w_gate_up[l], b_gate_up[l].reshape(N_EXPERTS, 1, 2 * D_FF),
                     w_down[l], b_down[l].reshape(N_EXPERTS, 1, D))
        out = combine(ys, pos_t, gates_t, h2, ln2_g[l].reshape(1, D), ln2_b[l].reshape(1, D))
        h = out.reshape(B, S, D)
    return h
```

```python
import functools
import math

import jax
import jax.numpy as jnp
from jax import lax
from jax.experimental import pallas as pl
from jax.experimental.pallas import tpu as pltpu

D_MODEL = 1024
ATT_GROUPS = ((128, 1), (512, 4), (2048, 16))
N_GROUPS = len(ATT_GROUPS)
HEADS_PER_GROUP = 8
HEAD_DIM = 64
GROUP_WIDTH = HEADS_PER_GROUP * HEAD_DIM
QKV_WIDTH = N_GROUPS * GROUP_WIDTH
ATT_BLOCK = 128
N_BUCKETS = 32
MAX_DISTANCE = 2048
CONV_CH = 768
CONV_WIDTH = 31
CONV_HALO = 32
N_EXPERTS = 32
TOP_K = 4
D_FF = 1024
SWIGLU_LIMIT = 7.0
SWIGLU_ALPHA = 1.702
IN_WIDTH = 3 * QKV_WIDTH + 2 * CONV_CH + 2 * D_MODEL
LN_EPS = 1e-5
NEG_INF = -1e30
DEPTH = 1
DEEPNORM_ALPHA = (2 * DEPTH) ** 0.25

LANES = 128
SUBLANES = 8
EXPERT_BLOCK = 512
VMEM_LIMIT = 56 * 1024 * 1024

F32 = jnp.float32
BF16 = jnp.bfloat16


def _params(sem, vmem=VMEM_LIMIT):
    return pltpu.CompilerParams(dimension_semantics=sem, vmem_limit_bytes=vmem)


def _sigmoid(x):
    return 0.5 * jnp.tanh(0.5 * x) + 0.5


def _layer_norm(x, g, b):
    mu = jnp.mean(x, axis=-1, keepdims=True)
    xc = x - mu
    var = jnp.mean(xc * xc, axis=-1, keepdims=True)
    return xc * lax.rsqrt(var + LN_EPS) * g + b


def _in_proj_body(x_ref, w_ref, o_ref):
    o_ref[...] = jnp.dot(x_ref[...], w_ref[...], preferred_element_type=F32).astype(o_ref.dtype)


def in_proj(xb, w_b, tn, name, tm=1024):
    T = xb.shape[0]
    N = w_b.shape[1]
    tm = min(tm, T)
    return pl.pallas_call(
        _in_proj_body,
        grid=(N // tn, T // tm),
        in_specs=[pl.BlockSpec((tm, D_MODEL), lambda n, m: (m, 0)),
                  pl.BlockSpec((D_MODEL, tn), lambda n, m: (0, n))],
        out_specs=pl.BlockSpec((tm, tn), lambda n, m: (m, n)),
        out_shape=jax.ShapeDtypeStruct((T, N), BF16),
        compiler_params=_params(("arbitrary", "arbitrary")),
        name=name,
    )(xb, w_b)


QKV_COLS = 3 * GROUP_WIDTH


def _in_proj_dilate_body(*refs, tm, nslab):
    x_slabs, w_ref = refs[:nslab], refs[nslab]
    qkv_ref, glu_ref, gate_ref = refs[nslab + 1:nslab + 4]
    d_refs = refs[nslab + 4:]
    cols = []
    for c, x_ref in enumerate(x_slabs):
        cs = slice(c * LANES, (c + 1) * LANES)
        cols.append(x_ref[...].astype(BF16))
        for (_, dil), d_ref in zip(ATT_GROUPS[1:], d_refs):
            for r in range(dil):
                d_ref[r, :, cs] = x_ref[pl.ds(r, tm // dil, stride=dil), :].astype(BF16)
    xb = jnp.concatenate(cols, axis=1)
    proj = jnp.dot(xb, w_ref[...], preferred_element_type=F32)
    qkv_ref[...] = proj[:, :QKV_COLS].astype(qkv_ref.dtype)
    u_val = proj[:, QKV_COLS:QKV_COLS + CONV_CH]
    u_gate = proj[:, QKV_COLS + CONV_CH:QKV_COLS + 2 * CONV_CH]
    glu_ref[...] = (u_val * _sigmoid(u_gate)).astype(glu_ref.dtype)
    gate_ref[...] = _sigmoid(proj[:, QKV_COLS + 2 * CONV_CH:]).astype(gate_ref.dtype)


def in_proj_dilate(x3, w_b, name, tm=512):
    B, S, D = x3.shape
    N = w_b.shape[1]
    tm = min(tm, S)
    nslab = D // LANES
    dils = [dil for _, dil in ATT_GROUPS[1:]]
    per_seq = S // tm
    widths = (QKV_COLS, CONV_CH, 2 * D_MODEL)
    assert N == QKV_COLS + 2 * CONV_CH + 2 * D_MODEL
    return pl.pallas_call(
        functools.partial(_in_proj_dilate_body, tm=tm, nslab=nslab),
        grid=(B, per_seq),
        in_specs=[pl.BlockSpec((None, tm, LANES), functools.partial(lambda c, b, i: (b, i, c), c))
                  for c in range(nslab)]
        + [pl.BlockSpec((D, N), lambda b, i: (0, 0))],
        out_specs=[pl.BlockSpec((None, tm, w), lambda b, i: (b, i, 0)) for w in widths]
        + [pl.BlockSpec((None, dil, tm // dil, D), lambda b, i: (b, 0, i, 0)) for dil in dils],
        out_shape=[jax.ShapeDtypeStruct((B, S, w), BF16) for w in widths]
        + [jax.ShapeDtypeStruct((B, dil, S // dil, D), BF16) for dil in dils],
        compiler_params=_params(("arbitrary", "arbitrary")),
        name=name,
    )(*([x3] * nslab), w_b)


def _t5_bucket(dist):
    max_exact = N_BUCKETS // 2
    log_ratio = jnp.log(jnp.maximum(dist, max_exact).astype(F32) / max_exact) / math.log(MAX_DISTANCE / max_exact)
    large = jnp.minimum(max_exact + (log_ratio * (N_BUCKETS - max_exact)).astype(jnp.int32), N_BUCKETS - 1)
    return jnp.where(dist < max_exact, dist, large)


def _bias_body(tbl_ref, bucket_ref, band_ref, o_ref):
    g = pl.program_id(0)
    h = pl.program_id(1)
    col = g * HEADS_PER_GROUP + h
    bucket = bucket_ref[...]
    acc = jnp.zeros(bucket.shape, F32)
    for k in range(N_BUCKETS):
        acc = jnp.where(bucket == k, tbl_ref[k, col], acc)
    band = band_ref[...] != 0
    kj = lax.broadcasted_iota(jnp.int32, bucket.shape, 1)
    o_ref[0] = jnp.where(band, acc, NEG_INF)
    o_ref[1] = jnp.where(band & (kj >= ATT_BLOCK), acc, NEG_INF)


def bias_tables(rel_bias):
    qi = jnp.arange(ATT_BLOCK)[:, None]
    kj = jnp.arange(2 * ATT_BLOCK)[None, :]
    dist = qi - kj + ATT_BLOCK
    buckets, bands = [], []
    for window, dil in ATT_GROUPS:
        bands.append(((dist >= 0) & (dist <= window // dil)).astype(jnp.int32))
        buckets.append(_t5_bucket(jnp.maximum(dist, 0) * dil).astype(jnp.int32))
    buckets = jnp.stack(buckets)
    bands = jnp.stack(bands)
    blk = (None, ATT_BLOCK, 2 * ATT_BLOCK)
    return pl.pallas_call(
        _bias_body,
        grid=(N_GROUPS, HEADS_PER_GROUP),
        in_specs=[pl.BlockSpec(memory_space=pltpu.SMEM),
                  pl.BlockSpec(blk, lambda g, h: (g, 0, 0)),
                  pl.BlockSpec(blk, lambda g, h: (g, 0, 0))],
        out_specs=pl.BlockSpec((None, None, 2, ATT_BLOCK, 2 * ATT_BLOCK), lambda g, h: (g, h, 0, 0, 0)),
        out_shape=jax.ShapeDtypeStruct((N_GROUPS, HEADS_PER_GROUP, 2, ATT_BLOCK, 2 * ATT_BLOCK), F32),
        compiler_params=_params(("arbitrary", "arbitrary")),
        name="bias_tables",
    )(rel_bias.astype(F32), buckets, bands)


def _attn_body(q_ref, kc_ref, kp_ref, vc_ref, vp_ref, bm_ref, o_ref, st_ref, k_all, v_all, s_buf, *, tq):
    i = pl.program_id(2)
    nsub = tq // ATT_BLOCK
    k_all[0:ATT_BLOCK] = kp_ref[...]
    k_all[ATT_BLOCK:] = kc_ref[...]
    v_all[0:ATT_BLOCK] = vp_ref[...]
    v_all[ATT_BLOCK:] = vc_ref[...]
    lo = lax.broadcasted_iota(jnp.int32, (ATT_BLOCK, LANES), 1) < HEAD_DIM
    nt_dims = (((1,), (1,)), ((), ()))
    first = jnp.where(i == 0, 1, 0)

    def scores(s, slot):
        r0 = s * ATT_BLOCK
        for j in range(HEADS_PER_GROUP // 2):
            cs = slice(j * LANES, (j + 1) * LANES)
            qp = q_ref[r0:r0 + ATT_BLOCK, cs] * jnp.asarray(HEAD_DIM ** -0.5, BF16)
            kp = k_all[r0:r0 + 2 * ATT_BLOCK, cs]
            for hh in range(2):
                h = 2 * j + hh
                qh = jnp.where(lo if hh == 0 else ~lo, qp, jnp.zeros_like(qp))
                bias = bm_ref[h, first] if s == 0 else bm_ref[h, 0]
                s_buf[slot, h] = lax.dot_general(qh, kp, nt_dims, preferred_element_type=F32) + bias

    def softmax_pv(s, slot):
        rows = slice(s * ATT_BLOCK, (s + 1) * ATT_BLOCK)
        st_ref[rows, :] = jnp.zeros((ATT_BLOCK, LANES), F32)
        for j in range(HEADS_PER_GROUP // 2):
            vp = v_all[s * ATT_BLOCK:(s + 2) * ATT_BLOCK, j * LANES:(j + 1) * LANES]
            for hh in range(2):
                h = 2 * j + hh
                sc = s_buf[slot, h]
                m = jnp.max(sc, axis=-1, keepdims=True)
                p = jnp.exp(sc - m)
                den = jnp.sum(p, axis=-1, keepdims=True)
                pv = jnp.dot(p.astype(BF16), vp, preferred_element_type=F32)
                c0 = h * HEAD_DIM
                o_ref[rows, c0:c0 + HEAD_DIM] = pv[:, hh * HEAD_DIM:(hh + 1) * HEAD_DIM]
                st_ref[rows, h:h + 1] = m
                st_ref[rows, HEADS_PER_GROUP + h:HEADS_PER_GROUP + h + 1] = den

    scores(0, 0)
    for s in range(nsub):
        softmax_pv(s, s % 2)
        if s + 1 < nsub:
            scores(s + 1, (s + 1) % 2)


def attention_group(qkv, bm_g, gi, tq=512):
    B, dil, L, _ = qkv.shape
    tq = min(tq, L)
    sub = tq // ATT_BLOCK

    def cur(col):
        return pl.BlockSpec((None, None, tq, GROUP_WIDTH), lambda b, r, i: (b, r, i, col))

    def prev(col):
        return pl.BlockSpec((None, None, ATT_BLOCK, GROUP_WIDTH),
                            lambda b, r, i: (b, r, jnp.maximum(i * sub - 1, 0), col))

    return pl.pallas_call(
        functools.partial(_attn_body, tq=tq),
        grid=(B, dil, L // tq),
        in_specs=[cur(0), cur(1), prev(1), cur(2), prev(2),
                  pl.BlockSpec((HEADS_PER_GROUP, 2, ATT_BLOCK, 2 * ATT_BLOCK), lambda b, r, i: (0, 0, 0, 0))],
        out_specs=[pl.BlockSpec((None, None, tq, GROUP_WIDTH), lambda b, r, i: (b, r, i, 0)),
                   pl.BlockSpec((None, None, tq, LANES), lambda b, r, i: (b, r, i, 0))],
        out_shape=[jax.ShapeDtypeStruct((B, dil, L, GROUP_WIDTH), F32),
                   jax.ShapeDtypeStruct((B, dil, L, LANES), F32)],
        scratch_shapes=[pltpu.VMEM((ATT_BLOCK + tq, GROUP_WIDTH), BF16),
                        pltpu.VMEM((ATT_BLOCK + tq, GROUP_WIDTH), BF16),
                        pltpu.VMEM((2, HEADS_PER_GROUP, ATT_BLOCK, 2 * ATT_BLOCK), F32)],
        compiler_params=_params(("arbitrary", "arbitrary", "arbitrary")),
        name=f"attention_g{gi}",
    )(qkv, qkv, qkv, qkv, qkv, bm_g)


def _split_bf16(x):
    hi = x.astype(BF16)
    lo = (x - hi.astype(F32)).astype(BF16)
    return hi, lo


def _load_token_tiles(ref, n):
    return jnp.concatenate([ref[pl.ds(c, n, stride=SUBLANES), :] for c in range(D_MODEL // LANES)], axis=1)


def _store_token_tiles(ref, x, n):
    for c in range(D_MODEL // LANES):
        ref[pl.ds(c, n, stride=SUBLANES), :] = x[:, c * LANES:(c + 1) * LANES]


def _to_token_order(blk_ref, tok_ref, dil, tm):
    n = tm // dil
    for r in range(dil):
        for c in range(tok_ref.shape[0]):
            tok_ref[c, pl.ds(r, n, stride=dil), :] = blk_ref[r, :, c * LANES:(c + 1) * LANES]


def _mixer_out_body(o0, o1, o2, s0, s1, s2, glu_in, glu_halo, ga_ref, gc_ref, x_ref,
                    wdw_ref, bdw_ref, cg_ref, cb_ref, woa_ref, woc_ref, wout_ref, g1_ref, b1_ref,
                    h_ref, ht_ref, glu_ref, dw_ref, shift_ref, tok_o1, tok_s1, tok_o2, tok_s2, *, tm, chunk):
    i = pl.program_id(1)

    ncol = GROUP_WIDTH // LANES
    _to_token_order(o1, tok_o1, ATT_GROUPS[1][1], tm)
    _to_token_order(s1, tok_s1, ATT_GROUPS[1][1], tm)
    _to_token_order(o2, tok_o2, ATT_GROUPS[2][1], tm)
    _to_token_order(s2, tok_s2, ATT_GROUPS[2][1], tm)
    outs = [o0[0],
            jnp.concatenate([tok_o1[c] for c in range(ncol)], axis=1),
            jnp.concatenate([tok_o2[c] for c in range(ncol)], axis=1)]
    sts = [s0[0], tok_s1[0], tok_s2[0]]

    mx = jnp.maximum(jnp.maximum(sts[0], sts[1]), sts[2])
    wts = [jnp.exp(st - mx) for st in sts]
    wsum = sum(pltpu.roll(st, LANES - HEADS_PER_GROUP, axis=1) * e for st, e in zip(sts, wts))
    row = lax.broadcasted_iota(jnp.int32, (LANES, GROUP_WIDTH), 0)
    colh = lax.broadcasted_iota(jnp.int32, (LANES, GROUP_WIDTH), 1) // HEAD_DIM
    expand = (row == colh).astype(BF16)
    attn = jnp.zeros((tm, GROUP_WIDTH), F32)
    head_lane = lax.broadcasted_iota(jnp.int32, (tm, LANES), 1) < HEADS_PER_GROUP
    for wt, o in zip(wts, outs):
        c_hi, c_lo = _split_bf16(jnp.where(head_lane, wt / wsum, 0.0))
        c = (jnp.dot(c_hi, expand, preferred_element_type=F32)
             + jnp.dot(c_lo, expand, preferred_element_type=F32))
        attn = attn + c * o
    a_out = jnp.dot(attn.astype(BF16), woa_ref[...], preferred_element_type=F32)

    glu_ref[0:CONV_HALO] = jnp.where(i == 0, 0.0, glu_halo[...].astype(F32))
    glu_ref[CONV_HALO:] = glu_in[...].astype(F32)
    first_tap = CONV_HALO - (CONV_WIDTH - 1)
    for b in range(1, SUBLANES):
        shift_ref[b - 1] = glu_ref[b:b + shift_ref.shape[1], :]

    for c0 in range(0, CONV_CH, LANES):
        cs = slice(c0, c0 + LANES)
        bias = jnp.broadcast_to(bdw_ref[:, cs], (chunk, LANES))
        for r0 in range(0, tm, chunk):
            acc = bias
            for j in range(CONV_WIDTH):
                a, b = divmod(first_tap + j, SUBLANES)
                lo_row = r0 + a * SUBLANES
                rows = glu_ref[lo_row:lo_row + chunk, cs] if b == 0 else shift_ref[b - 1, lo_row:lo_row + chunk, cs]
                acc = acc + wdw_ref[j:j + 1, cs] * rows
            dw_ref[r0:r0 + chunk, cs] = acc
    cn = _layer_norm(dw_ref[...], cg_ref[...], cb_ref[...])
    conv = cn * _sigmoid(cn)
    c_out = jnp.dot(conv.astype(BF16), woc_ref[...], preferred_element_type=F32)

    merged = ga_ref[...].astype(F32) * a_out + gc_ref[...].astype(F32) * c_out
    mix = jnp.dot(merged.astype(BF16), wout_ref[...], preferred_element_type=F32)
    h = _layer_norm(DEEPNORM_ALPHA * x_ref[...] + mix, g1_ref[...], b1_ref[...])
    h_ref[...] = h
    _store_token_tiles(ht_ref, h, tm)


def mixer_out(o_list, st_list, glu3, gates3, x3, w_dw, b_dw, cg, cb, woa_b, woc_b, wout_b, g1, b1, tm=512, chunk=64):
    B, S, _ = x3.shape
    tm = min(tm, S)
    halo_blocks = tm // CONV_HALO
    ncol = GROUP_WIDTH // LANES

    def tile(width, col=0):
        return pl.BlockSpec((None, tm, width), lambda b, i: (b, i, col))

    def dilated(gi, width):
        dil = ATT_GROUPS[gi][1]
        return pl.BlockSpec((None, dil, tm // dil, width), lambda b, i: (b, 0, i, 0))

    halo = pl.BlockSpec((None, CONV_HALO, CONV_CH), lambda b, i: (b, jnp.maximum(i * halo_blocks - 1, 0), 0))

    def whole(shape):
        return pl.BlockSpec(shape, lambda b, i: (0,) * len(shape))

    return pl.pallas_call(
        functools.partial(_mixer_out_body, tm=tm, chunk=chunk),
        grid=(B, S // tm),
        in_specs=[dilated(g, GROUP_WIDTH) for g in range(N_GROUPS)] + [dilated(g, LANES) for g in range(N_GROUPS)]
        + [tile(CONV_CH), halo, tile(D_MODEL, 0), tile(D_MODEL, 1), tile(D_MODEL),
           whole((CONV_WIDTH, CONV_CH)), whole((1, CONV_CH)), whole((1, CONV_CH)), whole((1, CONV_CH)),
           whole((GROUP_WIDTH, D_MODEL)), whole((CONV_CH, D_MODEL)), whole((D_MODEL, D_MODEL)),
           whole((1, D_MODEL)), whole((1, D_MODEL))],
        out_specs=[tile(D_MODEL), pl.BlockSpec((tm * SUBLANES, LANES), lambda b, i: (b * (S // tm) + i, 0))],
        out_shape=[jax.ShapeDtypeStruct((B, S, D_MODEL), F32),
                   jax.ShapeDtypeStruct((B * S * SUBLANES, LANES), F32)],
        scratch_shapes=[pltpu.VMEM((CONV_HALO + tm, CONV_CH), F32), pltpu.VMEM((tm, CONV_CH), F32),
                        pltpu.VMEM((SUBLANES - 1, tm + CONV_HALO - SUBLANES, CONV_CH), F32),
                        pltpu.VMEM((ncol, tm, LANES), F32), pltpu.VMEM((1, tm, LANES), F32),
                        pltpu.VMEM((ncol, tm, LANES), F32), pltpu.VMEM((1, tm, LANES), F32)],
        compiler_params=_params(("arbitrary", "arbitrary")),
        name="mixer_out",
    )(*o_list, *st_list, glu3, glu3, gates3, gates3, x3,
      w_dw, b_dw, cg, cb, woa_b, woc_b, wout_b, g1, b1)


def _router_body(h_ref, wt_ref, b_ref, gates_ref, ids_ref, rank_ref, cnt_ref, carry_ref, tri_ref, *, tm):
    step = pl.program_id(0)

    @pl.when(step == 0)
    def _():
        carry_ref[...] = jnp.zeros_like(carry_ref)
        r_i = lax.broadcasted_iota(jnp.int32, (tm, tm), 0)
        c_i = lax.broadcasted_iota(jnp.int32, (tm, tm), 1)
        tri_ref[...] = (r_i < c_i).astype(BF16)

    nt_dims = (((1,), (1,)), ((), ()))
    h_hi, h_lo = _split_bf16(h_ref[...])
    w_hi, w_lo = _split_bf16(wt_ref[...])
    logits = (lax.dot_general(w_hi, h_hi, nt_dims, preferred_element_type=F32)
              + (lax.dot_general(w_hi, h_lo, nt_dims, preferred_element_type=F32)
                 + lax.dot_general(w_lo, h_hi, nt_dims, preferred_element_type=F32))
              + b_ref[...])
    expert = lax.broadcasted_iota(jnp.int32, (N_EXPERTS, tm), 0).astype(F32)
    row = lax.broadcasted_iota(jnp.int32, (SUBLANES, tm), 0)

    work = logits
    vals, hots = [], []
    ids = jnp.zeros((SUBLANES, tm), F32)
    for k in range(TOP_K):
        v = jnp.max(work, axis=0, keepdims=True)
        idx = jnp.min(jnp.where(work == v, expert, float(N_EXPERTS)), axis=0, keepdims=True)
        hot = expert == idx
        work = jnp.where(hot, -jnp.inf, work)
        ids = jnp.where(row == k, idx, ids)
        vals.append(v)
        hots.append(hot)

    es = [jnp.exp(v - vals[0]) for v in vals]
    esum = es[0] + es[1] + es[2] + es[3]
    gates = jnp.zeros((SUBLANES, tm), F32)
    for k in range(TOP_K):
        gates = jnp.where(row == k, es[k] / esum, gates)

    sel = (hots[0] | hots[1] | hots[2] | hots[3])
    before = jnp.dot(sel.astype(BF16), tri_ref[...], preferred_element_type=F32) + carry_ref[:, 0:1]
    rank = jnp.zeros((SUBLANES, tm), F32)
    for k in range(TOP_K):
        rk = jnp.sum(jnp.where(hots[k], before, 0.0), axis=0, keepdims=True)
        rank = jnp.where(row == k, rk, rank)
    carry_ref[...] = carry_ref[...] + jnp.sum(sel.astype(F32), axis=1, keepdims=True)

    gates_ref[...] = gates
    ids_ref[...] = ids.astype(jnp.int32)
    rank_ref[...] = rank.astype(jnp.int32)
    cnt_ref[...] = carry_ref[...].astype(jnp.int32)


def router(h2, w_router, b_router, tm=512):
    T = h2.shape[0]
    tm = min(tm, T)
    tile = pl.BlockSpec((SUBLANES, tm), lambda i: (0, i))
    return pl.pallas_call(
        functools.partial(_router_body, tm=tm),
        grid=(T // tm,),
        in_specs=[pl.BlockSpec((tm, D_MODEL), lambda i: (i, 0)),
                  pl.BlockSpec((N_EXPERTS, D_MODEL), lambda i: (0, 0)),
                  pl.BlockSpec((N_EXPERTS, 1), lambda i: (0, 0))],
        out_specs=[tile, tile, tile, pl.BlockSpec((N_EXPERTS, LANES), lambda i: (0, 0))],
        out_shape=[jax.ShapeDtypeStruct((SUBLANES, T), F32), jax.ShapeDtypeStruct((SUBLANES, T), jnp.int32),
                   jax.ShapeDtypeStruct((SUBLANES, T), jnp.int32), jax.ShapeDtypeStruct((N_EXPERTS, LANES), jnp.int32)],
        scratch_shapes=[pltpu.VMEM((N_EXPERTS, LANES), F32), pltpu.VMEM((tm, tm), BF16)],
        compiler_params=_params(("arbitrary",)),
        name="router",
    )(h2, w_router.astype(F32).T, b_router.astype(F32).reshape(N_EXPERTS, 1))


def _token_copy(src, dst, s_tok, d_tok, sem, n=1):
    rows = n * SUBLANES
    return pltpu.make_async_copy(src.at[pl.ds(pl.multiple_of(s_tok * SUBLANES, SUBLANES), rows), :],
                                 dst.at[pl.ds(pl.multiple_of(d_tok * SUBLANES, SUBLANES), rows), :], sem)


def _dispatch_body(zstart_ref, zcount_ref, n_used_ref, pos_ref, h_ref, xs_ref, zeros_ref, ring, sems, zsem, *, tm, nb):
    half = EXPERT_BLOCK // 2
    bits = [1 << s for s in reversed(range(half.bit_length()))]

    def zero_fill(e, wait):
        start, count = zstart_ref[e], zcount_ref[e]
        for bit in bits:
            @pl.when((count & bit) != 0)
            def _():
                cp = _token_copy(zeros_ref, xs_ref, 0, 0 if wait else start + (count & ~(2 * bit - 1)), zsem, bit)
                cp.wait() if wait else cp.start()

    def zero_tail(blk, wait):
        for part in range(2):
            cp = _token_copy(zeros_ref, xs_ref, 0, 0 if wait else blk * EXPERT_BLOCK + part * half, zsem, half)
            cp.wait() if wait else cp.start()

    @pl.when(pl.program_id(0) == 0)
    def _():
        zeros_ref[...] = jnp.zeros_like(zeros_ref)
        lax.fori_loop(0, N_EXPERTS, lambda e, c: (zero_fill(e, False), c)[1], 0)
        lax.fori_loop(n_used_ref[0], nb, lambda blk, c: (zero_tail(blk, False), c)[1], 0)

    i = pl.program_id(0)
    last = pl.num_programs(0) - 1

    def retire(slot):
        for k in range(TOP_K):
            _token_copy(ring.at[slot], xs_ref, 0, 0, sems.at[slot], tm).wait()

    def step(slot):
        pl.when(i >= 2)(functools.partial(retire, slot))
        ring[slot] = h_ref[...]

        def issue(t, c):
            for k in range(TOP_K):
                _token_copy(ring.at[slot], xs_ref, t, pos_ref[k * tm + t], sems.at[slot]).start(priority=k % 2)
            return c

        lax.fori_loop(0, tm, issue, 0)

        @pl.when(i == last)
        def _():
            retire(slot)
            pl.when(i >= 1)(functools.partial(retire, 1 - slot))

    for slot in range(2):
        pl.when(i % 2 == slot)(functools.partial(step, slot))

    @pl.when(pl.program_id(0) == 0)
    def _():
        lax.fori_loop(0, N_EXPERTS, lambda e, c: (zero_fill(e, True), c)[1], 0)
        lax.fori_loop(n_used_ref[0], nb, lambda blk, c: (zero_tail(blk, True), c)[1], 0)


def _tile_major(pos_t, tm):
    T = pos_t.shape[1]
    return pos_t[:TOP_K].reshape(TOP_K, T // tm, tm).transpose(1, 0, 2).reshape(-1)


def dispatch(ht, pos_t, zstart, zcount, n_used, n_rows, tm=1024):
    T = ht.shape[0] // SUBLANES
    tm = min(tm, T)
    grid_spec = pltpu.PrefetchScalarGridSpec(
        num_scalar_prefetch=3,
        grid=(T // tm,),
        in_specs=[pl.BlockSpec((TOP_K * tm,), lambda i, zs, zc, nu: (i,), memory_space=pltpu.SMEM),
                  pl.BlockSpec((tm * SUBLANES, LANES), lambda i, zs, zc, nu: (i, 0))],
        out_specs=pl.BlockSpec(memory_space=pl.ANY),
        scratch_shapes=[pltpu.VMEM((EXPERT_BLOCK // 2 * SUBLANES, LANES), F32),
                        pltpu.VMEM((2, tm * SUBLANES, LANES), F32), pltpu.SemaphoreType.DMA((2,)),
                        pltpu.SemaphoreType.DMA],
    )
    return pl.pallas_call(
        functools.partial(_dispatch_body, tm=tm, nb=n_rows // EXPERT_BLOCK),
        grid_spec=grid_spec,
        out_shape=jax.ShapeDtypeStruct((n_rows * SUBLANES, LANES), F32),
        compiler_params=_params(("arbitrary",)),
        name="dispatch",
    )(zstart, zcount, n_used, _tile_major(pos_t, tm), ht)


def _combine_body(pos_ref, pos_next_ref, gates_ref, h_ref, g2_ref, b2_ref, ys_ref, o_ref, buf, sems, *, tm):
    i = pl.program_id(0)

    def issue(p_ref, s):
        def body(t, c):
            for k in range(TOP_K):
                _token_copy(ys_ref, buf.at[s, k], p_ref[k * tm + t], t, sems.at[s]).start(priority=k % 2)
            return c

        lax.fori_loop(0, tm, body, 0)

    @pl.when(i == 0)
    def _():
        issue(pos_ref, 0)

    def step(slot):
        @pl.when(i + 1 < pl.num_programs(0))
        def _():
            issue(pos_next_ref, 1 - slot)

        for k in range(TOP_K):
            _token_copy(ys_ref, buf.at[slot, k], 0, 0, sems.at[slot], tm).wait()

        gates = gates_ref[...].T
        ffn = gates[:, 0:1] * _load_token_tiles(buf.at[slot, 0], tm)
        for k in range(1, TOP_K):
            ffn = ffn + gates[:, k:k + 1] * _load_token_tiles(buf.at[slot, k], tm)
        o_ref[...] = _layer_norm(DEEPNORM_ALPHA * h_ref[...] + ffn, g2_ref[...], b2_ref[...])

    for slot in range(2):
        pl.when(i % 2 == slot)(functools.partial(step, slot))


def combine(ys, pos_t, gates_t, h2, g2, b2, tm=256):
    T = h2.shape[0]
    tm = min(tm, T)
    last = T // tm - 1
    pos_flat = _tile_major(pos_t, tm)
    return pl.pallas_call(
        functools.partial(_combine_body, tm=tm),
        grid=(T // tm,),
        in_specs=[pl.BlockSpec((TOP_K * tm,), lambda i: (i,), memory_space=pltpu.SMEM),
                  pl.BlockSpec((TOP_K * tm,), lambda i: (jnp.minimum(i + 1, last),), memory_space=pltpu.SMEM),
                  pl.BlockSpec((SUBLANES, tm), lambda i: (0, i)),
                  pl.BlockSpec((tm, D_MODEL), lambda i: (i, 0)),
                  pl.BlockSpec((1, D_MODEL), lambda i: (0, 0)),
                  pl.BlockSpec((1, D_MODEL), lambda i: (0, 0)),
                  pl.BlockSpec(memory_space=pl.ANY)],
        out_specs=pl.BlockSpec((tm, D_MODEL), lambda i: (i, 0)),
        out_shape=jax.ShapeDtypeStruct((T, D_MODEL), F32),
        scratch_shapes=[pltpu.VMEM((2, TOP_K, tm * SUBLANES, LANES), F32), pltpu.SemaphoreType.DMA((2,))],
        compiler_params=_params(("arbitrary",)),
        name="combine",
    )(pos_flat, pos_flat, gates_t, h2, g2, b2, ys)


def _experts_body(blk_e, blk_src, n_used, next_e, wslot, xs_ref, wgu_hbm, bgu_ref, wdn_hbm, bdn_ref, ys_ref,
                  wgu_f, wdn_f, wgu_b, wdn_b, sems):
    del blk_src
    i = pl.program_id(0)
    used = i < n_used[0]
    e = blk_e[i]

    def weight_copies(expert, slot):
        return (pltpu.make_async_copy(wgu_hbm.at[expert], wgu_f.at[slot], sems.at[slot]),
                pltpu.make_async_copy(wdn_hbm.at[expert], wdn_f.at[slot], sems.at[slot]))

    @pl.when(i == 0)
    def _():
        for cp in weight_copies(e, 0):
            cp.start()

    @pl.when((i == 0) | (e != blk_e[jnp.maximum(i - 1, 0)]))
    def _():
        nxt = next_e[e]
        for slot in range(2):
            @pl.when(wslot[e] == slot)
            def _():
                for cp in weight_copies(e, slot):
                    cp.wait()
                wgu_b[...] = wgu_f[slot].astype(BF16)
                wdn_b[...] = wdn_f[slot].astype(BF16)

                @pl.when(nxt < N_EXPERTS)
                def _():
                    for cp in weight_copies(nxt, 1 - slot):
                        cp.start()

    @pl.when(used)
    def _():
        x = _load_token_tiles(xs_ref, EXPERT_BLOCK).astype(BF16)
        hgu = jnp.dot(x, wgu_b[...], preferred_element_type=F32) + bgu_ref[...]
        gate = jnp.minimum(hgu[:, :D_FF], SWIGLU_LIMIT)
        up = jnp.clip(hgu[:, D_FF:], -SWIGLU_LIMIT, SWIGLU_LIMIT)
        act = (up + 1.0) * gate * _sigmoid(SWIGLU_ALPHA * gate)
        y = jnp.dot(act.astype(BF16), wdn_b[...], preferred_element_type=F32) + bdn_ref[...]
        _store_token_tiles(ys_ref, y, EXPERT_BLOCK)

    @pl.when(jnp.logical_not(used))
    def _():
        ys_ref[...] = jnp.zeros_like(ys_ref)


def experts(xs, blk_e, blk_src, n_used, next_e, wslot, wgu, bgu, wdn, bdn):
    n_rows = xs.shape[0] // SUBLANES
    nb = n_rows // EXPERT_BLOCK
    rows = EXPERT_BLOCK * SUBLANES
    grid_spec = pltpu.PrefetchScalarGridSpec(
        num_scalar_prefetch=5,
        grid=(nb,),
        in_specs=[pl.BlockSpec((rows, LANES), lambda i, e, s, n, ne, ws: (s[i], 0)),
                  pl.BlockSpec(memory_space=pl.ANY),
                  pl.BlockSpec((None, 1, 2 * D_FF), lambda i, e, s, n, ne, ws: (e[i], 0, 0)),
                  pl.BlockSpec(memory_space=pl.ANY),
                  pl.BlockSpec((None, 1, D_MODEL), lambda i, e, s, n, ne, ws: (e[i], 0, 0))],
        out_specs=pl.BlockSpec((rows, LANES), lambda i, e, s, n, ne, ws: (i, 0)),
        scratch_shapes=[pltpu.VMEM((2, D_MODEL, 2 * D_FF), F32), pltpu.VMEM((2, D_FF, D_MODEL), F32),
                        pltpu.VMEM((D_MODEL, 2 * D_FF), BF16), pltpu.VMEM((D_FF, D_MODEL), BF16),
                        pltpu.SemaphoreType.DMA((2,))],
    )
    return pl.pallas_call(
        _experts_body,
        grid_spec=grid_spec,
        out_shape=jax.ShapeDtypeStruct((n_rows * SUBLANES, LANES), F32),
        compiler_params=_params(("arbitrary",)),
        name="experts",
    )(blk_e, blk_src, n_used, next_e, wslot, xs, wgu, bgu, wdn, bdn)


def routing_layout(ids, rank, counts, n_blocks):
    padded = (counts + EXPERT_BLOCK - 1) // EXPERT_BLOCK * EXPERT_BLOCK
    pad_ends = jnp.cumsum(padded)
    pad_starts = pad_ends - padded
    start_of = jnp.zeros_like(ids)
    for e in range(N_EXPERTS):
        start_of = jnp.where(ids == e, pad_starts[e], start_of)
    pos_t = (start_of + rank).astype(jnp.int32)
    n_used = jnp.maximum(pad_ends[-1] // EXPERT_BLOCK, 1).astype(jnp.int32)
    blk = jnp.minimum(jnp.arange(n_blocks, dtype=jnp.int32), n_used - 1)
    blk_e = jnp.sum(pad_ends[None, :] <= (blk * EXPERT_BLOCK)[:, None], axis=1)
    blk_e = jnp.minimum(blk_e, N_EXPERTS - 1).astype(jnp.int32)
    zstart = (pad_starts + counts).astype(jnp.int32)
    zcount = (padded - counts).astype(jnp.int32)
    nonempty = counts > 0
    expert_ids = jnp.arange(N_EXPERTS, dtype=jnp.int32)
    later = jnp.where(nonempty[None, :] & (expert_ids[None, :] > expert_ids[:, None]), expert_ids[None, :], N_EXPERTS)
    next_e = jnp.min(later, axis=1).astype(jnp.int32)
    wslot = ((jnp.cumsum(nonempty) - 1) % 2).astype(jnp.int32)
    return pos_t, blk_e, blk, n_used.reshape(1), zstart, zcount, next_e, wslot


def kernel(x, w_in, rel_bias, w_dw, b_dw, conv_ln_g, conv_ln_b, w_o_attn, w_o_conv, w_out, ln1_g, ln1_b,
           w_router, b_router, w_gate_up, b_gate_up, w_down, b_down, ln2_g, ln2_b):
    B, S, D = x.shape
    T = B * S
    h = x
    bm = bias_tables(rel_bias)
    q_off, k_off, v_off, rest = 0, QKV_WIDTH, 2 * QKV_WIDTH, 3 * QKV_WIDTH
    for l in range(DEPTH):
        wb = w_in[l].astype(BF16)

        def group_cols(gi):
            return [wb[:, off + gi * GROUP_WIDTH:off + (gi + 1) * GROUP_WIDTH] for off in (q_off, k_off, v_off)]

        w_main = jnp.concatenate(group_cols(0) + [wb[:, rest:]], axis=1)
        qkv0, glu, gates, *x_dilated = in_proj_dilate(h, w_main, "in_proj_main")
        x_dil = [None] + x_dilated
        o_list, st_list = [], []
        for gi in range(N_GROUPS):
            dil = ATT_GROUPS[gi][1]
            if dil == 1:
                qkv = qkv0.reshape(B, 1, S, qkv0.shape[-1])
            else:
                w_g = jnp.concatenate(group_cols(gi), axis=1)
                qkv = in_proj(x_dil[gi].reshape(T, D), w_g, w_g.shape[1], f"in_proj_g{gi}")
                qkv = qkv.reshape(B, dil, S // dil, w_g.shape[1])
            o_g, st_g = attention_group(qkv, bm[gi], gi)
            o_list.append(o_g)
            st_list.append(st_g)
        h1, h1_tiles = mixer_out(o_list, st_list, glu, gates, h,
                       w_dw[l].reshape(CONV_WIDTH, CONV_CH), b_dw[l].reshape(1, CONV_CH),
                       conv_ln_g[l].reshape(1, CONV_CH), conv_ln_b[l].reshape(1, CONV_CH),
                       w_o_attn[l].astype(BF16), w_o_conv[l].astype(BF16), w_out[l].astype(BF16),
                       ln1_g[l].reshape(1, D), ln1_b[l].reshape(1, D))
        h2 = h1.reshape(T, D)
        gates_t, ids_t, rank_t, counts = router(h2, w_router[l], b_router[l])
        n_rows = T * TOP_K + N_EXPERTS * EXPERT_BLOCK
        pos_t, blk_e, blk_src, n_used, zstart, zcount, next_e, wslot = routing_layout(
            ids_t, rank_t, counts[:, 0], n_rows // EXPERT_BLOCK)
        xs = dispatch(h1_tiles, pos_t, zstart, zcount, n_used, n_rows)
        ys = experts(xs, blk_e, blk_src, n_used, next_e, wslot,
                     w_gate_up[l], b_gate_up[l].reshape(N_EXPERTS, 1, 2 * D_FF),
                     w_down[l], b_down[l].reshape(N_EXPERTS, 1, D))
        out = combine(ys, pos_t, gates_t, h2, ln2_g[l].reshape(1, D), ln2_b[l].reshape(1, D))
        h = out.reshape(B, S, D)
    return h
```

```python
import functools
import math

import jax
import jax.numpy as jnp
from jax import lax
from jax.experimental import pallas as pl
from jax.experimental.pallas import tpu as pltpu

D_MODEL = 1024
ATT_GROUPS = ((128, 1), (512, 4), (2048, 16))
N_GROUPS = len(ATT_GROUPS)
HEADS_PER_GROUP = 8
HEAD_DIM = 64
GROUP_WIDTH = HEADS_PER_GROUP * HEAD_DIM
QKV_WIDTH = N_GROUPS * GROUP_WIDTH
ATT_BLOCK = 128
N_BUCKETS = 32
MAX_DISTANCE = 2048
CONV_CH = 768
CONV_WIDTH = 31
CONV_HALO = 32
N_EXPERTS = 32
TOP_K = 4
D_FF = 1024
SWIGLU_LIMIT = 7.0
SWIGLU_ALPHA = 1.702
IN_WIDTH = 3 * QKV_WIDTH + 2 * CONV_CH + 2 * D_MODEL
LN_EPS = 1e-5
NEG_INF = -1e30
DEPTH = 1
DEEPNORM_ALPHA = (2 * DEPTH) ** 0.25

LANES = 128
SUBLANES = 8
EXPERT_BLOCK = 512
VMEM_LIMIT = 56 * 1024 * 1024

F32 = jnp.float32
BF16 = jnp.bfloat16


def _params(sem, vmem=VMEM_LIMIT):
    return pltpu.CompilerParams(dimension_semantics=sem, vmem_limit_bytes=vmem)


def _sigmoid(x):
    return 0.5 * jnp.tanh(0.5 * x) + 0.5


def _layer_norm(x, g, b):
    mu = jnp.mean(x, axis=-1, keepdims=True)
    xc = x - mu
    var = jnp.mean(xc * xc, axis=-1, keepdims=True)
    return xc * lax.rsqrt(var + LN_EPS) * g + b


def _in_proj_body(x_ref, w_ref, o_ref):
    o_ref[...] = jnp.dot(x_ref[...], w_ref[...], preferred_element_type=F32).astype(o_ref.dtype)


def in_proj(xb, w_b, tn, name, tm=1024):
    T = xb.shape[0]
    N = w_b.shape[1]
    tm = min(tm, T)
    return pl.pallas_call(
        _in_proj_body,
        grid=(N // tn, T // tm),
        in_specs=[pl.BlockSpec((tm, D_MODEL), lambda n, m: (m, 0)),
                  pl.BlockSpec((D_MODEL, tn), lambda n, m: (0, n))],
        out_specs=pl.BlockSpec((tm, tn), lambda n, m: (m, n)),
        out_shape=jax.ShapeDtypeStruct((T, N), BF16),
        compiler_params=_params(("arbitrary", "arbitrary")),
        name=name,
    )(xb, w_b)


QKV_COLS = 3 * GROUP_WIDTH


def _in_proj_dilate_body(*refs, tm, nslab):
    x_slabs, w_ref = refs[:nslab], refs[nslab]
    qkv_ref, glu_ref, gate_ref = refs[nslab + 1:nslab + 4]
    d_refs = refs[nslab + 4:]
    cols = []
    for c, x_ref in enumerate(x_slabs):
        cs = slice(c * LANES, (c + 1) * LANES)
        cols.append(x_ref[...].astype(BF16))
        for (_, dil), d_ref in zip(ATT_GROUPS[1:], d_refs):
            for r in range(dil):
                d_ref[r, :, cs] = x_ref[pl.ds(r, tm // dil, stride=dil), :].astype(BF16)
    xb = jnp.concatenate(cols, axis=1)
    proj = jnp.dot(xb, w_ref[...], preferred_element_type=F32)
    qkv_ref[...] = proj[:, :QKV_COLS].astype(qkv_ref.dtype)
    u_val = proj[:, QKV_COLS:QKV_COLS + CONV_CH]
    u_gate = proj[:, QKV_COLS + CONV_CH:QKV_COLS + 2 * CONV_CH]
    glu_ref[...] = (u_val * _sigmoid(u_gate)).astype(glu_ref.dtype)
    gate_ref[...] = _sigmoid(proj[:, QKV_COLS + 2 * CONV_CH:]).astype(gate_ref.dtype)


def in_proj_dilate(x3, w_b, name, tm=512):
    B, S, D = x3.shape
    N = w_b.shape[1]
    tm = min(tm, S)
    nslab = D // LANES
    dils = [dil for _, dil in ATT_GROUPS[1:]]
    per_seq = S // tm
    widths = (QKV_COLS, CONV_CH, 2 * D_MODEL)
    assert N == QKV_COLS + 2 * CONV_CH + 2 * D_MODEL
    return pl.pallas_call(
        functools.partial(_in_proj_dilate_body, tm=tm, nslab=nslab),
        grid=(B, per_seq),
        in_specs=[pl.BlockSpec((None, tm, LANES), functools.partial(lambda c, b, i: (b, i, c), c))
                  for c in range(nslab)]
        + [pl.BlockSpec((D, N), lambda b, i: (0, 0))],
        out_specs=[pl.BlockSpec((None, tm, w), lambda b, i: (b, i, 0)) for w in widths]
        + [pl.BlockSpec((None, dil, tm // dil, D), lambda b, i: (b, 0, i, 0)) for dil in dils],
        out_shape=[jax.ShapeDtypeStruct((B, S, w), BF16) for w in widths]
        + [jax.ShapeDtypeStruct((B, dil, S // dil, D), BF16) for dil in dils],
        compiler_params=_params(("arbitrary", "arbitrary")),
        name=name,
    )(*([x3] * nslab), w_b)


def _t5_bucket(dist):
    max_exact = N_BUCKETS // 2
    log_ratio = jnp.log(jnp.maximum(dist, max_exact).astype(F32) / max_exact) / math.log(MAX_DISTANCE / max_exact)
    large = jnp.minimum(max_exact + (log_ratio * (N_BUCKETS - max_exact)).astype(jnp.int32), N_BUCKETS - 1)
    return jnp.where(dist < max_exact, dist, large)


def _bias_body(tbl_ref, bucket_ref, band_ref, o_ref):
    g = pl.program_id(0)
    h = pl.program_id(1)
    col = g * HEADS_PER_GROUP + h
    bucket = bucket_ref[...]
    acc = jnp.zeros(bucket.shape, F32)
    for k in range(N_BUCKETS):
        acc = jnp.where(bucket == k, tbl_ref[k, col], acc)
    band = band_ref[...] != 0
    kj = lax.broadcasted_iota(jnp.int32, bucket.shape, 1)
    o_ref[0] = jnp.where(band, acc, NEG_INF)
    o_ref[1] = jnp.where(band & (kj >= ATT_BLOCK), acc, NEG_INF)


def bias_tables(rel_bias):
    qi = jnp.arange(ATT_BLOCK)[:, None]
    kj = jnp.arange(2 * ATT_BLOCK)[None, :]
    dist = qi - kj + ATT_BLOCK
    buckets, bands = [], []
    for window, dil in ATT_GROUPS:
        bands.append(((dist >= 0) & (dist <= window // dil)).astype(jnp.int32))
        buckets.append(_t5_bucket(jnp.maximum(dist, 0) * dil).astype(jnp.int32))
    buckets = jnp.stack(buckets)
    bands = jnp.stack(bands)
    blk = (None, ATT_BLOCK, 2 * ATT_BLOCK)
    return pl.pallas_call(
        _bias_body,
        grid=(N_GROUPS, HEADS_PER_GROUP),
        in_specs=[pl.BlockSpec(memory_space=pltpu.SMEM),
                  pl.BlockSpec(blk, lambda g, h: (g, 0, 0)),
                  pl.BlockSpec(blk, lambda g, h: (g, 0, 0))],
        out_specs=pl.BlockSpec((None, None, 2, ATT_BLOCK, 2 * ATT_BLOCK), lambda g, h: (g, h, 0, 0, 0)),
        out_shape=jax.ShapeDtypeStruct((N_GROUPS, HEADS_PER_GROUP, 2, ATT_BLOCK, 2 * ATT_BLOCK), F32),
        compiler_params=_params(("arbitrary", "arbitrary")),
        name="bias_tables",
    )(rel_bias.astype(F32), buckets, bands)


def _attn_body(q_ref, kc_ref, kp_ref, vc_ref, vp_ref, bm_ref, o_ref, st_ref, k_all, v_all, s_buf, *, tq):
    i = pl.program_id(2)
    nsub = tq // ATT_BLOCK
    k_all[0:ATT_BLOCK] = kp_ref[...]
    k_all[ATT_BLOCK:] = kc_ref[...]
    v_all[0:ATT_BLOCK] = vp_ref[...]
    v_all[ATT_BLOCK:] = vc_ref[...]
    lo = lax.broadcasted_iota(jnp.int32, (ATT_BLOCK, LANES), 1) < HEAD_DIM
    nt_dims = (((1,), (1,)), ((), ()))
    first = jnp.where(i == 0, 1, 0)

    def scores(s, slot):
        r0 = s * ATT_BLOCK
        for j in range(HEADS_PER_GROUP // 2):
            cs = slice(j * LANES, (j + 1) * LANES)
            qp = q_ref[r0:r0 + ATT_BLOCK, cs] * jnp.asarray(HEAD_DIM ** -0.5, BF16)
            kp = k_all[r0:r0 + 2 * ATT_BLOCK, cs]
            for hh in range(2):
                h = 2 * j + hh
                qh = jnp.where(lo if hh == 0 else ~lo, qp, jnp.zeros_like(qp))
                bias = bm_ref[h, first] if s == 0 else bm_ref[h, 0]
                s_buf[slot, h] = lax.dot_general(qh, kp, nt_dims, preferred_element_type=F32) + bias

    def softmax_pv(s, slot):
        rows = slice(s * ATT_BLOCK, (s + 1) * ATT_BLOCK)
        st_ref[rows, :] = jnp.zeros((ATT_BLOCK, LANES), F32)
        for j in range(HEADS_PER_GROUP // 2):
            vp = v_all[s * ATT_BLOCK:(s + 2) * ATT_BLOCK, j * LANES:(j + 1) * LANES]
            for hh in range(2):
                h = 2 * j + hh
                sc = s_buf[slot, h]
                m = jnp.max(sc, axis=-1, keepdims=True)
                p = jnp.exp(sc - m)
                den = jnp.sum(p, axis=-1, keepdims=True)
                pv = jnp.dot(p.astype(BF16), vp, preferred_element_type=F32)
                c0 = h * HEAD_DIM
                o_ref[rows, c0:c0 + HEAD_DIM] = pv[:, hh * HEAD_DIM:(hh + 1) * HEAD_DIM]
                st_ref[rows, h:h + 1] = m
                st_ref[rows, HEADS_PER_GROUP + h:HEADS_PER_GROUP + h + 1] = den

    scores(0, 0)
    for s in range(nsub):
        softmax_pv(s, s % 2)
        if s + 1 < nsub:
            scores(s + 1, (s + 1) % 2)


def attention_group(qkv, bm_g, gi, tq=1024):
    B, dil, L, _ = qkv.shape
    tq = min(tq, L)
    sub = tq // ATT_BLOCK

    def cur(col):
        return pl.BlockSpec((None, None, tq, GROUP_WIDTH), lambda b, r, i: (b, r, i, col))

    def prev(col):
        return pl.BlockSpec((None, None, ATT_BLOCK, GROUP_WIDTH),
                            lambda b, r, i: (b, r, jnp.maximum(i * sub - 1, 0), col))

    return pl.pallas_call(
        functools.partial(_attn_body, tq=tq),
        grid=(B, dil, L // tq),
        in_specs=[cur(0), cur(1), prev(1), cur(2), prev(2),
                  pl.BlockSpec((HEADS_PER_GROUP, 2, ATT_BLOCK, 2 * ATT_BLOCK), lambda b, r, i: (0, 0, 0, 0))],
        out_specs=[pl.BlockSpec((None, None, tq, GROUP_WIDTH), lambda b, r, i: (b, r, i, 0)),
                   pl.BlockSpec((None, None, tq, LANES), lambda b, r, i: (b, r, i, 0))],
        out_shape=[jax.ShapeDtypeStruct((B, dil, L, GROUP_WIDTH), F32),
                   jax.ShapeDtypeStruct((B, dil, L, LANES), F32)],
        scratch_shapes=[pltpu.VMEM((ATT_BLOCK + tq, GROUP_WIDTH), BF16),
                        pltpu.VMEM((ATT_BLOCK + tq, GROUP_WIDTH), BF16),
                        pltpu.VMEM((2, HEADS_PER_GROUP, ATT_BLOCK, 2 * ATT_BLOCK), F32)],
        compiler_params=_params(("arbitrary", "arbitrary", "arbitrary")),
        name=f"attention_g{gi}",
    )(qkv, qkv, qkv, qkv, qkv, bm_g)


def _split_bf16(x):
    hi = x.astype(BF16)
    lo = (x - hi.astype(F32)).astype(BF16)
    return hi, lo


def _load_token_tiles(ref, n):
    return jnp.concatenate([ref[pl.ds(c, n, stride=SUBLANES), :] for c in range(D_MODEL // LANES)], axis=1)


def _store_token_tiles(ref, x, n):
    for c in range(D_MODEL // LANES):
        ref[pl.ds(c, n, stride=SUBLANES), :] = x[:, c * LANES:(c + 1) * LANES]


def _to_token_order(blk_ref, tok_ref, dil, tm):
    n = tm // dil
    for r in range(dil):
        for c in range(tok_ref.shape[0]):
            tok_ref[c, pl.ds(r, n, stride=dil), :] = blk_ref[r, :, c * LANES:(c + 1) * LANES]


def _mixer_out_body(o0, o1, o2, s0, s1, s2, glu_in, glu_halo, ga_ref, gc_ref, x_ref,
                    wdw_ref, bdw_ref, cg_ref, cb_ref, woa_ref, woc_ref, wout_ref, g1_ref, b1_ref,
                    h_ref, ht_ref, glu_ref, dw_ref, shift_ref, tok_o1, tok_s1, tok_o2, tok_s2, *, tm, chunk):
    i = pl.program_id(1)

    ncol = GROUP_WIDTH // LANES
    _to_token_order(o1, tok_o1, ATT_GROUPS[1][1], tm)
    _to_token_order(s1, tok_s1, ATT_GROUPS[1][1], tm)
    _to_token_order(o2, tok_o2, ATT_GROUPS[2][1], tm)
    _to_token_order(s2, tok_s2, ATT_GROUPS[2][1], tm)
    outs = [o0[0],
            jnp.concatenate([tok_o1[c] for c in range(ncol)], axis=1),
            jnp.concatenate([tok_o2[c] for c in range(ncol)], axis=1)]
    sts = [s0[0], tok_s1[0], tok_s2[0]]

    mx = jnp.maximum(jnp.maximum(sts[0], sts[1]), sts[2])
    wts = [jnp.exp(st - mx) for st in sts]
    wsum = sum(pltpu.roll(st, LANES - HEADS_PER_GROUP, axis=1) * e for st, e in zip(sts, wts))
    row = lax.broadcasted_iota(jnp.int32, (LANES, GROUP_WIDTH), 0)
    colh = lax.broadcasted_iota(jnp.int32, (LANES, GROUP_WIDTH), 1) // HEAD_DIM
    expand = (row == colh).astype(BF16)
    attn = jnp.zeros((tm, GROUP_WIDTH), F32)
    head_lane = lax.broadcasted_iota(jnp.int32, (tm, LANES), 1) < HEADS_PER_GROUP
    for wt, o in zip(wts, outs):
        c_hi, c_lo = _split_bf16(jnp.where(head_lane, wt / wsum, 0.0))
        c = (jnp.dot(c_hi, expand, preferred_element_type=F32)
             + jnp.dot(c_lo, expand, preferred_element_type=F32))
        attn = attn + c * o
    a_out = jnp.dot(attn.astype(BF16), woa_ref[...], preferred_element_type=F32)

    glu_ref[0:CONV_HALO] = jnp.where(i == 0, 0.0, glu_halo[...].astype(F32))
    glu_ref[CONV_HALO:] = glu_in[...].astype(F32)
    first_tap = CONV_HALO - (CONV_WIDTH - 1)
    for b in range(1, SUBLANES):
        shift_ref[b - 1] = glu_ref[b:b + shift_ref.shape[1], :]

    for c0 in range(0, CONV_CH, LANES):
        cs = slice(c0, c0 + LANES)
        bias = jnp.broadcast_to(bdw_ref[:, cs], (chunk, LANES))
        for r0 in range(0, tm, chunk):
            acc = bias
            for j in range(CONV_WIDTH):
                a, b = divmod(first_tap + j, SUBLANES)
                lo_row = r0 + a * SUBLANES
                rows = glu_ref[lo_row:lo_row + chunk, cs] if b == 0 else shift_ref[b - 1, lo_row:lo_row + chunk, cs]
                acc = acc + wdw_ref[j:j + 1, cs] * rows
            dw_ref[r0:r0 + chunk, cs] = acc
    cn = _layer_norm(dw_ref[...], cg_ref[...], cb_ref[...])
    conv = cn * _sigmoid(cn)
    c_out = jnp.dot(conv.astype(BF16), woc_ref[...], preferred_element_type=F32)

    merged = ga_ref[...].astype(F32) * a_out + gc_ref[...].astype(F32) * c_out
    mix = jnp.dot(merged.astype(BF16), wout_ref[...], preferred_element_type=F32)
    h = _layer_norm(DEEPNORM_ALPHA * x_ref[...] + mix, g1_ref[...], b1_ref[...])
    h_ref[...] = h
    _store_token_tiles(ht_ref, h, tm)


def mixer_out(o_list, st_list, glu3, gates3, x3, w_dw, b_dw, cg, cb, woa_b, woc_b, wout_b, g1, b1, tm=512, chunk=64):
    B, S, _ = x3.shape
    tm = min(tm, S)
    halo_blocks = tm // CONV_HALO
    ncol = GROUP_WIDTH // LANES

    def tile(width, col=0):
        return pl.BlockSpec((None, tm, width), lambda b, i: (b, i, col))

    def dilated(gi, width):
        dil = ATT_GROUPS[gi][1]
        return pl.BlockSpec((None, dil, tm // dil, width), lambda b, i: (b, 0, i, 0))

    halo = pl.BlockSpec((None, CONV_HALO, CONV_CH), lambda b, i: (b, jnp.maximum(i * halo_blocks - 1, 0), 0))

    def whole(shape):
        return pl.BlockSpec(shape, lambda b, i: (0,) * len(shape))

    return pl.pallas_call(
        functools.partial(_mixer_out_body, tm=tm, chunk=chunk),
        grid=(B, S // tm),
        in_specs=[dilated(g, GROUP_WIDTH) for g in range(N_GROUPS)] + [dilated(g, LANES) for g in range(N_GROUPS)]
        + [tile(CONV_CH), halo, tile(D_MODEL, 0), tile(D_MODEL, 1), tile(D_MODEL),
           whole((CONV_WIDTH, CONV_CH)), whole((1, CONV_CH)), whole((1, CONV_CH)), whole((1, CONV_CH)),
           whole((GROUP_WIDTH, D_MODEL)), whole((CONV_CH, D_MODEL)), whole((D_MODEL, D_MODEL)),
           whole((1, D_MODEL)), whole((1, D_MODEL))],
        out_specs=[tile(D_MODEL), pl.BlockSpec((tm * SUBLANES, LANES), lambda b, i: (b * (S // tm) + i, 0))],
        out_shape=[jax.ShapeDtypeStruct((B, S, D_MODEL), F32),
                   jax.ShapeDtypeStruct((B * S * SUBLANES, LANES), F32)],
        scratch_shapes=[pltpu.VMEM((CONV_HALO + tm, CONV_CH), F32), pltpu.VMEM((tm, CONV_CH), F32),
                        pltpu.VMEM((SUBLANES - 1, tm + CONV_HALO - SUBLANES, CONV_CH), F32),
                        pltpu.VMEM((ncol, tm, LANES), F32), pltpu.VMEM((1, tm, LANES), F32),
                        pltpu.VMEM((ncol, tm, LANES), F32), pltpu.VMEM((1, tm, LANES), F32)],
        compiler_params=_params(("arbitrary", "arbitrary")),
        name="mixer_out",
    )(*o_list, *st_list, glu3, glu3, gates3, gates3, x3,
      w_dw, b_dw, cg, cb, woa_b, woc_b, wout_b, g1, b1)


def _router_body(h_ref, wt_ref, b_ref, gates_ref, ids_ref, rank_ref, cnt_ref, carry_ref, tri_ref, *, tm):
    step = pl.program_id(0)

    @pl.when(step == 0)
    def _():
        carry_ref[...] = jnp.zeros_like(carry_ref)
        r_i = lax.broadcasted_iota(jnp.int32, (tm, tm), 0)
        c_i = lax.broadcasted_iota(jnp.int32, (tm, tm), 1)
        tri_ref[...] = (r_i < c_i).astype(BF16)

    nt_dims = (((1,), (1,)), ((), ()))
    h_hi, h_lo = _split_bf16(h_ref[...])
    w_hi, w_lo = _split_bf16(wt_ref[...])
    logits = (lax.dot_general(w_hi, h_hi, nt_dims, preferred_element_type=F32)
              + (lax.dot_general(w_hi, h_lo, nt_dims, preferred_element_type=F32)
                 + lax.dot_general(w_lo, h_hi, nt_dims, preferred_element_type=F32))
              + b_ref[...])
    expert = lax.broadcasted_iota(jnp.int32, (N_EXPERTS, tm), 0).astype(F32)
    row = lax.broadcasted_iota(jnp.int32, (SUBLANES, tm), 0)

    work = logits
    vals, hots = [], []
    ids = jnp.zeros((SUBLANES, tm), F32)
    for k in range(TOP_K):
        v = jnp.max(work, axis=0, keepdims=True)
        idx = jnp.min(jnp.where(work == v, expert, float(N_EXPERTS)), axis=0, keepdims=True)
        hot = expert == idx
        work = jnp.where(hot, -jnp.inf, work)
        ids = jnp.where(row == k, idx, ids)
        vals.append(v)
        hots.append(hot)

    es = [jnp.exp(v - vals[0]) for v in vals]
    esum = es[0] + es[1] + es[2] + es[3]
    gates = jnp.zeros((SUBLANES, tm), F32)
    for k in range(TOP_K):
        gates = jnp.where(row == k, es[k] / esum, gates)

    sel = (hots[0] | hots[1] | hots[2] | hots[3])
    before = jnp.dot(sel.astype(BF16), tri_ref[...], preferred_element_type=F32) + carry_ref[:, 0:1]
    rank = jnp.zeros((SUBLANES, tm), F32)
    for k in range(TOP_K):
        rk = jnp.sum(jnp.where(hots[k], before, 0.0), axis=0, keepdims=True)
        rank = jnp.where(row == k, rk, rank)
    carry_ref[...] = carry_ref[...] + jnp.sum(sel.astype(F32), axis=1, keepdims=True)

    gates_ref[...] = gates
    ids_ref[...] = ids.astype(jnp.int32)
    rank_ref[...] = rank.astype(jnp.int32)
    cnt_ref[...] = carry_ref[...].astype(jnp.int32)


def router(h2, w_router, b_router, tm=512):
    T = h2.shape[0]
    tm = min(tm, T)
    tile = pl.BlockSpec((SUBLANES, tm), lambda i: (0, i))
    return pl.pallas_call(
        functools.partial(_router_body, tm=tm),
        grid=(T // tm,),
        in_specs=[pl.BlockSpec((tm, D_MODEL), lambda i: (i, 0)),
                  pl.BlockSpec((N_EXPERTS, D_MODEL), lambda i: (0, 0)),
                  pl.BlockSpec((N_EXPERTS, 1), lambda i: (0, 0))],
        out_specs=[tile, tile, tile, pl.BlockSpec((N_EXPERTS, LANES), lambda i: (0, 0))],
        out_shape=[jax.ShapeDtypeStruct((SUBLANES, T), F32), jax.ShapeDtypeStruct((SUBLANES, T), jnp.int32),
                   jax.ShapeDtypeStruct((SUBLANES, T), jnp.int32), jax.ShapeDtypeStruct((N_EXPERTS, LANES), jnp.int32)],
        scratch_shapes=[pltpu.VMEM((N_EXPERTS, LANES), F32), pltpu.VMEM((tm, tm), BF16)],
        compiler_params=_params(("arbitrary",)),
        name="router",
    )(h2, w_router.astype(F32).T, b_router.astype(F32).reshape(N_EXPERTS, 1))


def _token_copy(src, dst, s_tok, d_tok, sem, n=1):
    rows = n * SUBLANES
    return pltpu.make_async_copy(src.at[pl.ds(pl.multiple_of(s_tok * SUBLANES, SUBLANES), rows), :],
                                 dst.at[pl.ds(pl.multiple_of(d_tok * SUBLANES, SUBLANES), rows), :], sem)


def _dispatch_body(zstart_ref, zcount_ref, n_used_ref, pos_ref, h_ref, xs_ref, zeros_ref, ring, sems, zsem, *, tm, nb):
    half = EXPERT_BLOCK // 2
    bits = [1 << s for s in reversed(range(half.bit_length()))]

    def zero_fill(e, wait):
        start, count = zstart_ref[e], zcount_ref[e]
        for bit in bits:
            @pl.when((count & bit) != 0)
            def _():
                cp = _token_copy(zeros_ref, xs_ref, 0, 0 if wait else start + (count & ~(2 * bit - 1)), zsem, bit)
                cp.wait() if wait else cp.start()

    def zero_tail(blk, wait):
        for part in range(2):
            cp = _token_copy(zeros_ref, xs_ref, 0, 0 if wait else blk * EXPERT_BLOCK + part * half, zsem, half)
            cp.wait() if wait else cp.start()

    @pl.when(pl.program_id(0) == 0)
    def _():
        zeros_ref[...] = jnp.zeros_like(zeros_ref)
        lax.fori_loop(0, N_EXPERTS, lambda e, c: (zero_fill(e, False), c)[1], 0)
        lax.fori_loop(n_used_ref[0], nb, lambda blk, c: (zero_tail(blk, False), c)[1], 0)

    i = pl.program_id(0)
    last = pl.num_programs(0) - 1

    def retire(slot):
        for k in range(TOP_K):
            _token_copy(ring.at[slot], xs_ref, 0, 0, sems.at[slot], tm).wait()

    def step(slot):
        pl.when(i >= 2)(functools.partial(retire, slot))
        ring[slot] = h_ref[...]

        def issue(t, c):
            for k in range(TOP_K):
                _token_copy(ring.at[slot], xs_ref, t, pos_ref[k * tm + t], sems.at[slot]).start(priority=k % 2)
            return c

        lax.fori_loop(0, tm, issue, 0)

        @pl.when(i == last)
        def _():
            retire(slot)
            pl.when(i >= 1)(functools.partial(retire, 1 - slot))

    for slot in range(2):
        pl.when(i % 2 == slot)(functools.partial(step, slot))

    @pl.when(pl.program_id(0) == 0)
    def _():
        lax.fori_loop(0, N_EXPERTS, lambda e, c: (zero_fill(e, True), c)[1], 0)
        lax.fori_loop(n_used_ref[0], nb, lambda blk, c: (zero_tail(blk, True), c)[1], 0)


def _tile_major(pos_t, tm):
    T = pos_t.shape[1]
    return pos_t[:TOP_K].reshape(TOP_K, T // tm, tm).transpose(1, 0, 2).reshape(-1)


def dispatch(ht, pos_t, zstart, zcount, n_used, n_rows, tm=1024):
    T = ht.shape[0] // SUBLANES
    tm = min(tm, T)
    grid_spec = pltpu.PrefetchScalarGridSpec(
        num_scalar_prefetch=3,
        grid=(T // tm,),
        in_specs=[pl.BlockSpec((TOP_K * tm,), lambda i, zs, zc, nu: (i,), memory_space=pltpu.SMEM),
                  pl.BlockSpec((tm * SUBLANES, LANES), lambda i, zs, zc, nu: (i, 0))],
        out_specs=pl.BlockSpec(memory_space=pl.ANY),
        scratch_shapes=[pltpu.VMEM((EXPERT_BLOCK // 2 * SUBLANES, LANES), F32),
                        pltpu.VMEM((2, tm * SUBLANES, LANES), F32), pltpu.SemaphoreType.DMA((2,)),
                        pltpu.SemaphoreType.DMA],
    )
    return pl.pallas_call(
        functools.partial(_dispatch_body, tm=tm, nb=n_rows // EXPERT_BLOCK),
        grid_spec=grid_spec,
        out_shape=jax.ShapeDtypeStruct((n_rows * SUBLANES, LANES), F32),
        compiler_params=_params(("arbitrary",)),
        name="dispatch",
    )(zstart, zcount, n_used, _tile_major(pos_t, tm), ht)


def _combine_body(pos_ref, pos_next_ref, gates_ref, h_ref, g2_ref, b2_ref, ys_ref, o_ref, buf, sems, *, tm):
    i = pl.program_id(0)

    def issue(p_ref, s):
        def body(t, c):
            for k in range(TOP_K):
                _token_copy(ys_ref, buf.at[s, k], p_ref[k * tm + t], t, sems.at[s]).start(priority=k % 2)
            return c

        lax.fori_loop(0, tm, body, 0)

    @pl.when(i == 0)
    def _():
        issue(pos_ref, 0)

    def step(slot):
        @pl.when(i + 1 < pl.num_programs(0))
        def _():
            issue(pos_next_ref, 1 - slot)

        for k in range(TOP_K):
            _token_copy(ys_ref, buf.at[slot, k], 0, 0, sems.at[slot], tm).wait()

        gates = gates_ref[...].T
        ffn = gates[:, 0:1] * _load_token_tiles(buf.at[slot, 0], tm)
        for k in range(1, TOP_K):
            ffn = ffn + gates[:, k:k + 1] * _load_token_tiles(buf.at[slot, k], tm)
        o_ref[...] = _layer_norm(DEEPNORM_ALPHA * h_ref[...] + ffn, g2_ref[...], b2_ref[...])

    for slot in range(2):
        pl.when(i % 2 == slot)(functools.partial(step, slot))


def combine(ys, pos_t, gates_t, h2, g2, b2, tm=256):
    T = h2.shape[0]
    tm = min(tm, T)
    last = T // tm - 1
    pos_flat = _tile_major(pos_t, tm)
    return pl.pallas_call(
        functools.partial(_combine_body, tm=tm),
        grid=(T // tm,),
        in_specs=[pl.BlockSpec((TOP_K * tm,), lambda i: (i,), memory_space=pltpu.SMEM),
                  pl.BlockSpec((TOP_K * tm,), lambda i: (jnp.minimum(i + 1, last),), memory_space=pltpu.SMEM),
                  pl.BlockSpec((SUBLANES, tm), lambda i: (0, i)),
                  pl.BlockSpec((tm, D_MODEL), lambda i: (i, 0)),
                  pl.BlockSpec((1, D_MODEL), lambda i: (0, 0)),
                  pl.BlockSpec((1, D_MODEL), lambda i: (0, 0)),
                  pl.BlockSpec(memory_space=pl.ANY)],
        out_specs=pl.BlockSpec((tm, D_MODEL), lambda i: (i, 0)),
        out_shape=jax.ShapeDtypeStruct((T, D_MODEL), F32),
        scratch_shapes=[pltpu.VMEM((2, TOP_K, tm * SUBLANES, LANES), F32), pltpu.SemaphoreType.DMA((2,))],
        compiler_params=_params(("arbitrary",)),
        name="combine",
    )(pos_flat, pos_flat, gates_t, h2, g2, b2, ys)


def _experts_body(blk_e, blk_src, n_used, next_e, wslot, xs_ref, wgu_hbm, bgu_ref, wdn_hbm, bdn_ref, ys_ref,
                  wgu_f, wdn_f, wgu_b, wdn_b, sems):
    del blk_src
    i = pl.program_id(0)
    used = i < n_used[0]
    e = blk_e[i]

    def weight_copies(expert, slot):
        return (pltpu.make_async_copy(wgu_hbm.at[expert], wgu_f.at[slot], sems.at[slot]),
                pltpu.make_async_copy(wdn_hbm.at[expert], wdn_f.at[slot], sems.at[slot]))

    @pl.when(i == 0)
    def _():
        for cp in weight_copies(e, 0):
            cp.start()

    @pl.when((i == 0) | (e != blk_e[jnp.maximum(i - 1, 0)]))
    def _():
        nxt = next_e[e]
        for slot in range(2):
            @pl.when(wslot[e] == slot)
            def _():
                for cp in weight_copies(e, slot):
                    cp.wait()
                wgu_b[...] = wgu_f[slot].astype(BF16)
                wdn_b[...] = wdn_f[slot].astype(BF16)

                @pl.when(nxt < N_EXPERTS)
                def _():
                    for cp in weight_copies(nxt, 1 - slot):
                        cp.start()

    @pl.when(used)
    def _():
        x = _load_token_tiles(xs_ref, EXPERT_BLOCK).astype(BF16)
        hgu = jnp.dot(x, wgu_b[...], preferred_element_type=F32) + bgu_ref[...]
        gate = jnp.minimum(hgu[:, :D_FF], SWIGLU_LIMIT)
        up = jnp.clip(hgu[:, D_FF:], -SWIGLU_LIMIT, SWIGLU_LIMIT)
        act = (up + 1.0) * gate * _sigmoid(SWIGLU_ALPHA * gate)
        y = jnp.dot(act.astype(BF16), wdn_b[...], preferred_element_type=F32) + bdn_ref[...]
        _store_token_tiles(ys_ref, y, EXPERT_BLOCK)

    @pl.when(jnp.logical_not(used))
    def _():
        ys_ref[...] = jnp.zeros_like(ys_ref)


def experts(xs, blk_e, blk_src, n_used, next_e, wslot, wgu, bgu, wdn, bdn):
    n_rows = xs.shape[0] // SUBLANES
    nb = n_rows // EXPERT_BLOCK
    rows = EXPERT_BLOCK * SUBLANES
    grid_spec = pltpu.PrefetchScalarGridSpec(
        num_scalar_prefetch=5,
        grid=(nb,),
        in_specs=[pl.BlockSpec((rows, LANES), lambda i, e, s, n, ne, ws: (s[i], 0)),
                  pl.BlockSpec(memory_space=pl.ANY),
                  pl.BlockSpec((None, 1, 2 * D_FF), lambda i, e, s, n, ne, ws: (e[i], 0, 0)),
                  pl.BlockSpec(memory_space=pl.ANY),
                  pl.BlockSpec((None, 1, D_MODEL), lambda i, e, s, n, ne, ws: (e[i], 0, 0))],
        out_specs=pl.BlockSpec((rows, LANES), lambda i, e, s, n, ne, ws: (i, 0)),
        scratch_shapes=[pltpu.VMEM((2, D_MODEL, 2 * D_FF), F32), pltpu.VMEM((2, D_FF, D_MODEL), F32),
                        pltpu.VMEM((D_MODEL, 2 * D_FF), BF16), pltpu.VMEM((D_FF, D_MODEL), BF16),
                        pltpu.SemaphoreType.DMA((2,))],
    )
    return pl.pallas_call(
        _experts_body,
        grid_spec=grid_spec,
        out_shape=jax.ShapeDtypeStruct((n_rows * SUBLANES, LANES), F32),
        compiler_params=_params(("arbitrary",)),
        name="experts",
    )(blk_e, blk_src, n_used, next_e, wslot, xs, wgu, bgu, wdn, bdn)


def routing_layout(ids, rank, counts, n_blocks):
    padded = (counts + EXPERT_BLOCK - 1) // EXPERT_BLOCK * EXPERT_BLOCK
    pad_ends = jnp.cumsum(padded)
    pad_starts = pad_ends - padded
    start_of = jnp.zeros_like(ids)
    for e in range(N_EXPERTS):
        start_of = jnp.where(ids == e, pad_starts[e], start_of)
    pos_t = (start_of + rank).astype(jnp.int32)
    n_used = jnp.maximum(pad_ends[-1] // EXPERT_BLOCK, 1).astype(jnp.int32)
    blk = jnp.minimum(jnp.arange(n_blocks, dtype=jnp.int32), n_used - 1)
    blk_e = jnp.sum(pad_ends[None, :] <= (blk * EXPERT_BLOCK)[:, None], axis=1)
    blk_e = jnp.minimum(blk_e, N_EXPERTS - 1).astype(jnp.int32)
    zstart = (pad_starts + counts).astype(jnp.int32)
    zcount = (padded - counts).astype(jnp.int32)
    nonempty = counts > 0
    expert_ids = jnp.arange(N_EXPERTS, dtype=jnp.int32)
    later = jnp.where(nonempty[None, :] & (expert_ids[None, :] > expert_ids[:, None]), expert_ids[None, :], N_EXPERTS)
    next_e = jnp.min(later, axis=1).astype(jnp.int32)
    wslot = ((jnp.cumsum(nonempty) - 1) % 2).astype(jnp.int32)
    return pos_t, blk_e, blk, n_used.reshape(1), zstart, zcount, next_e, wslot


def kernel(x, w_in, rel_bias, w_dw, b_dw, conv_ln_g, conv_ln_b, w_o_attn, w_o_conv, w_out, ln1_g, ln1_b,
           w_router, b_router, w_gate_up, b_gate_up, w_down, b_down, ln2_g, ln2_b):
    B, S, D = x.shape
    T = B * S
    h = x
    bm = bias_tables(rel_bias)
    q_off, k_off, v_off, rest = 0, QKV_WIDTH, 2 * QKV_WIDTH, 3 * QKV_WIDTH
    for l in range(DEPTH):
        wb = w_in[l].astype(BF16)

        def group_cols(gi):
            return [wb[:, off + gi * GROUP_WIDTH:off + (gi + 1) * GROUP_WIDTH] for off in (q_off, k_off, v_off)]

        w_main = jnp.concatenate(group_cols(0) + [wb[:, rest:]], axis=1)
        qkv0, glu, gates, *x_dilated = in_proj_dilate(h, w_main, "in_proj_main")
        x_dil = [None] + x_dilated
        o_list, st_list = [], []
        for gi in range(N_GROUPS):
            dil = ATT_GROUPS[gi][1]
            if dil == 1:
                qkv = qkv0.reshape(B, 1, S, qkv0.shape[-1])
            else:
                w_g = jnp.concatenate(group_cols(gi), axis=1)
                qkv = in_proj(x_dil[gi].reshape(T, D), w_g, w_g.shape[1], f"in_proj_g{gi}")
                qkv = qkv.reshape(B, dil, S // dil, w_g.shape[1])
            o_g, st_g = attention_group(qkv, bm[gi], gi)
            o_list.append(o_g)
            st_list.append(st_g)
        h1, h1_tiles = mixer_out(o_list, st_list, glu, gates, h,
                       w_dw[l].reshape(CONV_WIDTH, CONV_CH), b_dw[l].reshape(1, CONV_CH),
                       conv_ln_g[l].reshape(1, CONV_CH), conv_ln_b[l].reshape(1, CONV_CH),
                       w_o_attn[l].astype(BF16), w_o_conv[l].astype(BF16), w_out[l].astype(BF16),
                       ln1_g[l].reshape(1, D), ln1_b[l].reshape(1, D))
        h2 = h1.reshape(T, D)
        gates_t, ids_t, rank_t, counts = router(h2, w_router[l], b_router[l])
        n_rows = T * TOP_K + N_EXPERTS * EXPERT_BLOCK
        pos_t, blk_e, blk_src, n_used, zstart, zcount, next_e, wslot = routing_layout(
            ids_t, rank_t, counts[:, 0], n_rows // EXPERT_BLOCK)
        xs = dispatch(h1_tiles, pos_t, zstart, zcount, n_used, n_rows)
        ys = experts(xs, blk_e, blk_src, n_used, next_e, wslot,
                     w_gate_up[l], b_gate_up[l].reshape(N_EXPERTS, 1, 2 * D_FF),
                     w_down[l], b_down[l].reshape(N_EXPERTS, 1, D))
        out = combine(ys, pos_t, gates_t, h2, ln2_g[l].reshape(1, D), ln2_b[l].reshape(1, D))
        h = out.reshape(B, S, D)
    return h
```

```python
import functools
import math

import jax
import jax.numpy as jnp
from jax import lax
from jax.experimental import pallas as pl
from jax.experimental.pallas import tpu as pltpu

D_MODEL = 1024
ATT_GROUPS = ((128, 1), (512, 4), (2048, 16))
N_GROUPS = len(ATT_GROUPS)
HEADS_PER_GROUP = 8
HEAD_DIM = 64
GROUP_WIDTH = HEADS_PER_GROUP * HEAD_DIM
QKV_WIDTH = N_GROUPS * GROUP_WIDTH
ATT_BLOCK = 128
N_BUCKETS = 32
MAX_DISTANCE = 2048
CONV_CH = 768
CONV_WIDTH = 31
CONV_HALO = 32
N_EXPERTS = 32
TOP_K = 4
D_FF = 1024
SWIGLU_LIMIT = 7.0
SWIGLU_ALPHA = 1.702
LN_EPS = 1e-5
NEG_INF = -1e30
DEPTH = 1
DEEPNORM_ALPHA = (2 * DEPTH) ** 0.25

LANES = 128
SUBLANES = 8
EXPERT_BLOCK = 512
VMEM_LIMIT = 56 * 1024 * 1024

F32 = jnp.float32
BF16 = jnp.bfloat16


def _params(sem, vmem=VMEM_LIMIT):
    return pltpu.CompilerParams(dimension_semantics=sem, vmem_limit_bytes=vmem)


def _sigmoid(x):
    return 0.5 * jnp.tanh(0.5 * x) + 0.5


def _layer_norm(x, g, b):
    mu = jnp.mean(x, axis=-1, keepdims=True)
    xc = x - mu
    var = jnp.mean(xc * xc, axis=-1, keepdims=True)
    return xc * lax.rsqrt(var + LN_EPS) * g + b


def _in_proj_body(x_ref, w_ref, o_ref):
    o_ref[...] = jnp.dot(x_ref[...], w_ref[...], preferred_element_type=F32).astype(o_ref.dtype)


def in_proj(xb, w_b, tn, name, tm=1024):
    T = xb.shape[0]
    N = w_b.shape[1]
    tm = min(tm, T)
    return pl.pallas_call(
        _in_proj_body,
        grid=(N // tn, T // tm),
        in_specs=[pl.BlockSpec((tm, D_MODEL), lambda n, m: (m, 0)),
                  pl.BlockSpec((D_MODEL, tn), lambda n, m: (0, n))],
        out_specs=pl.BlockSpec((tm, tn), lambda n, m: (m, n)),
        out_shape=jax.ShapeDtypeStruct((T, N), BF16),
        compiler_params=_params(("arbitrary", "arbitrary")),
        name=name,
    )(xb, w_b)


QKV_COLS = 3 * GROUP_WIDTH


def _in_proj_dilate_body(*refs, tm, nslab):
    x_slabs, w_ref = refs[:nslab], refs[nslab]
    qkv_ref, glu_ref, gate_ref = refs[nslab + 1:nslab + 4]
    d_refs = refs[nslab + 4:]
    cols = []
    for c, x_ref in enumerate(x_slabs):
        cs = slice(c * LANES, (c + 1) * LANES)
        cols.append(x_ref[...].astype(BF16))
        for (_, dil), d_ref in zip(ATT_GROUPS[1:], d_refs):
            for r in range(dil):
                d_ref[r, :, cs] = x_ref[pl.ds(r, tm // dil, stride=dil), :].astype(BF16)
    xb = jnp.concatenate(cols, axis=1)
    proj = jnp.dot(xb, w_ref[...], preferred_element_type=F32)
    qkv_ref[...] = proj[:, :QKV_COLS].astype(qkv_ref.dtype)
    u_val = proj[:, QKV_COLS:QKV_COLS + CONV_CH]
    u_gate = proj[:, QKV_COLS + CONV_CH:QKV_COLS + 2 * CONV_CH]
    glu_ref[...] = (u_val * _sigmoid(u_gate)).astype(glu_ref.dtype)
    gate_ref[...] = _sigmoid(proj[:, QKV_COLS + 2 * CONV_CH:]).astype(gate_ref.dtype)


def in_proj_dilate(x3, w_b, name, tm=512):
    B, S, D = x3.shape
    N = w_b.shape[1]
    tm = min(tm, S)
    nslab = D // LANES
    dils = [dil for _, dil in ATT_GROUPS[1:]]
    per_seq = S // tm
    widths = (QKV_COLS, CONV_CH, 2 * D_MODEL)
    assert N == QKV_COLS + 2 * CONV_CH + 2 * D_MODEL
    return pl.pallas_call(
        functools.partial(_in_proj_dilate_body, tm=tm, nslab=nslab),
        grid=(B, per_seq),
        in_specs=[pl.BlockSpec((None, tm, LANES), functools.partial(lambda c, b, i: (b, i, c), c))
                  for c in range(nslab)]
        + [pl.BlockSpec((D, N), lambda b, i: (0, 0))],
        out_specs=[pl.BlockSpec((None, tm, w), lambda b, i: (b, i, 0)) for w in widths]
        + [pl.BlockSpec((None, dil, tm // dil, D), lambda b, i: (b, 0, i, 0)) for dil in dils],
        out_shape=[jax.ShapeDtypeStruct((B, S, w), BF16) for w in widths]
        + [jax.ShapeDtypeStruct((B, dil, S // dil, D), BF16) for dil in dils],
        compiler_params=_params(("arbitrary", "arbitrary")),
        name=name,
    )(*([x3] * nslab), w_b)


def _t5_bucket(dist):
    max_exact = N_BUCKETS // 2
    log_ratio = jnp.log(jnp.maximum(dist, max_exact).astype(F32) / max_exact) / math.log(MAX_DISTANCE / max_exact)
    large = jnp.minimum(max_exact + (log_ratio * (N_BUCKETS - max_exact)).astype(jnp.int32), N_BUCKETS - 1)
    return jnp.where(dist < max_exact, dist, large)


def _bias_body(tbl_ref, bucket_ref, band_ref, o_ref):
    g = pl.program_id(0)
    h = pl.program_id(1)
    col = g * HEADS_PER_GROUP + h
    bucket = bucket_ref[...]
    acc = jnp.zeros(bucket.shape, F32)
    for k in range(N_BUCKETS):
        acc = jnp.where(bucket == k, tbl_ref[k, col], acc)
    band = band_ref[...] != 0
    kj = lax.broadcasted_iota(jnp.int32, bucket.shape, 1)
    o_ref[0] = jnp.where(band, acc, NEG_INF)
    o_ref[1] = jnp.where(band & (kj >= ATT_BLOCK), acc, NEG_INF)


def bias_tables(rel_bias):
    qi = jnp.arange(ATT_BLOCK)[:, None]
    kj = jnp.arange(2 * ATT_BLOCK)[None, :]
    dist = qi - kj + ATT_BLOCK
    buckets, bands = [], []
    for window, dil in ATT_GROUPS:
        bands.append(((dist >= 0) & (dist <= window // dil)).astype(jnp.int32))
        buckets.append(_t5_bucket(jnp.maximum(dist, 0) * dil).astype(jnp.int32))
    buckets = jnp.stack(buckets)
    bands = jnp.stack(bands)
    blk = (None, ATT_BLOCK, 2 * ATT_BLOCK)
    return pl.pallas_call(
        _bias_body,
        grid=(N_GROUPS, HEADS_PER_GROUP),
        in_specs=[pl.BlockSpec(memory_space=pltpu.SMEM),
                  pl.BlockSpec(blk, lambda g, h: (g, 0, 0)),
                  pl.BlockSpec(blk, lambda g, h: (g, 0, 0))],
        out_specs=pl.BlockSpec((None, None, 2, ATT_BLOCK, 2 * ATT_BLOCK), lambda g, h: (g, h, 0, 0, 0)),
        out_shape=jax.ShapeDtypeStruct((N_GROUPS, HEADS_PER_GROUP, 2, ATT_BLOCK, 2 * ATT_BLOCK), F32),
        compiler_params=_params(("arbitrary", "arbitrary")),
        name="bias_tables",
    )(rel_bias.astype(F32), buckets, bands)


def _attn_body(q_ref, kc_ref, kp_ref, vc_ref, vp_ref, bm_ref, o_ref, st_ref, k_all, v_all, s_buf, *, tq):
    i = pl.program_id(2)
    nsub = tq // ATT_BLOCK
    k_all[0:ATT_BLOCK] = kp_ref[...]
    k_all[ATT_BLOCK:] = kc_ref[...]
    v_all[0:ATT_BLOCK] = vp_ref[...]
    v_all[ATT_BLOCK:] = vc_ref[...]
    lo = lax.broadcasted_iota(jnp.int32, (ATT_BLOCK, LANES), 1) < HEAD_DIM
    nt_dims = (((1,), (1,)), ((), ()))
    first = jnp.where(i == 0, 1, 0)

    def scores(s, slot):
        r0 = s * ATT_BLOCK
        for j in range(HEADS_PER_GROUP // 2):
            cs = slice(j * LANES, (j + 1) * LANES)
            qp = q_ref[r0:r0 + ATT_BLOCK, cs] * jnp.asarray(HEAD_DIM ** -0.5, BF16)
            kp = k_all[r0:r0 + 2 * ATT_BLOCK, cs]
            for hh in range(2):
                h = 2 * j + hh
                qh = jnp.where(lo if hh == 0 else ~lo, qp, jnp.zeros_like(qp))
                bias = bm_ref[h, first] if s == 0 else bm_ref[h, 0]
                s_buf[slot, h] = lax.dot_general(qh, kp, nt_dims, preferred_element_type=F32) + bias

    def softmax_pv(s, slot):
        rows = slice(s * ATT_BLOCK, (s + 1) * ATT_BLOCK)
        st_ref[rows, :] = jnp.zeros((ATT_BLOCK, LANES), F32)
        for j in range(HEADS_PER_GROUP // 2):
            vp = v_all[s * ATT_BLOCK:(s + 2) * ATT_BLOCK, j * LANES:(j + 1) * LANES]
            for hh in range(2):
                h = 2 * j + hh
                sc = s_buf[slot, h]
                m = jnp.max(sc, axis=-1, keepdims=True)
                p = jnp.exp(sc - m)
                den = jnp.sum(p, axis=-1, keepdims=True)
                pv = jnp.dot(p.astype(BF16), vp, preferred_element_type=F32)
                c0 = h * HEAD_DIM
                o_ref[rows, c0:c0 + HEAD_DIM] = pv[:, hh * HEAD_DIM:(hh + 1) * HEAD_DIM]
                st_ref[rows, h:h + 1] = m
                st_ref[rows, HEADS_PER_GROUP + h:HEADS_PER_GROUP + h + 1] = den

    scores(0, 0)
    for s in range(nsub):
        softmax_pv(s, s % 2)
        if s + 1 < nsub:
            scores(s + 1, (s + 1) % 2)


def attention_group(qkv, bm_g, gi, tq=1024):
    B, dil, L, _ = qkv.shape
    tq = min(tq, L)
    sub = tq // ATT_BLOCK

    def cur(col):
        return pl.BlockSpec((None, None, tq, GROUP_WIDTH), lambda b, r, i: (b, r, i, col))

    def prev(col):
        return pl.BlockSpec((None, None, ATT_BLOCK, GROUP_WIDTH),
                            lambda b, r, i: (b, r, jnp.maximum(i * sub - 1, 0), col))

    return pl.pallas_call(
        functools.partial(_attn_body, tq=tq),
        grid=(B, dil, L // tq),
        in_specs=[cur(0), cur(1), prev(1), cur(2), prev(2),
                  pl.BlockSpec((HEADS_PER_GROUP, 2, ATT_BLOCK, 2 * ATT_BLOCK), lambda b, r, i: (0, 0, 0, 0))],
        out_specs=[pl.BlockSpec((None, None, tq, GROUP_WIDTH), lambda b, r, i: (b, r, i, 0)),
                   pl.BlockSpec((None, None, tq, LANES), lambda b, r, i: (b, r, i, 0))],
        out_shape=[jax.ShapeDtypeStruct((B, dil, L, GROUP_WIDTH), F32),
                   jax.ShapeDtypeStruct((B, dil, L, LANES), F32)],
        scratch_shapes=[pltpu.VMEM((ATT_BLOCK + tq, GROUP_WIDTH), BF16),
                        pltpu.VMEM((ATT_BLOCK + tq, GROUP_WIDTH), BF16),
                        pltpu.VMEM((2, HEADS_PER_GROUP, ATT_BLOCK, 2 * ATT_BLOCK), F32)],
        compiler_params=_params(("arbitrary", "arbitrary", "arbitrary")),
        name=f"attention_g{gi}",
    )(qkv, qkv, qkv, qkv, qkv, bm_g)


def _split_bf16(x):
    hi = x.astype(BF16)
    lo = (x - hi.astype(F32)).astype(BF16)
    return hi, lo


def _load_token_tiles(ref, n):
    return jnp.concatenate([ref[pl.ds(c, n, stride=SUBLANES), :] for c in range(D_MODEL // LANES)], axis=1)


def _store_token_tiles(ref, x, n):
    for c in range(D_MODEL // LANES):
        ref[pl.ds(c, n, stride=SUBLANES), :] = x[:, c * LANES:(c + 1) * LANES]


def _to_token_order(blk_ref, tok_ref, dil, tm):
    n = tm // dil
    for r in range(dil):
        for c in range(tok_ref.shape[0]):
            tok_ref[c, pl.ds(r, n, stride=dil), :] = blk_ref[r, :, c * LANES:(c + 1) * LANES]


def _mixer_out_body(o0, o1, o2, s0, s1, s2, glu_in, glu_halo, ga_ref, gc_ref, x_ref,
                    wdw_ref, bdw_ref, cg_ref, cb_ref, woa_ref, woc_ref, wout_ref, g1_ref, b1_ref,
                    h_ref, ht_ref, glu_ref, dw_ref, shift_ref, tok_o1, tok_s1, tok_o2, tok_s2, *, tm, chunk):
    i = pl.program_id(1)

    ncol = GROUP_WIDTH // LANES
    _to_token_order(o1, tok_o1, ATT_GROUPS[1][1], tm)
    _to_token_order(s1, tok_s1, ATT_GROUPS[1][1], tm)
    _to_token_order(o2, tok_o2, ATT_GROUPS[2][1], tm)
    _to_token_order(s2, tok_s2, ATT_GROUPS[2][1], tm)
    outs = [o0[0],
            jnp.concatenate([tok_o1[c] for c in range(ncol)], axis=1),
            jnp.concatenate([tok_o2[c] for c in range(ncol)], axis=1)]
    sts = [s0[0], tok_s1[0], tok_s2[0]]

    mx = jnp.maximum(jnp.maximum(sts[0], sts[1]), sts[2])
    wts = [jnp.exp(st - mx) for st in sts]
    wsum = sum(pltpu.roll(st, LANES - HEADS_PER_GROUP, axis=1) * e for st, e in zip(sts, wts))
    row = lax.broadcasted_iota(jnp.int32, (LANES, GROUP_WIDTH), 0)
    colh = lax.broadcasted_iota(jnp.int32, (LANES, GROUP_WIDTH), 1) // HEAD_DIM
    expand = (row == colh).astype(BF16)
    attn = jnp.zeros((tm, GROUP_WIDTH), F32)
    head_lane = lax.broadcasted_iota(jnp.int32, (tm, LANES), 1) < HEADS_PER_GROUP
    for wt, o in zip(wts, outs):
        c_hi, c_lo = _split_bf16(jnp.where(head_lane, wt / wsum, 0.0))
        c = (jnp.dot(c_hi, expand, preferred_element_type=F32)
             + jnp.dot(c_lo, expand, preferred_element_type=F32))
        attn = attn + c * o
    a_out = jnp.dot(attn.astype(BF16), woa_ref[...], preferred_element_type=F32)

    glu_ref[0:CONV_HALO] = jnp.where(i == 0, 0.0, glu_halo[...].astype(F32))
    glu_ref[CONV_HALO:] = glu_in[...].astype(F32)
    first_tap = CONV_HALO - (CONV_WIDTH - 1)
    for b in range(1, SUBLANES):
        shift_ref[b - 1] = glu_ref[b:b + shift_ref.shape[1], :]

    for c0 in range(0, CONV_CH, LANES):
        cs = slice(c0, c0 + LANES)
        bias = jnp.broadcast_to(bdw_ref[:, cs], (chunk, LANES))
        for r0 in range(0, tm, chunk):
            acc = bias
            for j in range(CONV_WIDTH):
                a, b = divmod(first_tap + j, SUBLANES)
                lo_row = r0 + a * SUBLANES
                rows = glu_ref[lo_row:lo_row + chunk, cs] if b == 0 else shift_ref[b - 1, lo_row:lo_row + chunk, cs]
                acc = acc + wdw_ref[j:j + 1, cs] * rows
            dw_ref[r0:r0 + chunk, cs] = acc
    cn = _layer_norm(dw_ref[...], cg_ref[...], cb_ref[...])
    conv = cn * _sigmoid(cn)
    c_out = jnp.dot(conv.astype(BF16), woc_ref[...], preferred_element_type=F32)

    merged = ga_ref[...].astype(F32) * a_out + gc_ref[...].astype(F32) * c_out
    mix = jnp.dot(merged.astype(BF16), wout_ref[...], preferred_element_type=F32)
    h = _layer_norm(DEEPNORM_ALPHA * x_ref[...] + mix, g1_ref[...], b1_ref[...])
    h_ref[...] = h
    _store_token_tiles(ht_ref, h, tm)


def mixer_out(o_list, st_list, glu3, gates3, x3, w_dw, b_dw, cg, cb, woa_b, woc_b, wout_b, g1, b1, tm=512, chunk=64):
    B, S, _ = x3.shape
    tm = min(tm, S)
    halo_blocks = tm // CONV_HALO
    ncol = GROUP_WIDTH // LANES

    def tile(width, col=0):
        return pl.BlockSpec((None, tm, width), lambda b, i: (b, i, col))

    def dilated(gi, width):
        dil = ATT_GROUPS[gi][1]
        return pl.BlockSpec((None, dil, tm // dil, width), lambda b, i: (b, 0, i, 0))

    halo = pl.BlockSpec((None, CONV_HALO, CONV_CH), lambda b, i: (b, jnp.maximum(i * halo_blocks - 1, 0), 0))

    def whole(shape):
        return pl.BlockSpec(shape, lambda b, i: (0,) * len(shape))

    return pl.pallas_call(
        functools.partial(_mixer_out_body, tm=tm, chunk=chunk),
        grid=(B, S // tm),
        in_specs=[dilated(g, GROUP_WIDTH) for g in range(N_GROUPS)] + [dilated(g, LANES) for g in range(N_GROUPS)]
        + [tile(CONV_CH), halo, tile(D_MODEL, 0), tile(D_MODEL, 1), tile(D_MODEL),
           whole((CONV_WIDTH, CONV_CH)), whole((1, CONV_CH)), whole((1, CONV_CH)), whole((1, CONV_CH)),
           whole((GROUP_WIDTH, D_MODEL)), whole((CONV_CH, D_MODEL)), whole((D_MODEL, D_MODEL)),
           whole((1, D_MODEL)), whole((1, D_MODEL))],
        out_specs=[tile(D_MODEL), pl.BlockSpec((tm * SUBLANES, LANES), lambda b, i: (b * (S // tm) + i, 0))],
        out_shape=[jax.ShapeDtypeStruct((B, S, D_MODEL), F32),
                   jax.ShapeDtypeStruct((B * S * SUBLANES, LANES), F32)],
        scratch_shapes=[pltpu.VMEM((CONV_HALO + tm, CONV_CH), F32), pltpu.VMEM((tm, CONV_CH), F32),
                        pltpu.VMEM((SUBLANES - 1, tm + CONV_HALO - SUBLANES, CONV_CH), F32),
                        pltpu.VMEM((ncol, tm, LANES), F32), pltpu.VMEM((1, tm, LANES), F32),
                        pltpu.VMEM((ncol, tm, LANES), F32), pltpu.VMEM((1, tm, LANES), F32)],
        compiler_params=_params(("arbitrary", "arbitrary")),
        name="mixer_out",
    )(*o_list, *st_list, glu3, glu3, gates3, gates3, x3,
      w_dw, b_dw, cg, cb, woa_b, woc_b, wout_b, g1, b1)


def _router_body(h_ref, wt_ref, b_ref, gates_ref, ids_ref, rank_ref, cnt_ref, carry_ref, tri_ref, *, tm):
    step = pl.program_id(0)

    @pl.when(step == 0)
    def _():
        carry_ref[...] = jnp.zeros_like(carry_ref)
        r_i = lax.broadcasted_iota(jnp.int32, (tm, tm), 0)
        c_i = lax.broadcasted_iota(jnp.int32, (tm, tm), 1)
        tri_ref[...] = (r_i < c_i).astype(BF16)

    nt_dims = (((1,), (1,)), ((), ()))
    h_hi, h_lo = _split_bf16(h_ref[...])
    w_hi, w_lo = _split_bf16(wt_ref[...])
    logits = (lax.dot_general(w_hi, h_hi, nt_dims, preferred_element_type=F32)
              + (lax.dot_general(w_hi, h_lo, nt_dims, preferred_element_type=F32)
                 + lax.dot_general(w_lo, h_hi, nt_dims, preferred_element_type=F32))
              + b_ref[...])
    expert = lax.broadcasted_iota(jnp.int32, (N_EXPERTS, tm), 0).astype(F32)
    row = lax.broadcasted_iota(jnp.int32, (SUBLANES, tm), 0)

    work = logits
    vals, hots = [], []
    ids = jnp.zeros((SUBLANES, tm), F32)
    for k in range(TOP_K):
        v = jnp.max(work, axis=0, keepdims=True)
        idx = jnp.min(jnp.where(work == v, expert, float(N_EXPERTS)), axis=0, keepdims=True)
        hot = expert == idx
        work = jnp.where(hot, -jnp.inf, work)
        ids = jnp.where(row == k, idx, ids)
        vals.append(v)
        hots.append(hot)

    es = [jnp.exp(v - vals[0]) for v in vals]
    esum = es[0] + es[1] + es[2] + es[3]
    gates = jnp.zeros((SUBLANES, tm), F32)
    for k in range(TOP_K):
        gates = jnp.where(row == k, es[k] / esum, gates)

    sel = (hots[0] | hots[1] | hots[2] | hots[3])
    before = jnp.dot(sel.astype(BF16), tri_ref[...], preferred_element_type=F32) + carry_ref[:, 0:1]
    rank = jnp.zeros((SUBLANES, tm), F32)
    for k in range(TOP_K):
        rk = jnp.sum(jnp.where(hots[k], before, 0.0), axis=0, keepdims=True)
        rank = jnp.where(row == k, rk, rank)
    carry_ref[...] = carry_ref[...] + jnp.sum(sel.astype(F32), axis=1, keepdims=True)

    gates_ref[...] = gates
    ids_ref[...] = ids.astype(jnp.int32)
    rank_ref[...] = rank.astype(jnp.int32)
    cnt_ref[...] = carry_ref[...].astype(jnp.int32)


def router(h2, w_router, b_router, tm=512):
    T = h2.shape[0]
    tm = min(tm, T)
    tile = pl.BlockSpec((SUBLANES, tm), lambda i: (0, i))
    return pl.pallas_call(
        functools.partial(_router_body, tm=tm),
        grid=(T // tm,),
        in_specs=[pl.BlockSpec((tm, D_MODEL), lambda i: (i, 0)),
                  pl.BlockSpec((N_EXPERTS, D_MODEL), lambda i: (0, 0)),
                  pl.BlockSpec((N_EXPERTS, 1), lambda i: (0, 0))],
        out_specs=[tile, tile, tile, pl.BlockSpec((N_EXPERTS, LANES), lambda i: (0, 0))],
        out_shape=[jax.ShapeDtypeStruct((SUBLANES, T), F32), jax.ShapeDtypeStruct((SUBLANES, T), jnp.int32),
                   jax.ShapeDtypeStruct((SUBLANES, T), jnp.int32), jax.ShapeDtypeStruct((N_EXPERTS, LANES), jnp.int32)],
        scratch_shapes=[pltpu.VMEM((N_EXPERTS, LANES), F32), pltpu.VMEM((tm, tm), BF16)],
        compiler_params=_params(("arbitrary",)),
        name="router",
    )(h2, w_router.astype(F32).T, b_router.astype(F32).reshape(N_EXPERTS, 1))


def _token_copy(src, dst, s_tok, d_tok, sem, n=1):
    rows = n * SUBLANES
    return pltpu.make_async_copy(src.at[pl.ds(pl.multiple_of(s_tok * SUBLANES, SUBLANES), rows), :],
                                 dst.at[pl.ds(pl.multiple_of(d_tok * SUBLANES, SUBLANES), rows), :], sem)


def _dispatch_body(zstart_ref, zcount_ref, n_used_ref, pos_ref, h_ref, xs_ref, zeros_ref, ring, sems, zsem, *, tm, nb):
    half = EXPERT_BLOCK // 2
    bits = [1 << s for s in reversed(range(half.bit_length()))]

    def zero_fill(e, wait):
        start, count = zstart_ref[e], zcount_ref[e]
        for bit in bits:
            @pl.when((count & bit) != 0)
            def _():
                cp = _token_copy(zeros_ref, xs_ref, 0, 0 if wait else start + (count & ~(2 * bit - 1)), zsem, bit)
                cp.wait() if wait else cp.start()

    def zero_tail(blk, wait):
        for part in range(2):
            cp = _token_copy(zeros_ref, xs_ref, 0, 0 if wait else blk * EXPERT_BLOCK + part * half, zsem, half)
            cp.wait() if wait else cp.start()

    @pl.when(pl.program_id(0) == 0)
    def _():
        zeros_ref[...] = jnp.zeros_like(zeros_ref)
        lax.fori_loop(0, N_EXPERTS, lambda e, c: (zero_fill(e, False), c)[1], 0)
        lax.fori_loop(n_used_ref[0], nb, lambda blk, c: (zero_tail(blk, False), c)[1], 0)

    i = pl.program_id(0)
    last = pl.num_programs(0) - 1

    def retire(slot):
        for k in range(TOP_K):
            _token_copy(ring.at[slot], xs_ref, 0, 0, sems.at[slot], tm).wait()

    def step(slot):
        pl.when(i >= 2)(functools.partial(retire, slot))
        ring[slot] = h_ref[...]

        def issue(t, c):
            for k in range(TOP_K):
                _token_copy(ring.at[slot], xs_ref, t, pos_ref[k * tm + t], sems.at[slot]).start(priority=k % 2)
            return c

        lax.fori_loop(0, tm, issue, 0)

        @pl.when(i == last)
        def _():
            retire(slot)
            pl.when(i >= 1)(functools.partial(retire, 1 - slot))

    for slot in range(2):
        pl.when(i % 2 == slot)(functools.partial(step, slot))

    @pl.when(pl.program_id(0) == 0)
    def _():
        lax.fori_loop(0, N_EXPERTS, lambda e, c: (zero_fill(e, True), c)[1], 0)
        lax.fori_loop(n_used_ref[0], nb, lambda blk, c: (zero_tail(blk, True), c)[1], 0)


def _tile_major(pos_t, tm):
    T = pos_t.shape[1]
    return pos_t[:TOP_K].reshape(TOP_K, T // tm, tm).transpose(1, 0, 2).reshape(-1)


def dispatch(ht, pos_t, zstart, zcount, n_used, n_rows, tm=1024):
    T = ht.shape[0] // SUBLANES
    tm = min(tm, T)
    grid_spec = pltpu.PrefetchScalarGridSpec(
        num_scalar_prefetch=3,
        grid=(T // tm,),
        in_specs=[pl.BlockSpec((TOP_K * tm,), lambda i, zs, zc, nu: (i,), memory_space=pltpu.SMEM),
                  pl.BlockSpec((tm * SUBLANES, LANES), lambda i, zs, zc, nu: (i, 0))],
        out_specs=pl.BlockSpec(memory_space=pl.ANY),
        scratch_shapes=[pltpu.VMEM((EXPERT_BLOCK // 2 * SUBLANES, LANES), F32),
                        pltpu.VMEM((2, tm * SUBLANES, LANES), F32), pltpu.SemaphoreType.DMA((2,)),
                        pltpu.SemaphoreType.DMA],
    )
    return pl.pallas_call(
        functools.partial(_dispatch_body, tm=tm, nb=n_rows // EXPERT_BLOCK),
        grid_spec=grid_spec,
        out_shape=jax.ShapeDtypeStruct((n_rows * SUBLANES, LANES), F32),
        compiler_params=_params(("arbitrary",)),
        name="dispatch",
    )(zstart, zcount, n_used, _tile_major(pos_t, tm), ht)


def _combine_body(pos_ref, pos_next_ref, gates_ref, h_ref, g2_ref, b2_ref, ys_ref, o_ref, buf, sems, *, tm):
    i = pl.program_id(0)

    def issue(p_ref, s):
        def body(t, c):
            for k in range(TOP_K):
                _token_copy(ys_ref, buf.at[s, k], p_ref[k * tm + t], t, sems.at[s]).start(priority=k % 2)
            return c

        lax.fori_loop(0, tm, body, 0)

    @pl.when(i == 0)
    def _():
        issue(pos_ref, 0)

    def step(slot):
        @pl.when(i + 1 < pl.num_programs(0))
        def _():
            issue(pos_next_ref, 1 - slot)

        for k in range(TOP_K):
            _token_copy(ys_ref, buf.at[slot, k], 0, 0, sems.at[slot], tm).wait()

        gates = gates_ref[...].T
        ffn = gates[:, 0:1] * _load_token_tiles(buf.at[slot, 0], tm)
        for k in range(1, TOP_K):
            ffn = ffn + gates[:, k:k + 1] * _load_token_tiles(buf.at[slot, k], tm)
        o_ref[...] = _layer_norm(DEEPNORM_ALPHA * h_ref[...] + ffn, g2_ref[...], b2_ref[...])

    for slot in range(2):
        pl.when(i % 2 == slot)(functools.partial(step, slot))


def combine(ys, pos_t, gates_t, h2, g2, b2, tm=256):
    T = h2.shape[0]
    tm = min(tm, T)
    last = T // tm - 1
    pos_flat = _tile_major(pos_t, tm)
    return pl.pallas_call(
        functools.partial(_combine_body, tm=tm),
        grid=(T // tm,),
        in_specs=[pl.BlockSpec((TOP_K * tm,), lambda i: (i,), memory_space=pltpu.SMEM),
                  pl.BlockSpec((TOP_K * tm,), lambda i: (jnp.minimum(i + 1, last),), memory_space=pltpu.SMEM),
                  pl.BlockSpec((SUBLANES, tm), lambda i: (0, i)),
                  pl.BlockSpec((tm, D_MODEL), lambda i: (i, 0)),
                  pl.BlockSpec((1, D_MODEL), lambda i: (0, 0)),
                  pl.BlockSpec((1, D_MODEL), lambda i: (0, 0)),
                  pl.BlockSpec(memory_space=pl.ANY)],
        out_specs=pl.BlockSpec((tm, D_MODEL), lambda i: (i, 0)),
        out_shape=jax.ShapeDtypeStruct((T, D_MODEL), F32),
        scratch_shapes=[pltpu.VMEM((2, TOP_K, tm * SUBLANES, LANES), F32), pltpu.SemaphoreType.DMA((2,))],
        compiler_params=_params(("arbitrary",)),
        name="combine",
    )(pos_flat, pos_flat, gates_t, h2, g2, b2, ys)


def _experts_body(blk_e, blk_src, n_used, next_e, wslot, xs_ref, wgu_hbm, bgu_ref, wdn_hbm, bdn_ref, ys_ref,
                  wgu_f, wdn_f, wgu_b, wdn_b, sems):
    del blk_src
    i = pl.program_id(0)
    used = i < n_used[0]
    e = blk_e[i]

    def weight_copies(expert, slot):
        return (pltpu.make_async_copy(wgu_hbm.at[expert], wgu_f.at[slot], sems.at[slot]),
                pltpu.make_async_copy(wdn_hbm.at[expert], wdn_f.at[slot], sems.at[slot]))

    @pl.when(i == 0)
    def _():
        for cp in weight_copies(e, 0):
            cp.start()

    @pl.when((i == 0) | (e != blk_e[jnp.maximum(i - 1, 0)]))
    def _():
        nxt = next_e[e]
        for slot in range(2):
            @pl.when(wslot[e] == slot)
            def _():
                for cp in weight_copies(e, slot):
                    cp.wait()
                wgu_b[...] = wgu_f[slot].astype(BF16)
                wdn_b[...] = wdn_f[slot].astype(BF16)

                @pl.when(nxt < N_EXPERTS)
                def _():
                    for cp in weight_copies(nxt, 1 - slot):
                        cp.start()

    @pl.when(used)
    def _():
        x = _load_token_tiles(xs_ref, EXPERT_BLOCK).astype(BF16)
        hgu = jnp.dot(x, wgu_b[...], preferred_element_type=F32) + bgu_ref[...]
        gate = jnp.minimum(hgu[:, :D_FF], SWIGLU_LIMIT)
        up = jnp.clip(hgu[:, D_FF:], -SWIGLU_LIMIT, SWIGLU_LIMIT)
        act = (up + 1.0) * gate * _sigmoid(SWIGLU_ALPHA * gate)
        y = jnp.dot(act.astype(BF16), wdn_b[...], preferred_element_type=F32) + bdn_ref[...]
        _store_token_tiles(ys_ref, y, EXPERT_BLOCK)

    @pl.when(jnp.logical_not(used))
    def _():
        ys_ref[...] = jnp.zeros_like(ys_ref)


def experts(xs, blk_e, blk_src, n_used, next_e, wslot, wgu, bgu, wdn, bdn):
    n_rows = xs.shape[0] // SUBLANES
    nb = n_rows // EXPERT_BLOCK
    rows = EXPERT_BLOCK * SUBLANES
    grid_spec = pltpu.PrefetchScalarGridSpec(
        num_scalar_prefetch=5,
        grid=(nb,),
        in_specs=[pl.BlockSpec((rows, LANES), lambda i, e, s, n, ne, ws: (s[i], 0)),
                  pl.BlockSpec(memory_space=pl.ANY),
                  pl.BlockSpec((None, 1, 2 * D_FF), lambda i, e, s, n, ne, ws: (e[i], 0, 0)),
                  pl.BlockSpec(memory_space=pl.ANY),
                  pl.BlockSpec((None, 1, D_MODEL), lambda i, e, s, n, ne, ws: (e[i], 0, 0))],
        out_specs=pl.BlockSpec((rows, LANES), lambda i, e, s, n, ne, ws: (i, 0)),
        scratch_shapes=[pltpu.VMEM((2, D_MODEL, 2 * D_FF), F32), pltpu.VMEM((2, D_FF, D_MODEL), F32),
                        pltpu.VMEM((D_MODEL, 2 * D_FF), BF16), pltpu.VMEM((D_FF, D_MODEL), BF16),
                        pltpu.SemaphoreType.DMA((2,))],
    )
    return pl.pallas_call(
        _experts_body,
        grid_spec=grid_spec,
        out_shape=jax.ShapeDtypeStruct((n_rows * SUBLANES, LANES), F32),
        compiler_params=_params(("arbitrary",)),
        name="experts",
    )(blk_e, blk_src, n_used, next_e, wslot, xs, wgu, bgu, wdn, bdn)


def routing_layout(ids, rank, counts, n_blocks):
    padded = (counts + EXPERT_BLOCK - 1) // EXPERT_BLOCK * EXPERT_BLOCK
    pad_ends = jnp.cumsum(padded)
    pad_starts = pad_ends - padded
    start_of = jnp.zeros_like(ids)
    for e in range(N_EXPERTS):
        start_of = jnp.where(ids == e, pad_starts[e], start_of)
    pos_t = (start_of + rank).astype(jnp.int32)
    n_used = jnp.maximum(pad_ends[-1] // EXPERT_BLOCK, 1).astype(jnp.int32)
    blk = jnp.minimum(jnp.arange(n_blocks, dtype=jnp.int32), n_used - 1)
    blk_e = jnp.sum(pad_ends[None, :] <= (blk * EXPERT_BLOCK)[:, None], axis=1)
    blk_e = jnp.minimum(blk_e, N_EXPERTS - 1).astype(jnp.int32)
    zstart = (pad_starts + counts).astype(jnp.int32)
    zcount = (padded - counts).astype(jnp.int32)
    nonempty = counts > 0
    expert_ids = jnp.arange(N_EXPERTS, dtype=jnp.int32)
    later = jnp.where(nonempty[None, :] & (expert_ids[None, :] > expert_ids[:, None]), expert_ids[None, :], N_EXPERTS)
    next_e = jnp.min(later, axis=1).astype(jnp.int32)
    wslot = ((jnp.cumsum(nonempty) - 1) % 2).astype(jnp.int32)
    return pos_t, blk_e, blk, n_used.reshape(1), zstart, zcount, next_e, wslot


def kernel(x, w_in, rel_bias, w_dw, b_dw, conv_ln_g, conv_ln_b, w_o_attn, w_o_conv, w_out, ln1_g, ln1_b,
           w_router, b_router, w_gate_up, b_gate_up, w_down, b_down, ln2_g, ln2_b):
    B, S, D = x.shape
    T = B * S
    h = x
    bm = bias_tables(rel_bias)
    q_off, k_off, v_off, rest = 0, QKV_WIDTH, 2 * QKV_WIDTH, 3 * QKV_WIDTH
    for l in range(DEPTH):
        wb = w_in[l].astype(BF16)

        def group_cols(gi):
            return [wb[:, off + gi * GROUP_WIDTH:off + (gi + 1) * GROUP_WIDTH] for off in (q_off, k_off, v_off)]

        w_main = jnp.concatenate(group_cols(0) + [wb[:, rest:]], axis=1)
        qkv0, glu, gates, *x_dilated = in_proj_dilate(h, w_main, "in_proj_main")
        x_dil = [None] + x_dilated
        o_list, st_list = [], []
        for gi in range(N_GROUPS):
            dil = ATT_GROUPS[gi][1]
            if dil == 1:
                qkv = qkv0.reshape(B, 1, S, qkv0.shape[-1])
            else:
                w_g = jnp.concatenate(group_cols(gi), axis=1)
                qkv = in_proj(x_dil[gi].reshape(T, D), w_g, w_g.shape[1], f"in_proj_g{gi}")
                qkv = qkv.reshape(B, dil, S // dil, w_g.shape[1])
            o_g, st_g = attention_group(qkv, bm[gi], gi)
            o_list.append(o_g)
            st_list.append(st_g)
        h1, h1_tiles = mixer_out(o_list, st_list, glu, gates, h,
                       w_dw[l].reshape(CONV_WIDTH, CONV_CH), b_dw[l].reshape(1, CONV_CH),
                       conv_ln_g[l].reshape(1, CONV_CH), conv_ln_b[l].reshape(1, CONV_CH),
                       w_o_attn[l].astype(BF16), w_o_conv[l].astype(BF16), w_out[l].astype(BF16),
                       ln1_g[l].reshape(1, D), ln1_b[l].reshape(1, D))
        h2 = h1.reshape(T, D)
        gates_t, ids_t, rank_t, counts = router(h2, w_router[l], b_router[l])
        n_rows = T * TOP_K + N_EXPERTS * EXPERT_BLOCK
        pos_t, blk_e, blk_src, n_used, zstart, zcount, next_e, wslot = routing_layout(
            ids_t, rank_t, counts[:, 0], n_rows // EXPERT_BLOCK)
        xs = dispatch(h1_tiles, pos_t, zstart, zcount, n_used, n_rows)
        ys = experts(xs, blk_e, blk_src, n_used, next_e, wslot,
                     w_gate_up[l], b_gate_up[l].reshape(N_EXPERTS, 1, 2 * D_FF),
                     w_down[l], b_down[l].reshape(N_EXPERTS, 1, D))
        out = combine(ys, pos_t, gates_t, h2, ln2_g[l].reshape(1, D), ln2_b[l].reshape(1, D))
        h = out.reshape(B, S, D)
    return h
```

```python
import functools
import math

import jax
import jax.numpy as jnp
from jax import lax
from jax.experimental import pallas as pl
from jax.experimental.pallas import tpu as pltpu

D_MODEL = 1024
ATT_GROUPS = ((128, 1), (512, 4), (2048, 16))
N_GROUPS = len(ATT_GROUPS)
HEADS_PER_GROUP = 8
HEAD_DIM = 64
GROUP_WIDTH = HEADS_PER_GROUP * HEAD_DIM
QKV_WIDTH = N_GROUPS * GROUP_WIDTH
ATT_BLOCK = 128
N_BUCKETS = 32
MAX_DISTANCE = 2048
CONV_CH = 768
CONV_WIDTH = 31
CONV_HALO = 32
N_EXPERTS = 32
TOP_K = 4
D_FF = 1024
SWIGLU_LIMIT = 7.0
SWIGLU_ALPHA = 1.702
IN_WIDTH = 3 * QKV_WIDTH + 2 * CONV_CH + 2 * D_MODEL
LN_EPS = 1e-5
NEG_INF = -1e30
DEPTH = 1
DEEPNORM_ALPHA = (2 * DEPTH) ** 0.25

LANES = 128
SUBLANES = 8
EXPERT_BLOCK = 512
VMEM_LIMIT = 56 * 1024 * 1024

F32 = jnp.float32
BF16 = jnp.bfloat16


def _params(sem, vmem=VMEM_LIMIT):
    return pltpu.CompilerParams(dimension_semantics=sem, vmem_limit_bytes=vmem)


def _sigmoid(x):
    return 0.5 * jnp.tanh(0.5 * x) + 0.5


def _layer_norm(x, g, b):
    mu = jnp.mean(x, axis=-1, keepdims=True)
    xc = x - mu
    var = jnp.mean(xc * xc, axis=-1, keepdims=True)
    return xc * lax.rsqrt(var + LN_EPS) * g + b


def _in_proj_body(x_ref, w_ref, o_ref):
    o_ref[...] = jnp.dot(x_ref[...], w_ref[...], preferred_element_type=F32).astype(o_ref.dtype)


def in_proj(xb, w_b, tn, name, tm=1024):
    T = xb.shape[0]
    N = w_b.shape[1]
    tm = min(tm, T)
    return pl.pallas_call(
        _in_proj_body,
        grid=(N // tn, T // tm),
        in_specs=[pl.BlockSpec((tm, D_MODEL), lambda n, m: (m, 0)),
                  pl.BlockSpec((D_MODEL, tn), lambda n, m: (0, n))],
        out_specs=pl.BlockSpec((tm, tn), lambda n, m: (m, n)),
        out_shape=jax.ShapeDtypeStruct((T, N), BF16),
        compiler_params=_params(("arbitrary", "arbitrary")),
        name=name,
    )(xb, w_b)


QKV_COLS = 3 * GROUP_WIDTH


def _in_proj_dilate_body(*refs, tm, nslab):
    x_slabs, w_ref = refs[:nslab], refs[nslab]
    qkv_ref, glu_ref, gate_ref = refs[nslab + 1:nslab + 4]
    d_refs = refs[nslab + 4:]
    cols = []
    for c, x_ref in enumerate(x_slabs):
        cs = slice(c * LANES, (c + 1) * LANES)
        cols.append(x_ref[...].astype(BF16))
        for (_, dil), d_ref in zip(ATT_GROUPS[1:], d_refs):
            for r in range(dil):
                d_ref[r, :, cs] = x_ref[pl.ds(r, tm // dil, stride=dil), :].astype(BF16)
    xb = jnp.concatenate(cols, axis=1)
    proj = jnp.dot(xb, w_ref[...], preferred_element_type=F32)
    qkv_ref[...] = proj[:, :QKV_COLS].astype(qkv_ref.dtype)
    u_val = proj[:, QKV_COLS:QKV_COLS + CONV_CH]
    u_gate = proj[:, QKV_COLS + CONV_CH:QKV_COLS + 2 * CONV_CH]
    glu_ref[...] = (u_val * _sigmoid(u_gate)).astype(glu_ref.dtype)
    gate_ref[...] = _sigmoid(proj[:, QKV_COLS + 2 * CONV_CH:]).astype(gate_ref.dtype)


def in_proj_dilate(x3, w_b, name, tm=512):
    B, S, D = x3.shape
    N = w_b.shape[1]
    tm = min(tm, S)
    nslab = D // LANES
    dils = [dil for _, dil in ATT_GROUPS[1:]]
    per_seq = S // tm
    widths = (QKV_COLS, CONV_CH, 2 * D_MODEL)
    assert N == QKV_COLS + 2 * CONV_CH + 2 * D_MODEL
    return pl.pallas_call(
        functools.partial(_in_proj_dilate_body, tm=tm, nslab=nslab),
        grid=(B, per_seq),
        in_specs=[pl.BlockSpec((None, tm, LANES), functools.partial(lambda c, b, i: (b, i, c), c))
                  for c in range(nslab)]
        + [pl.BlockSpec((D, N), lambda b, i: (0, 0))],
        out_specs=[pl.BlockSpec((None, tm, w), lambda b, i: (b, i, 0)) for w in widths]
        + [pl.BlockSpec((None, dil, tm // dil, D), lambda b, i: (b, 0, i, 0)) for dil in dils],
        out_shape=[jax.ShapeDtypeStruct((B, S, w), BF16) for w in widths]
        + [jax.ShapeDtypeStruct((B, dil, S // dil, D), BF16) for dil in dils],
        compiler_params=_params(("arbitrary", "arbitrary")),
        name=name,
    )(*([x3] * nslab), w_b)


def _t5_bucket(dist):
    max_exact = N_BUCKETS // 2
    log_ratio = jnp.log(jnp.maximum(dist, max_exact).astype(F32) / max_exact) / math.log(MAX_DISTANCE / max_exact)
    large = jnp.minimum(max_exact + (log_ratio * (N_BUCKETS - max_exact)).astype(jnp.int32), N_BUCKETS - 1)
    return jnp.where(dist < max_exact, dist, large)


def _bias_body(tbl_ref, bucket_ref, band_ref, o_ref):
    g = pl.program_id(0)
    h = pl.program_id(1)
    col = g * HEADS_PER_GROUP + h
    bucket = bucket_ref[...]
    acc = jnp.zeros(bucket.shape, F32)
    for k in range(N_BUCKETS):
        acc = jnp.where(bucket == k, tbl_ref[k, col], acc)
    band = band_ref[...] != 0
    kj = lax.broadcasted_iota(jnp.int32, bucket.shape, 1)
    o_ref[0] = jnp.where(band, acc, NEG_INF)
    o_ref[1] = jnp.where(band & (kj >= ATT_BLOCK), acc, NEG_INF)


def bias_tables(rel_bias):
    qi = jnp.arange(ATT_BLOCK)[:, None]
    kj = jnp.arange(2 * ATT_BLOCK)[None, :]
    dist = qi - kj + ATT_BLOCK
    buckets, bands = [], []
    for window, dil in ATT_GROUPS:
        bands.append(((dist >= 0) & (dist <= window // dil)).astype(jnp.int32))
        buckets.append(_t5_bucket(jnp.maximum(dist, 0) * dil).astype(jnp.int32))
    buckets = jnp.stack(buckets)
    bands = jnp.stack(bands)
    blk = (None, ATT_BLOCK, 2 * ATT_BLOCK)
    return pl.pallas_call(
        _bias_body,
        grid=(N_GROUPS, HEADS_PER_GROUP),
        in_specs=[pl.BlockSpec(memory_space=pltpu.SMEM),
                  pl.BlockSpec(blk, lambda g, h: (g, 0, 0)),
                  pl.BlockSpec(blk, lambda g, h: (g, 0, 0))],
        out_specs=pl.BlockSpec((None, None, 2, ATT_BLOCK, 2 * ATT_BLOCK), lambda g, h: (g, h, 0, 0, 0)),
        out_shape=jax.ShapeDtypeStruct((N_GROUPS, HEADS_PER_GROUP, 2, ATT_BLOCK, 2 * ATT_BLOCK), F32),
        compiler_params=_params(("arbitrary", "arbitrary")),
        name="bias_tables",
    )(rel_bias.astype(F32), buckets, bands)


def _attn_body(q_ref, kc_ref, kp_ref, vc_ref, vp_ref, bm_ref, o_ref, st_ref, k_all, v_all, s_buf, *, tq):
    i = pl.program_id(2)
    nsub = tq // ATT_BLOCK
    k_all[0:ATT_BLOCK] = kp_ref[...]
    k_all[ATT_BLOCK:] = kc_ref[...]
    v_all[0:ATT_BLOCK] = vp_ref[...]
    v_all[ATT_BLOCK:] = vc_ref[...]
    lo = lax.broadcasted_iota(jnp.int32, (ATT_BLOCK, LANES), 1) < HEAD_DIM
    nt_dims = (((1,), (1,)), ((), ()))
    first = jnp.where(i == 0, 1, 0)

    def scores(s, slot):
        r0 = s * ATT_BLOCK
        for j in range(HEADS_PER_GROUP // 2):
            cs = slice(j * LANES, (j + 1) * LANES)
            qp = q_ref[r0:r0 + ATT_BLOCK, cs] * jnp.asarray(HEAD_DIM ** -0.5, BF16)
            kp = k_all[r0:r0 + 2 * ATT_BLOCK, cs]
            for hh in range(2):
                h = 2 * j + hh
                qh = jnp.where(lo if hh == 0 else ~lo, qp, jnp.zeros_like(qp))
                bias = bm_ref[h, first] if s == 0 else bm_ref[h, 0]
                s_buf[slot, h] = lax.dot_general(qh, kp, nt_dims, preferred_element_type=F32) + bias

    def softmax_pv(s, slot):
        rows = slice(s * ATT_BLOCK, (s + 1) * ATT_BLOCK)
        st_ref[rows, :] = jnp.zeros((ATT_BLOCK, LANES), F32)
        for j in range(HEADS_PER_GROUP // 2):
            vp = v_all[s * ATT_BLOCK:(s + 2) * ATT_BLOCK, j * LANES:(j + 1) * LANES]
            for hh in range(2):
                h = 2 * j + hh
                sc = s_buf[slot, h]
                m = jnp.max(sc, axis=-1, keepdims=True)
                p = jnp.exp(sc - m)
                den = jnp.sum(p, axis=-1, keepdims=True)
                pv = jnp.dot(p.astype(BF16), vp, preferred_element_type=F32)
                c0 = h * HEAD_DIM
                o_ref[rows, c0:c0 + HEAD_DIM] = pv[:, hh * HEAD_DIM:(hh + 1) * HEAD_DIM]
                st_ref[rows, h:h + 1] = m
                st_ref[rows, HEADS_PER_GROUP + h:HEADS_PER_GROUP + h + 1] = den

    scores(0, 0)
    for s in range(nsub):
        softmax_pv(s, s % 2)
        if s + 1 < nsub:
            scores(s + 1, (s + 1) % 2)


def attention_group(qkv, bm_g, gi, tq=2048):
    B, dil, L, _ = qkv.shape
    tq = min(tq, L)
    sub = tq // ATT_BLOCK

    def cur(col):
        return pl.BlockSpec((None, None, tq, GROUP_WIDTH), lambda b, r, i: (b, r, i, col))

    def prev(col):
        return pl.BlockSpec((None, None, ATT_BLOCK, GROUP_WIDTH),
                            lambda b, r, i: (b, r, jnp.maximum(i * sub - 1, 0), col))

    return pl.pallas_call(
        functools.partial(_attn_body, tq=tq),
        grid=(B, dil, L // tq),
        in_specs=[cur(0), cur(1), prev(1), cur(2), prev(2),
                  pl.BlockSpec((HEADS_PER_GROUP, 2, ATT_BLOCK, 2 * ATT_BLOCK), lambda b, r, i: (0, 0, 0, 0))],
        out_specs=[pl.BlockSpec((None, None, tq, GROUP_WIDTH), lambda b, r, i: (b, r, i, 0)),
                   pl.BlockSpec((None, None, tq, LANES), lambda b, r, i: (b, r, i, 0))],
        out_shape=[jax.ShapeDtypeStruct((B, dil, L, GROUP_WIDTH), F32),
                   jax.ShapeDtypeStruct((B, dil, L, LANES), F32)],
        scratch_shapes=[pltpu.VMEM((ATT_BLOCK + tq, GROUP_WIDTH), BF16),
                        pltpu.VMEM((ATT_BLOCK + tq, GROUP_WIDTH), BF16),
                        pltpu.VMEM((2, HEADS_PER_GROUP, ATT_BLOCK, 2 * ATT_BLOCK), F32)],
        compiler_params=_params(("arbitrary", "arbitrary", "arbitrary")),
        name=f"attention_g{gi}",
    )(qkv, qkv, qkv, qkv, qkv, bm_g)


def _split_bf16(x):
    hi = x.astype(BF16)
    lo = (x - hi.astype(F32)).astype(BF16)
    return hi, lo


def _load_token_tiles(ref, n):
    return jnp.concatenate([ref[pl.ds(c, n, stride=SUBLANES), :] for c in range(D_MODEL // LANES)], axis=1)


def _store_token_tiles(ref, x, n):
    for c in range(D_MODEL // LANES):
        ref[pl.ds(c, n, stride=SUBLANES), :] = x[:, c * LANES:(c + 1) * LANES]


def _to_token_order(blk_ref, tok_ref, dil, tm):
    n = tm // dil
    for r in range(dil):
        for c in range(tok_ref.shape[0]):
            tok_ref[c, pl.ds(r, n, stride=dil), :] = blk_ref[r, :, c * LANES:(c + 1) * LANES]


def _mixer_out_body(o0, o1, o2, s0, s1, s2, glu_in, glu_halo, ga_ref, gc_ref, x_ref,
                    wdw_ref, bdw_ref, cg_ref, cb_ref, woa_ref, woc_ref, wout_ref, g1_ref, b1_ref,
                    h_ref, ht_ref, glu_ref, dw_ref, shift_ref, tok_o1, tok_s1, tok_o2, tok_s2, *, tm, chunk):
    i = pl.program_id(1)

    ncol = GROUP_WIDTH // LANES
    _to_token_order(o1, tok_o1, ATT_GROUPS[1][1], tm)
    _to_token_order(s1, tok_s1, ATT_GROUPS[1][1], tm)
    _to_token_order(o2, tok_o2, ATT_GROUPS[2][1], tm)
    _to_token_order(s2, tok_s2, ATT_GROUPS[2][1], tm)
    outs = [o0[0],
            jnp.concatenate([tok_o1[c] for c in range(ncol)], axis=1),
            jnp.concatenate([tok_o2[c] for c in range(ncol)], axis=1)]
    sts = [s0[0], tok_s1[0], tok_s2[0]]

    mx = jnp.maximum(jnp.maximum(sts[0], sts[1]), sts[2])
    wts = [jnp.exp(st - mx) for st in sts]
    wsum = sum(pltpu.roll(st, LANES - HEADS_PER_GROUP, axis=1) * e for st, e in zip(sts, wts))
    row = lax.broadcasted_iota(jnp.int32, (LANES, GROUP_WIDTH), 0)
    colh = lax.broadcasted_iota(jnp.int32, (LANES, GROUP_WIDTH), 1) // HEAD_DIM
    expand = (row == colh).astype(BF16)
    attn = jnp.zeros((tm, GROUP_WIDTH), F32)
    head_lane = lax.broadcasted_iota(jnp.int32, (tm, LANES), 1) < HEADS_PER_GROUP
    for wt, o in zip(wts, outs):
        c_hi, c_lo = _split_bf16(jnp.where(head_lane, wt / wsum, 0.0))
        c = (jnp.dot(c_hi, expand, preferred_element_type=F32)
             + jnp.dot(c_lo, expand, preferred_element_type=F32))
        attn = attn + c * o
    a_out = jnp.dot(attn.astype(BF16), woa_ref[...], preferred_element_type=F32)

    glu_ref[0:CONV_HALO] = jnp.where(i == 0, 0.0, glu_halo[...].astype(F32))
    glu_ref[CONV_HALO:] = glu_in[...].astype(F32)
    first_tap = CONV_HALO - (CONV_WIDTH - 1)
    for b in range(1, SUBLANES):
        shift_ref[b - 1] = glu_ref[b:b + shift_ref.shape[1], :]

    for c0 in range(0, CONV_CH, LANES):
        cs = slice(c0, c0 + LANES)
        bias = jnp.broadcast_to(bdw_ref[:, cs], (chunk, LANES))
        for r0 in range(0, tm, chunk):
            acc = bias
            for j in range(CONV_WIDTH):
                a, b = divmod(first_tap + j, SUBLANES)
                lo_row = r0 + a * SUBLANES
                rows = glu_ref[lo_row:lo_row + chunk, cs] if b == 0 else shift_ref[b - 1, lo_row:lo_row + chunk, cs]
                acc = acc + wdw_ref[j:j + 1, cs] * rows
            dw_ref[r0:r0 + chunk, cs] = acc
    cn = _layer_norm(dw_ref[...], cg_ref[...], cb_ref[...])
    conv = cn * _sigmoid(cn)
    c_out = jnp.dot(conv.astype(BF16), woc_ref[...], preferred_element_type=F32)

    merged = ga_ref[...].astype(F32) * a_out + gc_ref[...].astype(F32) * c_out
    mix = jnp.dot(merged.astype(BF16), wout_ref[...], preferred_element_type=F32)
    h = _layer_norm(DEEPNORM_ALPHA * x_ref[...] + mix, g1_ref[...], b1_ref[...])
    h_ref[...] = h
    _store_token_tiles(ht_ref, h, tm)


def mixer_out(o_list, st_list, glu3, gates3, x3, w_dw, b_dw, cg, cb, woa_b, woc_b, wout_b, g1, b1, tm=512, chunk=64):
    B, S, _ = x3.shape
    tm = min(tm, S)
    halo_blocks = tm // CONV_HALO
    ncol = GROUP_WIDTH // LANES

    def tile(width, col=0):
        return pl.BlockSpec((None, tm, width), lambda b, i: (b, i, col))

    def dilated(gi, width):
        dil = ATT_GROUPS[gi][1]
        return pl.BlockSpec((None, dil, tm // dil, width), lambda b, i: (b, 0, i, 0))

    halo = pl.BlockSpec((None, CONV_HALO, CONV_CH), lambda b, i: (b, jnp.maximum(i * halo_blocks - 1, 0), 0))

    def whole(shape):
        return pl.BlockSpec(shape, lambda b, i: (0,) * len(shape))

    return pl.pallas_call(
        functools.partial(_mixer_out_body, tm=tm, chunk=chunk),
        grid=(B, S // tm),
        in_specs=[dilated(g, GROUP_WIDTH) for g in range(N_GROUPS)] + [dilated(g, LANES) for g in range(N_GROUPS)]
        + [tile(CONV_CH), halo, tile(D_MODEL, 0), tile(D_MODEL, 1), tile(D_MODEL),
           whole((CONV_WIDTH, CONV_CH)), whole((1, CONV_CH)), whole((1, CONV_CH)), whole((1, CONV_CH)),
           whole((GROUP_WIDTH, D_MODEL)), whole((CONV_CH, D_MODEL)), whole((D_MODEL, D_MODEL)),
           whole((1, D_MODEL)), whole((1, D_MODEL))],
        out_specs=[tile(D_MODEL), pl.BlockSpec((tm * SUBLANES, LANES), lambda b, i: (b * (S // tm) + i, 0))],
        out_shape=[jax.ShapeDtypeStruct((B, S, D_MODEL), F32),
                   jax.ShapeDtypeStruct((B * S * SUBLANES, LANES), F32)],
        scratch_shapes=[pltpu.VMEM((CONV_HALO + tm, CONV_CH), F32), pltpu.VMEM((tm, CONV_CH), F32),
                        pltpu.VMEM((SUBLANES - 1, tm + CONV_HALO - SUBLANES, CONV_CH), F32),
                        pltpu.VMEM((ncol, tm, LANES), F32), pltpu.VMEM((1, tm, LANES), F32),
                        pltpu.VMEM((ncol, tm, LANES), F32), pltpu.VMEM((1, tm, LANES), F32)],
        compiler_params=_params(("arbitrary", "arbitrary")),
        name="mixer_out",
    )(*o_list, *st_list, glu3, glu3, gates3, gates3, x3,
      w_dw, b_dw, cg, cb, woa_b, woc_b, wout_b, g1, b1)


def _router_body(h_ref, wt_ref, b_ref, gates_ref, ids_ref, rank_ref, cnt_ref, carry_ref, tri_ref, *, tm):
    step = pl.program_id(0)

    @pl.when(step == 0)
    def _():
        carry_ref[...] = jnp.zeros_like(carry_ref)
        r_i = lax.broadcasted_iota(jnp.int32, (tm, tm), 0)
        c_i = lax.broadcasted_iota(jnp.int32, (tm, tm), 1)
        tri_ref[...] = (r_i < c_i).astype(BF16)

    nt_dims = (((1,), (1,)), ((), ()))
    h_hi, h_lo = _split_bf16(h_ref[...])
    w_hi, w_lo = _split_bf16(wt_ref[...])
    logits = (lax.dot_general(w_hi, h_hi, nt_dims, preferred_element_type=F32)
              + (lax.dot_general(w_hi, h_lo, nt_dims, preferred_element_type=F32)
                 + lax.dot_general(w_lo, h_hi, nt_dims, preferred_element_type=F32))
              + b_ref[...])
    expert = lax.broadcasted_iota(jnp.int32, (N_EXPERTS, tm), 0).astype(F32)
    row = lax.broadcasted_iota(jnp.int32, (SUBLANES, tm), 0)

    work = logits
    vals, hots = [], []
    ids = jnp.zeros((SUBLANES, tm), F32)
    for k in range(TOP_K):
        v = jnp.max(work, axis=0, keepdims=True)
        idx = jnp.min(jnp.where(work == v, expert, float(N_EXPERTS)), axis=0, keepdims=True)
        hot = expert == idx
        work = jnp.where(hot, -jnp.inf, work)
        ids = jnp.where(row == k, idx, ids)
        vals.append(v)
        hots.append(hot)

    es = [jnp.exp(v - vals[0]) for v in vals]
    esum = es[0] + es[1] + es[2] + es[3]
    gates = jnp.zeros((SUBLANES, tm), F32)
    for k in range(TOP_K):
        gates = jnp.where(row == k, es[k] / esum, gates)

    sel = (hots[0] | hots[1] | hots[2] | hots[3])
    before = jnp.dot(sel.astype(BF16), tri_ref[...], preferred_element_type=F32) + carry_ref[:, 0:1]
    rank = jnp.zeros((SUBLANES, tm), F32)
    for k in range(TOP_K):
        rk = jnp.sum(jnp.where(hots[k], before, 0.0), axis=0, keepdims=True)
        rank = jnp.where(row == k, rk, rank)
    carry_ref[...] = carry_ref[...] + jnp.sum(sel.astype(F32), axis=1, keepdims=True)

    gates_ref[...] = gates
    ids_ref[...] = ids.astype(jnp.int32)
    rank_ref[...] = rank.astype(jnp.int32)
    cnt_ref[...] = carry_ref[...].astype(jnp.int32)


def router(h2, w_router, b_router, tm=512):
    T = h2.shape[0]
    tm = min(tm, T)
    tile = pl.BlockSpec((SUBLANES, tm), lambda i: (0, i))
    return pl.pallas_call(
        functools.partial(_router_body, tm=tm),
        grid=(T // tm,),
        in_specs=[pl.BlockSpec((tm, D_MODEL), lambda i: (i, 0)),
                  pl.BlockSpec((N_EXPERTS, D_MODEL), lambda i: (0, 0)),
                  pl.BlockSpec((N_EXPERTS, 1), lambda i: (0, 0))],
        out_specs=[tile, tile, tile, pl.BlockSpec((N_EXPERTS, LANES), lambda i: (0, 0))],
        out_shape=[jax.ShapeDtypeStruct((SUBLANES, T), F32), jax.ShapeDtypeStruct((SUBLANES, T), jnp.int32),
                   jax.ShapeDtypeStruct((SUBLANES, T), jnp.int32), jax.ShapeDtypeStruct((N_EXPERTS, LANES), jnp.int32)],
        scratch_shapes=[pltpu.VMEM((N_EXPERTS, LANES), F32), pltpu.VMEM((tm, tm), BF16)],
        compiler_params=_params(("arbitrary",)),
        name="router",
    )(h2, w_router.astype(F32).T, b_router.astype(F32).reshape(N_EXPERTS, 1))


def _token_copy(src, dst, s_tok, d_tok, sem, n=1):
    rows = n * SUBLANES
    return pltpu.make_async_copy(src.at[pl.ds(pl.multiple_of(s_tok * SUBLANES, SUBLANES), rows), :],
                                 dst.at[pl.ds(pl.multiple_of(d_tok * SUBLANES, SUBLANES), rows), :], sem)


def _dispatch_body(zstart_ref, zcount_ref, n_used_ref, pos_ref, h_ref, xs_ref, zeros_ref, ring, sems, zsem, *, tm, nb):
    half = EXPERT_BLOCK // 2
    bits = [1 << s for s in reversed(range(half.bit_length()))]

    def zero_fill(e, wait):
        start, count = zstart_ref[e], zcount_ref[e]
        for bit in bits:
            @pl.when((count & bit) != 0)
            def _():
                cp = _token_copy(zeros_ref, xs_ref, 0, 0 if wait else start + (count & ~(2 * bit - 1)), zsem, bit)
                cp.wait() if wait else cp.start()

    def zero_tail(blk, wait):
        for part in range(2):
            cp = _token_copy(zeros_ref, xs_ref, 0, 0 if wait else blk * EXPERT_BLOCK + part * half, zsem, half)
            cp.wait() if wait else cp.start()

    @pl.when(pl.program_id(0) == 0)
    def _():
        zeros_ref[...] = jnp.zeros_like(zeros_ref)
        lax.fori_loop(0, N_EXPERTS, lambda e, c: (zero_fill(e, False), c)[1], 0)
        lax.fori_loop(n_used_ref[0], nb, lambda blk, c: (zero_tail(blk, False), c)[1], 0)

    i = pl.program_id(0)
    last = pl.num_programs(0) - 1

    def retire(slot):
        for k in range(TOP_K):
            _token_copy(ring.at[slot], xs_ref, 0, 0, sems.at[slot], tm).wait()

    def step(slot):
        pl.when(i >= 2)(functools.partial(retire, slot))
        ring[slot] = h_ref[...]

        def issue(t, c):
            for k in range(TOP_K):
                _token_copy(ring.at[slot], xs_ref, t, pos_ref[k * tm + t], sems.at[slot]).start(priority=k % 2)
            return c

        lax.fori_loop(0, tm, issue, 0)

        @pl.when(i == last)
        def _():
            retire(slot)
            pl.when(i >= 1)(functools.partial(retire, 1 - slot))

    for slot in range(2):
        pl.when(i % 2 == slot)(functools.partial(step, slot))

    @pl.when(pl.program_id(0) == 0)
    def _():
        lax.fori_loop(0, N_EXPERTS, lambda e, c: (zero_fill(e, True), c)[1], 0)
        lax.fori_loop(n_used_ref[0], nb, lambda blk, c: (zero_tail(blk, True), c)[1], 0)


def _tile_major(pos_t, tm):
    T = pos_t.shape[1]
    return pos_t[:TOP_K].reshape(TOP_K, T // tm, tm).transpose(1, 0, 2).reshape(-1)


def dispatch(ht, pos_t, zstart, zcount, n_used, n_rows, tm=1024):
    T = ht.shape[0] // SUBLANES
    tm = min(tm, T)
    grid_spec = pltpu.PrefetchScalarGridSpec(
        num_scalar_prefetch=3,
        grid=(T // tm,),
        in_specs=[pl.BlockSpec((TOP_K * tm,), lambda i, zs, zc, nu: (i,), memory_space=pltpu.SMEM),
                  pl.BlockSpec((tm * SUBLANES, LANES), lambda i, zs, zc, nu: (i, 0))],
        out_specs=pl.BlockSpec(memory_space=pl.ANY),
        scratch_shapes=[pltpu.VMEM((EXPERT_BLOCK // 2 * SUBLANES, LANES), F32),
                        pltpu.VMEM((2, tm * SUBLANES, LANES), F32), pltpu.SemaphoreType.DMA((2,)),
                        pltpu.SemaphoreType.DMA],
    )
    return pl.pallas_call(
        functools.partial(_dispatch_body, tm=tm, nb=n_rows // EXPERT_BLOCK),
        grid_spec=grid_spec,
        out_shape=jax.ShapeDtypeStruct((n_rows * SUBLANES, LANES), F32),
        compiler_params=_params(("arbitrary",)),
        name="dispatch",
    )(zstart, zcount, n_used, _tile_major(pos_t, tm), ht)


def _combine_body(pos_ref, pos_next_ref, gates_ref, h_ref, g2_ref, b2_ref, ys_ref, o_ref, buf, sems, *, tm):
    i = pl.program_id(0)

    def issue(p_ref, s):
        def body(t, c):
            for k in range(TOP_K):
                _token_copy(ys_ref, buf.at[s, k], p_ref[k * tm + t], t, sems.at[s]).start(priority=k % 2)
            return c

        lax.fori_loop(0, tm, body, 0)

    @pl.when(i == 0)
    def _():
        issue(pos_ref, 0)

    def step(slot):
        @pl.when(i + 1 < pl.num_programs(0))
        def _():
            issue(pos_next_ref, 1 - slot)

        for k in range(TOP_K):
            _token_copy(ys_ref, buf.at[slot, k], 0, 0, sems.at[slot], tm).wait()

        gates = gates_ref[...].T
        ffn = gates[:, 0:1] * _load_token_tiles(buf.at[slot, 0], tm)
        for k in range(1, TOP_K):
            ffn = ffn + gates[:, k:k + 1] * _load_token_tiles(buf.at[slot, k], tm)
        o_ref[...] = _layer_norm(DEEPNORM_ALPHA * h_ref[...] + ffn, g2_ref[...], b2_ref[...])

    for slot in range(2):
        pl.when(i % 2 == slot)(functools.partial(step, slot))


def combine(ys, pos_t, gates_t, h2, g2, b2, tm=256):
    T = h2.shape[0]
    tm = min(tm, T)
    last = T // tm - 1
    pos_flat = _tile_major(pos_t, tm)
    return pl.pallas_call(
        functools.partial(_combine_body, tm=tm),
        grid=(T // tm,),
        in_specs=[pl.BlockSpec((TOP_K * tm,), lambda i: (i,), memory_space=pltpu.SMEM),
                  pl.BlockSpec((TOP_K * tm,), lambda i: (jnp.minimum(i + 1, last),), memory_space=pltpu.SMEM),
                  pl.BlockSpec((SUBLANES, tm), lambda i: (0, i)),
                  pl.BlockSpec((tm, D_MODEL), lambda i: (i, 0)),
                  pl.BlockSpec((1, D_MODEL), lambda i: (0, 0)),
                  pl.BlockSpec((1, D_MODEL), lambda i: (0, 0)),
                  pl.BlockSpec(memory_space=pl.ANY)],
        out_specs=pl.BlockSpec((tm, D_MODEL), lambda i: (i, 0)),
        out_shape=jax.ShapeDtypeStruct((T, D_MODEL), F32),
        scratch_shapes=[pltpu.VMEM((2, TOP_K, tm * SUBLANES, LANES), F32), pltpu.SemaphoreType.DMA((2,))],
        compiler_params=_params(("arbitrary",)),
        name="combine",
    )(pos_flat, pos_flat, gates_t, h2, g2, b2, ys)


def _experts_body(blk_e, blk_src, n_used, next_e, wslot, xs_ref, wgu_hbm, bgu_ref, wdn_hbm, bdn_ref, ys_ref,
                  wgu_f, wdn_f, wgu_b, wdn_b, sems):
    del blk_src
    i = pl.program_id(0)
    used = i < n_used[0]
    e = blk_e[i]

    def weight_copies(expert, slot):
        return (pltpu.make_async_copy(wgu_hbm.at[expert], wgu_f.at[slot], sems.at[slot]),
                pltpu.make_async_copy(wdn_hbm.at[expert], wdn_f.at[slot], sems.at[slot]))

    @pl.when(i == 0)
    def _():
        for cp in weight_copies(e, 0):
            cp.start()

    @pl.when((i == 0) | (e != blk_e[jnp.maximum(i - 1, 0)]))
    def _():
        nxt = next_e[e]
        for slot in range(2):
            @pl.when(wslot[e] == slot)
            def _():
                for cp in weight_copies(e, slot):
                    cp.wait()
                wgu_b[...] = wgu_f[slot].astype(BF16)
                wdn_b[...] = wdn_f[slot].astype(BF16)

                @pl.when(nxt < N_EXPERTS)
                def _():
                    for cp in weight_copies(nxt, 1 - slot):
                        cp.start()

    @pl.when(used)
    def _():
        x = _load_token_tiles(xs_ref, EXPERT_BLOCK).astype(BF16)
        hgu = jnp.dot(x, wgu_b[...], preferred_element_type=F32) + bgu_ref[...]
        gate = jnp.minimum(hgu[:, :D_FF], SWIGLU_LIMIT)
        up = jnp.clip(hgu[:, D_FF:], -SWIGLU_LIMIT, SWIGLU_LIMIT)
        act = (up + 1.0) * gate * _sigmoid(SWIGLU_ALPHA * gate)
        y = jnp.dot(act.astype(BF16), wdn_b[...], preferred_element_type=F32) + bdn_ref[...]
        _store_token_tiles(ys_ref, y, EXPERT_BLOCK)

    @pl.when(jnp.logical_not(used))
    def _():
        ys_ref[...] = jnp.zeros_like(ys_ref)


def experts(xs, blk_e, blk_src, n_used, next_e, wslot, wgu, bgu, wdn, bdn):
    n_rows = xs.shape[0] // SUBLANES
    nb = n_rows // EXPERT_BLOCK
    rows = EXPERT_BLOCK * SUBLANES
    grid_spec = pltpu.PrefetchScalarGridSpec(
        num_scalar_prefetch=5,
        grid=(nb,),
        in_specs=[pl.BlockSpec((rows, LANES), lambda i, e, s, n, ne, ws: (s[i], 0)),
                  pl.BlockSpec(memory_space=pl.ANY),
                  pl.BlockSpec((None, 1, 2 * D_FF), lambda i, e, s, n, ne, ws: (e[i], 0, 0)),
                  pl.BlockSpec(memory_space=pl.ANY),
                  pl.BlockSpec((None, 1, D_MODEL), lambda i, e, s, n, ne, ws: (e[i], 0, 0))],
        out_specs=pl.BlockSpec((rows, LANES), lambda i, e, s, n, ne, ws: (i, 0)),
        scratch_shapes=[pltpu.VMEM((2, D_MODEL, 2 * D_FF), F32), pltpu.VMEM((2, D_FF, D_MODEL), F32),
                        pltpu.VMEM((D_MODEL, 2 * D_FF), BF16), pltpu.VMEM((D_FF, D_MODEL), BF16),
                        pltpu.SemaphoreType.DMA((2,))],
    )
    return pl.pallas_call(
        _experts_body,
        grid_spec=grid_spec,
        out_shape=jax.ShapeDtypeStruct((n_rows * SUBLANES, LANES), F32),
        compiler_params=_params(("arbitrary",)),
        name="experts",
    )(blk_e, blk_src, n_used, next_e, wslot, xs, wgu, bgu, wdn, bdn)


def routing_layout(ids, rank, counts, n_blocks):
    padded = (counts + EXPERT_BLOCK - 1) // EXPERT_BLOCK * EXPERT_BLOCK
    pad_ends = jnp.cumsum(padded)
    pad_starts = pad_ends - padded
    start_of = jnp.zeros_like(ids)
    for e in range(N_EXPERTS):
        start_of = jnp.where(ids == e, pad_starts[e], start_of)
    pos_t = (start_of + rank).astype(jnp.int32)
    n_used = jnp.maximum(pad_ends[-1] // EXPERT_BLOCK, 1).astype(jnp.int32)
    blk = jnp.minimum(jnp.arange(n_blocks, dtype=jnp.int32), n_used - 1)
    blk_e = jnp.sum(pad_ends[None, :] <= (blk * EXPERT_BLOCK)[:, None], axis=1)
    blk_e = jnp.minimum(blk_e, N_EXPERTS - 1).astype(jnp.int32)
    zstart = (pad_starts + counts).astype(jnp.int32)
    zcount = (padded - counts).astype(jnp.int32)
    nonempty = counts > 0
    expert_ids = jnp.arange(N_EXPERTS, dtype=jnp.int32)
    later = jnp.where(nonempty[None, :] & (expert_ids[None, :] > expert_ids[:, None]), expert_ids[None, :], N_EXPERTS)
    next_e = jnp.min(later, axis=1).astype(jnp.int32)
    wslot = ((jnp.cumsum(nonempty) - 1) % 2).astype(jnp.int32)
    return pos_t, blk_e, blk, n_used.reshape(1), zstart, zcount, next_e, wslot


def kernel(x, w_in, rel_bias, w_dw, b_dw, conv_ln_g, conv_ln_b, w_o_attn, w_o_conv, w_out, ln1_g, ln1_b,
           w_router, b_router, w_gate_up, b_gate_up, w_down, b_down, ln2_g, ln2_b):
    B, S, D = x.shape
    T = B * S
    h = x
    bm = bias_tables(rel_bias)
    q_off, k_off, v_off, rest = 0, QKV_WIDTH, 2 * QKV_WIDTH, 3 * QKV_WIDTH
    for l in range(DEPTH):
        wb = w_in[l].astype(BF16)

        def group_cols(gi):
            return [wb[:, off + gi * GROUP_WIDTH:off + (gi + 1) * GROUP_WIDTH] for off in (q_off, k_off, v_off)]

        w_main = jnp.concatenate(group_cols(0) + [wb[:, rest:]], axis=1)
        qkv0, glu, gates, *x_dilated = in_proj_dilate(h, w_main, "in_proj_main")
        x_dil = [None] + x_dilated
        o_list, st_list = [], []
        for gi in range(N_GROUPS):
            dil = ATT_GROUPS[gi][1]
            if dil == 1:
                qkv = qkv0.reshape(B, 1, S, qkv0.shape[-1])
            else:
                w_g = jnp.concatenate(group_cols(gi), axis=1)
                qkv = in_proj(x_dil[gi].reshape(T, D), w_g, w_g.shape[1], f"in_proj_g{gi}")
                qkv = qkv.reshape(B, dil, S // dil, w_g.shape[1])
            o_g, st_g = attention_group(qkv, bm[gi], gi)
            o_list.append(o_g)
            st_list.append(st_g)
        h1, h1_tiles = mixer_out(o_list, st_list, glu, gates, h,
                       w_dw[l].reshape(CONV_WIDTH, CONV_CH), b_dw[l].reshape(1, CONV_CH),
                       conv_ln_g[l].reshape(1, CONV_CH), conv_ln_b[l].reshape(1, CONV_CH),
                       w_o_attn[l].astype(BF16), w_o_conv[l].astype(BF16), w_out[l].astype(BF16),
                       ln1_g[l].reshape(1, D), ln1_b[l].reshape(1, D))
        h2 = h1.reshape(T, D)
        gates_t, ids_t, rank_t, counts = router(h2, w_router[l], b_router[l])
        n_rows = T * TOP_K + N_EXPERTS * EXPERT_BLOCK
        pos_t, blk_e, blk_src, n_used, zstart, zcount, next_e, wslot = routing_layout(
            ids_t, rank_t, counts[:, 0], n_rows // EXPERT_BLOCK)
        xs = dispatch(h1_tiles, pos_t, zstart, zcount, n_used, n_rows)
        ys = experts(xs, blk_e, blk_src, n_used, next_e, wslot,
                     w_gate_up[l], b_gate_up[l].reshape(N_EXPERTS, 1, 2 * D_FF),
                     w_down[l], b_down[l].reshape(N_EXPERTS, 1, D))
        out = combine(ys, pos_t, gates_t, h2, ln2_g[l].reshape(1, D), ln2_b[l].reshape(1, D))
        h = out.reshape(B, S, D)
    return h
```

```python
import functools
import math

import jax
import jax.numpy as jnp
from jax import lax
from jax.experimental import pallas as pl
from jax.experimental.pallas import tpu as pltpu

D_MODEL = 1024
ATT_GROUPS = ((128, 1), (512, 4), (2048, 16))
N_GROUPS = len(ATT_GROUPS)
HEADS_PER_GROUP = 8
HEAD_DIM = 64
GROUP_WIDTH = HEADS_PER_GROUP * HEAD_DIM
QKV_WIDTH = N_GROUPS * GROUP_WIDTH
ATT_BLOCK = 128
N_BUCKETS = 32
MAX_DISTANCE = 2048
CONV_CH = 768
CONV_WIDTH = 31
CONV_HALO = 32
N_EXPERTS = 32
TOP_K = 4
D_FF = 1024
SWIGLU_LIMIT = 7.0
SWIGLU_ALPHA = 1.702
IN_WIDTH = 3 * QKV_WIDTH + 2 * CONV_CH + 2 * D_MODEL
LN_EPS = 1e-5
NEG_INF = -1e30
DEPTH = 1
DEEPNORM_ALPHA = (2 * DEPTH) ** 0.25

LANES = 128
SUBLANES = 8
EXPERT_BLOCK = 512
VMEM_LIMIT = 56 * 1024 * 1024

F32 = jnp.float32
BF16 = jnp.bfloat16


def _params(sem, vmem=VMEM_LIMIT):
    return pltpu.CompilerParams(dimension_semantics=sem, vmem_limit_bytes=vmem)


def _sigmoid(x):
    return 0.5 * jnp.tanh(0.5 * x) + 0.5


def _layer_norm(x, g, b):
    mu = jnp.mean(x, axis=-1, keepdims=True)
    xc = x - mu
    var = jnp.mean(xc * xc, axis=-1, keepdims=True)
    return xc * lax.rsqrt(var + LN_EPS) * g + b


def _in_proj_body(x_ref, w_ref, o_ref):
    o_ref[...] = jnp.dot(x_ref[...], w_ref[...], preferred_element_type=F32).astype(o_ref.dtype)


def in_proj(xb, w_b, tn, name, tm=1024):
    T = xb.shape[0]
    N = w_b.shape[1]
    tm = min(tm, T)
    return pl.pallas_call(
        _in_proj_body,
        grid=(N // tn, T // tm),
        in_specs=[pl.BlockSpec((tm, D_MODEL), lambda n, m: (m, 0)),
                  pl.BlockSpec((D_MODEL, tn), lambda n, m: (0, n))],
        out_specs=pl.BlockSpec((tm, tn), lambda n, m: (m, n)),
        out_shape=jax.ShapeDtypeStruct((T, N), BF16),
        compiler_params=_params(("arbitrary", "arbitrary")),
        name=name,
    )(xb, w_b)


QKV_COLS = 3 * GROUP_WIDTH


def _in_proj_dilate_body(*refs, tm, nslab):
    x_slabs, w_ref = refs[:nslab], refs[nslab]
    qkv_ref, glu_ref, gate_ref = refs[nslab + 1:nslab + 4]
    d_refs = refs[nslab + 4:]
    cols = []
    for c, x_ref in enumerate(x_slabs):
        cs = slice(c * LANES, (c + 1) * LANES)
        cols.append(x_ref[...].astype(BF16))
        for (_, dil), d_ref in zip(ATT_GROUPS[1:], d_refs):
            for r in range(dil):
                d_ref[r, :, cs] = x_ref[pl.ds(r, tm // dil, stride=dil), :].astype(BF16)
    xb = jnp.concatenate(cols, axis=1)
    proj = jnp.dot(xb, w_ref[...], preferred_element_type=F32)
    qkv_ref[...] = proj[:, :QKV_COLS].astype(qkv_ref.dtype)
    u_val = proj[:, QKV_COLS:QKV_COLS + CONV_CH]
    u_gate = proj[:, QKV_COLS + CONV_CH:QKV_COLS + 2 * CONV_CH]
    glu_ref[...] = (u_val * _sigmoid(u_gate)).astype(glu_ref.dtype)
    gate_ref[...] = _sigmoid(proj[:, QKV_COLS + 2 * CONV_CH:]).astype(gate_ref.dtype)


def in_proj_dilate(x3, w_b, name, tm=512):
    B, S, D = x3.shape
    N = w_b.shape[1]
    tm = min(tm, S)
    nslab = D // LANES
    dils = [dil for _, dil in ATT_GROUPS[1:]]
    per_seq = S // tm
    widths = (QKV_COLS, CONV_CH, 2 * D_MODEL)
    assert N == QKV_COLS + 2 * CONV_CH + 2 * D_MODEL
    return pl.pallas_call(
        functools.partial(_in_proj_dilate_body, tm=tm, nslab=nslab),
        grid=(B, per_seq),
        in_specs=[pl.BlockSpec((None, tm, LANES), functools.partial(lambda c, b, i: (b, i, c), c))
                  for c in range(nslab)]
        + [pl.BlockSpec((D, N), lambda b, i: (0, 0))],
        out_specs=[pl.BlockSpec((None, tm, w), lambda b, i: (b, i, 0)) for w in widths]
        + [pl.BlockSpec((None, dil, tm // dil, D), lambda b, i: (b, 0, i, 0)) for dil in dils],
        out_shape=[jax.ShapeDtypeStruct((B, S, w), BF16) for w in widths]
        + [jax.ShapeDtypeStruct((B, dil, S // dil, D), BF16) for dil in dils],
        compiler_params=_params(("arbitrary", "arbitrary")),
        name=name,
    )(*([x3] * nslab), w_b)


def _t5_bucket(dist):
    max_exact = N_BUCKETS // 2
    log_ratio = jnp.log(jnp.maximum(dist, max_exact).astype(F32) / max_exact) / math.log(MAX_DISTANCE / max_exact)
    large = jnp.minimum(max_exact + (log_ratio * (N_BUCKETS - max_exact)).astype(jnp.int32), N_BUCKETS - 1)
    return jnp.where(dist < max_exact, dist, large)


def _bias_body(tbl_ref, bucket_ref, band_ref, o_ref):
    g = pl.program_id(0)
    h = pl.program_id(1)
    col = g * HEADS_PER_GROUP + h
    bucket = bucket_ref[...]
    acc = jnp.zeros(bucket.shape, F32)
    for k in range(N_BUCKETS):
        acc = jnp.where(bucket == k, tbl_ref[k, col], acc)
    band = band_ref[...] != 0
    kj = lax.broadcasted_iota(jnp.int32, bucket.shape, 1)
    o_ref[0] = jnp.where(band, acc, NEG_INF)
    o_ref[1] = jnp.where(band & (kj >= ATT_BLOCK), acc, NEG_INF)


def bias_tables(rel_bias):
    qi = jnp.arange(ATT_BLOCK)[:, None]
    kj = jnp.arange(2 * ATT_BLOCK)[None, :]
    dist = qi - kj + ATT_BLOCK
    buckets, bands = [], []
    for window, dil in ATT_GROUPS:
        bands.append(((dist >= 0) & (dist <= window // dil)).astype(jnp.int32))
        buckets.append(_t5_bucket(jnp.maximum(dist, 0) * dil).astype(jnp.int32))
    buckets = jnp.stack(buckets)
    bands = jnp.stack(bands)
    blk = (None, ATT_BLOCK, 2 * ATT_BLOCK)
    return pl.pallas_call(
        _bias_body,
        grid=(N_GROUPS, HEADS_PER_GROUP),
        in_specs=[pl.BlockSpec(memory_space=pltpu.SMEM),
                  pl.BlockSpec(blk, lambda g, h: (g, 0, 0)),
                  pl.BlockSpec(blk, lambda g, h: (g, 0, 0))],
        out_specs=pl.BlockSpec((None, None, 2, ATT_BLOCK, 2 * ATT_BLOCK), lambda g, h: (g, h, 0, 0, 0)),
        out_shape=jax.ShapeDtypeStruct((N_GROUPS, HEADS_PER_GROUP, 2, ATT_BLOCK, 2 * ATT_BLOCK), F32),
        compiler_params=_params(("arbitrary", "arbitrary")),
        name="bias_tables",
    )(rel_bias.astype(F32), buckets, bands)


def _attn_body(q_ref, kc_ref, kp_ref, vc_ref, vp_ref, bm_ref, o_ref, st_ref, k_all, v_all, s_buf, *, tq):
    i = pl.program_id(2)
    nsub = tq // ATT_BLOCK
    k_all[0:ATT_BLOCK] = kp_ref[...]
    k_all[ATT_BLOCK:] = kc_ref[...]
    v_all[0:ATT_BLOCK] = vp_ref[...]
    v_all[ATT_BLOCK:] = vc_ref[...]
    lo = lax.broadcasted_iota(jnp.int32, (ATT_BLOCK, LANES), 1) < HEAD_DIM
    nt_dims = (((1,), (1,)), ((), ()))
    first = jnp.where(i == 0, 1, 0)

    def scores(s, slot):
        r0 = s * ATT_BLOCK
        for j in range(HEADS_PER_GROUP // 2):
            cs = slice(j * LANES, (j + 1) * LANES)
            qp = q_ref[r0:r0 + ATT_BLOCK, cs] * jnp.asarray(HEAD_DIM ** -0.5, BF16)
            kp = k_all[r0:r0 + 2 * ATT_BLOCK, cs]
            for hh in range(2):
                h = 2 * j + hh
                qh = jnp.where(lo if hh == 0 else ~lo, qp, jnp.zeros_like(qp))
                bias = bm_ref[h, first] if s == 0 else bm_ref[h, 0]
                s_buf[slot, h] = lax.dot_general(qh, kp, nt_dims, preferred_element_type=F32) + bias

    def softmax_pv(s, slot):
        rows = slice(s * ATT_BLOCK, (s + 1) * ATT_BLOCK)
        st_ref[rows, :] = jnp.zeros((ATT_BLOCK, LANES), F32)
        for j in range(HEADS_PER_GROUP // 2):
            vp = v_all[s * ATT_BLOCK:(s + 2) * ATT_BLOCK, j * LANES:(j + 1) * LANES]
            for hh in range(2):
                h = 2 * j + hh
                sc = s_buf[slot, h]
                m = jnp.max(sc, axis=-1, keepdims=True)
                p = jnp.exp(sc - m)
                den = jnp.sum(p, axis=-1, keepdims=True)
                pv = jnp.dot(p.astype(BF16), vp, preferred_element_type=F32)
                c0 = h * HEAD_DIM
                o_ref[rows, c0:c0 + HEAD_DIM] = pv[:, hh * HEAD_DIM:(hh + 1) * HEAD_DIM]
                st_ref[rows, h:h + 1] = m
                st_ref[rows, HEADS_PER_GROUP + h:HEADS_PER_GROUP + h + 1] = den

    scores(0, 0)
    for s in range(nsub):
        softmax_pv(s, s % 2)
        if s + 1 < nsub:
            scores(s + 1, (s + 1) % 2)


def attention_group(qkv, bm_g, gi, tq=1024):
    B, dil, L, _ = qkv.shape
    tq = min(tq, L)
    sub = tq // ATT_BLOCK

    def cur(col):
        return pl.BlockSpec((None, None, tq, GROUP_WIDTH), lambda b, r, i: (b, r, i, col))

    def prev(col):
        return pl.BlockSpec((None, None, ATT_BLOCK, GROUP_WIDTH),
                            lambda b, r, i: (b, r, jnp.maximum(i * sub - 1, 0), col))

    return pl.pallas_call(
        functools.partial(_attn_body, tq=tq),
        grid=(B, dil, L // tq),
        in_specs=[cur(0), cur(1), prev(1), cur(2), prev(2),
                  pl.BlockSpec((HEADS_PER_GROUP, 2, ATT_BLOCK, 2 * ATT_BLOCK), lambda b, r, i: (0, 0, 0, 0))],
        out_specs=[pl.BlockSpec((None, None, tq, GROUP_WIDTH), lambda b, r, i: (b, r, i, 0)),
                   pl.BlockSpec((None, None, tq, LANES), lambda b, r, i: (b, r, i, 0))],
        out_shape=[jax.ShapeDtypeStruct((B, dil, L, GROUP_WIDTH), F32),
                   jax.ShapeDtypeStruct((B, dil, L, LANES), F32)],
        scratch_shapes=[pltpu.VMEM((ATT_BLOCK + tq, GROUP_WIDTH), BF16),
                        pltpu.VMEM((ATT_BLOCK + tq, GROUP_WIDTH), BF16),
                        pltpu.VMEM((2, HEADS_PER_GROUP, ATT_BLOCK, 2 * ATT_BLOCK), F32)],
        compiler_params=_params(("arbitrary", "arbitrary", "arbitrary")),
        name=f"attention_g{gi}",
    )(qkv, qkv, qkv, qkv, qkv, bm_g)


def _split_bf16(x):
    hi = x.astype(BF16)
    lo = (x - hi.astype(F32)).astype(BF16)
    return hi, lo


def _load_token_tiles(ref, n):
    return jnp.concatenate([ref[pl.ds(c, n, stride=SUBLANES), :] for c in range(D_MODEL // LANES)], axis=1)


def _store_token_tiles(ref, x, n):
    for c in range(D_MODEL // LANES):
        ref[pl.ds(c, n, stride=SUBLANES), :] = x[:, c * LANES:(c + 1) * LANES]


def _to_token_order(blk_ref, tok_ref, dil, tm):
    n = tm // dil
    for r in range(dil):
        for c in range(tok_ref.shape[0]):
            tok_ref[c, pl.ds(r, n, stride=dil), :] = blk_ref[r, :, c * LANES:(c + 1) * LANES]


def _route(h, first_step, wt_ref, b_ref, gates_ref, ids_ref, rank_ref, cnt_ref, carry_ref, tri_ref, tm):
    @pl.when(first_step)
    def _():
        carry_ref[...] = jnp.zeros_like(carry_ref)
        r_i = lax.broadcasted_iota(jnp.int32, (tm, tm), 0)
        c_i = lax.broadcasted_iota(jnp.int32, (tm, tm), 1)
        tri_ref[...] = (r_i < c_i).astype(BF16)

    nt_dims = (((1,), (1,)), ((), ()))
    h_hi, h_lo = _split_bf16(h)
    w_hi, w_lo = _split_bf16(wt_ref[...])
    logits = (lax.dot_general(w_hi, h_hi, nt_dims, preferred_element_type=F32)
              + (lax.dot_general(w_hi, h_lo, nt_dims, preferred_element_type=F32)
                 + lax.dot_general(w_lo, h_hi, nt_dims, preferred_element_type=F32))
              + b_ref[...])
    expert = lax.broadcasted_iota(jnp.int32, (N_EXPERTS, tm), 0).astype(F32)
    row = lax.broadcasted_iota(jnp.int32, (SUBLANES, tm), 0)

    work = logits
    vals, hots = [], []
    ids = jnp.zeros((SUBLANES, tm), F32)
    for k in range(TOP_K):
        v = jnp.max(work, axis=0, keepdims=True)
        idx = jnp.min(jnp.where(work == v, expert, float(N_EXPERTS)), axis=0, keepdims=True)
        hot = expert == idx
        work = jnp.where(hot, -jnp.inf, work)
        ids = jnp.where(row == k, idx, ids)
        vals.append(v)
        hots.append(hot)

    es = [jnp.exp(v - vals[0]) for v in vals]
    esum = es[0] + es[1] + es[2] + es[3]
    gates = jnp.zeros((SUBLANES, tm), F32)
    for k in range(TOP_K):
        gates = jnp.where(row == k, es[k] / esum, gates)

    sel = (hots[0] | hots[1] | hots[2] | hots[3])
    before = jnp.dot(sel.astype(BF16), tri_ref[...], preferred_element_type=F32) + carry_ref[:, 0:1]
    rank = jnp.zeros((SUBLANES, tm), F32)
    for k in range(TOP_K):
        rk = jnp.sum(jnp.where(hots[k], before, 0.0), axis=0, keepdims=True)
        rank = jnp.where(row == k, rk, rank)
    carry_ref[...] = carry_ref[...] + jnp.sum(sel.astype(F32), axis=1, keepdims=True)

    gates_ref[...] = gates
    ids_ref[...] = ids.astype(jnp.int32)
    rank_ref[...] = rank.astype(jnp.int32)
    cnt_ref[...] = carry_ref[...].astype(jnp.int32)


def _mixer_out_body(o0, o1, o2, s0, s1, s2, glu_in, glu_halo, ga_ref, gc_ref, x_ref,
                    wdw_ref, bdw_ref, cg_ref, cb_ref, woa_ref, woc_ref, wout_ref, g1_ref, b1_ref, wr_ref, br_ref,
                    h_ref, ht_ref, gates_ref, ids_ref, rank_ref, cnt_ref,
                    glu_ref, dw_ref, shift_ref, tok_o1, tok_s1, tok_o2, tok_s2, carry_ref, tri_ref, *, tm, chunk):
    i = pl.program_id(1)

    ncol = GROUP_WIDTH // LANES
    _to_token_order(o1, tok_o1, ATT_GROUPS[1][1], tm)
    _to_token_order(s1, tok_s1, ATT_GROUPS[1][1], tm)
    _to_token_order(o2, tok_o2, ATT_GROUPS[2][1], tm)
    _to_token_order(s2, tok_s2, ATT_GROUPS[2][1], tm)
    outs = [o0[0],
            jnp.concatenate([tok_o1[c] for c in range(ncol)], axis=1),
            jnp.concatenate([tok_o2[c] for c in range(ncol)], axis=1)]
    sts = [s0[0], tok_s1[0], tok_s2[0]]

    mx = jnp.maximum(jnp.maximum(sts[0], sts[1]), sts[2])
    wts = [jnp.exp(st - mx) for st in sts]
    wsum = sum(pltpu.roll(st, LANES - HEADS_PER_GROUP, axis=1) * e for st, e in zip(sts, wts))
    row = lax.broadcasted_iota(jnp.int32, (LANES, GROUP_WIDTH), 0)
    colh = lax.broadcasted_iota(jnp.int32, (LANES, GROUP_WIDTH), 1) // HEAD_DIM
    expand = (row == colh).astype(BF16)
    attn = jnp.zeros((tm, GROUP_WIDTH), F32)
    head_lane = lax.broadcasted_iota(jnp.int32, (tm, LANES), 1) < HEADS_PER_GROUP
    for wt, o in zip(wts, outs):
        c_hi, c_lo = _split_bf16(jnp.where(head_lane, wt / wsum, 0.0))
        c = (jnp.dot(c_hi, expand, preferred_element_type=F32)
             + jnp.dot(c_lo, expand, preferred_element_type=F32))
        attn = attn + c * o
    a_out = jnp.dot(attn.astype(BF16), woa_ref[...], preferred_element_type=F32)

    glu_ref[0:CONV_HALO] = jnp.where(i == 0, 0.0, glu_halo[...].astype(F32))
    glu_ref[CONV_HALO:] = glu_in[...].astype(F32)
    first_tap = CONV_HALO - (CONV_WIDTH - 1)
    for b in range(1, SUBLANES):
        shift_ref[b - 1] = glu_ref[b:b + shift_ref.shape[1], :]

    for c0 in range(0, CONV_CH, LANES):
        cs = slice(c0, c0 + LANES)
        bias = jnp.broadcast_to(bdw_ref[:, cs], (chunk, LANES))
        for r0 in range(0, tm, chunk):
            acc = bias
            for j in range(CONV_WIDTH):
                a, b = divmod(first_tap + j, SUBLANES)
                lo_row = r0 + a * SUBLANES
                rows = glu_ref[lo_row:lo_row + chunk, cs] if b == 0 else shift_ref[b - 1, lo_row:lo_row + chunk, cs]
                acc = acc + wdw_ref[j:j + 1, cs] * rows
            dw_ref[r0:r0 + chunk, cs] = acc
    cn = _layer_norm(dw_ref[...], cg_ref[...], cb_ref[...])
    conv = cn * _sigmoid(cn)
    c_out = jnp.dot(conv.astype(BF16), woc_ref[...], preferred_element_type=F32)

    merged = ga_ref[...].astype(F32) * a_out + gc_ref[...].astype(F32) * c_out
    mix = jnp.dot(merged.astype(BF16), wout_ref[...], preferred_element_type=F32)
    h = _layer_norm(DEEPNORM_ALPHA * x_ref[...] + mix, g1_ref[...], b1_ref[...])
    h_ref[...] = h
    _store_token_tiles(ht_ref, h, tm)
    _route(h, (pl.program_id(0) == 0) & (i == 0), wr_ref, br_ref, gates_ref, ids_ref, rank_ref, cnt_ref,
           carry_ref, tri_ref, tm)


def mixer_out(o_list, st_list, glu3, gates3, x3, w_dw, b_dw, cg, cb, woa_b, woc_b, wout_b, g1, b1,
              w_router, b_router, tm=512, chunk=64):
    B, S, _ = x3.shape
    tm = min(tm, S)
    per_seq = S // tm
    halo_blocks = tm // CONV_HALO
    route_tile = pl.BlockSpec((SUBLANES, tm), lambda b, i: (0, b * per_seq + i))
    ncol = GROUP_WIDTH // LANES

    def tile(width, col=0):
        return pl.BlockSpec((None, tm, width), lambda b, i: (b, i, col))

    def dilated(gi, width):
        dil = ATT_GROUPS[gi][1]
        return pl.BlockSpec((None, dil, tm // dil, width), lambda b, i: (b, 0, i, 0))

    halo = pl.BlockSpec((None, CONV_HALO, CONV_CH), lambda b, i: (b, jnp.maximum(i * halo_blocks - 1, 0), 0))

    def whole(shape):
        return pl.BlockSpec(shape, lambda b, i: (0,) * len(shape))

    return pl.pallas_call(
        functools.partial(_mixer_out_body, tm=tm, chunk=chunk),
        grid=(B, S // tm),
        in_specs=[dilated(g, GROUP_WIDTH) for g in range(N_GROUPS)] + [dilated(g, LANES) for g in range(N_GROUPS)]
        + [tile(CONV_CH), halo, tile(D_MODEL, 0), tile(D_MODEL, 1), tile(D_MODEL),
           whole((CONV_WIDTH, CONV_CH)), whole((1, CONV_CH)), whole((1, CONV_CH)), whole((1, CONV_CH)),
           whole((GROUP_WIDTH, D_MODEL)), whole((CONV_CH, D_MODEL)), whole((D_MODEL, D_MODEL)),
           whole((1, D_MODEL)), whole((1, D_MODEL)), whole((N_EXPERTS, D_MODEL)), whole((N_EXPERTS, 1))],
        out_specs=[tile(D_MODEL), pl.BlockSpec((tm * SUBLANES, LANES), lambda b, i: (b * (S // tm) + i, 0)),
                   route_tile, route_tile, route_tile, whole((N_EXPERTS, LANES))],
        out_shape=[jax.ShapeDtypeStruct((B, S, D_MODEL), F32),
                   jax.ShapeDtypeStruct((B * S * SUBLANES, LANES), F32),
                   jax.ShapeDtypeStruct((SUBLANES, B * S), F32), jax.ShapeDtypeStruct((SUBLANES, B * S), jnp.int32),
                   jax.ShapeDtypeStruct((SUBLANES, B * S), jnp.int32),
                   jax.ShapeDtypeStruct((N_EXPERTS, LANES), jnp.int32)],
        scratch_shapes=[pltpu.VMEM((CONV_HALO + tm, CONV_CH), F32), pltpu.VMEM((tm, CONV_CH), F32),
                        pltpu.VMEM((SUBLANES - 1, tm + CONV_HALO - SUBLANES, CONV_CH), F32),
                        pltpu.VMEM((ncol, tm, LANES), F32), pltpu.VMEM((1, tm, LANES), F32),
                        pltpu.VMEM((ncol, tm, LANES), F32), pltpu.VMEM((1, tm, LANES), F32),
                        pltpu.VMEM((N_EXPERTS, LANES), F32), pltpu.VMEM((tm, tm), BF16)],
        compiler_params=_params(("arbitrary", "arbitrary")),
        name="mixer_out",
    )(*o_list, *st_list, glu3, glu3, gates3, gates3, x3,
      w_dw, b_dw, cg, cb, woa_b, woc_b, wout_b, g1, b1,
      w_router.astype(F32).T, b_router.astype(F32).reshape(N_EXPERTS, 1))


def _token_copy(src, dst, s_tok, d_tok, sem, n=1):
    rows = n * SUBLANES
    return pltpu.make_async_copy(src.at[pl.ds(pl.multiple_of(s_tok * SUBLANES, SUBLANES), rows), :],
                                 dst.at[pl.ds(pl.multiple_of(d_tok * SUBLANES, SUBLANES), rows), :], sem)


def _dispatch_body(zstart_ref, zcount_ref, n_used_ref, pos_ref, h_ref, xs_ref, zeros_ref, ring, sems, zsem, *, tm, nb):
    half = EXPERT_BLOCK // 2
    bits = [1 << s for s in reversed(range(half.bit_length()))]

    def zero_fill(e, wait):
        start, count = zstart_ref[e], zcount_ref[e]
        for bit in bits:
            @pl.when((count & bit) != 0)
            def _():
                cp = _token_copy(zeros_ref, xs_ref, 0, 0 if wait else start + (count & ~(2 * bit - 1)), zsem, bit)
                cp.wait() if wait else cp.start()

    def zero_tail(blk, wait):
        for part in range(2):
            cp = _token_copy(zeros_ref, xs_ref, 0, 0 if wait else blk * EXPERT_BLOCK + part * half, zsem, half)
            cp.wait() if wait else cp.start()

    @pl.when(pl.program_id(0) == 0)
    def _():
        zeros_ref[...] = jnp.zeros_like(zeros_ref)
        lax.fori_loop(0, N_EXPERTS, lambda e, c: (zero_fill(e, False), c)[1], 0)
        lax.fori_loop(n_used_ref[0], nb, lambda blk, c: (zero_tail(blk, False), c)[1], 0)

    i = pl.program_id(0)
    last = pl.num_programs(0) - 1

    def retire(slot):
        for k in range(TOP_K):
            _token_copy(ring.at[slot], xs_ref, 0, 0, sems.at[slot], tm).wait()

    def step(slot):
        pl.when(i >= 2)(functools.partial(retire, slot))
        ring[slot] = h_ref[...]

        def issue(t, c):
            for k in range(TOP_K):
                _token_copy(ring.at[slot], xs_ref, t, pos_ref[k * tm + t], sems.at[slot]).start(priority=k % 2)
            return c

        lax.fori_loop(0, tm, issue, 0)

        @pl.when(i == last)
        def _():
            retire(slot)
            pl.when(i >= 1)(functools.partial(retire, 1 - slot))

    for slot in range(2):
        pl.when(i % 2 == slot)(functools.partial(step, slot))

    @pl.when(pl.program_id(0) == 0)
    def _():
        lax.fori_loop(0, N_EXPERTS, lambda e, c: (zero_fill(e, True), c)[1], 0)
        lax.fori_loop(n_used_ref[0], nb, lambda blk, c: (zero_tail(blk, True), c)[1], 0)


def _tile_major(pos_t, tm):
    T = pos_t.shape[1]
    return pos_t[:TOP_K].reshape(TOP_K, T // tm, tm).transpose(1, 0, 2).reshape(-1)


def dispatch(ht, pos_t, zstart, zcount, n_used, n_rows, tm=1024):
    T = ht.shape[0] // SUBLANES
    tm = min(tm, T)
    grid_spec = pltpu.PrefetchScalarGridSpec(
        num_scalar_prefetch=3,
        grid=(T // tm,),
        in_specs=[pl.BlockSpec((TOP_K * tm,), lambda i, zs, zc, nu: (i,), memory_space=pltpu.SMEM),
                  pl.BlockSpec((tm * SUBLANES, LANES), lambda i, zs, zc, nu: (i, 0))],
        out_specs=pl.BlockSpec(memory_space=pl.ANY),
        scratch_shapes=[pltpu.VMEM((EXPERT_BLOCK // 2 * SUBLANES, LANES), F32),
                        pltpu.VMEM((2, tm * SUBLANES, LANES), F32), pltpu.SemaphoreType.DMA((2,)),
                        pltpu.SemaphoreType.DMA],
    )
    return pl.pallas_call(
        functools.partial(_dispatch_body, tm=tm, nb=n_rows // EXPERT_BLOCK),
        grid_spec=grid_spec,
        out_shape=jax.ShapeDtypeStruct((n_rows * SUBLANES, LANES), F32),
        compiler_params=_params(("arbitrary",)),
        name="dispatch",
    )(zstart, zcount, n_used, _tile_major(pos_t, tm), ht)


def _combine_body(pos_ref, pos_next_ref, gates_ref, h_ref, g2_ref, b2_ref, ys_ref, o_ref, buf, sems, *, tm):
    i = pl.program_id(0)

    def issue(p_ref, s):
        def body(t, c):
            for k in range(TOP_K):
                _token_copy(ys_ref, buf.at[s, k], p_ref[k * tm + t], t, sems.at[s]).start(priority=k % 2)
            return c

        lax.fori_loop(0, tm, body, 0)

    @pl.when(i == 0)
    def _():
        issue(pos_ref, 0)

    def step(slot):
        @pl.when(i + 1 < pl.num_programs(0))
        def _():
            issue(pos_next_ref, 1 - slot)

        for k in range(TOP_K):
            _token_copy(ys_ref, buf.at[slot, k], 0, 0, sems.at[slot], tm).wait()

        gates = gates_ref[...].T
        ffn = gates[:, 0:1] * _load_token_tiles(buf.at[slot, 0], tm)
        for k in range(1, TOP_K):
            ffn = ffn + gates[:, k:k + 1] * _load_token_tiles(buf.at[slot, k], tm)
        o_ref[...] = _layer_norm(DEEPNORM_ALPHA * h_ref[...] + ffn, g2_ref[...], b2_ref[...])

    for slot in range(2):
        pl.when(i % 2 == slot)(functools.partial(step, slot))


def combine(ys, pos_t, gates_t, h2, g2, b2, tm=256):
    T = h2.shape[0]
    tm = min(tm, T)
    last = T // tm - 1
    pos_flat = _tile_major(pos_t, tm)
    return pl.pallas_call(
        functools.partial(_combine_body, tm=tm),
        grid=(T // tm,),
        in_specs=[pl.BlockSpec((TOP_K * tm,), lambda i: (i,), memory_space=pltpu.SMEM),
                  pl.BlockSpec((TOP_K * tm,), lambda i: (jnp.minimum(i + 1, last),), memory_space=pltpu.SMEM),
                  pl.BlockSpec((SUBLANES, tm), lambda i: (0, i)),
                  pl.BlockSpec((tm, D_MODEL), lambda i: (i, 0)),
                  pl.BlockSpec((1, D_MODEL), lambda i: (0, 0)),
                  pl.BlockSpec((1, D_MODEL), lambda i: (0, 0)),
                  pl.BlockSpec(memory_space=pl.ANY)],
        out_specs=pl.BlockSpec((tm, D_MODEL), lambda i: (i, 0)),
        out_shape=jax.ShapeDtypeStruct((T, D_MODEL), F32),
        scratch_shapes=[pltpu.VMEM((2, TOP_K, tm * SUBLANES, LANES), F32), pltpu.SemaphoreType.DMA((2,))],
        compiler_params=_params(("arbitrary",)),
        name="combine",
    )(pos_flat, pos_flat, gates_t, h2, g2, b2, ys)


def _experts_body(blk_e, blk_src, n_used, next_e, wslot, xs_ref, wgu_hbm, bgu_ref, wdn_hbm, bdn_ref, ys_ref,
                  wgu_f, wdn_f, wgu_b, wdn_b, sems):
    del blk_src
    i = pl.program_id(0)
    used = i < n_used[0]
    e = blk_e[i]

    def weight_copies(expert, slot):
        return (pltpu.make_async_copy(wgu_hbm.at[expert], wgu_f.at[slot], sems.at[slot]),
                pltpu.make_async_copy(wdn_hbm.at[expert], wdn_f.at[slot], sems.at[slot]))

    @pl.when(i == 0)
    def _():
        for cp in weight_copies(e, 0):
            cp.start()

    @pl.when((i == 0) | (e != blk_e[jnp.maximum(i - 1, 0)]))
    def _():
        nxt = next_e[e]
        for slot in range(2):
            @pl.when(wslot[e] == slot)
            def _():
                for cp in weight_copies(e, slot):
                    cp.wait()
                wgu_b[...] = wgu_f[slot].astype(BF16)
                wdn_b[...] = wdn_f[slot].astype(BF16)

                @pl.when(nxt < N_EXPERTS)
                def _():
                    for cp in weight_copies(nxt, 1 - slot):
                        cp.start()

    @pl.when(used)
    def _():
        x = _load_token_tiles(xs_ref, EXPERT_BLOCK).astype(BF16)
        hgu = jnp.dot(x, wgu_b[...], preferred_element_type=F32) + bgu_ref[...]
        gate = jnp.minimum(hgu[:, :D_FF], SWIGLU_LIMIT)
        up = jnp.clip(hgu[:, D_FF:], -SWIGLU_LIMIT, SWIGLU_LIMIT)
        act = (up + 1.0) * gate * _sigmoid(SWIGLU_ALPHA * gate)
        y = jnp.dot(act.astype(BF16), wdn_b[...], preferred_element_type=F32) + bdn_ref[...]
        _store_token_tiles(ys_ref, y, EXPERT_BLOCK)

    @pl.when(jnp.logical_not(used))
    def _():
        ys_ref[...] = jnp.zeros_like(ys_ref)


def experts(xs, blk_e, blk_src, n_used, next_e, wslot, wgu, bgu, wdn, bdn):
    n_rows = xs.shape[0] // SUBLANES
    nb = n_rows // EXPERT_BLOCK
    rows = EXPERT_BLOCK * SUBLANES
    grid_spec = pltpu.PrefetchScalarGridSpec(
        num_scalar_prefetch=5,
        grid=(nb,),
        in_specs=[pl.BlockSpec((rows, LANES), lambda i, e, s, n, ne, ws: (s[i], 0)),
                  pl.BlockSpec(memory_space=pl.ANY),
                  pl.BlockSpec((None, 1, 2 * D_FF), lambda i, e, s, n, ne, ws: (e[i], 0, 0)),
                  pl.BlockSpec(memory_space=pl.ANY),
                  pl.BlockSpec((None, 1, D_MODEL), lambda i, e, s, n, ne, ws: (e[i], 0, 0))],
        out_specs=pl.BlockSpec((rows, LANES), lambda i, e, s, n, ne, ws: (i, 0)),
        scratch_shapes=[pltpu.VMEM((2, D_MODEL, 2 * D_FF), F32), pltpu.VMEM((2, D_FF, D_MODEL), F32),
                        pltpu.VMEM((D_MODEL, 2 * D_FF), BF16), pltpu.VMEM((D_FF, D_MODEL), BF16),
                        pltpu.SemaphoreType.DMA((2,))],
    )
    return pl.pallas_call(
        _experts_body,
        grid_spec=grid_spec,
        out_shape=jax.ShapeDtypeStruct((n_rows * SUBLANES, LANES), F32),
        compiler_params=_params(("arbitrary",)),
        name="experts",
    )(blk_e, blk_src, n_used, next_e, wslot, xs, wgu, bgu, wdn, bdn)


def routing_layout(ids, rank, counts, n_blocks):
    padded = (counts + EXPERT_BLOCK - 1) // EXPERT_BLOCK * EXPERT_BLOCK
    pad_ends = jnp.cumsum(padded)
    pad_starts = pad_ends - padded
    start_of = jnp.zeros_like(ids)
    for e in range(N_EXPERTS):
        start_of = jnp.where(ids == e, pad_starts[e], start_of)
    pos_t = (start_of + rank).astype(jnp.int32)
    n_used = jnp.maximum(pad_ends[-1] // EXPERT_BLOCK, 1).astype(jnp.int32)
    blk = jnp.minimum(jnp.arange(n_blocks, dtype=jnp.int32), n_used - 1)
    blk_e = jnp.sum(pad_ends[None, :] <= (blk * EXPERT_BLOCK)[:, None], axis=1)
    blk_e = jnp.minimum(blk_e, N_EXPERTS - 1).astype(jnp.int32)
    zstart = (pad_starts + counts).astype(jnp.int32)
    zcount = (padded - counts).astype(jnp.int32)
    nonempty = counts > 0
    expert_ids = jnp.arange(N_EXPERTS, dtype=jnp.int32)
    later = jnp.where(nonempty[None, :] & (expert_ids[None, :] > expert_ids[:, None]), expert_ids[None, :], N_EXPERTS)
    next_e = jnp.min(later, axis=1).astype(jnp.int32)
    wslot = ((jnp.cumsum(nonempty) - 1) % 2).astype(jnp.int32)
    return pos_t, blk_e, blk, n_used.reshape(1), zstart, zcount, next_e, wslot


def kernel(x, w_in, rel_bias, w_dw, b_dw, conv_ln_g, conv_ln_b, w_o_attn, w_o_conv, w_out, ln1_g, ln1_b,
           w_router, b_router, w_gate_up, b_gate_up, w_down, b_down, ln2_g, ln2_b):
    B, S, D = x.shape
    T = B * S
    h = x
    bm = bias_tables(rel_bias)
    q_off, k_off, v_off, rest = 0, QKV_WIDTH, 2 * QKV_WIDTH, 3 * QKV_WIDTH
    for l in range(DEPTH):
        wb = w_in[l].astype(BF16)

        def group_cols(gi):
            return [wb[:, off + gi * GROUP_WIDTH:off + (gi + 1) * GROUP_WIDTH] for off in (q_off, k_off, v_off)]

        w_main = jnp.concatenate(group_cols(0) + [wb[:, rest:]], axis=1)
        qkv0, glu, gates, *x_dilated = in_proj_dilate(h, w_main, "in_proj_main")
        x_dil = [None] + x_dilated
        o_list, st_list = [], []
        for gi in range(N_GROUPS):
            dil = ATT_GROUPS[gi][1]
            if dil == 1:
                qkv = qkv0.reshape(B, 1, S, qkv0.shape[-1])
            else:
                w_g = jnp.concatenate(group_cols(gi), axis=1)
                qkv = in_proj(x_dil[gi].reshape(T, D), w_g, w_g.shape[1], f"in_proj_g{gi}")
                qkv = qkv.reshape(B, dil, S // dil, w_g.shape[1])
            o_g, st_g = attention_group(qkv, bm[gi], gi)
            o_list.append(o_g)
            st_list.append(st_g)
        h1, h1_tiles, gates_t, ids_t, rank_t, counts = mixer_out(o_list, st_list, glu, gates, h,
                       w_dw[l].reshape(CONV_WIDTH, CONV_CH), b_dw[l].reshape(1, CONV_CH),
                       conv_ln_g[l].reshape(1, CONV_CH), conv_ln_b[l].reshape(1, CONV_CH),
                       w_o_attn[l].astype(BF16), w_o_conv[l].astype(BF16), w_out[l].astype(BF16),
                       ln1_g[l].reshape(1, D), ln1_b[l].reshape(1, D), w_router[l], b_router[l])
        h2 = h1.reshape(T, D)
        n_rows = T * TOP_K + N_EXPERTS * EXPERT_BLOCK
        pos_t, blk_e, blk_src, n_used, zstart, zcount, next_e, wslot = routing_layout(
            ids_t, rank_t, counts[:, 0], n_rows // EXPERT_BLOCK)
        xs = dispatch(h1_tiles, pos_t, zstart, zcount, n_used, n_rows)
        ys = experts(xs, blk_e, blk_src, n_used, next_e, wslot,
                     w_gate_up[l], b_gate_up[l].reshape(N_EXPERTS, 1, 2 * D_FF),
                     w_down[l], b_down[l].reshape(N_EXPERTS, 1, D))
        out = combine(ys, pos_t, gates_t, h2, ln2_g[l].reshape(1, D), ln2_b[l].reshape(1, D))
        h = out.reshape(B, S, D)
    return h
```
